```python
import jax
import jax.numpy as jnp
from jax import lax
import numpy as np

D_MODEL = 1024
BATCH = 8
SEQ = 16384
DEPTH = 2

CHUNK = 64
EPS = 1e-6
NEG_INF = -1e30

RET_HEADS = 4
RET_QK_DIM = 128
RET_V_DIM = 256
ATT_HEADS = 8
ATT_HEAD_DIM = 64
ATT_PAST_CHUNKS = 8
MAX_REL = 128
SGU_BLOCK = 128
SGU_GROUPS = 8
SGU_WIDTH = 2048
FFN_HIDDEN = 2816
CONV_WIDTH = 3

RET_QK_W = RET_HEADS * RET_QK_DIM
RET_V_W = RET_HEADS * RET_V_DIM
ATT_W = ATT_HEADS * ATT_HEAD_DIM
AB_IN_W = 2 * RET_QK_W + 2 * RET_V_W + 3 * ATT_W
AB_OUT_W = RET_V_W + ATT_W
N_EVEN = (DEPTH + 1) // 2
N_ODD = DEPTH // 2

kernel_name = "hybrid_retention_chunkattn_gmlp_convffn"


def rms_norm(x, g):
    xf = x.astype(jnp.float32)
    y = xf * lax.rsqrt(jnp.mean(xf * xf, axis=-1, keepdims=True) + EPS)
    return (y * g.astype(jnp.float32)).astype(x.dtype)


def layer_norm(x, g, b):
    xf = x.astype(jnp.float32)
    mu = jnp.mean(xf, axis=-1, keepdims=True)
    var = jnp.mean(jnp.square(xf - mu), axis=-1, keepdims=True)
    y = (xf - mu) * lax.rsqrt(var + EPS)
    return (y * g.astype(jnp.float32) + b.astype(jnp.float32)).astype(x.dtype)


def rotary(x, pos):
    half = x.shape[-1] // 2
    inv = 1.0 / (10000.0 ** jnp.linspace(0.0, 1.0, half, dtype=jnp.float32))
    ang = pos.astype(jnp.float32)[:, None] * inv[None, :]
    cos = jnp.cos(ang)[None, :, None, :]
    sin = jnp.sin(ang)[None, :, None, :]
    xf = x.astype(jnp.float32)
    x1, x2 = xf[..., :half], xf[..., half:]
    return jnp.concatenate([x1 * cos - x2 * sin, x1 * sin + x2 * cos], axis=-1)


def retention(q, k, v):
    B, T, H, dk = q.shape
    dv = v.shape[-1]
    nc = T // CHUNK
    f32 = jnp.float32
    log_g = jnp.log1p(-jnp.exp2(-5.0 - jnp.arange(H, dtype=f32)))
    qc = q.astype(f32).reshape(B, nc, CHUNK, H, dk)
    kc = (k.astype(f32) * dk ** -0.5).reshape(B, nc, CHUNK, H, dk)
    vc = v.astype(f32).reshape(B, nc, CHUNK, H, dv)
    idx = jnp.arange(CHUNK, dtype=f32)
    d_intra = jnp.exp(log_g[:, None, None] * jnp.abs(idx[:, None] - idx[None, :]))
    s = jnp.einsum('bnihd,bnjhd->bnhij', qc, kc) * d_intra
    o_intra = jnp.einsum('bnhij,bnjhe->bnihe', s, vc)
    k_dec = jnp.exp(log_g[None, :] * (CHUNK - 1 - idx)[:, None])
    q_dec = jnp.exp(log_g[None, :] * (idx + 1.0)[:, None])
    chunk_dec = jnp.exp(log_g * CHUNK)

    def step(state, inp):
        qn, kn, vn = inp
        o = jnp.einsum('bihd,bhde->bihe', qn * q_dec[None, :, :, None], state)
        state = state * chunk_dec[None, :, None, None] + jnp.einsum(
            'bjhd,bjhe->bhde', kn * k_dec[None, :, :, None], vn)
        return state, o

    init = jnp.zeros((B, H, dk, dv), f32)
    _, o_inter = lax.scan(step, init, (qc.swapaxes(0, 1), kc.swapaxes(0, 1), vc.swapaxes(0, 1)))
    o = o_intra + o_inter.swapaxes(0, 1)
    return o.reshape(B, T, H, dv)


def _chunk_attn_one_head(args):
    q, k, v, rel_bias = args
    B, T, d = q.shape
    nc = T // CHUNK
    nb = ATT_PAST_CHUNKS + 1
    qc = q.reshape(B, nc, CHUNK, d)
    pad = ((0, 0), (ATT_PAST_CHUNKS * CHUNK, 0), (0, 0))
    kp = jnp.pad(k, pad).reshape(B, nc + ATT_PAST_CHUNKS, CHUNK, d)
    vp = jnp.pad(v, pad).reshape(B, nc + ATT_PAST_CHUNKS, CHUNK, d)
    band_idx = jnp.arange(nc)[:, None] + jnp.arange(nb)[None, :]
    kb = kp[:, band_idx].reshape(B, nc, nb * CHUNK, d)
    vb = vp[:, band_idx].reshape(B, nc, nb * CHUNK, d)
    s = jnp.einsum('bnid,bnjd->bnij', qc, kb).astype(jnp.float32) * (d ** -0.5)
    qpos = jnp.arange(CHUNK) + ATT_PAST_CHUNKS * CHUNK
    kpos = jnp.arange(nb * CHUNK)
    rel = jnp.clip(qpos[:, None] - kpos[None, :], -MAX_REL, MAX_REL) + MAX_REL
    s = s + rel_bias.astype(jnp.float32)[rel][None, None]
    valid = jnp.repeat(band_idx >= ATT_PAST_CHUNKS, CHUNK, axis=1)
    s = jnp.where(valid[None, :, None, :], s, NEG_INF)
    p = jax.nn.softmax(s, axis=-1).astype(v.dtype)
    o = jnp.einsum('bnij,bnjd->bnid', p, vb)
    return o.reshape(B, T, d)


def chunk_rel_attention(q, k, v, rel_bias):
    o = lax.map(_chunk_attn_one_head,
                (q.transpose(2, 0, 1, 3), k.transpose(2, 0, 1, 3), v.transpose(2, 0, 1, 3), rel_bias))
    return o.transpose(1, 2, 0, 3)


def ab_mixer(h, w_in, w_out, rel_bias, pos):
    B, T, _ = h.shape
    z = h @ w_in
    splits = [RET_QK_W, 2 * RET_QK_W, 2 * RET_QK_W + RET_V_W, 2 * RET_QK_W + 2 * RET_V_W,
              2 * RET_QK_W + 2 * RET_V_W + ATT_W, 2 * RET_QK_W + 2 * RET_V_W + 2 * ATT_W]
    q_a, k_a, v_a, g_a, q_b, k_b, v_b = jnp.split(z, splits, axis=-1)
    q_a = rotary(q_a.reshape(B, T, RET_HEADS, RET_QK_DIM), pos)
    k_a = rotary(k_a.reshape(B, T, RET_HEADS, RET_QK_DIM), pos)
    r = retention(q_a, k_a, v_a.reshape(B, T, RET_HEADS, RET_V_DIM))
    mu = jnp.mean(r, axis=-1, keepdims=True)
    var = jnp.mean(jnp.square(r - mu), axis=-1, keepdims=True)
    r = ((r - mu) * lax.rsqrt(var + EPS)).reshape(B, T, RET_V_W)
    y_a = (jax.nn.silu(g_a.astype(jnp.float32)) * r).astype(h.dtype)
    y_b = chunk_rel_attention(q_b.reshape(B, T, ATT_HEADS, ATT_HEAD_DIM),
                              k_b.reshape(B, T, ATT_HEADS, ATT_HEAD_DIM),
                              v_b.reshape(B, T, ATT_HEADS, ATT_HEAD_DIM),
                              rel_bias).reshape(B, T, ATT_W)
    return jnp.concatenate([y_a, y_b.astype(h.dtype)], axis=-1) @ w_out


def sgu_mixer(h, w_in, ln_g, ln_b, w_s, b_s, w_out):
    B, T, _ = h.shape
    z = jax.nn.gelu(h @ w_in)
    u, v = jnp.split(z, 2, axis=-1)
    v = layer_norm(v, ln_g, ln_b)
    nb = T // SGU_BLOCK
    vg = v.reshape(B, nb, SGU_BLOCK, SGU_GROUPS, SGU_WIDTH // SGU_GROUPS)
    i = jnp.arange(SGU_BLOCK)
    mask = (i[None, :] // CHUNK) <= (i[:, None] // CHUNK)
    w = jnp.where(mask[None], w_s, jnp.zeros_like(w_s))
    mixed = jnp.einsum('gij,bnjgc->bnigc', w, vg) + b_s.T[None, None, :, :, None]
    y = u * mixed.reshape(B, T, SGU_WIDTH)
    return y @ w_out


def conv_ffn(h, w_up, conv_w, conv_b, w_down):
    z = h @ w_up
    c = z.shape[-1]
    z = lax.conv_general_dilated(z, conv_w[:, None, :].astype(z.dtype), window_strides=(1,),
                                 padding=[(CONV_WIDTH - 1, 0)],
                                 dimension_numbers=('NWC', 'WIO', 'NWC'),
                                 feature_group_count=c) + conv_b
    gate, up = jnp.split(z, 2, axis=-1)
    return (jax.nn.gelu(gate) * up) @ w_down


def _fwd_setup_inputs(seed: int = 0) -> dict:
    key = jax.random.key(seed)
    ks = jax.random.split(key, 20)
    n = jax.random.normal
    f32 = jnp.float32
    nr = 2 * MAX_REL + 1
    return {
        "x": n(ks[0], (BATCH, SEQ, D_MODEL), f32),
        "attn_norm_g": 1.0 + 0.02 * n(ks[1], (DEPTH, D_MODEL), f32),
        "ffn_norm_g": 1.0 + 0.02 * n(ks[2], (DEPTH, D_MODEL), f32),
        "ab_w_in": n(ks[3], (N_EVEN, D_MODEL, AB_IN_W), f32) * D_MODEL ** -0.5,
        "ab_w_out": n(ks[4], (N_EVEN, AB_OUT_W, D_MODEL), f32) * AB_OUT_W ** -0.5,
        "ab_rel_bias": 0.1 * n(ks[5], (N_EVEN, ATT_HEADS, nr), f32),
        "c_w_in": n(ks[6], (N_ODD, D_MODEL, 2 * SGU_WIDTH), f32) * D_MODEL ** -0.5,
        "c_ln_g": 1.0 + 0.02 * n(ks[7], (N_ODD, SGU_WIDTH), f32),
        "c_ln_b": 0.02 * n(ks[8], (N_ODD, SGU_WIDTH), f32),
        "c_w_s": n(ks[9], (N_ODD, SGU_GROUPS, SGU_BLOCK, SGU_BLOCK), f32) * SGU_BLOCK ** -0.5,
        "c_b_s": 1.0 + 0.02 * n(ks[10], (N_ODD, SGU_GROUPS, SGU_BLOCK), f32),
        "c_w_out": n(ks[11], (N_ODD, SGU_WIDTH, D_MODEL), f32) * SGU_WIDTH ** -0.5,
        "ffn_w_up": n(ks[12], (DEPTH, D_MODEL, 2 * FFN_HIDDEN), f32) * D_MODEL ** -0.5,
        "ffn_conv_w": n(ks[13], (DEPTH, CONV_WIDTH, 2 * FFN_HIDDEN), f32) * CONV_WIDTH ** -0.5,
        "ffn_conv_b": 0.02 * n(ks[14], (DEPTH, 2 * FFN_HIDDEN), f32),
        "ffn_w_down": n(ks[15], (DEPTH, FFN_HIDDEN, D_MODEL), f32) * FFN_HIDDEN ** -0.5,
        "final_norm_g": 1.0 + 0.02 * n(ks[16], (D_MODEL,), f32),
    }


def _fwd_reference(x, attn_norm_g, ffn_norm_g, ab_w_in, ab_w_out, ab_rel_bias, c_w_in, c_ln_g,
              c_ln_b, c_w_s, c_b_s, c_w_out, ffn_w_up, ffn_conv_w, ffn_conv_b, ffn_w_down,
              final_norm_g):
    h = x
    pos = jnp.arange(x.shape[1])
    for layer in range(DEPTH):
        hn = rms_norm(h, attn_norm_g[layer])
        i = layer // 2
        if layer % 2 == 0:
            h = h + ab_mixer(hn, ab_w_in[i], ab_w_out[i], ab_rel_bias[i], pos).astype(h.dtype)
        else:
            h = h + sgu_mixer(hn, c_w_in[i], c_ln_g[i], c_ln_b[i], c_w_s[i], c_b_s[i],
                              c_w_out[i]).astype(h.dtype)
        h = h + conv_ffn(rms_norm(h, ffn_norm_g[layer]), ffn_w_up[layer], ffn_conv_w[layer],
                         ffn_conv_b[layer], ffn_w_down[layer]).astype(h.dtype)
    return rms_norm(h, final_norm_g)


import jax as _jax
import jax.numpy as _jnp

TWIN_FORMAT = 'train_step'
FWD_PARAMS = ['x', 'attn_norm_g', 'ffn_norm_g', 'ab_w_in', 'ab_w_out', 'ab_rel_bias', 'c_w_in', 'c_ln_g', 'c_ln_b', 'c_w_s', 'c_b_s', 'c_w_out', 'ffn_w_up', 'ffn_conv_w', 'ffn_conv_b', 'ffn_w_down', 'final_norm_g']
TWIN_WEIGHTS = ['attn_norm_g', 'ffn_norm_g', 'ab_w_in', 'ab_w_out', 'ab_rel_bias', 'c_w_in', 'c_ln_g', 'c_ln_b', 'c_w_s', 'c_b_s', 'c_w_out', 'ffn_w_up', 'ffn_conv_w', 'ffn_conv_b', 'ffn_w_down', 'final_norm_g']
TWIN_DIFF_INPUT = 'x'
TWIN_INPUTS = ['x', 'attn_norm_g', 'ffn_norm_g', 'ab_w_in', 'ab_w_out', 'ab_rel_bias', 'c_w_in', 'c_ln_g', 'c_ln_b', 'c_w_s', 'c_b_s', 'c_w_out', 'ffn_w_up', 'ffn_conv_w', 'ffn_conv_b', 'ffn_w_down', 'final_norm_g', 'loss_target', 'm_attn_norm_g', 'm_ffn_norm_g', 'm_ab_w_in', 'm_ab_w_out', 'm_ab_rel_bias', 'm_c_w_in', 'm_c_ln_g', 'm_c_ln_b', 'm_c_w_s', 'm_c_b_s', 'm_c_w_out', 'm_ffn_w_up', 'm_ffn_conv_w', 'm_ffn_conv_b', 'm_ffn_w_down', 'm_final_norm_g', 'v_attn_norm_g', 'v_ffn_norm_g', 'v_ab_w_in', 'v_ab_w_out', 'v_ab_rel_bias', 'v_c_w_in', 'v_c_ln_g', 'v_c_ln_b', 'v_c_w_s', 'v_c_b_s', 'v_c_w_out', 'v_ffn_w_up', 'v_ffn_conv_w', 'v_ffn_conv_b', 'v_ffn_w_down', 'v_final_norm_g']
TWIN_OUTPUTS = ['loss', 'grad_x', 'grad_attn_norm_g', 'grad_ffn_norm_g', 'grad_ab_w_in', 'grad_ab_w_out', 'grad_ab_rel_bias', 'grad_c_w_in', 'grad_c_ln_g', 'grad_c_ln_b', 'grad_c_w_s', 'grad_c_b_s', 'grad_c_w_out', 'grad_ffn_w_up', 'grad_ffn_conv_w', 'grad_ffn_conv_b', 'grad_ffn_w_down', 'grad_final_norm_g', 'delta_attn_norm_g', 'delta_ffn_norm_g', 'delta_ab_w_in', 'delta_ab_w_out', 'delta_ab_rel_bias', 'delta_c_w_in', 'delta_c_ln_g', 'delta_c_ln_b', 'delta_c_w_s', 'delta_c_b_s', 'delta_c_w_out', 'delta_ffn_w_up', 'delta_ffn_conv_w', 'delta_ffn_conv_b', 'delta_ffn_w_down', 'delta_final_norm_g', 'new_m_attn_norm_g', 'new_m_ffn_norm_g', 'new_m_ab_w_in', 'new_m_ab_w_out', 'new_m_ab_rel_bias', 'new_m_c_w_in', 'new_m_c_ln_g', 'new_m_c_ln_b', 'new_m_c_w_s', 'new_m_c_b_s', 'new_m_c_w_out', 'new_m_ffn_w_up', 'new_m_ffn_conv_w', 'new_m_ffn_conv_b', 'new_m_ffn_w_down', 'new_m_final_norm_g', 'new_v_attn_norm_g', 'new_v_ffn_norm_g', 'new_v_ab_w_in', 'new_v_ab_w_out', 'new_v_ab_rel_bias', 'new_v_c_w_in', 'new_v_c_ln_g', 'new_v_c_ln_b', 'new_v_c_w_s', 'new_v_c_b_s', 'new_v_c_w_out', 'new_v_ffn_w_up', 'new_v_ffn_conv_w', 'new_v_ffn_conv_b', 'new_v_ffn_w_down', 'new_v_final_norm_g']
TWIN_LEAF_KINDS = {'loss': 'loss', 'grad_x': 'grad_x', 'grad_attn_norm_g': 'grad_w', 'grad_ffn_norm_g': 'grad_w', 'grad_ab_w_in': 'grad_w', 'grad_ab_w_out': 'grad_w', 'grad_ab_rel_bias': 'grad_w', 'grad_c_w_in': 'grad_w', 'grad_c_ln_g': 'grad_w', 'grad_c_ln_b': 'grad_w', 'grad_c_w_s': 'grad_w', 'grad_c_b_s': 'grad_w', 'grad_c_w_out': 'grad_w', 'grad_ffn_w_up': 'grad_w', 'grad_ffn_conv_w': 'grad_w', 'grad_ffn_conv_b': 'grad_w', 'grad_ffn_w_down': 'grad_w', 'grad_final_norm_g': 'grad_w', 'delta_attn_norm_g': 'delta_w', 'delta_ffn_norm_g': 'delta_w', 'delta_ab_w_in': 'delta_w', 'delta_ab_w_out': 'delta_w', 'delta_ab_rel_bias': 'delta_w', 'delta_c_w_in': 'delta_w', 'delta_c_ln_g': 'delta_w', 'delta_c_ln_b': 'delta_w', 'delta_c_w_s': 'delta_w', 'delta_c_b_s': 'delta_w', 'delta_c_w_out': 'delta_w', 'delta_ffn_w_up': 'delta_w', 'delta_ffn_conv_w': 'delta_w', 'delta_ffn_conv_b': 'delta_w', 'delta_ffn_w_down': 'delta_w', 'delta_final_norm_g': 'delta_w', 'new_m_attn_norm_g': 'new_m', 'new_m_ffn_norm_g': 'new_m', 'new_m_ab_w_in': 'new_m', 'new_m_ab_w_out': 'new_m', 'new_m_ab_rel_bias': 'new_m', 'new_m_c_w_in': 'new_m', 'new_m_c_ln_g': 'new_m', 'new_m_c_ln_b': 'new_m', 'new_m_c_w_s': 'new_m', 'new_m_c_b_s': 'new_m', 'new_m_c_w_out': 'new_m', 'new_m_ffn_w_up': 'new_m', 'new_m_ffn_conv_w': 'new_m', 'new_m_ffn_conv_b': 'new_m', 'new_m_ffn_w_down': 'new_m', 'new_m_final_norm_g': 'new_m', 'new_v_attn_norm_g': 'new_v', 'new_v_ffn_norm_g': 'new_v', 'new_v_ab_w_in': 'new_v', 'new_v_ab_w_out': 'new_v', 'new_v_ab_rel_bias': 'new_v', 'new_v_c_w_in': 'new_v', 'new_v_c_ln_g': 'new_v', 'new_v_c_ln_b': 'new_v', 'new_v_c_w_s': 'new_v', 'new_v_c_b_s': 'new_v', 'new_v_c_w_out': 'new_v', 'new_v_ffn_w_up': 'new_v', 'new_v_ffn_conv_w': 'new_v', 'new_v_ffn_conv_b': 'new_v', 'new_v_ffn_w_down': 'new_v', 'new_v_final_norm_g': 'new_v'}


def _forward(args):
    return _fwd_reference(*[args[k] for k in FWD_PARAMS])


def _output_shape():
    def fwd():
        inp = _fwd_setup_inputs(0)
        return _fwd_reference(*[inp[k] for k in FWD_PARAMS])
    out = _jax.eval_shape(fwd)
    return out.shape, out.dtype

N_MICROBATCH = 1
ADAM_LR = 0.001
ADAM_B1 = 0.9
ADAM_B2 = 0.999
ADAM_EPS = 1e-08
ADAM_WD = 0.01
ADAM_STEP = 10
PER_EXAMPLE_BATCH_AXIS = {'x': 0, 'loss_target': 0}
SHARED_INPUTS = []
_WEIGHT_DTYPES = {'attn_norm_g': _jnp.float32, 'ffn_norm_g': _jnp.float32, 'ab_w_in': _jnp.float32, 'ab_w_out': _jnp.float32, 'ab_rel_bias': _jnp.float32, 'c_w_in': _jnp.float32, 'c_ln_g': _jnp.float32, 'c_ln_b': _jnp.float32, 'c_w_s': _jnp.float32, 'c_b_s': _jnp.float32, 'c_w_out': _jnp.float32, 'ffn_w_up': _jnp.float32, 'ffn_conv_w': _jnp.float32, 'ffn_conv_b': _jnp.float32, 'ffn_w_down': _jnp.float32, 'final_norm_g': _jnp.float32}
MOMENT_SCALE = {'attn_norm_g': 3.594475e-01, 'ffn_norm_g': 2.758070e-01, 'ab_w_in': 1.992225e-01, 'ab_w_out': 2.084557e-01, 'ab_rel_bias': 2.302073e-02, 'c_w_in': 1.350461e-01, 'c_ln_g': 9.874132e-02, 'c_ln_b': 9.469946e-02, 'c_w_s': 1.381333e-01, 'c_b_s': 1.744001e-01, 'c_w_out': 2.197387e-01, 'ffn_w_up': 1.124681e-01, 'ffn_conv_w': 1.124571e-01, 'ffn_conv_b': 1.201944e-01, 'ffn_w_down': 1.841943e-01, 'final_norm_g': 1.282405e+02}


def _to_microbatches(a, axis):
    t = _jnp.moveaxis(a, axis, 0)
    t = t.reshape((N_MICROBATCH, t.shape[0] // N_MICROBATCH) + t.shape[1:])
    return _jnp.moveaxis(t, 1, axis + 1)


def setup_inputs(seed: int = 0) -> dict:
    inp = _fwd_setup_inputs(seed)
    key = _jax.random.fold_in(_jax.random.key(seed), 7919)
    shape, _ = _output_shape()
    out = dict(inp)
    out["loss_target"] = _jax.random.normal(_jax.random.fold_in(key, 0), shape, _jnp.float32)
    for i, name in enumerate(TWIN_WEIGHTS):
        w = inp[name].astype(_jnp.float32)
        if MOMENT_SCALE is None:
            s = _jnp.sqrt(_jnp.mean(_jnp.square(w)) + 1e-30)
        else:
            s = MOMENT_SCALE[name]
        km, kv = _jax.random.split(_jax.random.fold_in(key, i + 1))
        out[name] = w
        out["m_" + name] = s * _jax.random.normal(km, w.shape, _jnp.float32)
        out["v_" + name] = (s * s) * _jax.random.uniform(kv, w.shape, _jnp.float32, 0.5, 1.5)
    if N_MICROBATCH > 1:
        for name, axis in PER_EXAMPLE_BATCH_AXIS.items():
            out[name] = _to_microbatches(out[name], axis)
    return {'x': out['x'], 'attn_norm_g': out['attn_norm_g'], 'ffn_norm_g': out['ffn_norm_g'], 'ab_w_in': out['ab_w_in'], 'ab_w_out': out['ab_w_out'], 'ab_rel_bias': out['ab_rel_bias'], 'c_w_in': out['c_w_in'], 'c_ln_g': out['c_ln_g'], 'c_ln_b': out['c_ln_b'], 'c_w_s': out['c_w_s'], 'c_b_s': out['c_b_s'], 'c_w_out': out['c_w_out'], 'ffn_w_up': out['ffn_w_up'], 'ffn_conv_w': out['ffn_conv_w'], 'ffn_conv_b': out['ffn_conv_b'], 'ffn_w_down': out['ffn_w_down'], 'final_norm_g': out['final_norm_g'], 'loss_target': out['loss_target'], 'm_attn_norm_g': out['m_attn_norm_g'], 'm_ffn_norm_g': out['m_ffn_norm_g'], 'm_ab_w_in': out['m_ab_w_in'], 'm_ab_w_out': out['m_ab_w_out'], 'm_ab_rel_bias': out['m_ab_rel_bias'], 'm_c_w_in': out['m_c_w_in'], 'm_c_ln_g': out['m_c_ln_g'], 'm_c_ln_b': out['m_c_ln_b'], 'm_c_w_s': out['m_c_w_s'], 'm_c_b_s': out['m_c_b_s'], 'm_c_w_out': out['m_c_w_out'], 'm_ffn_w_up': out['m_ffn_w_up'], 'm_ffn_conv_w': out['m_ffn_conv_w'], 'm_ffn_conv_b': out['m_ffn_conv_b'], 'm_ffn_w_down': out['m_ffn_w_down'], 'm_final_norm_g': out['m_final_norm_g'], 'v_attn_norm_g': out['v_attn_norm_g'], 'v_ffn_norm_g': out['v_ffn_norm_g'], 'v_ab_w_in': out['v_ab_w_in'], 'v_ab_w_out': out['v_ab_w_out'], 'v_ab_rel_bias': out['v_ab_rel_bias'], 'v_c_w_in': out['v_c_w_in'], 'v_c_ln_g': out['v_c_ln_g'], 'v_c_ln_b': out['v_c_ln_b'], 'v_c_w_s': out['v_c_w_s'], 'v_c_b_s': out['v_c_b_s'], 'v_c_w_out': out['v_c_w_out'], 'v_ffn_w_up': out['v_ffn_w_up'], 'v_ffn_conv_w': out['v_ffn_conv_w'], 'v_ffn_conv_b': out['v_ffn_conv_b'], 'v_ffn_w_down': out['v_ffn_w_down'], 'v_final_norm_g': out['v_final_norm_g']}


def _loss(weights, diff, rest, loss_target):
    with _jax.named_scope("forward"):
        args = {**rest, TWIN_DIFF_INPUT: diff, **{k: w.astype(_WEIGHT_DTYPES[k]) for k, w in weights.items()}}
        y = _forward(args)
    with _jax.named_scope("loss_head"):
        err = _jnp.square(y.astype(_jnp.float32) - loss_target)
        return 0.5 * _jnp.sum(_jnp.mean(err, axis=-1)) if err.ndim else 0.5 * err


def _adamw(w, g, m, v):
    m = ADAM_B1 * m + (1.0 - ADAM_B1) * g
    v = ADAM_B2 * v + (1.0 - ADAM_B2) * _jnp.square(g)
    m_hat = m / (1.0 - ADAM_B1 ** ADAM_STEP)
    v_hat = v / (1.0 - ADAM_B2 ** ADAM_STEP)
    delta = -ADAM_LR * (m_hat / (_jnp.sqrt(v_hat) + ADAM_EPS) + ADAM_WD * w)
    return delta, m, v


def reference(x, attn_norm_g, ffn_norm_g, ab_w_in, ab_w_out, ab_rel_bias, c_w_in, c_ln_g, c_ln_b, c_w_s, c_b_s, c_w_out, ffn_w_up, ffn_conv_w, ffn_conv_b, ffn_w_down, final_norm_g, loss_target, m_attn_norm_g, m_ffn_norm_g, m_ab_w_in, m_ab_w_out, m_ab_rel_bias, m_c_w_in, m_c_ln_g, m_c_ln_b, m_c_w_s, m_c_b_s, m_c_w_out, m_ffn_w_up, m_ffn_conv_w, m_ffn_conv_b, m_ffn_w_down, m_final_norm_g, v_attn_norm_g, v_ffn_norm_g, v_ab_w_in, v_ab_w_out, v_ab_rel_bias, v_c_w_in, v_c_ln_g, v_c_ln_b, v_c_w_s, v_c_b_s, v_c_w_out, v_ffn_w_up, v_ffn_conv_w, v_ffn_conv_b, v_ffn_w_down, v_final_norm_g):
    given = dict(x=x, attn_norm_g=attn_norm_g, ffn_norm_g=ffn_norm_g, ab_w_in=ab_w_in, ab_w_out=ab_w_out, ab_rel_bias=ab_rel_bias, c_w_in=c_w_in, c_ln_g=c_ln_g, c_ln_b=c_ln_b, c_w_s=c_w_s, c_b_s=c_b_s, c_w_out=c_w_out, ffn_w_up=ffn_w_up, ffn_conv_w=ffn_conv_w, ffn_conv_b=ffn_conv_b, ffn_w_down=ffn_w_down, final_norm_g=final_norm_g, loss_target=loss_target, m_attn_norm_g=m_attn_norm_g, m_ffn_norm_g=m_ffn_norm_g, m_ab_w_in=m_ab_w_in, m_ab_w_out=m_ab_w_out, m_ab_rel_bias=m_ab_rel_bias, m_c_w_in=m_c_w_in, m_c_ln_g=m_c_ln_g, m_c_ln_b=m_c_ln_b, m_c_w_s=m_c_w_s, m_c_b_s=m_c_b_s, m_c_w_out=m_c_w_out, m_ffn_w_up=m_ffn_w_up, m_ffn_conv_w=m_ffn_conv_w, m_ffn_conv_b=m_ffn_conv_b, m_ffn_w_down=m_ffn_w_down, m_final_norm_g=m_final_norm_g, v_attn_norm_g=v_attn_norm_g, v_ffn_norm_g=v_ffn_norm_g, v_ab_w_in=v_ab_w_in, v_ab_w_out=v_ab_w_out, v_ab_rel_bias=v_ab_rel_bias, v_c_w_in=v_c_w_in, v_c_ln_g=v_c_ln_g, v_c_ln_b=v_c_ln_b, v_c_w_s=v_c_w_s, v_c_b_s=v_c_b_s, v_c_w_out=v_c_w_out, v_ffn_w_up=v_ffn_w_up, v_ffn_conv_w=v_ffn_conv_w, v_ffn_conv_b=v_ffn_conv_b, v_ffn_w_down=v_ffn_w_down, v_final_norm_g=v_final_norm_g)
    weights = {n: given[n] for n in TWIN_WEIGHTS}
    shared = {n: given[n] for n in SHARED_INPUTS}
    per_example = {n: given[n] for n in ['x']}
    grad_fn = _jax.value_and_grad(_loss, argnums=(0, 1))

    def one_microbatch(ex, loss_target):
        ex = dict(ex)
        diff = ex.pop(TWIN_DIFF_INPUT)
        return grad_fn(weights, diff, {**shared, **ex}, loss_target)

    if N_MICROBATCH == 1:
        loss, (grad_w, grad_x) = one_microbatch(per_example, given["loss_target"])
    else:
        def body(carry, xs):
            loss_sum, grad_sum = carry
            l_k, (gw_k, gx_k) = one_microbatch(xs[0], xs[1])
            with _jax.named_scope("update"):
                return (loss_sum + l_k, _jax.tree.map(_jnp.add, grad_sum, gw_k)), gx_k

        init = (_jnp.zeros((), _jnp.float32), _jax.tree.map(_jnp.zeros_like, weights))
        (loss, grad_w), grad_x = _jax.lax.scan(body, init, (per_example, given["loss_target"]))
    with _jax.named_scope("update"):
        delta_w, new_m, new_v = {}, {}, {}
        for n in TWIN_WEIGHTS:
            delta_w[n], new_m[n], new_v[n] = _adamw(weights[n], grad_w[n], given["m_" + n], given["v_" + n])
    return (loss, grad_x, *[grad_w[n] for n in TWIN_WEIGHTS], *[delta_w[n] for n in TWIN_WEIGHTS],
            *[new_m[n] for n in TWIN_WEIGHTS], *[new_v[n] for n in TWIN_WEIGHTS])
```

```python
import functools
import math

import jax
import jax.numpy as jnp
from jax import lax
from jax.experimental import pallas as pl
from jax.experimental.pallas import tpu as pltpu

F32 = jnp.float32
BF16 = jnp.bfloat16
S = jax.ShapeDtypeStruct
MESH = pl.DeviceIdType.MESH

D_MODEL = 1024
CHUNK = 64
EPS = 1e-6
NEG_INF = -1e30
RET_HEADS, RET_QK, RET_V = 4, 128, 256
ATT_HEADS, ATT_D, ATT_PAST, MAX_REL = 8, 64, 8, 128
BAND = (ATT_PAST + 1) * CHUNK
PADK = ATT_PAST * CHUNK
SGU_BLOCK, SGU_GROUPS, SGU_WIDTH = 128, 8, 2048
SGU_GW = SGU_WIDTH // SGU_GROUPS
FFN_HIDDEN = 2816
N_REL = 2 * MAX_REL + 1
RET_SCALE = RET_QK ** -0.5
ATT_SCALE = ATT_D ** -0.5
ADAM_LR, ADAM_B1, ADAM_B2, ADAM_EPS, ADAM_WD, ADAM_STEP = 0.001, 0.9, 0.999, 1e-08, 0.01, 10

V7X_VMEM_BYTES = 64 * 1024 * 1024
VMEM_LIMIT = V7X_VMEM_BYTES * 7 // 8
LANES = 128
PACK_W = 1024
N_CHIPS = 4
N_DEV = 8

GELU_C = math.sqrt(2.0 / math.pi)
GELU_A = 0.044715


def _cparams(*sem):
    return pltpu.CompilerParams(dimension_semantics=tuple(sem) if sem else None, vmem_limit_bytes=VMEM_LIMIT)


def _tile(n, target, unit=LANES):
    best = None
    for t in range(unit, min(n, target) + 1, unit):
        if n % t == 0:
            best = t
    return best if best is not None else n


def _gelu(x):
    t = jnp.tanh(GELU_C * (x + GELU_A * x * x * x))
    return 0.5 * x * (1.0 + t)


def _gelu_and_grad(x):
    x2 = x * x
    t = jnp.tanh(GELU_C * (x + GELU_A * x2 * x))
    g = 0.5 * x * (1.0 + t)
    dg = 0.5 * (1.0 + t) + 0.5 * x * (1.0 - t * t) * (GELU_C * (1.0 + 3.0 * GELU_A * x2))
    return g, dg


def _sigmoid(x):
    return 1.0 / (1.0 + jnp.exp(-x))


def _dot(a, b):
    return jnp.dot(a, b, preferred_element_type=F32)


def _dot_nt(a, b):
    return lax.dot_general(a, b, (((1,), (1,)), ((), ())), preferred_element_type=F32)


def _dot_tn(a, b):
    return lax.dot_general(a, b, (((0,), (0,)), ((), ())), preferred_element_type=F32)


def _rmsnorm_fwd(x, g, name):
    T, D = x.shape
    tr = _tile(T, 512, 8)

    def body(x_ref, g_ref, o_ref):
        xv = x_ref[...]
        r = lax.rsqrt(jnp.mean(xv * xv, axis=-1, keepdims=True) + EPS)
        o_ref[...] = (xv * r * g_ref[...]).astype(o_ref.dtype)

    return pl.pallas_call(
        body, grid=(T // tr,),
        in_specs=[pl.BlockSpec((tr, D), lambda i: (i, 0)), pl.BlockSpec((1, D), lambda i: (0, 0))],
        out_specs=pl.BlockSpec((tr, D), lambda i: (i, 0)),
        out_shape=S((T, D), BF16), name=name, compiler_params=_cparams("parallel"))(x, g)


def _rmsnorm_bwd(x, dy, g, dres, name):
    T, D = x.shape
    tr = _tile(T, 512, 8)

    def body(x_ref, dy_ref, g_ref, dres_ref, dx_ref, dxb_ref, dg_ref):
        @pl.when(pl.program_id(0) == 0)
        def _():
            dg_ref[...] = jnp.zeros_like(dg_ref)

        xv = x_ref[...]
        r = lax.rsqrt(jnp.mean(xv * xv, axis=-1, keepdims=True) + EPS)
        xh = xv * r
        dyv = dy_ref[...]
        dg_ref[...] += jnp.sum(dyv * xh, axis=0, keepdims=True)
        dxh = dyv * g_ref[...]
        dx = dres_ref[...] + r * (dxh - xh * jnp.mean(dxh * xh, axis=-1, keepdims=True))
        dx_ref[...] = dx
        dxb_ref[...] = dx.astype(BF16)

    row = pl.BlockSpec((tr, D), lambda i: (i, 0))
    vec = pl.BlockSpec((1, D), lambda i: (0, 0))
    return pl.pallas_call(
        body, grid=(T // tr,), in_specs=[row, row, vec, row], out_specs=[row, row, vec],
        out_shape=[S((T, D), F32), S((T, D), BF16), S((1, D), F32)], name=name,
        compiler_params=_cparams("arbitrary"))(x, dy, g, dres)


def _mm(a, b, name, res=None, out_dtype=F32):
    M, K = a.shape
    N = b.shape[1]
    tm, tn, tk = _tile(M, 1024, 8), _tile(N, 1408), _tile(K, 1536)
    nk = K // tk

    def body(*refs):
        if res is not None:
            a_ref, b_ref, r_ref, o_ref = refs[:4]
        else:
            a_ref, b_ref, o_ref = refs[:3]
            r_ref = None
        p = _dot(a_ref[...], b_ref[...])

        def finish(v):
            if r_ref is not None:
                v = v + r_ref[...]
            o_ref[...] = v.astype(o_ref.dtype)

        if nk == 1:
            finish(p)
        else:
            acc = refs[-1]
            k = pl.program_id(2)

            @pl.when(k == 0)
            def _():
                acc[...] = p

            @pl.when(k > 0)
            def _():
                acc[...] += p

            @pl.when(k == nk - 1)
            def _():
                finish(acc[...])

    in_specs = [pl.BlockSpec((tm, tk), lambda i, j, k: (i, k)), pl.BlockSpec((tk, tn), lambda i, j, k: (k, j))]
    args = [a, b]
    if res is not None:
        in_specs.append(pl.BlockSpec((tm, tn), lambda i, j, k: (i, j)))
        args.append(res)
    return pl.pallas_call(
        body, grid=(M // tm, N // tn, nk), in_specs=in_specs,
        out_specs=pl.BlockSpec((tm, tn), lambda i, j, k: (i, j)),
        out_shape=S((M, N), out_dtype),
        scratch_shapes=[pltpu.VMEM((tm, tn), F32)] if nk > 1 else [],
        name=name, compiler_params=_cparams("parallel", "parallel", "arbitrary"))(*args)


def _mm_tn(a, g, name):
    T, Ka = a.shape
    N = g.shape[1]
    tm, tn, tt = _tile(Ka, 1408), _tile(N, 1408), _tile(T, 1024, 8)

    def body(a_ref, g_ref, o_ref):
        @pl.when(pl.program_id(2) == 0)
        def _():
            o_ref[...] = jnp.zeros_like(o_ref)

        o_ref[...] += _dot_tn(a_ref[...], g_ref[...])

    return pl.pallas_call(
        body, grid=(Ka // tm, N // tn, T // tt),
        in_specs=[pl.BlockSpec((tt, tm), lambda i, j, k: (k, i)), pl.BlockSpec((tt, tn), lambda i, j, k: (k, j))],
        out_specs=pl.BlockSpec((tm, tn), lambda i, j, k: (i, j)),
        out_shape=S((Ka, N), F32), name=name,
        compiler_params=_cparams("parallel", "parallel", "arbitrary"))(a, g)


def _rotate(x, c2, s2):
    return x * c2 + pltpu.roll(x, RET_QK // 2, 1) * s2


def _unrotate(d, c2, s2):
    return d * c2 - pltpu.roll(d, RET_QK // 2, 1) * s2


def _ret_specs(RB, blockmap):
    q = pl.BlockSpec((RB, RET_QK), lambda h, n: (blockmap(n), h))
    k = pl.BlockSpec((RB, RET_QK), lambda h, n: (blockmap(n), RET_HEADS + h))
    v = pl.BlockSpec((RB, RET_V), lambda h, n: (blockmap(n), RET_HEADS + h))
    g = pl.BlockSpec((RB, RET_V), lambda h, n: (blockmap(n), 2 * RET_HEADS + h))
    tab = pl.BlockSpec((RB, RET_QK), lambda h, n: (blockmap(n), 0))
    return q, k, v, g, tab


def _ret_decay_specs():
    return [pl.BlockSpec((None, CHUNK, CHUNK), lambda h, n: (h, 0, 0)),
            pl.BlockSpec((None, CHUNK, RET_QK), lambda h, n: (h, 0, 0)),
            pl.BlockSpec((None, CHUNK, RET_QK), lambda h, n: (h, 0, 0)),
            pl.BlockSpec((None, 1, RET_V), lambda h, n: (h, 0, 0))]


def _retention_fwd(z, c2, s2, dintra, qdec, kdec, cdec):
    T = z.shape[0]
    RB = min(512, T)
    nch, nb = RB // CHUNK, T // RB

    def body(q_ref, k_ref, v_ref, g_ref, c2_ref, s2_ref, di_ref, qd_ref, kd_ref, cd_ref, ya_ref, r_ref, st_ref, state):
        @pl.when(pl.program_id(1) == 0)
        def _():
            state[...] = jnp.zeros_like(state)

        dmat, qdv, kdv, cdv = di_ref[...], qd_ref[...], kd_ref[...], cd_ref[...]
        for c in range(nch):
            rows = slice(c * CHUNK, (c + 1) * CHUNK)
            c2v, s2v = c2_ref[rows, :], s2_ref[rows, :]
            qr = _rotate(q_ref[rows, :], c2v, s2v)
            kr = _rotate(k_ref[rows, :], c2v, s2v) * RET_SCALE
            vb = v_ref[rows, :].astype(BF16)
            sm = _dot_nt(qr.astype(BF16), kr.astype(BF16)) * dmat
            sb = state[...].astype(BF16)
            st_ref[c] = sb
            o = _dot(sm.astype(BF16), vb) + _dot((qr * qdv).astype(BF16), sb)
            state[...] = state[...] * cdv + _dot_tn((kr * kdv).astype(BF16), vb)
            r_ref[rows, :] = o
            mu = jnp.mean(o, axis=-1, keepdims=True)
            oc = o - mu
            rn = oc * lax.rsqrt(jnp.mean(oc * oc, axis=-1, keepdims=True) + EPS)
            gv = g_ref[rows, :]
            ya_ref[rows, :] = (gv * _sigmoid(gv) * rn).astype(BF16)

    q, k, v, g, tab = _ret_specs(RB, lambda n: n)
    wide = pl.BlockSpec((RB, RET_V), lambda h, n: (n, h))
    return pl.pallas_call(
        body, grid=(RET_HEADS, nb),
        in_specs=[q, k, v, g, tab, tab] + _ret_decay_specs(),
        out_specs=[wide, wide, pl.BlockSpec((None, nch, RET_QK, RET_V), lambda h, n: (h, n, 0, 0))],
        out_shape=[S((T, RET_HEADS * RET_V), BF16), S((T, RET_HEADS * RET_V), F32),
                   S((RET_HEADS, T // CHUNK, RET_QK, RET_V), BF16)],
        scratch_shapes=[pltpu.VMEM((RET_QK, RET_V), F32)], name="retention_fwd",
        compiler_params=_cparams("parallel", "arbitrary"))(z, z, z, z, c2, s2, dintra, qdec, kdec, cdec)


def _retention_bwd(z, c2, s2, dintra, qdec, kdec, cdec, r, dycat, st):
    T = z.shape[0]
    RB = min(512, T)
    nch, nb = RB // CHUNK, T // RB

    def body(q_ref, k_ref, v_ref, g_ref, c2_ref, s2_ref, di_ref, qd_ref, kd_ref, cd_ref, r_ref, dy_ref, st_ref,
             dq_ref, dk_ref, dv_ref, dg_ref, dstate):
        @pl.when(pl.program_id(1) == 0)
        def _():
            dstate[...] = jnp.zeros_like(dstate)

        dmat, qdv, kdv, cdv = di_ref[...], qd_ref[...], kd_ref[...], cd_ref[...]
        for c in reversed(range(nch)):
            rows = slice(c * CHUNK, (c + 1) * CHUNK)
            c2v, s2v = c2_ref[rows, :], s2_ref[rows, :]
            qr = _rotate(q_ref[rows, :], c2v, s2v)
            kr = _rotate(k_ref[rows, :], c2v, s2v) * RET_SCALE
            qb, kb = qr.astype(BF16), kr.astype(BF16)
            vb = v_ref[rows, :].astype(BF16)
            o, gv, dy = r_ref[rows, :], g_ref[rows, :], dy_ref[rows, :]
            mu = jnp.mean(o, axis=-1, keepdims=True)
            oc = o - mu
            rstd = lax.rsqrt(jnp.mean(oc * oc, axis=-1, keepdims=True) + EPS)
            rn = oc * rstd
            sg = _sigmoid(gv)
            dg_ref[rows, :] = (dy * rn * (sg * (1.0 + gv * (1.0 - sg)))).astype(BF16)
            drn = dy * (gv * sg)
            do = rstd * (drn - jnp.mean(drn, axis=-1, keepdims=True) - rn * jnp.mean(drn * rn, axis=-1, keepdims=True))
            dob = do.astype(BF16)
            sm = (_dot_nt(qb, kb) * dmat).astype(BF16)
            kdb = (kr * kdv).astype(BF16)
            dsb = dstate[...].astype(BF16)
            dv_ref[rows, :] = (_dot_tn(sm, dob) + _dot(kdb, dsb)).astype(BF16)
            ds = (_dot_nt(dob, vb) * dmat).astype(BF16)
            dqr = _dot(ds, kb) + _dot_nt(dob, st_ref[c]) * qdv
            dkr = (_dot_tn(ds, qb) + _dot_nt(vb, dsb) * kdv) * RET_SCALE
            dstate[...] = dstate[...] * cdv + _dot_tn((qr * qdv).astype(BF16), dob)
            dq_ref[rows, :] = _unrotate(dqr, c2v, s2v).astype(BF16)
            dk_ref[rows, :] = _unrotate(dkr, c2v, s2v).astype(BF16)

    rev = lambda n: nb - 1 - n
    q, k, v, g, tab = _ret_specs(RB, rev)
    wide = pl.BlockSpec((RB, RET_V), lambda h, n: (rev(n), h))
    narrow = pl.BlockSpec((RB, RET_QK), lambda h, n: (rev(n), h))
    return pl.pallas_call(
        body, grid=(RET_HEADS, nb),
        in_specs=[q, k, v, g, tab, tab] + _ret_decay_specs() + [
            wide, wide, pl.BlockSpec((None, nch, RET_QK, RET_V), lambda h, n: (h, rev(n), 0, 0))],
        out_specs=[narrow, narrow, wide, wide],
        out_shape=[S((T, RET_HEADS * RET_QK), BF16), S((T, RET_HEADS * RET_QK), BF16),
                   S((T, RET_HEADS * RET_V), BF16), S((T, RET_HEADS * RET_V), BF16)],
        scratch_shapes=[pltpu.VMEM((RET_QK, RET_V), F32)], name="retention_bwd",
        compiler_params=_cparams("parallel", "arbitrary"))(z, z, z, z, c2, s2, dintra, qdec, kdec, cdec, r, dycat, st)


def _rel_index(i):
    r = lax.broadcasted_iota(jnp.int32, (3 * LANES, 5 * LANES), 0)
    j = lax.broadcasted_iota(jnp.int32, (3 * LANES, 5 * LANES), 1)
    idx = jnp.clip(i + PADK - j, -MAX_REL, MAX_REL) + MAX_REL
    return (r == idx).astype(BF16)


def _split3(v):
    hi = v.astype(BF16)
    r1 = v - hi.astype(F32)
    mid = r1.astype(BF16)
    lo = (r1 - mid.astype(F32)).astype(BF16)
    return hi, mid, lo


def _bias_build(rb):
    rbp = jnp.pad(rb, ((0, 0), (0, 3 * LANES - N_REL)))

    def body(rb_ref, o_ref):
        e = _rel_index(pl.program_id(0))
        hi, mid, lo = _split3(rb_ref[...])
        o_ref[...] = _dot(hi, e) + _dot(mid, e) + _dot(lo, e)

    return pl.pallas_call(
        body, grid=(CHUNK,), in_specs=[pl.BlockSpec((ATT_HEADS, 3 * LANES), lambda i: (0, 0))],
        out_specs=pl.BlockSpec((None, ATT_HEADS, 5 * LANES), lambda i: (i, 0, 0)),
        out_shape=S((CHUNK, ATT_HEADS, 5 * LANES), F32), name="bias_build",
        compiler_params=_cparams("parallel"))(rbp)


def _bias_grad(dbt):
    def body(d_ref, o_ref):
        @pl.when(pl.program_id(0) == 0)
        def _():
            o_ref[...] = jnp.zeros_like(o_ref)

        e = _rel_index(pl.program_id(0))
        hi, mid, lo = _split3(d_ref[...])
        o_ref[...] += _dot_nt(hi, e) + _dot_nt(mid, e) + _dot_nt(lo, e)

    return pl.pallas_call(
        body, grid=(CHUNK,), in_specs=[pl.BlockSpec((None, ATT_HEADS, 5 * LANES), lambda i: (i, 0, 0))],
        out_specs=pl.BlockSpec((ATT_HEADS, 3 * LANES), lambda i: (0, 0)),
        out_shape=S((ATT_HEADS, 3 * LANES), F32), name="bias_grad",
        compiler_params=_cparams("arbitrary"))(dbt)


ATT_RB = 512


def _att_probs(qm, kband, bias, chunk_id):
    s = _dot_nt(qm, kband) * ATT_SCALE + bias
    col = lax.broadcasted_iota(jnp.int32, (CHUNK, BAND), 1)
    s = jnp.where(col + chunk_id * CHUNK >= PADK, s, NEG_INF)
    p = jnp.exp(s - jnp.max(s, axis=-1, keepdims=True))
    return p / jnp.sum(p, axis=-1, keepdims=True)


def _attention_fwd(z, kpad, vpad, bias):
    T = z.shape[0]
    nq = T // ATT_RB
    nch = ATT_RB // CHUNK
    qcol = (2 * RET_HEADS * RET_QK + 2 * RET_HEADS * RET_V) // LANES

    def body(q_ref, klo, khi, vlo, vhi, b_ref, o_ref, kw, vw):
        kw[0:ATT_RB, :] = klo[...]
        kw[ATT_RB:, :] = khi[...]
        vw[0:ATT_RB, :] = vlo[...]
        vw[ATT_RB:, :] = vhi[...]
        lane = lax.broadcasted_iota(jnp.int32, (CHUNK, LANES), 1)
        n = pl.program_id(1)
        for c in range(nch):
            rows = slice(c * CHUNK, (c + 1) * CHUNK)
            qc = q_ref[rows, :]
            kband = kw[c * CHUNK:c * CHUNK + BAND, :]
            vband = vw[c * CHUNK:c * CHUNK + BAND, :]
            outs = []
            for e in range(2):
                qm = jnp.where((lane >= ATT_D) == (e == 1), qc, 0.0).astype(BF16)
                p = _att_probs(qm, kband, b_ref[e], n * nch + c)
                outs.append(_dot(p.astype(BF16), vband))
            o_ref[rows, :] = jnp.where(lane < ATT_D, outs[0], outs[1]).astype(BF16)

    kv = lambda off: pl.BlockSpec((ATT_RB, LANES), lambda hp, n: (n + off, hp))
    return pl.pallas_call(
        body, grid=(ATT_HEADS // 2, nq),
        in_specs=[pl.BlockSpec((ATT_RB, LANES), lambda hp, n: (n, qcol + hp)), kv(0), kv(1), kv(0), kv(1),
                  pl.BlockSpec((2, CHUNK, BAND), lambda hp, n: (hp, 0, 0))],
        out_specs=pl.BlockSpec((ATT_RB, LANES), lambda hp, n: (n, hp)),
        out_shape=S((T, ATT_HEADS * ATT_D), BF16),
        scratch_shapes=[pltpu.VMEM((2 * ATT_RB, LANES), BF16), pltpu.VMEM((2 * ATT_RB, LANES), BF16)],
        name="attention_fwd", compiler_params=_cparams("parallel", "parallel"))(z, kpad, kpad, vpad, vpad, bias)


def _attention_bwd(z, kpad, vpad, bias, dycat):
    T = z.shape[0]
    nq = T // ATT_RB
    nch = ATT_RB // CHUNK
    qcol = (2 * RET_HEADS * RET_QK + 2 * RET_HEADS * RET_V) // LANES
    dycol = RET_HEADS * RET_V // LANES

    def body(q_ref, klo, khi, vlo, vhi, b_ref, dy_ref, dq_ref, dk_ref, dv_ref, db_ref, kw, vw, dkw, dvw):
        n = pl.program_id(1)

        @pl.when(n == 0)
        def _():
            dkw[...] = jnp.zeros_like(dkw)
            dvw[...] = jnp.zeros_like(dvw)
            db_ref[...] = jnp.zeros_like(db_ref)

        @pl.when(n > 0)
        def _():
            dkw[0:ATT_RB, :] = dkw[ATT_RB:, :]
            dvw[0:ATT_RB, :] = dvw[ATT_RB:, :]
            dkw[ATT_RB:, :] = jnp.zeros((ATT_RB, LANES), F32)
            dvw[ATT_RB:, :] = jnp.zeros((ATT_RB, LANES), F32)

        @pl.when(n < nq)
        def _():
            kw[0:ATT_RB, :] = klo[...]
            kw[ATT_RB:, :] = khi[...]
            vw[0:ATT_RB, :] = vlo[...]
            vw[ATT_RB:, :] = vhi[...]
            lane = lax.broadcasted_iota(jnp.int32, (CHUNK, LANES), 1)
            for c in range(nch):
                rows = slice(c * CHUNK, (c + 1) * CHUNK)
                band = slice(c * CHUNK, c * CHUNK + BAND)
                qc, dyc = q_ref[rows, :], dy_ref[rows, :]
                kband, vband = kw[band, :], vw[band, :]
                dq = jnp.zeros((CHUNK, LANES), F32)
                for e in range(2):
                    mine = (lane >= ATT_D) == (e == 1)
                    qm = jnp.where(mine, qc, 0.0).astype(BF16)
                    dom = jnp.where(mine, dyc, 0.0).astype(BF16)
                    p = _att_probs(qm, kband, b_ref[e], n * nch + c)
                    dp = _dot_nt(dom, vband)
                    ds = p * (dp - jnp.sum(dp * p, axis=-1, keepdims=True))
                    db_ref[e] += ds
                    dsb = (ds * ATT_SCALE).astype(BF16)
                    dq = dq + jnp.where(mine, _dot(dsb, kband), 0.0)
                    dkw[band, :] += _dot_tn(dsb, qm)
                    dvw[band, :] += _dot_tn(p.astype(BF16), dom)
                dq_ref[rows, :] = dq.astype(BF16)

        dk_ref[...] = dkw[0:ATT_RB, :].astype(BF16)
        dv_ref[...] = dvw[0:ATT_RB, :].astype(BF16)

    qn = lambda n: jnp.minimum(n, nq - 1)
    kv = lambda off: pl.BlockSpec((ATT_RB, LANES), lambda hp, n: (qn(n) + off, hp))
    out_kv = pl.BlockSpec((ATT_RB, LANES), lambda hp, n: (n, hp))
    return pl.pallas_call(
        body, grid=(ATT_HEADS // 2, nq + 1),
        in_specs=[pl.BlockSpec((ATT_RB, LANES), lambda hp, n: (qn(n), qcol + hp)), kv(0), kv(1), kv(0), kv(1),
                  pl.BlockSpec((2, CHUNK, BAND), lambda hp, n: (hp, 0, 0)),
                  pl.BlockSpec((ATT_RB, LANES), lambda hp, n: (qn(n), dycol + hp))],
        out_specs=[pl.BlockSpec((ATT_RB, LANES), lambda hp, n: (qn(n), hp)), out_kv, out_kv,
                   pl.BlockSpec((2, CHUNK, BAND), lambda hp, n: (hp, 0, 0))],
        out_shape=[S((T, ATT_HEADS * ATT_D), BF16), S((T + PADK, ATT_HEADS * ATT_D), BF16),
                   S((T + PADK, ATT_HEADS * ATT_D), BF16), S((ATT_HEADS, CHUNK, BAND), F32)],
        scratch_shapes=[pltpu.VMEM((2 * ATT_RB, LANES), BF16), pltpu.VMEM((2 * ATT_RB, LANES), BF16),
                        pltpu.VMEM((2 * ATT_RB, LANES), F32), pltpu.VMEM((2 * ATT_RB, LANES), F32)],
        name="attention_bwd", compiler_params=_cparams("parallel", "arbitrary"))(z, kpad, kpad, vpad, vpad, bias, dycat)


HALO = 8


def _conv_specs(tb, tc, nct, T):
    per = tb // HALO
    last = T // HALO - 1

    def at(half):
        off = half * nct
        return [pl.BlockSpec((HALO, tc), lambda j, i: (jnp.maximum(i * per - 1, 0), j + off)),
                pl.BlockSpec((tb, tc), lambda j, i: (i, j + off)),
                pl.BlockSpec((HALO, tc), lambda j, i: (jnp.minimum((i + 1) * per, last), j + off))]

    return at(0), at(1)


def _causal_conv(ext, w_ref, b_ref):
    zc = w_ref[0:1, :] * pltpu.roll(ext, 2, 0) + w_ref[1:2, :] * pltpu.roll(ext, 1, 0) + w_ref[2:3, :] * ext + b_ref[...]
    return zc[HALO:]


def _convglu_fwd(z, cw, cb, name):
    T = z.shape[0]
    tb, tc = _tile(T, 1024, 8), 256
    nct = FFN_HIDDEN // tc

    def body(gp_ref, g_ref, up_ref, u_ref, wg_ref, wu_ref, bg_ref, bu_ref, o_ref):
        first = pl.program_id(1) == 0

        def conv(p_ref, blk_ref, w_ref, b_ref):
            prev = jnp.where(first, 0.0, p_ref[...])
            return _causal_conv(jnp.concatenate([prev, blk_ref[...]], axis=0), w_ref, b_ref)

        o_ref[...] = (_gelu(conv(gp_ref, g_ref, wg_ref, bg_ref)) * conv(up_ref, u_ref, wu_ref, bu_ref)).astype(BF16)

    (gp, gb, _), (up, ub, _) = _conv_specs(tb, tc, nct, T)
    wspec = lambda off, rows: pl.BlockSpec((rows, tc), lambda j, i: (0, j + off))
    return pl.pallas_call(
        body, grid=(nct, T // tb),
        in_specs=[gp, gb, up, ub, wspec(0, 3), wspec(nct, 3), wspec(0, 1), wspec(nct, 1)],
        out_specs=pl.BlockSpec((tb, tc), lambda j, i: (i, j)),
        out_shape=S((T, FFN_HIDDEN), BF16), name=name,
        compiler_params=_cparams("parallel", "parallel"))(z, z, z, z, cw, cw, cb, cb)


def _convglu_bwd(z, df, cw, cb, name):
    T = z.shape[0]
    tb, tc = _tile(T, 1024, 8), 256
    nct = FFN_HIDDEN // tc
    nrb = T // tb

    def body(gp_ref, g_ref, gn_ref, up_ref, u_ref, un_ref, df_ref, dfn_ref, wg_ref, wu_ref, bg_ref, bu_ref,
             dzg_ref, dzu_ref, dwg_ref, dwu_ref, dbg_ref, dbu_ref):
        i = pl.program_id(1)
        first, last = i == 0, i == nrb - 1

        @pl.when(first)
        def _():
            for ref in (dwg_ref, dwu_ref, dbg_ref, dbu_ref):
                ref[...] = jnp.zeros_like(ref)

        def ext_of(p_ref, blk_ref, n_ref):
            return jnp.concatenate([jnp.where(first, 0.0, p_ref[...]), blk_ref[...], n_ref[...]], axis=0)

        gext, uext = ext_of(gp_ref, g_ref, gn_ref), ext_of(up_ref, u_ref, un_ref)
        gc, uc = _causal_conv(gext, wg_ref, bg_ref), _causal_conv(uext, wu_ref, bu_ref)
        dfe = jnp.concatenate([df_ref[...], jnp.where(last, 0.0, dfn_ref[...])], axis=0)
        ge, gd = _gelu_and_grad(gc)
        dgc, duc = dfe * uc * gd, dfe * ge
        n = tb + HALO

        def back(d, ext, w_ref, dz_ref, dw_ref, db_ref):
            dz = w_ref[2:3, :] * d + w_ref[1:2, :] * pltpu.roll(d, n - 1, 0) + w_ref[0:1, :] * pltpu.roll(d, n - 2, 0)
            dz_ref[...] = dz[:tb].astype(BF16)
            dblk = d[:tb]
            db_ref[...] += jnp.sum(dblk, axis=0, keepdims=True)
            taps = [pltpu.roll(ext, 2, 0)[HALO:HALO + tb], pltpu.roll(ext, 1, 0)[HALO:HALO + tb], ext[HALO:HALO + tb]]
            for k, t in enumerate(taps):
                dw_ref[k:k + 1, :] += jnp.sum(dblk * t, axis=0, keepdims=True)

        back(dgc, gext, wg_ref, dzg_ref, dwg_ref, dbg_ref)
        back(duc, uext, wu_ref, dzu_ref, dwu_ref, dbu_ref)

    gspecs, uspecs = _conv_specs(tb, tc, nct, T)
    per = tb // HALO
    dfs = [pl.BlockSpec((tb, tc), lambda j, i: (i, j)),
           pl.BlockSpec((HALO, tc), lambda j, i: (jnp.minimum((i + 1) * per, T // HALO - 1), j))]
    wspec = lambda off, rows: pl.BlockSpec((rows, tc), lambda j, i: (0, j + off))
    acc = lambda rows: pl.BlockSpec((rows, tc), lambda j, i: (0, j))
    blk = pl.BlockSpec((tb, tc), lambda j, i: (i, j))
    return pl.pallas_call(
        body, grid=(nct, nrb),
        in_specs=gspecs + uspecs + dfs + [wspec(0, 3), wspec(nct, 3), wspec(0, 1), wspec(nct, 1)],
        out_specs=[blk, blk, acc(3), acc(3), acc(1), acc(1)],
        out_shape=[S((T, FFN_HIDDEN), BF16), S((T, FFN_HIDDEN), BF16), S((3, FFN_HIDDEN), F32), S((3, FFN_HIDDEN), F32),
                   S((1, FFN_HIDDEN), F32), S((1, FFN_HIDDEN), F32)],
        name=name, compiler_params=_cparams("parallel", "arbitrary"))(z, z, z, z, z, z, df, df, cw, cw, cb, cb)


SGU_RB = 256


def _sgu_weights(ws_ref):
    i = lax.broadcasted_iota(jnp.int32, (SGU_BLOCK, SGU_BLOCK), 0)
    j = lax.broadcasted_iota(jnp.int32, (SGU_BLOCK, SGU_BLOCK), 1)
    mask = (j < CHUNK) | (i >= CHUNK)
    return mask, [jnp.where(mask, ws_ref[g], 0.0).astype(BF16) for g in range(SGU_GROUPS)]


def _sgu_norm(zv, lng, lnb):
    mu = jnp.mean(zv, axis=-1, keepdims=True)
    vc = zv - mu
    rstd = lax.rsqrt(jnp.mean(vc * vc, axis=-1, keepdims=True) + EPS)
    vh = vc * rstd
    return vh, rstd, vh * lng + lnb


def _sgu_fwd(zpre, lng, lnb, ws, bst):
    T = zpre.shape[0]
    nb = SGU_RB // SGU_BLOCK

    def body(z_ref, lng_ref, lnb_ref, ws_ref, bst_ref, o_ref):
        _, wm = _sgu_weights(ws_ref)
        u = _gelu(z_ref[:, :SGU_WIDTH])
        _, _, vn = _sgu_norm(_gelu(z_ref[:, SGU_WIDTH:]), lng_ref[...], lnb_ref[...])
        vnb = vn.astype(BF16)
        for b in range(nb):
            rows = slice(b * SGU_BLOCK, (b + 1) * SGU_BLOCK)
            for g in range(SGU_GROUPS):
                cols = slice(g * SGU_GW, (g + 1) * SGU_GW)
                mixed = _dot(wm[g], vnb[rows, cols]) + bst_ref[:, g:g + 1]
                o_ref[rows, cols] = (u[rows, cols] * mixed).astype(BF16)

    vec = pl.BlockSpec((1, SGU_WIDTH), lambda i: (0, 0))
    return pl.pallas_call(
        body, grid=(T // SGU_RB,),
        in_specs=[pl.BlockSpec((SGU_RB, 2 * SGU_WIDTH), lambda i: (i, 0)), vec, vec,
                  pl.BlockSpec((SGU_GROUPS, SGU_BLOCK, SGU_BLOCK), lambda i: (0, 0, 0)),
                  pl.BlockSpec((SGU_BLOCK, SGU_GROUPS), lambda i: (0, 0))],
        out_specs=pl.BlockSpec((SGU_RB, SGU_WIDTH), lambda i: (i, 0)),
        out_shape=S((T, SGU_WIDTH), BF16), name="sgu_fwd", compiler_params=_cparams("parallel"))(zpre, lng, lnb, ws, bst)


def _sgu_bwd(zpre, dy, lng, lnb, ws, bst):
    T = zpre.shape[0]
    nb = SGU_RB // SGU_BLOCK

    def body(z_ref, dy_ref, lng_ref, lnb_ref, ws_ref, bst_ref, dz_ref, dws_ref, dbst_ref, dlng_ref, dlnb_ref, dvn):
        @pl.when(pl.program_id(0) == 0)
        def _():
            for ref in (dws_ref, dbst_ref, dlng_ref, dlnb_ref):
                ref[...] = jnp.zeros_like(ref)

        mask, wm = _sgu_weights(ws_ref)
        u, ud = _gelu_and_grad(z_ref[:, :SGU_WIDTH])
        v, vd = _gelu_and_grad(z_ref[:, SGU_WIDTH:])
        vh, rstd, vn = _sgu_norm(v, lng_ref[...], lnb_ref[...])
        vnb = vn.astype(BF16)
        lane8 = lax.broadcasted_iota(jnp.int32, (SGU_BLOCK, SGU_GROUPS), 1)
        dbs = jnp.zeros((SGU_BLOCK, SGU_GROUPS), F32)
        for b in range(nb):
            rows = slice(b * SGU_BLOCK, (b + 1) * SGU_BLOCK)
            for g in range(SGU_GROUPS):
                cols = slice(g * SGU_GW, (g + 1) * SGU_GW)
                vg = vnb[rows, cols]
                mixed = _dot(wm[g], vg) + bst_ref[:, g:g + 1]
                dyv = dy_ref[rows, cols]
                dz_ref[rows, cols] = (dyv * mixed * ud[rows, cols]).astype(BF16)
                dmix = dyv * u[rows, cols]
                dmb = dmix.astype(BF16)
                dvn[rows, cols] = _dot_tn(wm[g], dmb)
                dws_ref[g] += jnp.where(mask, _dot_nt(dmb, vg), 0.0)
                dbs = dbs + jnp.where(lane8 == g, jnp.sum(dmix, axis=-1, keepdims=True), 0.0)
        dbst_ref[...] += dbs
        dvnv = dvn[...]
        dlng_ref[...] += jnp.sum(dvnv * vh, axis=0, keepdims=True)
        dlnb_ref[...] += jnp.sum(dvnv, axis=0, keepdims=True)
        dvh = dvnv * lng_ref[...]
        dv = rstd * (dvh - jnp.mean(dvh, axis=-1, keepdims=True) - vh * jnp.mean(dvh * vh, axis=-1, keepdims=True))
        dz_ref[:, SGU_WIDTH:] = (dv * vd).astype(BF16)

    vec = pl.BlockSpec((1, SGU_WIDTH), lambda i: (0, 0))
    wsp = pl.BlockSpec((SGU_GROUPS, SGU_BLOCK, SGU_BLOCK), lambda i: (0, 0, 0))
    bsp = pl.BlockSpec((SGU_BLOCK, SGU_GROUPS), lambda i: (0, 0))
    return pl.pallas_call(
        body, grid=(T // SGU_RB,),
        in_specs=[pl.BlockSpec((SGU_RB, 2 * SGU_WIDTH), lambda i: (i, 0)),
                  pl.BlockSpec((SGU_RB, SGU_WIDTH), lambda i: (i, 0)), vec, vec, wsp, bsp],
        out_specs=[pl.BlockSpec((SGU_RB, 2 * SGU_WIDTH), lambda i: (i, 0)), wsp, bsp, vec, vec],
        out_shape=[S((T, 2 * SGU_WIDTH), BF16), S((SGU_GROUPS, SGU_BLOCK, SGU_BLOCK), F32),
                   S((SGU_BLOCK, SGU_GROUPS), F32), S((1, SGU_WIDTH), F32), S((1, SGU_WIDTH), F32)],
        scratch_shapes=[pltpu.VMEM((SGU_RB, SGU_WIDTH), F32)], name="sgu_bwd",
        compiler_params=_cparams("arbitrary"))(zpre, dy, lng, lnb, ws, bst)


def _loss_head(h, tgt, g):
    T, D = h.shape
    tr = _tile(T, 512, 8)

    def body(h_ref, t_ref, g_ref, ls_ref, dh_ref, dhb_ref, dg_ref):
        @pl.when(pl.program_id(0) == 0)
        def _():
            ls_ref[...] = jnp.zeros_like(ls_ref)
            dg_ref[...] = jnp.zeros_like(dg_ref)

        hv = h_ref[...]
        r = lax.rsqrt(jnp.mean(hv * hv, axis=-1, keepdims=True) + EPS)
        xh = hv * r
        diff = xh * g_ref[...] - t_ref[...]
        per_row = jnp.mean(diff * diff, axis=-1, keepdims=True)
        ls_ref[...] += jnp.sum(per_row, axis=0, keepdims=True)
        dy = diff * (1.0 / D)
        dg_ref[...] += jnp.sum(dy * xh, axis=0, keepdims=True)
        dxh = dy * g_ref[...]
        dh = r * (dxh - xh * jnp.mean(dxh * xh, axis=-1, keepdims=True))
        dh_ref[...] = dh
        dhb_ref[...] = dh.astype(BF16)

    row = pl.BlockSpec((tr, D), lambda i: (i, 0))
    vec = pl.BlockSpec((1, D), lambda i: (0, 0))
    return pl.pallas_call(
        body, grid=(T // tr,), in_specs=[row, row, vec],
        out_specs=[pl.BlockSpec((1, LANES), lambda i: (0, 0)), row, row, vec],
        out_shape=[S((1, LANES), F32), S((T, D), F32), S((T, D), BF16), S((1, D), F32)],
        name="loss_head", compiler_params=_cparams("arbitrary"))(h, tgt, g)


ANY = pl.BlockSpec(memory_space=pl.ANY)


def _place():
    return lax.axis_index("x"), lax.axis_index("y"), lax.axis_index("c")


def _gather_weights(p):
    R = p.shape[0]
    Rh = R // 2

    def body(p_ref, out_ref, send_sems, recv_sems, local_sem):
        x, y, c = _place()
        sibling = (x, y, 1 - c)
        chips = [(1 - x, y), (x, 1 - y), (1 - x, 1 - y)]

        def half(px, py, pc):
            return out_ref.at[2 * px + py, pl.ds(pl.multiple_of(pc * Rh, 16), Rh), :]

        def copy(k, block, to, src=None):
            return pltpu.make_async_remote_copy(
                src_ref=half(*block) if src is None else src, dst_ref=half(*block),
                send_sem=send_sems.at[k], recv_sem=recv_sems.at[k], device_id=to, device_id_type=MESH)

        mine = pltpu.make_async_copy(p_ref, out_ref.at[2 * x + y], local_sem)
        mine.start()
        my_half = p_ref.at[pl.ds(pl.multiple_of(c * Rh, 16), Rh), :]
        first = [copy(j, (x, y, c), (*chip, c), src=my_half) for j, chip in enumerate(chips)]
        for cp in first:
            cp.start()
        passed = [copy(3 + j, (*chip, c), sibling) for j, chip in enumerate(chips)]
        for j, chip in enumerate(chips):
            copy(j, (*chip, c), (x, y, c)).wait_recv()
            passed[j].start()
        for j, chip in enumerate(chips):
            copy(3 + j, (*chip, 1 - c), (x, y, c)).wait_recv()
        for cp in first + passed:
            cp.wait_send()
        mine.wait()

    return pl.pallas_call(
        body, out_shape=S((N_CHIPS, R, PACK_W), p.dtype), in_specs=[ANY], out_specs=ANY,
        scratch_shapes=[pltpu.SemaphoreType.DMA((6,)), pltpu.SemaphoreType.DMA((6,)), pltpu.SemaphoreType.DMA],
        name="gather_weights")(p)


def _exchange_all(src, per_target, name):
    Rh = src.shape[1] // 2 if per_target else src.shape[0]

    def body(src_ref, out_ref, send_sems, recv_sems, local_sem):
        x, y, c = _place()

        def ident(px, py, pc):
            return 4 * px + 2 * py + pc

        def block_for(px, py, pc):
            if per_target:
                return src_ref.at[2 * px + py, pl.ds(pl.multiple_of(pc * Rh, 16), Rh), :]
            return src_ref

        mine = pltpu.make_async_copy(block_for(x, y, c), out_ref.at[ident(x, y, c)], local_sem)
        mine.start()
        peers = []
        for k in range(1, N_DEV):
            fx, fy, fc = (k >> 2) & 1, (k >> 1) & 1, k & 1
            peers.append((k, (x ^ fx, y ^ fy, c ^ fc)))
        copies = [pltpu.make_async_remote_copy(
            src_ref=block_for(*peer), dst_ref=out_ref.at[ident(x, y, c)], send_sem=send_sems.at[k - 1],
            recv_sem=recv_sems.at[k - 1], device_id=peer, device_id_type=MESH) for k, peer in peers]
        for cp in copies:
            cp.start()
        for k, peer in peers:
            pltpu.make_async_remote_copy(
                src_ref=block_for(*peer), dst_ref=out_ref.at[ident(*peer)], send_sem=send_sems.at[k - 1],
                recv_sem=recv_sems.at[k - 1], device_id=peer, device_id_type=MESH).wait_recv()
        for cp in copies:
            cp.wait_send()
        mine.wait()

    return pl.pallas_call(
        body, out_shape=S((N_DEV, Rh, src.shape[-1]), src.dtype), in_specs=[ANY], out_specs=ANY,
        scratch_shapes=[pltpu.SemaphoreType.DMA((N_DEV - 1,)), pltpu.SemaphoreType.DMA((N_DEV - 1,)),
                        pltpu.SemaphoreType.DMA], name=name)(src)


def _exchange_halves(h):
    Rh, W = h.shape

    def body(h_ref, out_ref, send_sem, recv_sem, local_sem):
        x, y, c = _place()

        def rows(pc):
            return out_ref.at[pl.ds(pl.multiple_of(pc * Rh, 16), Rh), :]

        mine = pltpu.make_async_copy(h_ref, rows(c), local_sem)
        mine.start()
        cp = pltpu.make_async_remote_copy(src_ref=h_ref, dst_ref=rows(c), send_sem=send_sem, recv_sem=recv_sem,
                                          device_id=(x, y, 1 - c), device_id_type=MESH)
        cp.start()
        pltpu.make_async_remote_copy(src_ref=h_ref, dst_ref=rows(1 - c), send_sem=send_sem, recv_sem=recv_sem,
                                     device_id=(x, y, 1 - c), device_id_type=MESH).wait_recv()
        cp.wait_send()
        mine.wait()

    return pl.pallas_call(
        body, out_shape=S((2 * Rh, W), h.dtype), in_specs=[ANY], out_specs=ANY,
        scratch_shapes=[pltpu.SemaphoreType.DMA, pltpu.SemaphoreType.DMA, pltpu.SemaphoreType.DMA],
        name="exchange_halves")(h)


def _sum_slots(buf, name):
    n, R, W = buf.shape
    tr = _tile(R, 256, 8)

    def body(b_ref, o_ref):
        acc = b_ref[0].astype(F32)
        for s in range(1, n):
            acc = acc + b_ref[s].astype(F32)
        o_ref[...] = acc

    return pl.pallas_call(
        body, grid=(R // tr,), in_specs=[pl.BlockSpec((n, tr, W), lambda i: (0, i, 0))],
        out_specs=pl.BlockSpec((tr, W), lambda i: (i, 0)), out_shape=S((R, W), F32), name=name,
        compiler_params=_cparams("parallel"))(buf)


def _adamw(w, g, m, v, name):
    R, W = w.shape
    tr = _tile(R, 256, 8)

    def body(w_ref, g_ref, m_ref, v_ref, d_ref, mo_ref, vo_ref):
        gv = g_ref[...]
        mn = ADAM_B1 * m_ref[...] + (1.0 - ADAM_B1) * gv
        vn = ADAM_B2 * v_ref[...] + (1.0 - ADAM_B2) * (gv * gv)
        m_hat = mn / (1.0 - ADAM_B1 ** ADAM_STEP)
        v_hat = vn / (1.0 - ADAM_B2 ** ADAM_STEP)
        d_ref[...] = -ADAM_LR * (m_hat / (jnp.sqrt(v_hat) + ADAM_EPS) + ADAM_WD * w_ref[...])
        mo_ref[...] = mn
        vo_ref[...] = vn

    blk = pl.BlockSpec((tr, W), lambda i: (i, 0))
    return pl.pallas_call(
        body, grid=(R // tr,), in_specs=[blk] * 4, out_specs=[blk] * 3, out_shape=[S((R, W), F32)] * 3, name=name,
        compiler_params=_cparams("parallel"))(w, g, m, v)


def _tables(T):
    f32 = F32
    half = RET_QK // 2
    inv = 1.0 / (10000.0 ** jnp.linspace(0.0, 1.0, half, dtype=f32))
    ang = jnp.arange(T).astype(f32)[:, None] * inv[None, :]
    cos, sin = jnp.cos(ang), jnp.sin(ang)
    c2 = jnp.concatenate([cos, cos], axis=-1)
    s2 = jnp.concatenate([-sin, sin], axis=-1)
    log_g = jnp.log1p(-jnp.exp2(-5.0 - jnp.arange(RET_HEADS, dtype=f32)))
    idx = jnp.arange(CHUNK, dtype=f32)
    dintra = jnp.exp(log_g[:, None, None] * jnp.abs(idx[:, None] - idx[None, :]))
    kdec = jnp.exp(log_g[None, :] * (CHUNK - 1 - idx)[:, None]).T
    qdec = jnp.exp(log_g[None, :] * (idx + 1.0)[:, None]).T
    cdec = jnp.exp(log_g * CHUNK)
    bc = lambda a, w: jnp.broadcast_to(a[:, :, None], (RET_HEADS, a.shape[1], w))
    return c2, s2, dintra, bc(qdec, RET_QK), bc(kdec, RET_QK), jnp.broadcast_to(cdec[:, None, None], (RET_HEADS, 1, RET_V))


def _local_step(x, tgt, p):
    T = x.shape[0]
    tab = _tables(T)
    row = lambda a: a.reshape(1, -1)
    tr = lambda w: jnp.transpose(w)
    kcol = 2 * RET_HEADS * RET_QK + 2 * RET_HEADS * RET_V + ATT_HEADS * ATT_D

    hn0 = _rmsnorm_fwd(x, row(p["attn_norm_g"][0]), "norm_a0")
    z0 = _mm(hn0, p["ab_w_in"][0], "mm_ab_in")
    ya, r, st = _retention_fwd(z0, *tab)
    bias_t = _bias_build(p["ab_rel_bias"][0])
    bias = jnp.transpose(bias_t, (1, 0, 2))[:, :, :BAND]
    front = ((PADK, 0), (0, 0))
    kpad = jnp.pad(z0[:, kcol:kcol + ATT_HEADS * ATT_D].astype(BF16), front)
    vpad = jnp.pad(z0[:, kcol + ATT_HEADS * ATT_D:].astype(BF16), front)
    yb = _attention_fwd(z0, kpad, vpad, bias)
    ycat = jnp.concatenate([ya, yb], axis=1)
    h1 = _mm(ycat, p["ab_w_out"][0], "mm_ab_out", res=x)

    def ffn_fwd(h, l):
        hf = _rmsnorm_fwd(h, row(p["ffn_norm_g"][l]), f"norm_f{l}")
        zf = _mm(hf, p["ffn_w_up"][l], f"mm_up{l}")
        f = _convglu_fwd(zf, p["ffn_conv_w"][l], row(p["ffn_conv_b"][l]), f"convglu_fwd{l}")
        return hf, zf, f, _mm(f, p["ffn_w_down"][l], f"mm_down{l}", res=h)

    hf0, zf0, f0, h2 = ffn_fwd(h1, 0)
    hn1 = _rmsnorm_fwd(h2, row(p["attn_norm_g"][1]), "norm_a1")
    zc = _mm(hn1, p["c_w_in"][0], "mm_c_in")
    lng, lnb, bst, ws = row(p["c_ln_g"][0]), row(p["c_ln_b"][0]), tr(p["c_b_s"][0]), p["c_w_s"][0]
    y1 = _sgu_fwd(zc, lng, lnb, ws, bst)
    h3 = _mm(y1, p["c_w_out"][0], "mm_c_out", res=h2)
    hf1, zf1, f1, h4 = ffn_fwd(h3, 1)
    lsum, dh4, dh4b, dgfin = _loss_head(h4, tgt, row(p["final_norm_g"]))

    g = {}

    def ffn_bwd(dh, dhb, h_in, hf, zf, f, l):
        d_down = _mm_tn(f, dhb, f"mmt_down{l}")
        df = _mm(dhb, tr(p["ffn_w_down"][l]), f"mmb_down{l}")
        dzg, dzu, dwg, dwu, dbg, dbu = _convglu_bwd(zf, df, p["ffn_conv_w"][l], row(p["ffn_conv_b"][l]), f"convglu_bwd{l}")
        dz = jnp.concatenate([dzg, dzu], axis=1)
        d_up = _mm_tn(hf, dz, f"mmt_up{l}")
        dhf = _mm(dz, tr(p["ffn_w_up"][l]), f"mmb_up{l}")
        dh_in, dh_in_b, dgf = _rmsnorm_bwd(h_in, dhf, row(p["ffn_norm_g"][l]), dh, f"norm_f{l}_bwd")
        return dh_in, dh_in_b, dict(ffn_w_down=d_down, ffn_w_up=d_up, ffn_norm_g=dgf[0],
                                    ffn_conv_w=jnp.concatenate([dwg, dwu], axis=1), ffn_conv_b=jnp.concatenate([dbg, dbu], axis=1)[0])

    dh3, dh3b, gf1 = ffn_bwd(dh4, dh4b, h3, hf1, zf1, f1, 1)
    g["c_w_out"] = _mm_tn(y1, dh3b, "mmt_c_out")[None]
    dy1 = _mm(dh3b, tr(p["c_w_out"][0]), "mmb_c_out")
    dzc, dws, dbst, dlng, dlnb = _sgu_bwd(zc, dy1, lng, lnb, ws, bst)
    g["c_w_s"], g["c_b_s"], g["c_ln_g"], g["c_ln_b"] = dws[None], tr(dbst)[None], dlng, dlnb
    g["c_w_in"] = _mm_tn(hn1, dzc, "mmt_c_in")[None]
    dhn1 = _mm(dzc, tr(p["c_w_in"][0]), "mmb_c_in")
    dh2, dh2b, dga1 = _rmsnorm_bwd(h2, dhn1, row(p["attn_norm_g"][1]), dh3, "norm_a1_bwd")
    dh1, dh1b, gf0 = ffn_bwd(dh2, dh2b, h1, hf0, zf0, f0, 0)
    g["ab_w_out"] = _mm_tn(ycat, dh1b, "mmt_ab_out")[None]
    dycat = _mm(dh1b, tr(p["ab_w_out"][0]), "mmb_ab_out")
    dqb, dkp, dvp, dbias = _attention_bwd(z0, kpad, vpad, bias, dycat)
    dqa, dka, dva, dga = _retention_bwd(z0, *tab, r, dycat, st)
    dz0 = jnp.concatenate([dqa, dka, dva, dga, dqb, dkp[PADK:], dvp[PADK:]], axis=1)
    g["ab_w_in"] = _mm_tn(hn0, dz0, "mmt_ab_in")[None]
    dhn0 = _mm(dz0, tr(p["ab_w_in"][0]), "mmb_ab_in")
    gx, _, dga0 = _rmsnorm_bwd(x, dhn0, row(p["attn_norm_g"][0]), dh1, "norm_a0_bwd")
    dbt = jnp.pad(jnp.transpose(dbias, (1, 0, 2)), ((0, 0), (0, 0), (0, 5 * LANES - BAND)))
    g["ab_rel_bias"] = _bias_grad(dbt)[None, :, :N_REL]

    g["attn_norm_g"] = jnp.stack([dga0[0], dga1[0]])
    g["final_norm_g"] = dgfin[0]
    for k in gf0:
        g[k] = jnp.stack([gf0[k], gf1[k]])
    return lsum[0, 0], gx, g


BIG = [("ab_w_in", 2), ("ab_w_out", 1), ("c_w_in", 2), ("c_w_out", 1), ("ffn_w_up", 2), ("ffn_w_down", 1)]
SMALL_SHARDED = [("c_ln_g", 1), ("c_ln_b", 1), ("ffn_conv_w", 2)]
REPLICATED = ["attn_norm_g", "ffn_norm_g", "ab_rel_bias", "c_w_s", "c_b_s", "ffn_conv_b", "final_norm_g"]


def _rows_of(n_elems):
    return -(-n_elems // PACK_W)


def _flat_rows(a):
    f = a.reshape(-1)
    rows = _rows_of(f.shape[0])
    return jnp.pad(f, (0, rows * PACK_W - f.shape[0])).reshape(rows, PACK_W)


def _pad_rows(a, mult):
    extra = (-a.shape[0]) % mult
    return jnp.pad(a, ((0, extra), (0, 0))) if extra else a


def _pack(arrs, mult):
    return _pad_rows(jnp.concatenate([_flat_rows(a) for a in arrs], axis=0), mult)


def _unpack(buf, shapes):
    out, r = [], 0
    for shp in shapes:
        n = math.prod(shp)
        rows = _rows_of(n)
        out.append(buf[r:r + rows].reshape(-1)[:n].reshape(shp))
        r += rows
    return out


def _shard_major(full, axis):
    shp = full.shape
    split = full.reshape(shp[:axis] + (N_CHIPS, shp[axis] // N_CHIPS) + shp[axis + 1:])
    return jnp.moveaxis(split, axis, 0)


def _from_shards(sh, axis):
    m = jnp.moveaxis(sh, 0, axis)
    shp = m.shape
    return m.reshape(shp[:axis] + (shp[axis] * shp[axis + 1],) + shp[axis + 2:])


def _as_bf16_pairs(a):
    return lax.bitcast_convert_type(a.astype(F32), BF16)


def _from_bf16_pairs(a):
    return lax.bitcast_convert_type(a, F32)


def kernel(x, attn_norm_g, ffn_norm_g, ab_w_in, ab_w_out, ab_rel_bias, c_w_in, c_ln_g, c_ln_b, c_w_s, c_b_s, c_w_out, ffn_w_up, ffn_conv_w, ffn_conv_b, ffn_w_down, final_norm_g, loss_target, m_attn_norm_g, m_ffn_norm_g, m_ab_w_in, m_ab_w_out, m_ab_rel_bias, m_c_w_in, m_c_ln_g, m_c_ln_b, m_c_w_s, m_c_b_s, m_c_w_out, m_ffn_w_up, m_ffn_conv_w, m_ffn_conv_b, m_ffn_w_down, m_final_norm_g, v_attn_norm_g, v_ffn_norm_g, v_ab_w_in, v_ab_w_out, v_ab_rel_bias, v_c_w_in, v_c_ln_g, v_c_ln_b, v_c_w_s, v_c_b_s, v_c_w_out, v_ffn_w_up, v_ffn_conv_w, v_ffn_conv_b, v_ffn_w_down, v_final_norm_g):
    w = dict(attn_norm_g=attn_norm_g, ffn_norm_g=ffn_norm_g, ab_w_in=ab_w_in, ab_w_out=ab_w_out, ab_rel_bias=ab_rel_bias,
             c_w_in=c_w_in, c_ln_g=c_ln_g, c_ln_b=c_ln_b, c_w_s=c_w_s, c_b_s=c_b_s, c_w_out=c_w_out, ffn_w_up=ffn_w_up,
             ffn_conv_w=ffn_conv_w, ffn_conv_b=ffn_conv_b, ffn_w_down=ffn_w_down, final_norm_g=final_norm_g)
    m = dict(attn_norm_g=m_attn_norm_g, ffn_norm_g=m_ffn_norm_g, ab_w_in=m_ab_w_in, ab_w_out=m_ab_w_out,
             ab_rel_bias=m_ab_rel_bias, c_w_in=m_c_w_in, c_ln_g=m_c_ln_g, c_ln_b=m_c_ln_b, c_w_s=m_c_w_s, c_b_s=m_c_b_s,
             c_w_out=m_c_w_out, ffn_w_up=m_ffn_w_up, ffn_conv_w=m_ffn_conv_w, ffn_conv_b=m_ffn_conv_b,
             ffn_w_down=m_ffn_w_down, final_norm_g=m_final_norm_g)
    v = dict(attn_norm_g=v_attn_norm_g, ffn_norm_g=v_ffn_norm_g, ab_w_in=v_ab_w_in, ab_w_out=v_ab_w_out,
             ab_rel_bias=v_ab_rel_bias, c_w_in=v_c_w_in, c_ln_g=v_c_ln_g, c_ln_b=v_c_ln_b, c_w_s=v_c_w_s, c_b_s=v_c_b_s,
             c_w_out=v_c_w_out, ffn_w_up=v_ffn_w_up, ffn_conv_w=v_ffn_conv_w, ffn_conv_b=v_ffn_conv_b,
             ffn_w_down=v_ffn_w_down, final_norm_g=v_final_norm_g)
    names = list(w)
    chip = 2 * lax.axis_index("x") + lax.axis_index("y")

    send = [w[n].astype(BF16) for n, _ in BIG] + [_as_bf16_pairs(w[n]) for n, _ in SMALL_SHARDED]
    shard_shapes = [a.shape for a in send]
    gathered = _gather_weights(_pack(send, 32))
    parts = [_unpack(gathered[s], shard_shapes) for s in range(N_CHIPS)]
    full = dict(w)
    for i, (n, axis) in enumerate(BIG):
        full[n] = _from_shards(jnp.stack([parts[s][i] for s in range(N_CHIPS)]), axis)
    for i, (n, axis) in enumerate(SMALL_SHARDED):
        full[n] = _from_shards(_from_bf16_pairs(jnp.stack([parts[s][len(BIG) + i] for s in range(N_CHIPS)])), axis)

    lsum, grad_x, g = _local_step(x[0], loss_target[0], full)
    loss = lax.psum(0.5 * lsum, ("x", "y", "c"))

    gbig = jnp.concatenate([_shard_major(g[n], axis).reshape(N_CHIPS, -1, PACK_W) for n, axis in BIG], axis=1)
    slots = _exchange_all(gbig.astype(BF16), True, "exchange_big")
    gshard = _exchange_halves(_sum_slots(slots, "sum_big"))
    pack_big = lambda d: jnp.concatenate([d[n].reshape(-1, PACK_W) for n, _ in BIG], axis=0)
    big_out = _adamw(pack_big(w), gshard, pack_big(m), pack_big(v), "adamw_big")
    big_shapes = [w[n].shape for n, _ in BIG]

    small_names = REPLICATED + [n for n, _ in SMALL_SHARDED]
    gsmall = _pack([g[n] for n in small_names], 16)
    gsum = _sum_slots(_exchange_all(gsmall, False, "exchange_small"), "sum_small")
    gsmall_full = dict(zip(small_names, _unpack(gsum, [g[n].shape for n in small_names])))
    for n, axis in SMALL_SHARDED:
        size = w[n].shape[axis]
        gsmall_full[n] = lax.dynamic_slice_in_dim(gsmall_full[n], chip * size, size, axis)
    pack_small = lambda d: _pack([d[n] for n in small_names], 8)
    small_out = _adamw(pack_small(w), pack_small(gsmall_full), pack_small(m), pack_small(v), "adamw_small")
    small_shapes = [w[n].shape for n in small_names]

    grads = dict(zip([n for n, _ in BIG], _unpack(gshard, big_shapes)))
    grads.update(gsmall_full)
    outs = [grads]
    for k in range(3):
        d = dict(zip([n for n, _ in BIG], _unpack(big_out[k], big_shapes)))
        d.update(zip(small_names, _unpack(small_out[k], small_shapes)))
        outs.append(d)
    return (loss, grad_x[None], *[o[n] for o in outs for n in names])
```

```python
import functools
import math

import jax
import jax.numpy as jnp
from jax import lax
from jax.experimental import pallas as pl
from jax.experimental.pallas import tpu as pltpu

F32 = jnp.float32
BF16 = jnp.bfloat16
S = jax.ShapeDtypeStruct
MESH = pl.DeviceIdType.MESH

D_MODEL = 1024
CHUNK = 64
EPS = 1e-6
NEG_INF = -1e30
RET_HEADS, RET_QK, RET_V = 4, 128, 256
ATT_HEADS, ATT_D, ATT_PAST, MAX_REL = 8, 64, 8, 128
BAND = (ATT_PAST + 1) * CHUNK
PADK = ATT_PAST * CHUNK
SGU_BLOCK, SGU_GROUPS, SGU_WIDTH = 128, 8, 2048
SGU_GW = SGU_WIDTH // SGU_GROUPS
FFN_HIDDEN = 2816
N_REL = 2 * MAX_REL + 1
RET_SCALE = RET_QK ** -0.5
ATT_SCALE = ATT_D ** -0.5
ADAM_LR, ADAM_B1, ADAM_B2, ADAM_EPS, ADAM_WD, ADAM_STEP = 0.001, 0.9, 0.999, 1e-08, 0.01, 10

V7X_VMEM_BYTES = 64 * 1024 * 1024
VMEM_LIMIT = V7X_VMEM_BYTES * 7 // 8
LANES = 128
PACK_W = 1024
N_CHIPS = 4
N_DEV = 8

GELU_C = math.sqrt(2.0 / math.pi)
GELU_A = 0.044715


def _cparams(*sem):
    return pltpu.CompilerParams(dimension_semantics=tuple(sem) if sem else None, vmem_limit_bytes=VMEM_LIMIT)


def _tile(n, target, unit=LANES):
    best = None
    for t in range(unit, min(n, target) + 1, unit):
        if n % t == 0:
            best = t
    return best if best is not None else n


def _gelu(x):
    t = jnp.tanh(GELU_C * (x + GELU_A * x * x * x))
    return 0.5 * x * (1.0 + t)


def _gelu_and_grad(x):
    x2 = x * x
    t = jnp.tanh(GELU_C * (x + GELU_A * x2 * x))
    g = 0.5 * x * (1.0 + t)
    dg = 0.5 * (1.0 + t) + 0.5 * x * (1.0 - t * t) * (GELU_C * (1.0 + 3.0 * GELU_A * x2))
    return g, dg


def _sigmoid(x):
    return 1.0 / (1.0 + jnp.exp(-x))


def _dot(a, b):
    return jnp.dot(a, b, preferred_element_type=F32)


def _dot_nt(a, b):
    return lax.dot_general(a, b, (((1,), (1,)), ((), ())), preferred_element_type=F32)


def _dot_tn(a, b):
    return lax.dot_general(a, b, (((0,), (0,)), ((), ())), preferred_element_type=F32)


def _rmsnorm_fwd(x, g, name):
    T, D = x.shape
    tr = _tile(T, 512, 8)

    def body(x_ref, g_ref, o_ref):
        xv = x_ref[...]
        r = lax.rsqrt(jnp.mean(xv * xv, axis=-1, keepdims=True) + EPS)
        o_ref[...] = (xv * r * g_ref[...]).astype(o_ref.dtype)

    return pl.pallas_call(
        body, grid=(T // tr,),
        in_specs=[pl.BlockSpec((tr, D), lambda i: (i, 0)), pl.BlockSpec((1, D), lambda i: (0, 0))],
        out_specs=pl.BlockSpec((tr, D), lambda i: (i, 0)),
        out_shape=S((T, D), BF16), name=name, compiler_params=_cparams("parallel"))(x, g)


def _rmsnorm_bwd(x, dy, g, dres, name):
    T, D = x.shape
    tr = _tile(T, 512, 8)

    def body(x_ref, dy_ref, g_ref, dres_ref, dx_ref, dxb_ref, dg_ref):
        @pl.when(pl.program_id(0) == 0)
        def _():
            dg_ref[...] = jnp.zeros_like(dg_ref)

        xv = x_ref[...]
        r = lax.rsqrt(jnp.mean(xv * xv, axis=-1, keepdims=True) + EPS)
        xh = xv * r
        dyv = dy_ref[...]
        dg_ref[...] += jnp.sum(dyv * xh, axis=0, keepdims=True)
        dxh = dyv * g_ref[...]
        dx = dres_ref[...] + r * (dxh - xh * jnp.mean(dxh * xh, axis=-1, keepdims=True))
        dx_ref[...] = dx
        dxb_ref[...] = dx.astype(BF16)

    row = pl.BlockSpec((tr, D), lambda i: (i, 0))
    vec = pl.BlockSpec((1, D), lambda i: (0, 0))
    return pl.pallas_call(
        body, grid=(T // tr,), in_specs=[row, row, vec, row], out_specs=[row, row, vec],
        out_shape=[S((T, D), F32), S((T, D), BF16), S((1, D), F32)], name=name,
        compiler_params=_cparams("arbitrary"))(x, dy, g, dres)


def _pieces(a):
    return list(a) if isinstance(a, (list, tuple)) else [a]


def _piece_layout(widths, tile):
    out, s = [], 0
    for w in widths:
        out.append((s, w // tile))
        s += w // tile
    return out


def _common_tile(widths, target):
    return _tile(functools.reduce(math.gcd, widths), target)


def _mm(a, b, name, res=None, out_dtype=F32):
    pieces = _pieces(a)
    M = pieces[0].shape[0]
    widths = [p.shape[1] for p in pieces]
    K, N = sum(widths), b.shape[1]
    tm, tn, tk = _tile(M, 1024, 8), _tile(N, 1408), _common_tile(widths, 1536)
    nk, npc = K // tk, len(pieces)
    layout = _piece_layout(widths, tk)

    def body(*refs):
        a_refs, b_ref = refs[:npc], refs[npc]
        r_ref = refs[npc + 1] if res is not None else None
        o_ref = refs[npc + 1 + (res is not None)]

        def finish(v):
            if r_ref is not None:
                v = v + r_ref[...]
            o_ref[...] = v.astype(o_ref.dtype)

        if nk == 1:
            finish(_dot(a_refs[0][...], b_ref[...]))
            return
        acc = refs[-1]
        k = pl.program_id(2)
        for a_ref, (s, c) in zip(a_refs, layout):
            def add(a_ref=a_ref):
                acc[...] += _dot(a_ref[...], b_ref[...])

            if s == 0:
                @pl.when(k == 0)
                def _(a_ref=a_ref):
                    acc[...] = _dot(a_ref[...], b_ref[...])

                if c > 1:
                    pl.when((k > 0) & (k < c))(add)
            else:
                pl.when((k >= s) & (k < s + c))(add)

        @pl.when(k == nk - 1)
        def _():
            finish(acc[...])

    in_specs = [pl.BlockSpec((tm, tk), lambda i, j, k, s=s, c=c: (i, jnp.clip(k - s, 0, c - 1))) for s, c in layout]
    in_specs.append(pl.BlockSpec((tk, tn), lambda i, j, k: (k, j)))
    args = pieces + [b]
    if res is not None:
        in_specs.append(pl.BlockSpec((tm, tn), lambda i, j, k: (i, j)))
        args.append(res)
    return pl.pallas_call(
        body, grid=(M // tm, N // tn, nk), in_specs=in_specs,
        out_specs=pl.BlockSpec((tm, tn), lambda i, j, k: (i, j)),
        out_shape=S((M, N), out_dtype),
        scratch_shapes=[pltpu.VMEM((tm, tn), F32)] if nk > 1 else [],
        name=name, compiler_params=_cparams("parallel", "parallel", "arbitrary"))(*args)


def _mm_tn(a, g, name):
    ap, gp = _pieces(a), _pieces(g)
    T = ap[0].shape[0]
    aw, gw = [p.shape[1] for p in ap], [p.shape[1] for p in gp]
    tm, tn, tt = _common_tile(aw, 1408), _common_tile(gw, 1408), _tile(T, 1024, 8)
    alay, glay = _piece_layout(aw, tm), _piece_layout(gw, tn)
    na = len(ap)

    def inside(idx, s, c, single):
        return None if single else (idx >= s) & (idx < s + c)

    def body(*refs):
        a_refs, g_refs, o_ref = refs[:na], refs[na:-1], refs[-1]
        i, j = pl.program_id(0), pl.program_id(1)

        @pl.when(pl.program_id(2) == 0)
        def _():
            o_ref[...] = jnp.zeros_like(o_ref)

        for a_ref, (sa, ca) in zip(a_refs, alay):
            for g_ref, (sg, cg) in zip(g_refs, glay):
                def add(a_ref=a_ref, g_ref=g_ref):
                    o_ref[...] += _dot_tn(a_ref[...], g_ref[...])

                conds = [c for c in (inside(i, sa, ca, na == 1), inside(j, sg, cg, len(gp) == 1)) if c is not None]
                if not conds:
                    add()
                else:
                    pl.when(functools.reduce(lambda u, v: u & v, conds))(add)

    def spec(tile, lay, single, axis):
        s, c = lay

        def index(i, j, k):
            idx = (i, j)[axis]
            if single:
                return (k, idx)
            on = (idx >= s) & (idx < s + c)
            return (jnp.where(on, k, 0), jnp.clip(idx - s, 0, c - 1))

        return pl.BlockSpec((tt, tile), index)

    in_specs = [spec(tm, lay, na == 1, 0) for lay in alay] + [spec(tn, lay, len(gp) == 1, 1) for lay in glay]
    return pl.pallas_call(
        body, grid=(sum(aw) // tm, sum(gw) // tn, T // tt), in_specs=in_specs,
        out_specs=pl.BlockSpec((tm, tn), lambda i, j, k: (i, j)),
        out_shape=S((sum(aw), sum(gw)), F32), name=name,
        compiler_params=_cparams("parallel", "parallel", "arbitrary"))(*ap, *gp)


def _rotate(x, c2, s2):
    return x * c2 + pltpu.roll(x, RET_QK // 2, 1) * s2


def _unrotate(d, c2, s2):
    return d * c2 - pltpu.roll(d, RET_QK // 2, 1) * s2


def _ret_specs(RB, blockmap):
    q = pl.BlockSpec((RB, RET_QK), lambda h, n: (blockmap(n), h))
    k = pl.BlockSpec((RB, RET_QK), lambda h, n: (blockmap(n), RET_HEADS + h))
    v = pl.BlockSpec((RB, RET_V), lambda h, n: (blockmap(n), RET_HEADS + h))
    g = pl.BlockSpec((RB, RET_V), lambda h, n: (blockmap(n), 2 * RET_HEADS + h))
    tab = pl.BlockSpec((RB, RET_QK), lambda h, n: (blockmap(n), 0))
    return q, k, v, g, tab


def _ret_decay_specs():
    return [pl.BlockSpec((None, CHUNK, CHUNK), lambda h, n: (h, 0, 0)),
            pl.BlockSpec((None, CHUNK, RET_QK), lambda h, n: (h, 0, 0)),
            pl.BlockSpec((None, CHUNK, RET_QK), lambda h, n: (h, 0, 0)),
            pl.BlockSpec((None, 1, RET_V), lambda h, n: (h, 0, 0))]


def _retention_fwd(z, c2, s2, dintra, qdec, kdec, cdec):
    T = z.shape[0]
    RB = min(512, T)
    nch, nb = RB // CHUNK, T // RB

    def body(q_ref, k_ref, v_ref, g_ref, c2_ref, s2_ref, di_ref, qd_ref, kd_ref, cd_ref, ya_ref, r_ref, st_ref, state):
        @pl.when(pl.program_id(1) == 0)
        def _():
            state[...] = jnp.zeros_like(state)

        dmat, qdv, kdv, cdv = di_ref[...], qd_ref[...], kd_ref[...], cd_ref[...]
        for c in range(nch):
            rows = slice(c * CHUNK, (c + 1) * CHUNK)
            c2v, s2v = c2_ref[rows, :], s2_ref[rows, :]
            qr = _rotate(q_ref[rows, :], c2v, s2v)
            kr = _rotate(k_ref[rows, :], c2v, s2v) * RET_SCALE
            vb = v_ref[rows, :].astype(BF16)
            sm = _dot_nt(qr.astype(BF16), kr.astype(BF16)) * dmat
            sb = state[...].astype(BF16)
            st_ref[c] = sb
            o = _dot(sm.astype(BF16), vb) + _dot((qr * qdv).astype(BF16), sb)
            state[...] = state[...] * cdv + _dot_tn((kr * kdv).astype(BF16), vb)
            r_ref[rows, :] = o
            mu = jnp.mean(o, axis=-1, keepdims=True)
            oc = o - mu
            rn = oc * lax.rsqrt(jnp.mean(oc * oc, axis=-1, keepdims=True) + EPS)
            gv = g_ref[rows, :]
            ya_ref[rows, :] = (gv * _sigmoid(gv) * rn).astype(BF16)

    q, k, v, g, tab = _ret_specs(RB, lambda n: n)
    wide = pl.BlockSpec((RB, RET_V), lambda h, n: (n, h))
    return pl.pallas_call(
        body, grid=(RET_HEADS, nb),
        in_specs=[q, k, v, g, tab, tab] + _ret_decay_specs(),
        out_specs=[wide, wide, pl.BlockSpec((None, nch, RET_QK, RET_V), lambda h, n: (h, n, 0, 0))],
        out_shape=[S((T, RET_HEADS * RET_V), BF16), S((T, RET_HEADS * RET_V), F32),
                   S((RET_HEADS, T // CHUNK, RET_QK, RET_V), BF16)],
        scratch_shapes=[pltpu.VMEM((RET_QK, RET_V), F32)], name="retention_fwd",
        compiler_params=_cparams("parallel", "arbitrary"))(z, z, z, z, c2, s2, dintra, qdec, kdec, cdec)


def _retention_bwd(z, c2, s2, dintra, qdec, kdec, cdec, r, dycat, st):
    T = z.shape[0]
    RB = min(512, T)
    nch, nb = RB // CHUNK, T // RB

    def body(q_ref, k_ref, v_ref, g_ref, c2_ref, s2_ref, di_ref, qd_ref, kd_ref, cd_ref, r_ref, dy_ref, st_ref,
             dq_ref, dk_ref, dv_ref, dg_ref, dstate):
        @pl.when(pl.program_id(1) == 0)
        def _():
            dstate[...] = jnp.zeros_like(dstate)

        dmat, qdv, kdv, cdv = di_ref[...], qd_ref[...], kd_ref[...], cd_ref[...]
        for c in reversed(range(nch)):
            rows = slice(c * CHUNK, (c + 1) * CHUNK)
            c2v, s2v = c2_ref[rows, :], s2_ref[rows, :]
            qr = _rotate(q_ref[rows, :], c2v, s2v)
            kr = _rotate(k_ref[rows, :], c2v, s2v) * RET_SCALE
            qb, kb = qr.astype(BF16), kr.astype(BF16)
            vb = v_ref[rows, :].astype(BF16)
            o, gv, dy = r_ref[rows, :], g_ref[rows, :], dy_ref[rows, :]
            mu = jnp.mean(o, axis=-1, keepdims=True)
            oc = o - mu
            rstd = lax.rsqrt(jnp.mean(oc * oc, axis=-1, keepdims=True) + EPS)
            rn = oc * rstd
            sg = _sigmoid(gv)
            dg_ref[rows, :] = (dy * rn * (sg * (1.0 + gv * (1.0 - sg)))).astype(BF16)
            drn = dy * (gv * sg)
            do = rstd * (drn - jnp.mean(drn, axis=-1, keepdims=True) - rn * jnp.mean(drn * rn, axis=-1, keepdims=True))
            dob = do.astype(BF16)
            sm = (_dot_nt(qb, kb) * dmat).astype(BF16)
            kdb = (kr * kdv).astype(BF16)
            dsb = dstate[...].astype(BF16)
            dv_ref[rows, :] = (_dot_tn(sm, dob) + _dot(kdb, dsb)).astype(BF16)
            ds = (_dot_nt(dob, vb) * dmat).astype(BF16)
            dqr = _dot(ds, kb) + _dot_nt(dob, st_ref[c]) * qdv
            dkr = (_dot_tn(ds, qb) + _dot_nt(vb, dsb) * kdv) * RET_SCALE
            dstate[...] = dstate[...] * cdv + _dot_tn((qr * qdv).astype(BF16), dob)
            dq_ref[rows, :] = _unrotate(dqr, c2v, s2v).astype(BF16)
            dk_ref[rows, :] = _unrotate(dkr, c2v, s2v).astype(BF16)

    rev = lambda n: nb - 1 - n
    q, k, v, g, tab = _ret_specs(RB, rev)
    wide = pl.BlockSpec((RB, RET_V), lambda h, n: (rev(n), h))
    narrow = pl.BlockSpec((RB, RET_QK), lambda h, n: (rev(n), h))
    return pl.pallas_call(
        body, grid=(RET_HEADS, nb),
        in_specs=[q, k, v, g, tab, tab] + _ret_decay_specs() + [
            wide, wide, pl.BlockSpec((None, nch, RET_QK, RET_V), lambda h, n: (h, rev(n), 0, 0))],
        out_specs=[narrow, narrow, wide, wide],
        out_shape=[S((T, RET_HEADS * RET_QK), BF16), S((T, RET_HEADS * RET_QK), BF16),
                   S((T, RET_HEADS * RET_V), BF16), S((T, RET_HEADS * RET_V), BF16)],
        scratch_shapes=[pltpu.VMEM((RET_QK, RET_V), F32)], name="retention_bwd",
        compiler_params=_cparams("parallel", "arbitrary"))(z, z, z, z, c2, s2, dintra, qdec, kdec, cdec, r, dycat, st)


def _rel_index(i):
    r = lax.broadcasted_iota(jnp.int32, (3 * LANES, 5 * LANES), 0)
    j = lax.broadcasted_iota(jnp.int32, (3 * LANES, 5 * LANES), 1)
    idx = jnp.clip(i + PADK - j, -MAX_REL, MAX_REL) + MAX_REL
    return (r == idx).astype(BF16)


def _split3(v):
    hi = v.astype(BF16)
    r1 = v - hi.astype(F32)
    mid = r1.astype(BF16)
    lo = (r1 - mid.astype(F32)).astype(BF16)
    return hi, mid, lo


def _bias_build(rb):
    rbp = jnp.pad(rb, ((0, 0), (0, 3 * LANES - N_REL)))

    def body(rb_ref, o_ref):
        e = _rel_index(pl.program_id(0))
        hi, mid, lo = _split3(rb_ref[...])
        o_ref[...] = _dot(hi, e) + _dot(mid, e) + _dot(lo, e)

    return pl.pallas_call(
        body, grid=(CHUNK,), in_specs=[pl.BlockSpec((ATT_HEADS, 3 * LANES), lambda i: (0, 0))],
        out_specs=pl.BlockSpec((None, ATT_HEADS, 5 * LANES), lambda i: (i, 0, 0)),
        out_shape=S((CHUNK, ATT_HEADS, 5 * LANES), F32), name="bias_build",
        compiler_params=_cparams("parallel"))(rbp)


ATT_RB = 512
ATT_QT = 256
ATT_CPT = ATT_QT // CHUNK
ATT_KT = ATT_QT + PADK
ATT_QCOL = (2 * RET_HEADS * RET_QK + 2 * RET_HEADS * RET_V) // LANES
ATT_KCOL = ATT_QCOL + ATT_HEADS * ATT_D // LANES
ATT_VCOL = ATT_KCOL + ATT_HEADS * ATT_D // LANES


def _bias_grad(dbt):
    def body(d_ref, o_ref):
        @pl.when((pl.program_id(0) == 0) & (pl.program_id(1) == 0))
        def _():
            o_ref[...] = jnp.zeros_like(o_ref)

        e = _rel_index(pl.program_id(0))
        hi, mid, lo = _split3(d_ref[...])
        o_ref[...] += _dot_nt(hi, e) + _dot_nt(mid, e) + _dot_nt(lo, e)

    return pl.pallas_call(
        body, grid=(CHUNK, ATT_CPT),
        in_specs=[pl.BlockSpec((None, None, ATT_HEADS, 5 * LANES), lambda i, c: (c, i, 0, 0))],
        out_specs=pl.BlockSpec((ATT_HEADS, 3 * LANES), lambda i, c: (0, 0)),
        out_shape=S((ATT_HEADS, 3 * LANES), F32), name="bias_grad",
        compiler_params=_cparams("arbitrary", "arbitrary"))(dbt)


def _bias_tiles(bias):
    parts = [jnp.pad(bias, ((0, 0), (0, 0), (CHUNK * ci, ATT_KT - BAND - CHUNK * ci)), constant_values=NEG_INF)
             for ci in range(ATT_CPT)]
    return jnp.stack(parts, axis=1).reshape(ATT_HEADS, ATT_QT, ATT_KT)


def _bias_bands(dbias_tiles):
    d = dbias_tiles.reshape(ATT_HEADS, ATT_CPT, CHUNK, ATT_KT)
    bands = jnp.stack([d[:, ci, :, CHUNK * ci:CHUNK * ci + BAND] for ci in range(ATT_CPT)])
    return jnp.pad(jnp.transpose(bands, (0, 2, 1, 3)), ((0, 0), (0, 0), (0, 0), (0, 5 * LANES - BAND)))


def _att_fill(kw, vw, klo, khi, vlo, vhi):
    kw[0:ATT_RB, :] = klo[...].astype(BF16)
    kw[ATT_RB:, :] = khi[...].astype(BF16)
    vw[0:ATT_RB, :] = vlo[...].astype(BF16)
    vw[ATT_RB:, :] = vhi[...].astype(BF16)


def _att_probs(qm, kwin, bias, first_key):
    s = _dot_nt(qm, kwin) * ATT_SCALE + bias
    col = lax.broadcasted_iota(jnp.int32, (ATT_QT, ATT_KT), 1)
    s = jnp.where(col + first_key >= 0, s, NEG_INF)
    p = jnp.exp(s - jnp.max(s, axis=-1, keepdims=True))
    return p / jnp.sum(p, axis=-1, keepdims=True)


def _att_in_specs(nq):
    qn = lambda n: jnp.minimum(n, nq - 1)
    blk = lambda col, back: pl.BlockSpec((ATT_RB, LANES), lambda hp, n: (jnp.maximum(qn(n) - back, 0), col + hp))
    return [blk(ATT_QCOL, 0), blk(ATT_KCOL, 1), blk(ATT_KCOL, 0), blk(ATT_VCOL, 1), blk(ATT_VCOL, 0),
            pl.BlockSpec((2, ATT_QT, ATT_KT), lambda hp, n: (hp, 0, 0))]


def _attention_fwd(z, bias_t):
    T = z.shape[0]
    nq = T // ATT_RB

    def body(q_ref, klo, khi, vlo, vhi, b_ref, o_ref, kw, vw):
        _att_fill(kw, vw, klo, khi, vlo, vhi)
        lane = lax.broadcasted_iota(jnp.int32, (ATT_QT, LANES), 1)
        n = pl.program_id(1)
        for t in range(ATT_RB // ATT_QT):
            rows = slice(t * ATT_QT, (t + 1) * ATT_QT)
            win = slice(t * ATT_QT, t * ATT_QT + ATT_KT)
            qc = q_ref[rows, :]
            outs = []
            for e in range(2):
                qm = jnp.where((lane >= ATT_D) == (e == 1), qc, 0.0).astype(BF16)
                p = _att_probs(qm, kw[win, :], b_ref[e], (n - 1) * ATT_RB + t * ATT_QT)
                outs.append(_dot(p.astype(BF16), vw[win, :]))
            o_ref[rows, :] = jnp.where(lane < ATT_D, outs[0], outs[1]).astype(BF16)

    return pl.pallas_call(
        body, grid=(ATT_HEADS // 2, nq), in_specs=_att_in_specs(nq),
        out_specs=pl.BlockSpec((ATT_RB, LANES), lambda hp, n: (n, hp)),
        out_shape=S((T, ATT_HEADS * ATT_D), BF16),
        scratch_shapes=[pltpu.VMEM((2 * ATT_RB, LANES), BF16), pltpu.VMEM((2 * ATT_RB, LANES), BF16)],
        name="attention_fwd", compiler_params=_cparams("parallel", "parallel"))(z, z, z, z, z, bias_t)


def _attention_bwd(z, bias_t, dycat):
    T = z.shape[0]
    nq = T // ATT_RB
    dycol = RET_HEADS * RET_V // LANES

    def body(q_ref, klo, khi, vlo, vhi, b_ref, dy_ref, dq_ref, dk_ref, dv_ref, db_ref, kw, vw, dkw, dvw):
        n = pl.program_id(1)

        @pl.when(n == 0)
        def _():
            dkw[...] = jnp.zeros_like(dkw)
            dvw[...] = jnp.zeros_like(dvw)
            db_ref[...] = jnp.zeros_like(db_ref)

        @pl.when(n > 0)
        def _():
            dkw[0:ATT_RB, :] = dkw[ATT_RB:, :]
            dvw[0:ATT_RB, :] = dvw[ATT_RB:, :]
            dkw[ATT_RB:, :] = jnp.zeros((ATT_RB, LANES), F32)
            dvw[ATT_RB:, :] = jnp.zeros((ATT_RB, LANES), F32)

        @pl.when(n < nq)
        def _():
            _att_fill(kw, vw, klo, khi, vlo, vhi)
            lane = lax.broadcasted_iota(jnp.int32, (ATT_QT, LANES), 1)
            for t in range(ATT_RB // ATT_QT):
                rows = slice(t * ATT_QT, (t + 1) * ATT_QT)
                win = slice(t * ATT_QT, t * ATT_QT + ATT_KT)
                qc, dyc = q_ref[rows, :], dy_ref[rows, :]
                kwin, vwin = kw[win, :], vw[win, :]
                dq = jnp.zeros((ATT_QT, LANES), F32)
                for e in range(2):
                    mine = (lane >= ATT_D) == (e == 1)
                    qm = jnp.where(mine, qc, 0.0).astype(BF16)
                    dom = jnp.where(mine, dyc, 0.0).astype(BF16)
                    p = _att_probs(qm, kwin, b_ref[e], (n - 1) * ATT_RB + t * ATT_QT)
                    dp = _dot_nt(dom, vwin)
                    ds = p * (dp - jnp.sum(dp * p, axis=-1, keepdims=True))
                    db_ref[e] += ds
                    dsb = (ds * ATT_SCALE).astype(BF16)
                    dq = dq + jnp.where(mine, _dot(dsb, kwin), 0.0)
                    dkw[win, :] += _dot_tn(dsb, qm)
                    dvw[win, :] += _dot_tn(p.astype(BF16), dom)
                dq_ref[rows, :] = dq.astype(BF16)

        dk_ref[...] = dkw[0:ATT_RB, :].astype(BF16)
        dv_ref[...] = dvw[0:ATT_RB, :].astype(BF16)

    qn = lambda n: jnp.minimum(n, nq - 1)
    out_kv = pl.BlockSpec((ATT_RB, LANES), lambda hp, n: (jnp.maximum(n - 1, 0), hp))
    return pl.pallas_call(
        body, grid=(ATT_HEADS // 2, nq + 1),
        in_specs=_att_in_specs(nq) + [pl.BlockSpec((ATT_RB, LANES), lambda hp, n: (qn(n), dycol + hp))],
        out_specs=[pl.BlockSpec((ATT_RB, LANES), lambda hp, n: (qn(n), hp)), out_kv, out_kv,
                   pl.BlockSpec((2, ATT_QT, ATT_KT), lambda hp, n: (hp, 0, 0))],
        out_shape=[S((T, ATT_HEADS * ATT_D), BF16), S((T, ATT_HEADS * ATT_D), BF16),
                   S((T, ATT_HEADS * ATT_D), BF16), S((ATT_HEADS, ATT_QT, ATT_KT), F32)],
        scratch_shapes=[pltpu.VMEM((2 * ATT_RB, LANES), BF16), pltpu.VMEM((2 * ATT_RB, LANES), BF16),
                        pltpu.VMEM((2 * ATT_RB, LANES), F32), pltpu.VMEM((2 * ATT_RB, LANES), F32)],
        name="attention_bwd", compiler_params=_cparams("parallel", "arbitrary"))(z, z, z, z, z, bias_t, dycat)


HALO = 8


def _conv_specs(tb, tc, nct, T):
    per = tb // HALO
    last = T // HALO - 1

    def at(half):
        off = half * nct
        return [pl.BlockSpec((HALO, tc), lambda j, i: (jnp.maximum(i * per - 1, 0), j + off)),
                pl.BlockSpec((tb, tc), lambda j, i: (i, j + off)),
                pl.BlockSpec((HALO, tc), lambda j, i: (jnp.minimum((i + 1) * per, last), j + off))]

    return at(0), at(1)


def _causal_conv(ext, w_ref, b_ref):
    zc = w_ref[0:1, :] * pltpu.roll(ext, 2, 0) + w_ref[1:2, :] * pltpu.roll(ext, 1, 0) + w_ref[2:3, :] * ext + b_ref[...]
    return zc[HALO:]


def _convglu_fwd(z, cw, cb, name):
    T = z.shape[0]
    tb, tc = _tile(T, 1024, 8), 256
    nct = FFN_HIDDEN // tc

    def body(gp_ref, g_ref, up_ref, u_ref, wg_ref, wu_ref, bg_ref, bu_ref, o_ref):
        first = pl.program_id(1) == 0

        def conv(p_ref, blk_ref, w_ref, b_ref):
            prev = jnp.where(first, 0.0, p_ref[...])
            return _causal_conv(jnp.concatenate([prev, blk_ref[...]], axis=0), w_ref, b_ref)

        o_ref[...] = (_gelu(conv(gp_ref, g_ref, wg_ref, bg_ref)) * conv(up_ref, u_ref, wu_ref, bu_ref)).astype(BF16)

    (gp, gb, _), (up, ub, _) = _conv_specs(tb, tc, nct, T)
    wspec = lambda off, rows: pl.BlockSpec((rows, tc), lambda j, i: (0, j + off))
    return pl.pallas_call(
        body, grid=(nct, T // tb),
        in_specs=[gp, gb, up, ub, wspec(0, 3), wspec(nct, 3), wspec(0, 1), wspec(nct, 1)],
        out_specs=pl.BlockSpec((tb, tc), lambda j, i: (i, j)),
        out_shape=S((T, FFN_HIDDEN), BF16), name=name,
        compiler_params=_cparams("parallel", "parallel"))(z, z, z, z, cw, cw, cb, cb)


def _convglu_bwd(z, df, cw, cb, name):
    T = z.shape[0]
    tb, tc = _tile(T, 1024, 8), 256
    nct = FFN_HIDDEN // tc
    nrb = T // tb

    def body(gp_ref, g_ref, gn_ref, up_ref, u_ref, un_ref, df_ref, dfn_ref, wg_ref, wu_ref, bg_ref, bu_ref,
             dzg_ref, dzu_ref, dwg_ref, dwu_ref, dbg_ref, dbu_ref):
        i = pl.program_id(1)
        first, last = i == 0, i == nrb - 1

        @pl.when(first)
        def _():
            for ref in (dwg_ref, dwu_ref, dbg_ref, dbu_ref):
                ref[...] = jnp.zeros_like(ref)

        def ext_of(p_ref, blk_ref, n_ref):
            return jnp.concatenate([jnp.where(first, 0.0, p_ref[...]), blk_ref[...], n_ref[...]], axis=0)

        gext, uext = ext_of(gp_ref, g_ref, gn_ref), ext_of(up_ref, u_ref, un_ref)
        gc, uc = _causal_conv(gext, wg_ref, bg_ref), _causal_conv(uext, wu_ref, bu_ref)
        dfe = jnp.concatenate([df_ref[...], jnp.where(last, 0.0, dfn_ref[...])], axis=0)
        ge, gd = _gelu_and_grad(gc)
        dgc, duc = dfe * uc * gd, dfe * ge
        n = tb + HALO

        def back(d, ext, w_ref, dz_ref, dw_ref, db_ref):
            dz = w_ref[2:3, :] * d + w_ref[1:2, :] * pltpu.roll(d, n - 1, 0) + w_ref[0:1, :] * pltpu.roll(d, n - 2, 0)
            dz_ref[...] = dz[:tb].astype(BF16)
            dblk = d[:tb]
            db_ref[...] += jnp.sum(dblk, axis=0, keepdims=True)
            taps = [pltpu.roll(ext, 2, 0)[HALO:HALO + tb], pltpu.roll(ext, 1, 0)[HALO:HALO + tb], ext[HALO:HALO + tb]]
            for k, t in enumerate(taps):
                dw_ref[k:k + 1, :] += jnp.sum(dblk * t, axis=0, keepdims=True)

        back(dgc, gext, wg_ref, dzg_ref, dwg_ref, dbg_ref)
        back(duc, uext, wu_ref, dzu_ref, dwu_ref, dbu_ref)

    gspecs, uspecs = _conv_specs(tb, tc, nct, T)
    per = tb // HALO
    dfs = [pl.BlockSpec((tb, tc), lambda j, i: (i, j)),
           pl.BlockSpec((HALO, tc), lambda j, i: (jnp.minimum((i + 1) * per, T // HALO - 1), j))]
    wspec = lambda off, rows: pl.BlockSpec((rows, tc), lambda j, i: (0, j + off))
    acc = lambda rows: pl.BlockSpec((rows, tc), lambda j, i: (0, j))
    blk = pl.BlockSpec((tb, tc), lambda j, i: (i, j))
    return pl.pallas_call(
        body, grid=(nct, nrb),
        in_specs=gspecs + uspecs + dfs + [wspec(0, 3), wspec(nct, 3), wspec(0, 1), wspec(nct, 1)],
        out_specs=[blk, blk, acc(3), acc(3), acc(1), acc(1)],
        out_shape=[S((T, FFN_HIDDEN), BF16), S((T, FFN_HIDDEN), BF16), S((3, FFN_HIDDEN), F32), S((3, FFN_HIDDEN), F32),
                   S((1, FFN_HIDDEN), F32), S((1, FFN_HIDDEN), F32)],
        name=name, compiler_params=_cparams("parallel", "arbitrary"))(z, z, z, z, z, z, df, df, cw, cw, cb, cb)


SGU_RB = 256


def _sgu_weights(ws_ref):
    i = lax.broadcasted_iota(jnp.int32, (SGU_BLOCK, SGU_BLOCK), 0)
    j = lax.broadcasted_iota(jnp.int32, (SGU_BLOCK, SGU_BLOCK), 1)
    mask = (j < CHUNK) | (i >= CHUNK)
    return mask, [jnp.where(mask, ws_ref[g], 0.0).astype(BF16) for g in range(SGU_GROUPS)]


def _sgu_norm(zv, lng, lnb):
    mu = jnp.mean(zv, axis=-1, keepdims=True)
    vc = zv - mu
    rstd = lax.rsqrt(jnp.mean(vc * vc, axis=-1, keepdims=True) + EPS)
    vh = vc * rstd
    return vh, rstd, vh * lng + lnb


def _sgu_fwd(zpre, lng, lnb, ws, bst):
    T = zpre.shape[0]
    nb = SGU_RB // SGU_BLOCK

    def body(z_ref, lng_ref, lnb_ref, ws_ref, bst_ref, o_ref):
        _, wm = _sgu_weights(ws_ref)
        u = _gelu(z_ref[:, :SGU_WIDTH])
        _, _, vn = _sgu_norm(_gelu(z_ref[:, SGU_WIDTH:]), lng_ref[...], lnb_ref[...])
        vnb = vn.astype(BF16)
        for b in range(nb):
            rows = slice(b * SGU_BLOCK, (b + 1) * SGU_BLOCK)
            for g in range(SGU_GROUPS):
                cols = slice(g * SGU_GW, (g + 1) * SGU_GW)
                mixed = _dot(wm[g], vnb[rows, cols]) + bst_ref[:, g:g + 1]
                o_ref[rows, cols] = (u[rows, cols] * mixed).astype(BF16)

    vec = pl.BlockSpec((1, SGU_WIDTH), lambda i: (0, 0))
    return pl.pallas_call(
        body, grid=(T // SGU_RB,),
        in_specs=[pl.BlockSpec((SGU_RB, 2 * SGU_WIDTH), lambda i: (i, 0)), vec, vec,
                  pl.BlockSpec((SGU_GROUPS, SGU_BLOCK, SGU_BLOCK), lambda i: (0, 0, 0)),
                  pl.BlockSpec((SGU_BLOCK, SGU_GROUPS), lambda i: (0, 0))],
        out_specs=pl.BlockSpec((SGU_RB, SGU_WIDTH), lambda i: (i, 0)),
        out_shape=S((T, SGU_WIDTH), BF16), name="sgu_fwd", compiler_params=_cparams("parallel"))(zpre, lng, lnb, ws, bst)


def _sgu_bwd(zpre, dy, lng, lnb, ws, bst):
    T = zpre.shape[0]
    nb = SGU_RB // SGU_BLOCK

    def body(z_ref, dy_ref, lng_ref, lnb_ref, ws_ref, bst_ref, dz_ref, dws_ref, dbst_ref, dlng_ref, dlnb_ref, dvn):
        @pl.when(pl.program_id(0) == 0)
        def _():
            for ref in (dws_ref, dbst_ref, dlng_ref, dlnb_ref):
                ref[...] = jnp.zeros_like(ref)

        mask, wm = _sgu_weights(ws_ref)
        u, ud = _gelu_and_grad(z_ref[:, :SGU_WIDTH])
        v, vd = _gelu_and_grad(z_ref[:, SGU_WIDTH:])
        vh, rstd, vn = _sgu_norm(v, lng_ref[...], lnb_ref[...])
        vnb = vn.astype(BF16)
        lane8 = lax.broadcasted_iota(jnp.int32, (SGU_BLOCK, SGU_GROUPS), 1)
        dbs = jnp.zeros((SGU_BLOCK, SGU_GROUPS), F32)
        for b in range(nb):
            rows = slice(b * SGU_BLOCK, (b + 1) * SGU_BLOCK)
            for g in range(SGU_GROUPS):
                cols = slice(g * SGU_GW, (g + 1) * SGU_GW)
                vg = vnb[rows, cols]
                mixed = _dot(wm[g], vg) + bst_ref[:, g:g + 1]
                dyv = dy_ref[rows, cols]
                dz_ref[rows, cols] = (dyv * mixed * ud[rows, cols]).astype(BF16)
                dmix = dyv * u[rows, cols]
                dmb = dmix.astype(BF16)
                dvn[rows, cols] = _dot_tn(wm[g], dmb)
                dws_ref[g] += jnp.where(mask, _dot_nt(dmb, vg), 0.0)
                dbs = dbs + jnp.where(lane8 == g, jnp.sum(dmix, axis=-1, keepdims=True), 0.0)
        dbst_ref[...] += dbs
        dvnv = dvn[...]
        dlng_ref[...] += jnp.sum(dvnv * vh, axis=0, keepdims=True)
        dlnb_ref[...] += jnp.sum(dvnv, axis=0, keepdims=True)
        dvh = dvnv * lng_ref[...]
        dv = rstd * (dvh - jnp.mean(dvh, axis=-1, keepdims=True) - vh * jnp.mean(dvh * vh, axis=-1, keepdims=True))
        dz_ref[:, SGU_WIDTH:] = (dv * vd).astype(BF16)

    vec = pl.BlockSpec((1, SGU_WIDTH), lambda i: (0, 0))
    wsp = pl.BlockSpec((SGU_GROUPS, SGU_BLOCK, SGU_BLOCK), lambda i: (0, 0, 0))
    bsp = pl.BlockSpec((SGU_BLOCK, SGU_GROUPS), lambda i: (0, 0))
    return pl.pallas_call(
        body, grid=(T // SGU_RB,),
        in_specs=[pl.BlockSpec((SGU_RB, 2 * SGU_WIDTH), lambda i: (i, 0)),
                  pl.BlockSpec((SGU_RB, SGU_WIDTH), lambda i: (i, 0)), vec, vec, wsp, bsp],
        out_specs=[pl.BlockSpec((SGU_RB, 2 * SGU_WIDTH), lambda i: (i, 0)), wsp, bsp, vec, vec],
        out_shape=[S((T, 2 * SGU_WIDTH), BF16), S((SGU_GROUPS, SGU_BLOCK, SGU_BLOCK), F32),
                   S((SGU_BLOCK, SGU_GROUPS), F32), S((1, SGU_WIDTH), F32), S((1, SGU_WIDTH), F32)],
        scratch_shapes=[pltpu.VMEM((SGU_RB, SGU_WIDTH), F32)], name="sgu_bwd",
        compiler_params=_cparams("arbitrary"))(zpre, dy, lng, lnb, ws, bst)


def _loss_head(h, tgt, g):
    T, D = h.shape
    tr = _tile(T, 512, 8)

    def body(h_ref, t_ref, g_ref, ls_ref, dh_ref, dhb_ref, dg_ref):
        @pl.when(pl.program_id(0) == 0)
        def _():
            ls_ref[...] = jnp.zeros_like(ls_ref)
            dg_ref[...] = jnp.zeros_like(dg_ref)

        hv = h_ref[...]
        r = lax.rsqrt(jnp.mean(hv * hv, axis=-1, keepdims=True) + EPS)
        xh = hv * r
        diff = xh * g_ref[...] - t_ref[...]
        per_row = jnp.mean(diff * diff, axis=-1, keepdims=True)
        ls_ref[...] += jnp.sum(per_row, axis=0, keepdims=True)
        dy = diff * (1.0 / D)
        dg_ref[...] += jnp.sum(dy * xh, axis=0, keepdims=True)
        dxh = dy * g_ref[...]
        dh = r * (dxh - xh * jnp.mean(dxh * xh, axis=-1, keepdims=True))
        dh_ref[...] = dh
        dhb_ref[...] = dh.astype(BF16)

    row = pl.BlockSpec((tr, D), lambda i: (i, 0))
    vec = pl.BlockSpec((1, D), lambda i: (0, 0))
    return pl.pallas_call(
        body, grid=(T // tr,), in_specs=[row, row, vec],
        out_specs=[pl.BlockSpec((1, LANES), lambda i: (0, 0)), row, row, vec],
        out_shape=[S((1, LANES), F32), S((T, D), F32), S((T, D), BF16), S((1, D), F32)],
        name="loss_head", compiler_params=_cparams("arbitrary"))(h, tgt, g)


ANY = pl.BlockSpec(memory_space=pl.ANY)
SWAP_PARTS = 8


def _place():
    return lax.axis_index("x"), lax.axis_index("y"), lax.axis_index("c")


def _gather_weights(p):
    R = p.shape[0]
    Rh = R // 2

    def body(p_ref, out_ref, send_sems, recv_sems, local_sem):
        x, y, c = _place()
        sibling = (x, y, 1 - c)
        chips = [(1 - x, y), (x, 1 - y), (1 - x, 1 - y)]

        def half(px, py, pc):
            return out_ref.at[2 * px + py, pl.ds(pl.multiple_of(pc * Rh, 16), Rh), :]

        def copy(k, block, to, src=None):
            return pltpu.make_async_remote_copy(
                src_ref=half(*block) if src is None else src, dst_ref=half(*block),
                send_sem=send_sems.at[k], recv_sem=recv_sems.at[k], device_id=to, device_id_type=MESH)

        mine = pltpu.make_async_copy(p_ref, out_ref.at[2 * x + y], local_sem)
        mine.start()
        my_half = p_ref.at[pl.ds(pl.multiple_of(c * Rh, 16), Rh), :]
        first = [copy(j, (x, y, c), (*chip, c), src=my_half) for j, chip in enumerate(chips)]
        for cp in first:
            cp.start()
        passed = [copy(3 + j, (*chip, c), sibling) for j, chip in enumerate(chips)]
        for j, chip in enumerate(chips):
            copy(j, (*chip, c), (x, y, c)).wait_recv()
            passed[j].start()
        for j, chip in enumerate(chips):
            copy(3 + j, (*chip, 1 - c), (x, y, c)).wait_recv()
        for cp in first + passed:
            cp.wait_send()
        mine.wait()

    return pl.pallas_call(
        body, out_shape=S((N_CHIPS, R, PACK_W), p.dtype), in_specs=[ANY], out_specs=ANY,
        scratch_shapes=[pltpu.SemaphoreType.DMA((6,)), pltpu.SemaphoreType.DMA((6,)), pltpu.SemaphoreType.DMA],
        name="gather_weights")(p)


def _exchange_all(src, per_target, name):
    Rh = src.shape[1] // 2 if per_target else src.shape[0]

    def body(src_ref, out_ref, send_sems, recv_sems, local_sem):
        x, y, c = _place()

        def ident(px, py, pc):
            return 4 * px + 2 * py + pc

        def block_for(px, py, pc):
            if per_target:
                return src_ref.at[2 * px + py, pl.ds(pl.multiple_of(pc * Rh, 16), Rh), :]
            return src_ref

        mine = pltpu.make_async_copy(block_for(x, y, c), out_ref.at[ident(x, y, c)], local_sem)
        mine.start()
        peers = []
        for k in range(1, N_DEV):
            fx, fy, fc = (k >> 2) & 1, (k >> 1) & 1, k & 1
            peers.append((k, (x ^ fx, y ^ fy, c ^ fc)))
        copies = [pltpu.make_async_remote_copy(
            src_ref=block_for(*peer), dst_ref=out_ref.at[ident(x, y, c)], send_sem=send_sems.at[k - 1],
            recv_sem=recv_sems.at[k - 1], device_id=peer, device_id_type=MESH) for k, peer in peers]
        for cp in copies:
            cp.start()
        for k, peer in peers:
            pltpu.make_async_remote_copy(
                src_ref=block_for(*peer), dst_ref=out_ref.at[ident(*peer)], send_sem=send_sems.at[k - 1],
                recv_sem=recv_sems.at[k - 1], device_id=peer, device_id_type=MESH).wait_recv()
        for cp in copies:
            cp.wait_send()
        mine.wait()

    return pl.pallas_call(
        body, out_shape=S((N_DEV, Rh, src.shape[-1]), src.dtype), in_specs=[ANY], out_specs=ANY,
        scratch_shapes=[pltpu.SemaphoreType.DMA((N_DEV - 1,)), pltpu.SemaphoreType.DMA((N_DEV - 1,)),
                        pltpu.SemaphoreType.DMA], name=name)(src)


def _exchange_halves(h):
    Rh, W = h.shape
    nparts = SWAP_PARTS if Rh % (8 * SWAP_PARTS) == 0 else 1
    rp = Rh // nparts

    def body(h_ref, out_ref, send_sems, recv_sems, local_sem):
        x, y, c = _place()

        def rows(pc, k):
            return out_ref.at[pl.ds(pl.multiple_of(pc * Rh + k * rp, 8), rp), :]

        def part(k, half):
            return pltpu.make_async_remote_copy(
                src_ref=h_ref.at[pl.ds(k * rp, rp), :], dst_ref=rows(half, k), send_sem=send_sems.at[k],
                recv_sem=recv_sems.at[k], device_id=(x, y, 1 - c), device_id_type=MESH)

        mine = pltpu.make_async_copy(h_ref, out_ref.at[pl.ds(pl.multiple_of(c * Rh, 16), Rh), :], local_sem)
        mine.start()
        for k in range(nparts):
            part(k, c).start()
        for k in range(nparts):
            part(k, 1 - c).wait_recv()
        for k in range(nparts):
            part(k, c).wait_send()
        mine.wait()

    return pl.pallas_call(
        body, out_shape=S((2 * Rh, W), h.dtype), in_specs=[ANY], out_specs=ANY,
        scratch_shapes=[pltpu.SemaphoreType.DMA((nparts,)), pltpu.SemaphoreType.DMA((nparts,)), pltpu.SemaphoreType.DMA],
        name="exchange_halves")(h)


def _sum_slots(buf, name):
    n, R, W = buf.shape
    tr = _tile(R, 256, 8)

    def body(b_ref, o_ref):
        acc = b_ref[0].astype(F32)
        for s in range(1, n):
            acc = acc + b_ref[s].astype(F32)
        o_ref[...] = acc

    return pl.pallas_call(
        body, grid=(R // tr,), in_specs=[pl.BlockSpec((n, tr, W), lambda i: (0, i, 0))],
        out_specs=pl.BlockSpec((tr, W), lambda i: (i, 0)), out_shape=S((R, W), F32), name=name,
        compiler_params=_cparams("parallel"))(buf)


def _adamw(w, g, m, v, name):
    R, W = w.shape
    tr = _tile(R, 256, 8)

    def body(w_ref, g_ref, m_ref, v_ref, d_ref, mo_ref, vo_ref):
        gv = g_ref[...]
        mn = ADAM_B1 * m_ref[...] + (1.0 - ADAM_B1) * gv
        vn = ADAM_B2 * v_ref[...] + (1.0 - ADAM_B2) * (gv * gv)
        m_hat = mn / (1.0 - ADAM_B1 ** ADAM_STEP)
        v_hat = vn / (1.0 - ADAM_B2 ** ADAM_STEP)
        d_ref[...] = -ADAM_LR * (m_hat / (jnp.sqrt(v_hat) + ADAM_EPS) + ADAM_WD * w_ref[...])
        mo_ref[...] = mn
        vo_ref[...] = vn

    blk = pl.BlockSpec((tr, W), lambda i: (i, 0))
    return pl.pallas_call(
        body, grid=(R // tr,), in_specs=[blk] * 4, out_specs=[blk] * 3, out_shape=[S((R, W), F32)] * 3, name=name,
        compiler_params=_cparams("parallel"))(w, g, m, v)


def _tables(T):
    f32 = F32
    half = RET_QK // 2
    inv = 1.0 / (10000.0 ** jnp.linspace(0.0, 1.0, half, dtype=f32))
    ang = jnp.arange(T).astype(f32)[:, None] * inv[None, :]
    cos, sin = jnp.cos(ang), jnp.sin(ang)
    c2 = jnp.concatenate([cos, cos], axis=-1)
    s2 = jnp.concatenate([-sin, sin], axis=-1)
    log_g = jnp.log1p(-jnp.exp2(-5.0 - jnp.arange(RET_HEADS, dtype=f32)))
    idx = jnp.arange(CHUNK, dtype=f32)
    dintra = jnp.exp(log_g[:, None, None] * jnp.abs(idx[:, None] - idx[None, :]))
    kdec = jnp.exp(log_g[None, :] * (CHUNK - 1 - idx)[:, None]).T
    qdec = jnp.exp(log_g[None, :] * (idx + 1.0)[:, None]).T
    cdec = jnp.exp(log_g * CHUNK)
    bc = lambda a, w: jnp.broadcast_to(a[:, :, None], (RET_HEADS, a.shape[1], w))
    return c2, s2, dintra, bc(qdec, RET_QK), bc(kdec, RET_QK), jnp.broadcast_to(cdec[:, None, None], (RET_HEADS, 1, RET_V))


def _local_step(x, tgt, p):
    T = x.shape[0]
    tab = _tables(T)
    row = lambda a: a.reshape(1, -1)
    tr = lambda w: jnp.transpose(w)

    hn0 = _rmsnorm_fwd(x, row(p["attn_norm_g"][0]), "norm_a0")
    z0 = _mm(hn0, p["ab_w_in"][0], "mm_ab_in")
    ya, r, st = _retention_fwd(z0, *tab)
    bias_t = _bias_tiles(jnp.transpose(_bias_build(p["ab_rel_bias"][0]), (1, 0, 2))[:, :, :BAND])
    yb = _attention_fwd(z0, bias_t)
    h1 = _mm([ya, yb], p["ab_w_out"][0], "mm_ab_out", res=x)

    def ffn_fwd(h, l):
        hf = _rmsnorm_fwd(h, row(p["ffn_norm_g"][l]), f"norm_f{l}")
        zf = _mm(hf, p["ffn_w_up"][l], f"mm_up{l}")
        f = _convglu_fwd(zf, p["ffn_conv_w"][l], row(p["ffn_conv_b"][l]), f"convglu_fwd{l}")
        return hf, zf, f, _mm(f, p["ffn_w_down"][l], f"mm_down{l}", res=h)

    hf0, zf0, f0, h2 = ffn_fwd(h1, 0)
    hn1 = _rmsnorm_fwd(h2, row(p["attn_norm_g"][1]), "norm_a1")
    zc = _mm(hn1, p["c_w_in"][0], "mm_c_in")
    lng, lnb, bst, ws = row(p["c_ln_g"][0]), row(p["c_ln_b"][0]), tr(p["c_b_s"][0]), p["c_w_s"][0]
    y1 = _sgu_fwd(zc, lng, lnb, ws, bst)
    h3 = _mm(y1, p["c_w_out"][0], "mm_c_out", res=h2)
    hf1, zf1, f1, h4 = ffn_fwd(h3, 1)
    lsum, dh4, dh4b, dgfin = _loss_head(h4, tgt, row(p["final_norm_g"]))

    g = {}

    def ffn_bwd(dh, dhb, h_in, hf, zf, f, l):
        d_down = _mm_tn(f, dhb, f"mmt_down{l}")
        df = _mm(dhb, tr(p["ffn_w_down"][l]), f"mmb_down{l}")
        dzg, dzu, dwg, dwu, dbg, dbu = _convglu_bwd(zf, df, p["ffn_conv_w"][l], row(p["ffn_conv_b"][l]), f"convglu_bwd{l}")
        d_up = _mm_tn(hf, [dzg, dzu], f"mmt_up{l}")
        dhf = _mm([dzg, dzu], tr(p["ffn_w_up"][l]), f"mmb_up{l}")
        dh_in, dh_in_b, dgf = _rmsnorm_bwd(h_in, dhf, row(p["ffn_norm_g"][l]), dh, f"norm_f{l}_bwd")
        return dh_in, dh_in_b, dict(ffn_w_down=d_down, ffn_w_up=d_up, ffn_norm_g=dgf[0],
                                    ffn_conv_w=jnp.concatenate([dwg, dwu], axis=1), ffn_conv_b=jnp.concatenate([dbg, dbu], axis=1)[0])

    dh3, dh3b, gf1 = ffn_bwd(dh4, dh4b, h3, hf1, zf1, f1, 1)
    g["c_w_out"] = _mm_tn(y1, dh3b, "mmt_c_out")[None]
    dy1 = _mm(dh3b, tr(p["c_w_out"][0]), "mmb_c_out")
    dzc, dws, dbst, dlng, dlnb = _sgu_bwd(zc, dy1, lng, lnb, ws, bst)
    g["c_w_s"], g["c_b_s"], g["c_ln_g"], g["c_ln_b"] = dws[None], tr(dbst)[None], dlng, dlnb
    g["c_w_in"] = _mm_tn(hn1, dzc, "mmt_c_in")[None]
    dhn1 = _mm(dzc, tr(p["c_w_in"][0]), "mmb_c_in")
    dh2, dh2b, dga1 = _rmsnorm_bwd(h2, dhn1, row(p["attn_norm_g"][1]), dh3, "norm_a1_bwd")
    dh1, dh1b, gf0 = ffn_bwd(dh2, dh2b, h1, hf0, zf0, f0, 0)
    g["ab_w_out"] = _mm_tn([ya, yb], dh1b, "mmt_ab_out")[None]
    dycat = _mm(dh1b, tr(p["ab_w_out"][0]), "mmb_ab_out")
    dqb, dkb, dvb, dbias_t = _attention_bwd(z0, bias_t, dycat)
    dqa, dka, dva, dga = _retention_bwd(z0, *tab, r, dycat, st)
    dz0 = [dqa, dka, dva, dga, dqb, dkb, dvb]
    g["ab_w_in"] = _mm_tn(hn0, dz0, "mmt_ab_in")[None]
    dhn0 = _mm(dz0, tr(p["ab_w_in"][0]), "mmb_ab_in")
    gx, _, dga0 = _rmsnorm_bwd(x, dhn0, row(p["attn_norm_g"][0]), dh1, "norm_a0_bwd")
    g["ab_rel_bias"] = _bias_grad(_bias_bands(dbias_t))[None, :, :N_REL]

    g["attn_norm_g"] = jnp.stack([dga0[0], dga1[0]])
    g["final_norm_g"] = dgfin[0]
    for k in gf0:
        g[k] = jnp.stack([gf0[k], gf1[k]])
    return lsum[0, 0], gx, g


BIG = [("ab_w_in", 2), ("ab_w_out", 1), ("c_w_in", 2), ("c_w_out", 1), ("ffn_w_up", 2), ("ffn_w_down", 1)]
SMALL_SHARDED = [("c_ln_g", 1), ("c_ln_b", 1), ("ffn_conv_w", 2)]
REPLICATED = ["attn_norm_g", "ffn_norm_g", "ab_rel_bias", "c_w_s", "c_b_s", "ffn_conv_b", "final_norm_g"]


def _rows_of(n_elems):
    return -(-n_elems // PACK_W)


def _flat_rows(a):
    f = a.reshape(-1)
    rows = _rows_of(f.shape[0])
    return jnp.pad(f, (0, rows * PACK_W - f.shape[0])).reshape(rows, PACK_W)


def _pad_rows(a, mult):
    extra = (-a.shape[0]) % mult
    return jnp.pad(a, ((0, extra), (0, 0))) if extra else a


def _pack(arrs, mult):
    return _pad_rows(jnp.concatenate([_flat_rows(a) for a in arrs], axis=0), mult)


def _unpack(buf, shapes):
    out, r = [], 0
    for shp in shapes:
        n = math.prod(shp)
        rows = _rows_of(n)
        out.append(buf[r:r + rows].reshape(-1)[:n].reshape(shp))
        r += rows
    return out


def _shard_major(full, axis):
    shp = full.shape
    split = full.reshape(shp[:axis] + (N_CHIPS, shp[axis] // N_CHIPS) + shp[axis + 1:])
    return jnp.moveaxis(split, axis, 0)


def _from_shards(sh, axis):
    m = jnp.moveaxis(sh, 0, axis)
    shp = m.shape
    return m.reshape(shp[:axis] + (shp[axis] * shp[axis + 1],) + shp[axis + 2:])


def _as_bf16_pairs(a):
    return lax.bitcast_convert_type(a.astype(F32), BF16)


def _from_bf16_pairs(a):
    return lax.bitcast_convert_type(a, F32)


def kernel(x, attn_norm_g, ffn_norm_g, ab_w_in, ab_w_out, ab_rel_bias, c_w_in, c_ln_g, c_ln_b, c_w_s, c_b_s, c_w_out, ffn_w_up, ffn_conv_w, ffn_conv_b, ffn_w_down, final_norm_g, loss_target, m_attn_norm_g, m_ffn_norm_g, m_ab_w_in, m_ab_w_out, m_ab_rel_bias, m_c_w_in, m_c_ln_g, m_c_ln_b, m_c_w_s, m_c_b_s, m_c_w_out, m_ffn_w_up, m_ffn_conv_w, m_ffn_conv_b, m_ffn_w_down, m_final_norm_g, v_attn_norm_g, v_ffn_norm_g, v_ab_w_in, v_ab_w_out, v_ab_rel_bias, v_c_w_in, v_c_ln_g, v_c_ln_b, v_c_w_s, v_c_b_s, v_c_w_out, v_ffn_w_up, v_ffn_conv_w, v_ffn_conv_b, v_ffn_w_down, v_final_norm_g):
    w = dict(attn_norm_g=attn_norm_g, ffn_norm_g=ffn_norm_g, ab_w_in=ab_w_in, ab_w_out=ab_w_out, ab_rel_bias=ab_rel_bias,
             c_w_in=c_w_in, c_ln_g=c_ln_g, c_ln_b=c_ln_b, c_w_s=c_w_s, c_b_s=c_b_s, c_w_out=c_w_out, ffn_w_up=ffn_w_up,
             ffn_conv_w=ffn_conv_w, ffn_conv_b=ffn_conv_b, ffn_w_down=ffn_w_down, final_norm_g=final_norm_g)
    m = dict(attn_norm_g=m_attn_norm_g, ffn_norm_g=m_ffn_norm_g, ab_w_in=m_ab_w_in, ab_w_out=m_ab_w_out,
             ab_rel_bias=m_ab_rel_bias, c_w_in=m_c_w_in, c_ln_g=m_c_ln_g, c_ln_b=m_c_ln_b, c_w_s=m_c_w_s, c_b_s=m_c_b_s,
             c_w_out=m_c_w_out, ffn_w_up=m_ffn_w_up, ffn_conv_w=m_ffn_conv_w, ffn_conv_b=m_ffn_conv_b,
             ffn_w_down=m_ffn_w_down, final_norm_g=m_final_norm_g)
    v = dict(attn_norm_g=v_attn_norm_g, ffn_norm_g=v_ffn_norm_g, ab_w_in=v_ab_w_in, ab_w_out=v_ab_w_out,
             ab_rel_bias=v_ab_rel_bias, c_w_in=v_c_w_in, c_ln_g=v_c_ln_g, c_ln_b=v_c_ln_b, c_w_s=v_c_w_s, c_b_s=v_c_b_s,
             c_w_out=v_c_w_out, ffn_w_up=v_ffn_w_up, ffn_conv_w=v_ffn_conv_w, ffn_conv_b=v_ffn_conv_b,
             ffn_w_down=v_ffn_w_down, final_norm_g=v_final_norm_g)
    names = list(w)
    chip = 2 * lax.axis_index("x") + lax.axis_index("y")

    send = [w[n].astype(BF16) for n, _ in BIG] + [_as_bf16_pairs(w[n]) for n, _ in SMALL_SHARDED]
    shard_shapes = [a.shape for a in send]
    gathered = _gather_weights(_pack(send, 32))
    parts = [_unpack(gathered[s], shard_shapes) for s in range(N_CHIPS)]
    full = dict(w)
    for i, (n, axis) in enumerate(BIG):
        full[n] = _from_shards(jnp.stack([parts[s][i] for s in range(N_CHIPS)]), axis)
    for i, (n, axis) in enumerate(SMALL_SHARDED):
        full[n] = _from_shards(_from_bf16_pairs(jnp.stack([parts[s][len(BIG) + i] for s in range(N_CHIPS)])), axis)

    lsum, grad_x, g = _local_step(x[0], loss_target[0], full)
    loss = lax.psum(0.5 * lsum, ("x", "y", "c"))

    gbig = jnp.concatenate([_shard_major(g[n], axis).reshape(N_CHIPS, -1, PACK_W) for n, axis in BIG], axis=1)
    slots = _exchange_all(gbig.astype(BF16), True, "exchange_big")
    gshard = _exchange_halves(_sum_slots(slots, "sum_big"))
    pack_big = lambda d: jnp.concatenate([d[n].reshape(-1, PACK_W) for n, _ in BIG], axis=0)
    big_out = _adamw(pack_big(w), gshard, pack_big(m), pack_big(v), "adamw_big")
    big_shapes = [w[n].shape for n, _ in BIG]

    small_names = REPLICATED + [n for n, _ in SMALL_SHARDED]
    gsmall = _pack([g[n] for n in small_names], 16)
    gsum = _sum_slots(_exchange_all(gsmall, False, "exchange_small"), "sum_small")
    gsmall_full = dict(zip(small_names, _unpack(gsum, [g[n].shape for n in small_names])))
    for n, axis in SMALL_SHARDED:
        size = w[n].shape[axis]
        gsmall_full[n] = lax.dynamic_slice_in_dim(gsmall_full[n], chip * size, size, axis)
    pack_small = lambda d: _pack([d[n] for n in small_names], 8)
    small_out = _adamw(pack_small(w), pack_small(gsmall_full), pack_small(m), pack_small(v), "adamw_small")
    small_shapes = [w[n].shape for n in small_names]

    grads = dict(zip([n for n, _ in BIG], _unpack(gshard, big_shapes)))
    grads.update(gsmall_full)
    outs = [grads]
    for k in range(3):
        d = dict(zip([n for n, _ in BIG], _unpack(big_out[k], big_shapes)))
        d.update(zip(small_names, _unpack(small_out[k], small_shapes)))
        outs.append(d)
    return (loss, grad_x[None], *[o[n] for o in outs for n in names])
```

```python
import functools
import math

import jax
import jax.numpy as jnp
from jax import lax
from jax.experimental import pallas as pl
from jax.experimental.pallas import tpu as pltpu

F32 = jnp.float32
BF16 = jnp.bfloat16
S = jax.ShapeDtypeStruct
MESH = pl.DeviceIdType.MESH

D_MODEL = 1024
CHUNK = 64
EPS = 1e-6
NEG_INF = -1e30
RET_HEADS, RET_QK, RET_V = 4, 128, 256
ATT_HEADS, ATT_D, ATT_PAST, MAX_REL = 8, 64, 8, 128
BAND = (ATT_PAST + 1) * CHUNK
PADK = ATT_PAST * CHUNK
SGU_BLOCK, SGU_GROUPS, SGU_WIDTH = 128, 8, 2048
SGU_GW = SGU_WIDTH // SGU_GROUPS
FFN_HIDDEN = 2816
N_REL = 2 * MAX_REL + 1
RET_SCALE = RET_QK ** -0.5
ATT_SCALE = ATT_D ** -0.5
ADAM_LR, ADAM_B1, ADAM_B2, ADAM_EPS, ADAM_WD, ADAM_STEP = 0.001, 0.9, 0.999, 1e-08, 0.01, 10

V7X_VMEM_BYTES = 64 * 1024 * 1024
VMEM_LIMIT = V7X_VMEM_BYTES * 7 // 8
LANES = 128
PACK_W = 1024
N_CHIPS = 4
N_DEV = 8

GELU_C = math.sqrt(2.0 / math.pi)
GELU_A = 0.044715


def _cparams(*sem):
    return pltpu.CompilerParams(dimension_semantics=tuple(sem) if sem else None, vmem_limit_bytes=VMEM_LIMIT)


def _tile(n, target, unit=LANES):
    best = None
    for t in range(unit, min(n, target) + 1, unit):
        if n % t == 0:
            best = t
    return best if best is not None else n


def _gelu(x):
    t = jnp.tanh(GELU_C * (x + GELU_A * x * x * x))
    return 0.5 * x * (1.0 + t)


def _gelu_and_grad(x):
    x2 = x * x
    t = jnp.tanh(GELU_C * (x + GELU_A * x2 * x))
    g = 0.5 * x * (1.0 + t)
    dg = 0.5 * (1.0 + t) + 0.5 * x * (1.0 - t * t) * (GELU_C * (1.0 + 3.0 * GELU_A * x2))
    return g, dg


def _sigmoid(x):
    return 1.0 / (1.0 + jnp.exp(-x))


def _dot(a, b):
    return jnp.dot(a, b, preferred_element_type=F32)


def _dot_nt(a, b):
    return lax.dot_general(a, b, (((1,), (1,)), ((), ())), preferred_element_type=F32)


def _dot_tn(a, b):
    return lax.dot_general(a, b, (((0,), (0,)), ((), ())), preferred_element_type=F32)


def _rmsnorm_fwd(x, g, name):
    T, D = x.shape
    tr = _tile(T, 512, 8)

    def body(x_ref, g_ref, o_ref):
        xv = x_ref[...]
        r = lax.rsqrt(jnp.mean(xv * xv, axis=-1, keepdims=True) + EPS)
        o_ref[...] = (xv * r * g_ref[...]).astype(o_ref.dtype)

    return pl.pallas_call(
        body, grid=(T // tr,),
        in_specs=[pl.BlockSpec((tr, D), lambda i: (i, 0)), pl.BlockSpec((1, D), lambda i: (0, 0))],
        out_specs=pl.BlockSpec((tr, D), lambda i: (i, 0)),
        out_shape=S((T, D), BF16), name=name, compiler_params=_cparams("parallel"))(x, g)


def _rmsnorm_bwd(x, dy, g, dres, name):
    T, D = x.shape
    tr = _tile(T, 512, 8)

    def body(x_ref, dy_ref, g_ref, dres_ref, dx_ref, dxb_ref, dg_ref):
        @pl.when(pl.program_id(0) == 0)
        def _():
            dg_ref[...] = jnp.zeros_like(dg_ref)

        xv = x_ref[...]
        r = lax.rsqrt(jnp.mean(xv * xv, axis=-1, keepdims=True) + EPS)
        xh = xv * r
        dyv = dy_ref[...]
        dg_ref[...] += jnp.sum(dyv * xh, axis=0, keepdims=True)
        dxh = dyv * g_ref[...]
        dx = dres_ref[...] + r * (dxh - xh * jnp.mean(dxh * xh, axis=-1, keepdims=True))
        dx_ref[...] = dx
        dxb_ref[...] = dx.astype(BF16)

    row = pl.BlockSpec((tr, D), lambda i: (i, 0))
    vec = pl.BlockSpec((1, D), lambda i: (0, 0))
    return pl.pallas_call(
        body, grid=(T // tr,), in_specs=[row, row, vec, row], out_specs=[row, row, vec],
        out_shape=[S((T, D), F32), S((T, D), BF16), S((1, D), F32)], name=name,
        compiler_params=_cparams("arbitrary"))(x, dy, g, dres)


def _pieces(a):
    return list(a) if isinstance(a, (list, tuple)) else [a]


def _piece_layout(widths, tile):
    out, s = [], 0
    for w in widths:
        out.append((s, w // tile))
        s += w // tile
    return out


def _common_tile(widths, target):
    return _tile(functools.reduce(math.gcd, widths), target)


def _mm(a, b, name, res=None, out_dtype=F32, plan=None):
    pieces = _pieces(a)
    M = pieces[0].shape[0]
    widths = [p.shape[1] for p in pieces]
    K, N = sum(widths), b.shape[1]
    tm, tn, tk = _tile(M, 1024, 8), _tile(N, 1408), _common_tile(widths, 1536)
    nk, npc = K // tk, len(pieces)
    layout = _piece_layout(widths, tk)

    def body(*refs):
        a_refs, b_ref = refs[:npc], refs[npc]
        r_ref = refs[npc + 1] if res is not None else None
        o_ref = refs[npc + 1 + (res is not None)]

        def finish(v):
            if r_ref is not None:
                v = v + r_ref[...]
            o_ref[...] = v.astype(o_ref.dtype)

        if nk == 1:
            finish(_dot(a_refs[0][...], b_ref[...]))
            return
        acc = refs[-1]
        k = pl.program_id(2)
        for a_ref, (s, c) in zip(a_refs, layout):
            def add(a_ref=a_ref):
                acc[...] += _dot(a_ref[...], b_ref[...])

            if s == 0:
                @pl.when(k == 0)
                def _(a_ref=a_ref):
                    acc[...] = _dot(a_ref[...], b_ref[...])

                if c > 1:
                    pl.when((k > 0) & (k < c))(add)
            else:
                pl.when((k >= s) & (k < s + c))(add)

        @pl.when(k == nk - 1)
        def _():
            finish(acc[...])

    in_specs = [pl.BlockSpec((tm, tk), lambda i, j, k, s=s, c=c: (i, jnp.clip(k - s, 0, c - 1))) for s, c in layout]
    in_specs.append(pl.BlockSpec((tk, tn), lambda i, j, k: (k, j)))
    args = pieces + [b]
    if res is not None:
        in_specs.append(pl.BlockSpec((tm, tn), lambda i, j, k: (i, j)))
        args.append(res)
    (out,), extra = _call(
        body, grid=(M // tm, N // tn, nk), in_specs=in_specs,
        out_specs=[pl.BlockSpec((tm, tn), lambda i, j, k: (i, j))],
        out_shape=[S((M, N), out_dtype)],
        scratch_shapes=[pltpu.VMEM((tm, tn), F32)] if nk > 1 else [],
        name=name, sem=("parallel", "parallel", "arbitrary"), args=args, plan=plan)
    return out if plan is None else (out, extra)


def _mm_tn(a, g, name):
    ap, gp = _pieces(a), _pieces(g)
    T = ap[0].shape[0]
    aw, gw = [p.shape[1] for p in ap], [p.shape[1] for p in gp]
    tm, tn, tt = _common_tile(aw, 1408), _common_tile(gw, 1408), _tile(T, 1024, 8)
    alay, glay = _piece_layout(aw, tm), _piece_layout(gw, tn)
    na = len(ap)

    def inside(idx, s, c, single):
        return None if single else (idx >= s) & (idx < s + c)

    def body(*refs):
        a_refs, g_refs, o_ref = refs[:na], refs[na:-1], refs[-1]
        i, j = pl.program_id(0), pl.program_id(1)

        @pl.when(pl.program_id(2) == 0)
        def _():
            o_ref[...] = jnp.zeros_like(o_ref)

        for a_ref, (sa, ca) in zip(a_refs, alay):
            for g_ref, (sg, cg) in zip(g_refs, glay):
                def add(a_ref=a_ref, g_ref=g_ref):
                    o_ref[...] += _dot_tn(a_ref[...], g_ref[...])

                conds = [c for c in (inside(i, sa, ca, na == 1), inside(j, sg, cg, len(gp) == 1)) if c is not None]
                if not conds:
                    add()
                else:
                    pl.when(functools.reduce(lambda u, v: u & v, conds))(add)

    def spec(tile, lay, single, axis):
        s, c = lay

        def index(i, j, k):
            idx = (i, j)[axis]
            if single:
                return (k, idx)
            on = (idx >= s) & (idx < s + c)
            return (jnp.where(on, k, 0), jnp.clip(idx - s, 0, c - 1))

        return pl.BlockSpec((tt, tile), index)

    in_specs = [spec(tm, lay, na == 1, 0) for lay in alay] + [spec(tn, lay, len(gp) == 1, 1) for lay in glay]
    return pl.pallas_call(
        body, grid=(sum(aw) // tm, sum(gw) // tn, T // tt), in_specs=in_specs,
        out_specs=pl.BlockSpec((tm, tn), lambda i, j, k: (i, j)),
        out_shape=S((sum(aw), sum(gw)), F32), name=name,
        compiler_params=_cparams("parallel", "parallel", "arbitrary"))(*ap, *gp)


def _rotate(x, c2, s2):
    return x * c2 + pltpu.roll(x, RET_QK // 2, 1) * s2


def _unrotate(d, c2, s2):
    return d * c2 - pltpu.roll(d, RET_QK // 2, 1) * s2


def _ret_specs(RB, blockmap):
    q = pl.BlockSpec((RB, RET_QK), lambda h, n: (blockmap(n), h))
    k = pl.BlockSpec((RB, RET_QK), lambda h, n: (blockmap(n), RET_HEADS + h))
    v = pl.BlockSpec((RB, RET_V), lambda h, n: (blockmap(n), RET_HEADS + h))
    g = pl.BlockSpec((RB, RET_V), lambda h, n: (blockmap(n), 2 * RET_HEADS + h))
    tab = pl.BlockSpec((RB, RET_QK), lambda h, n: (blockmap(n), 0))
    return q, k, v, g, tab


def _ret_decay_specs():
    return [pl.BlockSpec((None, CHUNK, CHUNK), lambda h, n: (h, 0, 0)),
            pl.BlockSpec((None, CHUNK, RET_QK), lambda h, n: (h, 0, 0)),
            pl.BlockSpec((None, CHUNK, RET_QK), lambda h, n: (h, 0, 0)),
            pl.BlockSpec((None, 1, RET_V), lambda h, n: (h, 0, 0))]


def _retention_fwd(z, c2, s2, dintra, qdec, kdec, cdec):
    T = z.shape[0]
    RB = min(512, T)
    nch, nb = RB // CHUNK, T // RB

    def body(q_ref, k_ref, v_ref, g_ref, c2_ref, s2_ref, di_ref, qd_ref, kd_ref, cd_ref, ya_ref, r_ref, st_ref, state):
        @pl.when(pl.program_id(1) == 0)
        def _():
            state[...] = jnp.zeros_like(state)

        dmat, qdv, kdv, cdv = di_ref[...], qd_ref[...], kd_ref[...], cd_ref[...]
        for c in range(nch):
            rows = slice(c * CHUNK, (c + 1) * CHUNK)
            c2v, s2v = c2_ref[rows, :], s2_ref[rows, :]
            qr = _rotate(q_ref[rows, :], c2v, s2v)
            kr = _rotate(k_ref[rows, :], c2v, s2v) * RET_SCALE
            vb = v_ref[rows, :].astype(BF16)
            sm = _dot_nt(qr.astype(BF16), kr.astype(BF16)) * dmat
            sb = state[...].astype(BF16)
            st_ref[c] = sb
            o = _dot(sm.astype(BF16), vb) + _dot((qr * qdv).astype(BF16), sb)
            state[...] = state[...] * cdv + _dot_tn((kr * kdv).astype(BF16), vb)
            r_ref[rows, :] = o
            mu = jnp.mean(o, axis=-1, keepdims=True)
            oc = o - mu
            rn = oc * lax.rsqrt(jnp.mean(oc * oc, axis=-1, keepdims=True) + EPS)
            gv = g_ref[rows, :]
            ya_ref[rows, :] = (gv * _sigmoid(gv) * rn).astype(BF16)

    q, k, v, g, tab = _ret_specs(RB, lambda n: n)
    wide = pl.BlockSpec((RB, RET_V), lambda h, n: (n, h))
    return pl.pallas_call(
        body, grid=(RET_HEADS, nb),
        in_specs=[q, k, v, g, tab, tab] + _ret_decay_specs(),
        out_specs=[wide, wide, pl.BlockSpec((None, nch, RET_QK, RET_V), lambda h, n: (h, n, 0, 0))],
        out_shape=[S((T, RET_HEADS * RET_V), BF16), S((T, RET_HEADS * RET_V), F32),
                   S((RET_HEADS, T // CHUNK, RET_QK, RET_V), BF16)],
        scratch_shapes=[pltpu.VMEM((RET_QK, RET_V), F32)], name="retention_fwd",
        compiler_params=_cparams("parallel", "arbitrary"))(z, z, z, z, c2, s2, dintra, qdec, kdec, cdec)


def _retention_bwd(z, c2, s2, dintra, qdec, kdec, cdec, r, dycat, st):
    T = z.shape[0]
    RB = min(512, T)
    nch, nb = RB // CHUNK, T // RB

    def body(q_ref, k_ref, v_ref, g_ref, c2_ref, s2_ref, di_ref, qd_ref, kd_ref, cd_ref, r_ref, dy_ref, st_ref,
             dq_ref, dk_ref, dv_ref, dg_ref, dstate):
        @pl.when(pl.program_id(1) == 0)
        def _():
            dstate[...] = jnp.zeros_like(dstate)

        dmat, qdv, kdv, cdv = di_ref[...], qd_ref[...], kd_ref[...], cd_ref[...]
        for c in reversed(range(nch)):
            rows = slice(c * CHUNK, (c + 1) * CHUNK)
            c2v, s2v = c2_ref[rows, :], s2_ref[rows, :]
            qr = _rotate(q_ref[rows, :], c2v, s2v)
            kr = _rotate(k_ref[rows, :], c2v, s2v) * RET_SCALE
            qb, kb = qr.astype(BF16), kr.astype(BF16)
            vb = v_ref[rows, :].astype(BF16)
            o, gv, dy = r_ref[rows, :], g_ref[rows, :], dy_ref[rows, :]
            mu = jnp.mean(o, axis=-1, keepdims=True)
            oc = o - mu
            rstd = lax.rsqrt(jnp.mean(oc * oc, axis=-1, keepdims=True) + EPS)
            rn = oc * rstd
            sg = _sigmoid(gv)
            dg_ref[rows, :] = (dy * rn * (sg * (1.0 + gv * (1.0 - sg)))).astype(BF16)
            drn = dy * (gv * sg)
            do = rstd * (drn - jnp.mean(drn, axis=-1, keepdims=True) - rn * jnp.mean(drn * rn, axis=-1, keepdims=True))
            dob = do.astype(BF16)
            sm = (_dot_nt(qb, kb) * dmat).astype(BF16)
            kdb = (kr * kdv).astype(BF16)
            dsb = dstate[...].astype(BF16)
            dv_ref[rows, :] = (_dot_tn(sm, dob) + _dot(kdb, dsb)).astype(BF16)
            ds = (_dot_nt(dob, vb) * dmat).astype(BF16)
            dqr = _dot(ds, kb) + _dot_nt(dob, st_ref[c]) * qdv
            dkr = (_dot_tn(ds, qb) + _dot_nt(vb, dsb) * kdv) * RET_SCALE
            dstate[...] = dstate[...] * cdv + _dot_tn((qr * qdv).astype(BF16), dob)
            dq_ref[rows, :] = _unrotate(dqr, c2v, s2v).astype(BF16)
            dk_ref[rows, :] = _unrotate(dkr, c2v, s2v).astype(BF16)

    rev = lambda n: nb - 1 - n
    q, k, v, g, tab = _ret_specs(RB, rev)
    wide = pl.BlockSpec((RB, RET_V), lambda h, n: (rev(n), h))
    narrow = pl.BlockSpec((RB, RET_QK), lambda h, n: (rev(n), h))
    return pl.pallas_call(
        body, grid=(RET_HEADS, nb),
        in_specs=[q, k, v, g, tab, tab] + _ret_decay_specs() + [
            wide, wide, pl.BlockSpec((None, nch, RET_QK, RET_V), lambda h, n: (h, rev(n), 0, 0))],
        out_specs=[narrow, narrow, wide, wide],
        out_shape=[S((T, RET_HEADS * RET_QK), BF16), S((T, RET_HEADS * RET_QK), BF16),
                   S((T, RET_HEADS * RET_V), BF16), S((T, RET_HEADS * RET_V), BF16)],
        scratch_shapes=[pltpu.VMEM((RET_QK, RET_V), F32)], name="retention_bwd",
        compiler_params=_cparams("parallel", "arbitrary"))(z, z, z, z, c2, s2, dintra, qdec, kdec, cdec, r, dycat, st)


def _rel_index(i):
    r = lax.broadcasted_iota(jnp.int32, (3 * LANES, 5 * LANES), 0)
    j = lax.broadcasted_iota(jnp.int32, (3 * LANES, 5 * LANES), 1)
    idx = jnp.clip(i + PADK - j, -MAX_REL, MAX_REL) + MAX_REL
    return (r == idx).astype(BF16)


def _split3(v):
    hi = v.astype(BF16)
    r1 = v - hi.astype(F32)
    mid = r1.astype(BF16)
    lo = (r1 - mid.astype(F32)).astype(BF16)
    return hi, mid, lo


def _bias_build(rb):
    rbp = jnp.pad(rb, ((0, 0), (0, 3 * LANES - N_REL)))

    def body(rb_ref, o_ref):
        e = _rel_index(pl.program_id(0))
        hi, mid, lo = _split3(rb_ref[...])
        o_ref[...] = _dot(hi, e) + _dot(mid, e) + _dot(lo, e)

    return pl.pallas_call(
        body, grid=(CHUNK,), in_specs=[pl.BlockSpec((ATT_HEADS, 3 * LANES), lambda i: (0, 0))],
        out_specs=pl.BlockSpec((None, ATT_HEADS, 5 * LANES), lambda i: (i, 0, 0)),
        out_shape=S((CHUNK, ATT_HEADS, 5 * LANES), F32), name="bias_build",
        compiler_params=_cparams("parallel"))(rbp)


ATT_RB = 512
ATT_QT = 256
ATT_CPT = ATT_QT // CHUNK
ATT_KT = ATT_QT + PADK
ATT_QCOL = (2 * RET_HEADS * RET_QK + 2 * RET_HEADS * RET_V) // LANES
ATT_KCOL = ATT_QCOL + ATT_HEADS * ATT_D // LANES
ATT_VCOL = ATT_KCOL + ATT_HEADS * ATT_D // LANES


def _bias_grad(dbt):
    def body(d_ref, o_ref):
        @pl.when(pl.program_id(0) == 0)
        def _():
            o_ref[...] = jnp.zeros_like(o_ref)

        e = _rel_index(pl.program_id(0))
        d = d_ref[0]
        for ci in range(1, ATT_CPT):
            d = d + d_ref[ci]
        hi, mid, lo = _split3(d)
        o_ref[...] += _dot_nt(hi, e) + _dot_nt(mid, e) + _dot_nt(lo, e)

    return pl.pallas_call(
        body, grid=(CHUNK,),
        in_specs=[pl.BlockSpec((ATT_CPT, None, ATT_HEADS, 5 * LANES), lambda i: (0, i, 0, 0))],
        out_specs=pl.BlockSpec((ATT_HEADS, 3 * LANES), lambda i: (0, 0)),
        out_shape=S((ATT_HEADS, 3 * LANES), F32), name="bias_grad",
        compiler_params=_cparams("arbitrary"))(dbt)


def _bias_tiles(bias):
    parts = [jnp.pad(bias, ((0, 0), (0, 0), (CHUNK * ci, ATT_KT - BAND - CHUNK * ci)), constant_values=NEG_INF)
             for ci in range(ATT_CPT)]
    return jnp.stack(parts, axis=1).reshape(ATT_HEADS, ATT_QT, ATT_KT)


def _bias_bands(dbias_tiles):
    d = dbias_tiles.reshape(ATT_HEADS, ATT_CPT, CHUNK, ATT_KT)
    bands = jnp.stack([d[:, ci, :, CHUNK * ci:CHUNK * ci + BAND] for ci in range(ATT_CPT)])
    return jnp.pad(jnp.transpose(bands, (0, 2, 1, 3)), ((0, 0), (0, 0), (0, 0), (0, 5 * LANES - BAND)))


def _att_fill(kw, vw, klo, khi, vlo, vhi):
    kw[0:ATT_RB, :] = klo[...].astype(BF16)
    kw[ATT_RB:, :] = khi[...].astype(BF16)
    vw[0:ATT_RB, :] = vlo[...].astype(BF16)
    vw[ATT_RB:, :] = vhi[...].astype(BF16)


def _att_probs(qm, kwin, bias, first_key):
    s = _dot_nt(qm, kwin) * ATT_SCALE + bias
    col = lax.broadcasted_iota(jnp.int32, (ATT_QT, ATT_KT), 1)
    s = jnp.where(col + first_key >= 0, s, NEG_INF)
    p = jnp.exp(s - jnp.max(s, axis=-1, keepdims=True))
    return p / jnp.sum(p, axis=-1, keepdims=True)


def _att_in_specs(nq):
    qn = lambda n: jnp.minimum(n, nq - 1)
    blk = lambda col, back: pl.BlockSpec((ATT_RB, LANES), lambda hp, n: (jnp.maximum(qn(n) - back, 0), col + hp))
    return [blk(ATT_QCOL, 0), blk(ATT_KCOL, 1), blk(ATT_KCOL, 0), blk(ATT_VCOL, 1), blk(ATT_VCOL, 0),
            pl.BlockSpec((2, ATT_QT, ATT_KT), lambda hp, n: (hp, 0, 0))]


def _attention_fwd(z, bias_t, plan=None):
    T = z.shape[0]
    nq = T // ATT_RB

    def body(q_ref, klo, khi, vlo, vhi, b_ref, o_ref, kw, vw):
        _att_fill(kw, vw, klo, khi, vlo, vhi)
        lane = lax.broadcasted_iota(jnp.int32, (ATT_QT, LANES), 1)
        n = pl.program_id(1)
        for t in range(ATT_RB // ATT_QT):
            rows = slice(t * ATT_QT, (t + 1) * ATT_QT)
            win = slice(t * ATT_QT, t * ATT_QT + ATT_KT)
            qc = q_ref[rows, :]
            outs = []
            for e in range(2):
                qm = jnp.where((lane >= ATT_D) == (e == 1), qc, 0.0).astype(BF16)
                p = _att_probs(qm, kw[win, :], b_ref[e], (n - 1) * ATT_RB + t * ATT_QT)
                outs.append(_dot(p.astype(BF16), vw[win, :]))
            o_ref[rows, :] = jnp.where(lane < ATT_D, outs[0], outs[1]).astype(BF16)

    (yb,), extra = _call(
        body, grid=(ATT_HEADS // 2, nq), in_specs=_att_in_specs(nq),
        out_specs=[pl.BlockSpec((ATT_RB, LANES), lambda hp, n: (n, hp))],
        out_shape=[S((T, ATT_HEADS * ATT_D), BF16)],
        scratch_shapes=[pltpu.VMEM((2 * ATT_RB, LANES), BF16), pltpu.VMEM((2 * ATT_RB, LANES), BF16)],
        name="attention_fwd", sem=("parallel", "parallel"), args=(z, z, z, z, z, bias_t), plan=plan)
    return yb, extra


def _attention_bwd(z, bias_t, dycat, plan=None):
    T = z.shape[0]
    nq = T // ATT_RB
    dycol = RET_HEADS * RET_V // LANES

    def body(q_ref, klo, khi, vlo, vhi, b_ref, dy_ref, dq_ref, dk_ref, dv_ref, db_ref, kw, vw, dkw, dvw):
        n = pl.program_id(1)

        @pl.when(n == 0)
        def _():
            dkw[...] = jnp.zeros_like(dkw)
            dvw[...] = jnp.zeros_like(dvw)
            db_ref[...] = jnp.zeros_like(db_ref)

        @pl.when(n > 0)
        def _():
            dkw[0:ATT_RB, :] = dkw[ATT_RB:, :]
            dvw[0:ATT_RB, :] = dvw[ATT_RB:, :]
            dkw[ATT_RB:, :] = jnp.zeros((ATT_RB, LANES), F32)
            dvw[ATT_RB:, :] = jnp.zeros((ATT_RB, LANES), F32)

        @pl.when(n < nq)
        def _():
            _att_fill(kw, vw, klo, khi, vlo, vhi)
            lane = lax.broadcasted_iota(jnp.int32, (ATT_QT, LANES), 1)
            for t in range(ATT_RB // ATT_QT):
                rows = slice(t * ATT_QT, (t + 1) * ATT_QT)
                win = slice(t * ATT_QT, t * ATT_QT + ATT_KT)
                qc, dyc = q_ref[rows, :], dy_ref[rows, :]
                kwin, vwin = kw[win, :], vw[win, :]
                dq = jnp.zeros((ATT_QT, LANES), F32)
                for e in range(2):
                    mine = (lane >= ATT_D) == (e == 1)
                    qm = jnp.where(mine, qc, 0.0).astype(BF16)
                    dom = jnp.where(mine, dyc, 0.0).astype(BF16)
                    p = _att_probs(qm, kwin, b_ref[e], (n - 1) * ATT_RB + t * ATT_QT)
                    dp = _dot_nt(dom, vwin)
                    ds = p * (dp - jnp.sum(dp * p, axis=-1, keepdims=True))
                    db_ref[e] += ds
                    dsb = (ds * ATT_SCALE).astype(BF16)
                    dq = dq + jnp.where(mine, _dot(dsb, kwin), 0.0)
                    dkw[win, :] += _dot_tn(dsb, qm)
                    dvw[win, :] += _dot_tn(p.astype(BF16), dom)
                dq_ref[rows, :] = dq.astype(BF16)

        dk_ref[...] = dkw[0:ATT_RB, :].astype(BF16)
        dv_ref[...] = dvw[0:ATT_RB, :].astype(BF16)

    qn = lambda n: jnp.minimum(n, nq - 1)
    out_kv = pl.BlockSpec((ATT_RB, LANES), lambda hp, n: (jnp.maximum(n - 1, 0), hp))
    return _call(
        body, grid=(ATT_HEADS // 2, nq + 1),
        in_specs=_att_in_specs(nq) + [pl.BlockSpec((ATT_RB, LANES), lambda hp, n: (qn(n), dycol + hp))],
        out_specs=[pl.BlockSpec((ATT_RB, LANES), lambda hp, n: (qn(n), hp)), out_kv, out_kv,
                   pl.BlockSpec((2, ATT_QT, ATT_KT), lambda hp, n: (hp, 0, 0))],
        out_shape=[S((T, ATT_HEADS * ATT_D), BF16), S((T, ATT_HEADS * ATT_D), BF16),
                   S((T, ATT_HEADS * ATT_D), BF16), S((ATT_HEADS, ATT_QT, ATT_KT), F32)],
        scratch_shapes=[pltpu.VMEM((2 * ATT_RB, LANES), BF16), pltpu.VMEM((2 * ATT_RB, LANES), BF16),
                        pltpu.VMEM((2 * ATT_RB, LANES), F32), pltpu.VMEM((2 * ATT_RB, LANES), F32)],
        name="attention_bwd", sem=("parallel", "arbitrary"), args=(z, z, z, z, z, bias_t, dycat), plan=plan)


HALO = 8


def _conv_specs(tb, tc, nct, T):
    per = tb // HALO
    last = T // HALO - 1

    def at(half):
        off = half * nct
        return [pl.BlockSpec((HALO, tc), lambda j, i: (jnp.maximum(i * per - 1, 0), j + off)),
                pl.BlockSpec((tb, tc), lambda j, i: (i, j + off)),
                pl.BlockSpec((HALO, tc), lambda j, i: (jnp.minimum((i + 1) * per, last), j + off))]

    return at(0), at(1)


def _causal_conv(ext, w_ref, b_ref):
    zc = w_ref[0:1, :] * pltpu.roll(ext, 2, 0) + w_ref[1:2, :] * pltpu.roll(ext, 1, 0) + w_ref[2:3, :] * ext + b_ref[...]
    return zc[HALO:]


def _convglu_fwd(z, cw, cb, name):
    T = z.shape[0]
    tb, tc = _tile(T, 1024, 8), 256
    nct = FFN_HIDDEN // tc

    def body(gp_ref, g_ref, up_ref, u_ref, wg_ref, wu_ref, bg_ref, bu_ref, o_ref):
        first = pl.program_id(1) == 0

        def conv(p_ref, blk_ref, w_ref, b_ref):
            prev = jnp.where(first, 0.0, p_ref[...])
            return _causal_conv(jnp.concatenate([prev, blk_ref[...]], axis=0), w_ref, b_ref)

        o_ref[...] = (_gelu(conv(gp_ref, g_ref, wg_ref, bg_ref)) * conv(up_ref, u_ref, wu_ref, bu_ref)).astype(BF16)

    (gp, gb, _), (up, ub, _) = _conv_specs(tb, tc, nct, T)
    wspec = lambda off, rows: pl.BlockSpec((rows, tc), lambda j, i: (0, j + off))
    return pl.pallas_call(
        body, grid=(nct, T // tb),
        in_specs=[gp, gb, up, ub, wspec(0, 3), wspec(nct, 3), wspec(0, 1), wspec(nct, 1)],
        out_specs=pl.BlockSpec((tb, tc), lambda j, i: (i, j)),
        out_shape=S((T, FFN_HIDDEN), BF16), name=name,
        compiler_params=_cparams("parallel", "parallel"))(z, z, z, z, cw, cw, cb, cb)


def _convglu_bwd(z, df, cw, cb, name):
    T = z.shape[0]
    tb, tc = _tile(T, 1024, 8), 256
    nct = FFN_HIDDEN // tc
    nrb = T // tb

    def body(gp_ref, g_ref, gn_ref, up_ref, u_ref, un_ref, df_ref, dfn_ref, wg_ref, wu_ref, bg_ref, bu_ref,
             dzg_ref, dzu_ref, dwg_ref, dwu_ref, dbg_ref, dbu_ref):
        i = pl.program_id(1)
        first, last = i == 0, i == nrb - 1

        @pl.when(first)
        def _():
            for ref in (dwg_ref, dwu_ref, dbg_ref, dbu_ref):
                ref[...] = jnp.zeros_like(ref)

        def ext_of(p_ref, blk_ref, n_ref):
            return jnp.concatenate([jnp.where(first, 0.0, p_ref[...]), blk_ref[...], n_ref[...]], axis=0)

        gext, uext = ext_of(gp_ref, g_ref, gn_ref), ext_of(up_ref, u_ref, un_ref)
        gc, uc = _causal_conv(gext, wg_ref, bg_ref), _causal_conv(uext, wu_ref, bu_ref)
        dfe = jnp.concatenate([df_ref[...], jnp.where(last, 0.0, dfn_ref[...])], axis=0)
        ge, gd = _gelu_and_grad(gc)
        dgc, duc = dfe * uc * gd, dfe * ge
        n = tb + HALO

        def back(d, ext, w_ref, dz_ref, dw_ref, db_ref):
            dz = w_ref[2:3, :] * d + w_ref[1:2, :] * pltpu.roll(d, n - 1, 0) + w_ref[0:1, :] * pltpu.roll(d, n - 2, 0)
            dz_ref[...] = dz[:tb].astype(BF16)
            dblk = d[:tb]
            db_ref[...] += jnp.sum(dblk, axis=0, keepdims=True)
            taps = [pltpu.roll(ext, 2, 0)[HALO:HALO + tb], pltpu.roll(ext, 1, 0)[HALO:HALO + tb], ext[HALO:HALO + tb]]
            for k, t in enumerate(taps):
                dw_ref[k:k + 1, :] += jnp.sum(dblk * t, axis=0, keepdims=True)

        back(dgc, gext, wg_ref, dzg_ref, dwg_ref, dbg_ref)
        back(duc, uext, wu_ref, dzu_ref, dwu_ref, dbu_ref)

    gspecs, uspecs = _conv_specs(tb, tc, nct, T)
    per = tb // HALO
    dfs = [pl.BlockSpec((tb, tc), lambda j, i: (i, j)),
           pl.BlockSpec((HALO, tc), lambda j, i: (jnp.minimum((i + 1) * per, T // HALO - 1), j))]
    wspec = lambda off, rows: pl.BlockSpec((rows, tc), lambda j, i: (0, j + off))
    acc = lambda rows: pl.BlockSpec((rows, tc), lambda j, i: (0, j))
    blk = pl.BlockSpec((tb, tc), lambda j, i: (i, j))
    return pl.pallas_call(
        body, grid=(nct, nrb),
        in_specs=gspecs + uspecs + dfs + [wspec(0, 3), wspec(nct, 3), wspec(0, 1), wspec(nct, 1)],
        out_specs=[blk, blk, acc(3), acc(3), acc(1), acc(1)],
        out_shape=[S((T, FFN_HIDDEN), BF16), S((T, FFN_HIDDEN), BF16), S((3, FFN_HIDDEN), F32), S((3, FFN_HIDDEN), F32),
                   S((1, FFN_HIDDEN), F32), S((1, FFN_HIDDEN), F32)],
        name=name, compiler_params=_cparams("parallel", "arbitrary"))(z, z, z, z, z, z, df, df, cw, cw, cb, cb)


SGU_RB = 256


def _sgu_weights(ws_ref):
    i = lax.broadcasted_iota(jnp.int32, (SGU_BLOCK, SGU_BLOCK), 0)
    j = lax.broadcasted_iota(jnp.int32, (SGU_BLOCK, SGU_BLOCK), 1)
    mask = (j < CHUNK) | (i >= CHUNK)
    return mask, [jnp.where(mask, ws_ref[g], 0.0).astype(BF16) for g in range(SGU_GROUPS)]


def _sgu_norm(zv, lng, lnb):
    mu = jnp.mean(zv, axis=-1, keepdims=True)
    vc = zv - mu
    rstd = lax.rsqrt(jnp.mean(vc * vc, axis=-1, keepdims=True) + EPS)
    vh = vc * rstd
    return vh, rstd, vh * lng + lnb


def _sgu_fwd(zpre, lng, lnb, ws, bst):
    T = zpre.shape[0]
    nb = SGU_RB // SGU_BLOCK

    def body(z_ref, lng_ref, lnb_ref, ws_ref, bst_ref, o_ref):
        _, wm = _sgu_weights(ws_ref)
        u = _gelu(z_ref[:, :SGU_WIDTH])
        _, _, vn = _sgu_norm(_gelu(z_ref[:, SGU_WIDTH:]), lng_ref[...], lnb_ref[...])
        vnb = vn.astype(BF16)
        for b in range(nb):
            rows = slice(b * SGU_BLOCK, (b + 1) * SGU_BLOCK)
            for g in range(SGU_GROUPS):
                cols = slice(g * SGU_GW, (g + 1) * SGU_GW)
                mixed = _dot(wm[g], vnb[rows, cols]) + bst_ref[:, g:g + 1]
                o_ref[rows, cols] = (u[rows, cols] * mixed).astype(BF16)

    vec = pl.BlockSpec((1, SGU_WIDTH), lambda i: (0, 0))
    return pl.pallas_call(
        body, grid=(T // SGU_RB,),
        in_specs=[pl.BlockSpec((SGU_RB, 2 * SGU_WIDTH), lambda i: (i, 0)), vec, vec,
                  pl.BlockSpec((SGU_GROUPS, SGU_BLOCK, SGU_BLOCK), lambda i: (0, 0, 0)),
                  pl.BlockSpec((SGU_BLOCK, SGU_GROUPS), lambda i: (0, 0))],
        out_specs=pl.BlockSpec((SGU_RB, SGU_WIDTH), lambda i: (i, 0)),
        out_shape=S((T, SGU_WIDTH), BF16), name="sgu_fwd", compiler_params=_cparams("parallel"))(zpre, lng, lnb, ws, bst)


def _sgu_bwd(zpre, dy, lng, lnb, ws, bst):
    T = zpre.shape[0]
    nb = SGU_RB // SGU_BLOCK

    def body(z_ref, dy_ref, lng_ref, lnb_ref, ws_ref, bst_ref, dz_ref, dws_ref, dbst_ref, dlng_ref, dlnb_ref, dvn):
        @pl.when(pl.program_id(0) == 0)
        def _():
            for ref in (dws_ref, dbst_ref, dlng_ref, dlnb_ref):
                ref[...] = jnp.zeros_like(ref)

        mask, wm = _sgu_weights(ws_ref)
        u, ud = _gelu_and_grad(z_ref[:, :SGU_WIDTH])
        v, vd = _gelu_and_grad(z_ref[:, SGU_WIDTH:])
        vh, rstd, vn = _sgu_norm(v, lng_ref[...], lnb_ref[...])
        vnb = vn.astype(BF16)
        lane8 = lax.broadcasted_iota(jnp.int32, (SGU_BLOCK, SGU_GROUPS), 1)
        dbs = jnp.zeros((SGU_BLOCK, SGU_GROUPS), F32)
        for b in range(nb):
            rows = slice(b * SGU_BLOCK, (b + 1) * SGU_BLOCK)
            for g in range(SGU_GROUPS):
                cols = slice(g * SGU_GW, (g + 1) * SGU_GW)
                vg = vnb[rows, cols]
                mixed = _dot(wm[g], vg) + bst_ref[:, g:g + 1]
                dyv = dy_ref[rows, cols]
                dz_ref[rows, cols] = (dyv * mixed * ud[rows, cols]).astype(BF16)
                dmix = dyv * u[rows, cols]
                dmb = dmix.astype(BF16)
                dvn[rows, cols] = _dot_tn(wm[g], dmb)
                dws_ref[g] += jnp.where(mask, _dot_nt(dmb, vg), 0.0)
                dbs = dbs + jnp.where(lane8 == g, jnp.sum(dmix, axis=-1, keepdims=True), 0.0)
        dbst_ref[...] += dbs
        dvnv = dvn[...]
        dlng_ref[...] += jnp.sum(dvnv * vh, axis=0, keepdims=True)
        dlnb_ref[...] += jnp.sum(dvnv, axis=0, keepdims=True)
        dvh = dvnv * lng_ref[...]
        dv = rstd * (dvh - jnp.mean(dvh, axis=-1, keepdims=True) - vh * jnp.mean(dvh * vh, axis=-1, keepdims=True))
        dz_ref[:, SGU_WIDTH:] = (dv * vd).astype(BF16)

    vec = pl.BlockSpec((1, SGU_WIDTH), lambda i: (0, 0))
    wsp = pl.BlockSpec((SGU_GROUPS, SGU_BLOCK, SGU_BLOCK), lambda i: (0, 0, 0))
    bsp = pl.BlockSpec((SGU_BLOCK, SGU_GROUPS), lambda i: (0, 0))
    return pl.pallas_call(
        body, grid=(T // SGU_RB,),
        in_specs=[pl.BlockSpec((SGU_RB, 2 * SGU_WIDTH), lambda i: (i, 0)),
                  pl.BlockSpec((SGU_RB, SGU_WIDTH), lambda i: (i, 0)), vec, vec, wsp, bsp],
        out_specs=[pl.BlockSpec((SGU_RB, 2 * SGU_WIDTH), lambda i: (i, 0)), wsp, bsp, vec, vec],
        out_shape=[S((T, 2 * SGU_WIDTH), BF16), S((SGU_GROUPS, SGU_BLOCK, SGU_BLOCK), F32),
                   S((SGU_BLOCK, SGU_GROUPS), F32), S((1, SGU_WIDTH), F32), S((1, SGU_WIDTH), F32)],
        scratch_shapes=[pltpu.VMEM((SGU_RB, SGU_WIDTH), F32)], name="sgu_bwd",
        compiler_params=_cparams("arbitrary"))(zpre, dy, lng, lnb, ws, bst)


def _loss_head(h, tgt, g):
    T, D = h.shape
    tr = _tile(T, 512, 8)

    def body(h_ref, t_ref, g_ref, ls_ref, dh_ref, dhb_ref, dg_ref):
        @pl.when(pl.program_id(0) == 0)
        def _():
            ls_ref[...] = jnp.zeros_like(ls_ref)
            dg_ref[...] = jnp.zeros_like(dg_ref)

        hv = h_ref[...]
        r = lax.rsqrt(jnp.mean(hv * hv, axis=-1, keepdims=True) + EPS)
        xh = hv * r
        diff = xh * g_ref[...] - t_ref[...]
        per_row = jnp.mean(diff * diff, axis=-1, keepdims=True)
        ls_ref[...] += jnp.sum(per_row, axis=0, keepdims=True)
        dy = diff * (1.0 / D)
        dg_ref[...] += jnp.sum(dy * xh, axis=0, keepdims=True)
        dxh = dy * g_ref[...]
        dh = r * (dxh - xh * jnp.mean(dxh * xh, axis=-1, keepdims=True))
        dh_ref[...] = dh
        dhb_ref[...] = dh.astype(BF16)

    row = pl.BlockSpec((tr, D), lambda i: (i, 0))
    vec = pl.BlockSpec((1, D), lambda i: (0, 0))
    return pl.pallas_call(
        body, grid=(T // tr,), in_specs=[row, row, vec],
        out_specs=[pl.BlockSpec((1, LANES), lambda i: (0, 0)), row, row, vec],
        out_shape=[S((1, LANES), F32), S((T, D), F32), S((T, D), BF16), S((1, D), F32)],
        name="loss_head", compiler_params=_cparams("arbitrary"))(h, tgt, g)


ANY = pl.BlockSpec(memory_space=pl.ANY)
COPY_PARTS = 4
SWAP_PARTS = 8
DMA = pltpu.SemaphoreType.DMA


def _place():
    return lax.axis_index("x"), lax.axis_index("y"), lax.axis_index("c")


def _nparts(rows, unit, want):
    n = want
    while n > 1 and rows % (unit * n):
        n //= 2
    return n


def _row_unit(dtype):
    return 16 if jnp.dtype(dtype).itemsize == 2 else 8


def _remote(src, dst, send_sems, recv_sems, k, to):
    return pltpu.make_async_remote_copy(src_ref=src, dst_ref=dst, send_sem=send_sems.at[k], recv_sem=recv_sems.at[k],
                                        device_id=to, device_id_type=MESH)


class _GatherPlan:
    def __init__(self, p):
        R, W = p.shape
        self.srcs = [p]
        self.Rh = R // 2
        self.unit = _row_unit(p.dtype)
        self.n = _nparts(self.Rh, self.unit, COPY_PARTS)
        self.out_shapes = [S((N_CHIPS, R, W), p.dtype)]
        n = self.n
        self.scratch = [DMA((3 * n,)), DMA((3 * n,)), DMA((3 * n,)), DMA((3 * n,)), DMA((2 * n,))]
        self.has_relay = True

    def _ops(self, srcs, outs, sems):
        (p_ref,), (out_ref,), (ici_s, ici_r, rel_s, rel_r, loc) = srcs, outs, sems
        x, y, c = _place()
        me, sibling = (x, y, c), (x, y, 1 - c)
        chips = [(1 - x, y), (x, 1 - y), (1 - x, 1 - y)]
        n, rp, Rh, unit = self.n, self.Rh // self.n, self.Rh, self.unit

        def part(px, py, pc, k):
            return out_ref.at[2 * px + py, pl.ds(pl.multiple_of(pc * Rh + k * rp, unit), rp), :]

        def mine(k):
            return p_ref.at[pl.ds(pl.multiple_of(c * Rh + k * rp, unit), rp), :]

        local = [pltpu.make_async_copy(p_ref.at[pl.ds(k * rp, rp), :], out_ref.at[2 * x + y, pl.ds(k * rp, rp), :], loc.at[k])
                 for k in range(2 * n)]
        send, arrive, relay, relayed = [], [], [], []
        for j, chip in enumerate(chips):
            for k in range(n):
                s = j * n + k
                send.append(_remote(mine(k), part(x, y, c, k), ici_s, ici_r, s, (*chip, c)))
                arrive.append(_remote(mine(k), part(*chip, c, k), ici_s, ici_r, s, me))
                relay.append(_remote(part(*chip, c, k), part(*chip, c, k), rel_s, rel_r, s, sibling))
                relayed.append(_remote(part(*chip, c, k), part(*chip, 1 - c, k), rel_s, rel_r, s, me))
        return local, send, arrive, relay, relayed

    def start(self, *refs):
        local, send, _, _, _ = self._ops(*refs)
        for cp in local + send:
            cp.start()

    def relay(self, *refs):
        _, _, arrive, relay, _ = self._ops(*refs)
        for a, r in zip(arrive, relay):
            a.wait_recv()
            r.start()

    def finish(self, *refs):
        local, send, _, relay, relayed = self._ops(*refs)
        for cp in relayed:
            cp.wait_recv()
        for cp in send + relay:
            cp.wait_send()
        for cp in local:
            cp.wait()


class _ExchangePlan:
    def __init__(self, src, per_target):
        self.srcs = [src]
        self.per_target = per_target
        self.Rh = src.shape[1] // 2 if per_target else src.shape[0]
        self.unit = _row_unit(src.dtype)
        self.n = _nparts(self.Rh, self.unit, COPY_PARTS)
        self.out_shapes = [S((N_DEV, self.Rh, src.shape[-1]), src.dtype)]
        self.scratch = [DMA((N_DEV - 2 + self.n,)), DMA((N_DEV - 2 + self.n,)), DMA((self.n,))]
        self.has_relay = False

    def _ops(self, srcs, outs, sems):
        (src_ref,), (out_ref,), (ss, rs, loc) = srcs, outs, sems
        x, y, c = _place()
        n, rp, Rh, unit = self.n, self.Rh // self.n, self.Rh, self.unit

        def ident(px, py, pc):
            return 4 * px + 2 * py + pc

        def block_for(px, py, pc, r0, rows):
            if self.per_target:
                return src_ref.at[2 * px + py, pl.ds(pl.multiple_of(pc * Rh + r0, unit), rows), :]
            return src_ref.at[pl.ds(r0, rows), :]

        def slot(d, r0, rows):
            return out_ref.at[d, pl.ds(r0, rows), :]

        me = ident(x, y, c)
        local = [pltpu.make_async_copy(block_for(x, y, c, q * rp, rp), slot(me, q * rp, rp), loc.at[q]) for q in range(n)]
        send, arrive = [], []
        for k in range(1, N_DEV):
            peer = (x ^ ((k >> 2) & 1), y ^ ((k >> 1) & 1), c ^ (k & 1))
            pieces = [(N_DEV - 2 + q, q * rp, rp) for q in range(n)] if k == 1 else [(k - 2, 0, Rh)]
            for sem, r0, rows in pieces:
                send.append(_remote(block_for(*peer, r0, rows), slot(me, r0, rows), ss, rs, sem, peer))
                arrive.append(_remote(block_for(*peer, r0, rows), slot(ident(*peer), r0, rows), ss, rs, sem, peer))
        return local, send, arrive

    def start(self, *refs):
        local, send, _ = self._ops(*refs)
        for cp in local + send:
            cp.start()

    def finish(self, *refs):
        local, send, arrive = self._ops(*refs)
        for cp in arrive:
            cp.wait_recv()
        for cp in send:
            cp.wait_send()
        for cp in local:
            cp.wait()


class _SwapPlan:
    def __init__(self, h):
        self.srcs = [h]
        self.Rh, W = h.shape
        self.unit = _row_unit(h.dtype)
        self.n = _nparts(self.Rh, self.unit, SWAP_PARTS)
        self.out_shapes = [S((2 * self.Rh, W), h.dtype)]
        self.scratch = [DMA((self.n,)), DMA((self.n,)), DMA((self.n,))]
        self.has_relay = False

    def _ops(self, srcs, outs, sems):
        (h_ref,), (out_ref,), (ss, rs, loc) = srcs, outs, sems
        x, y, c = _place()
        n, rp, Rh, unit = self.n, self.Rh // self.n, self.Rh, self.unit

        def rows(pc, k):
            return out_ref.at[pl.ds(pl.multiple_of(pc * Rh + k * rp, unit), rp), :]

        def mine(k):
            return h_ref.at[pl.ds(k * rp, rp), :]

        local = [pltpu.make_async_copy(mine(k), rows(c, k), loc.at[k]) for k in range(n)]
        send = [_remote(mine(k), rows(c, k), ss, rs, k, (x, y, 1 - c)) for k in range(n)]
        arrive = [_remote(mine(k), rows(1 - c, k), ss, rs, k, (x, y, 1 - c)) for k in range(n)]
        return local, send, arrive

    start = _ExchangePlan.start
    finish = _ExchangePlan.finish


def _run_plan(plan, name):
    ni, no = len(plan.srcs), len(plan.out_shapes)

    def body(*refs):
        parts = (refs[:ni], refs[ni:ni + no], refs[ni + no:])
        plan.start(*parts)
        if plan.has_relay:
            plan.relay(*parts)
        plan.finish(*parts)

    return pl.pallas_call(body, out_shape=plan.out_shapes, in_specs=[ANY] * ni, out_specs=[ANY] * no,
                          scratch_shapes=plan.scratch, name=name)(*plan.srcs)


def _call(body, *, grid, in_specs, out_specs, out_shape, name, sem, args, scratch_shapes=(), plan=None):
    if plan is None:
        return pl.pallas_call(body, grid=grid, in_specs=in_specs, out_specs=out_specs, out_shape=out_shape,
                              scratch_shapes=list(scratch_shapes), name=name, compiler_params=_cparams(*sem))(*args), None
    n_in, n_out, n_scr = len(in_specs), len(out_shape), len(scratch_shapes)
    pi, po = len(plan.srcs), len(plan.out_shapes)
    total = math.prod(grid)

    def wrapped(*refs):
        a, refs = refs[:n_in], refs[n_in:]
        pa, refs = refs[:pi], refs[pi:]
        o, refs = refs[:n_out], refs[n_out:]
        pout, refs = refs[:po], refs[po:]
        scr, psem = refs[:n_scr], refs[n_scr:]
        step = 0
        for d, gsize in enumerate(grid):
            step = step * gsize + pl.program_id(d)

        @pl.when(step == 0)
        def _():
            plan.start(pa, pout, psem)

        body(*a, *o, *scr)
        if plan.has_relay:
            @pl.when(step == (3 * total) // 4)
            def _():
                plan.relay(pa, pout, psem)

        @pl.when(step == total - 1)
        def _():
            plan.finish(pa, pout, psem)

    outs = pl.pallas_call(
        wrapped, grid=grid, in_specs=list(in_specs) + [ANY] * pi, out_specs=list(out_specs) + [ANY] * po,
        out_shape=list(out_shape) + plan.out_shapes, scratch_shapes=list(scratch_shapes) + plan.scratch, name=name,
        compiler_params=_cparams(*["arbitrary"] * len(grid)))(*args, *plan.srcs)
    return outs[:n_out], outs[n_out:]


def _sum_slots(buf, name):
    n, R, W = buf.shape
    tr = _tile(R, 256, 8)

    def body(b_ref, o_ref):
        acc = b_ref[0].astype(F32)
        for s in range(1, n):
            acc = acc + b_ref[s].astype(F32)
        o_ref[...] = acc

    return pl.pallas_call(
        body, grid=(R // tr,), in_specs=[pl.BlockSpec((n, tr, W), lambda i: (0, i, 0))],
        out_specs=pl.BlockSpec((tr, W), lambda i: (i, 0)), out_shape=S((R, W), F32), name=name,
        compiler_params=_cparams("parallel"))(buf)


def _adamw(w, g, m, v, name):
    R, W = w.shape
    tr = _tile(R, 256, 8)

    def body(w_ref, g_ref, m_ref, v_ref, d_ref, mo_ref, vo_ref):
        gv = g_ref[...]
        mn = ADAM_B1 * m_ref[...] + (1.0 - ADAM_B1) * gv
        vn = ADAM_B2 * v_ref[...] + (1.0 - ADAM_B2) * (gv * gv)
        m_hat = mn / (1.0 - ADAM_B1 ** ADAM_STEP)
        v_hat = vn / (1.0 - ADAM_B2 ** ADAM_STEP)
        d_ref[...] = -ADAM_LR * (m_hat / (jnp.sqrt(v_hat) + ADAM_EPS) + ADAM_WD * w_ref[...])
        mo_ref[...] = mn
        vo_ref[...] = vn

    blk = pl.BlockSpec((tr, W), lambda i: (i, 0))
    return pl.pallas_call(
        body, grid=(R // tr,), in_specs=[blk] * 4, out_specs=[blk] * 3, out_shape=[S((R, W), F32)] * 3, name=name,
        compiler_params=_cparams("parallel"))(w, g, m, v)


def _tables(T):
    f32 = F32
    half = RET_QK // 2
    inv = 1.0 / (10000.0 ** jnp.linspace(0.0, 1.0, half, dtype=f32))
    ang = jnp.arange(T).astype(f32)[:, None] * inv[None, :]
    cos, sin = jnp.cos(ang), jnp.sin(ang)
    c2 = jnp.concatenate([cos, cos], axis=-1)
    s2 = jnp.concatenate([-sin, sin], axis=-1)
    log_g = jnp.log1p(-jnp.exp2(-5.0 - jnp.arange(RET_HEADS, dtype=f32)))
    idx = jnp.arange(CHUNK, dtype=f32)
    dintra = jnp.exp(log_g[:, None, None] * jnp.abs(idx[:, None] - idx[None, :]))
    kdec = jnp.exp(log_g[None, :] * (CHUNK - 1 - idx)[:, None]).T
    qdec = jnp.exp(log_g[None, :] * (idx + 1.0)[:, None]).T
    cdec = jnp.exp(log_g * CHUNK)
    bc = lambda a, w: jnp.broadcast_to(a[:, :, None], (RET_HEADS, a.shape[1], w))
    return c2, s2, dintra, bc(qdec, RET_QK), bc(kdec, RET_QK), jnp.broadcast_to(cdec[:, None, None], (RET_HEADS, 1, RET_V))


def _local_step(x, tgt, p, late=None, exchange=None):
    T = x.shape[0]
    tab = _tables(T)
    row = lambda a: a.reshape(1, -1)
    tr = lambda w: jnp.transpose(w)

    hn0 = _rmsnorm_fwd(x, row(p["attn_norm_g"][0]), "norm_a0")
    z0 = _mm(hn0, p["ab_w_in"][0], "mm_ab_in")
    ya, r, st = _retention_fwd(z0, *tab)
    bias_t = _bias_tiles(jnp.transpose(_bias_build(p["ab_rel_bias"][0]), (1, 0, 2))[:, :, :BAND])
    yb, late_out = _attention_fwd(z0, bias_t, plan=late[0] if late else None)
    if late:
        p = {**p, **late[1](late_out[0])}
    h1 = _mm([ya, yb], p["ab_w_out"][0], "mm_ab_out", res=x)

    def ffn_fwd(h, l):
        hf = _rmsnorm_fwd(h, row(p["ffn_norm_g"][l]), f"norm_f{l}")
        zf = _mm(hf, p["ffn_w_up"][l], f"mm_up{l}")
        f = _convglu_fwd(zf, p["ffn_conv_w"][l], row(p["ffn_conv_b"][l]), f"convglu_fwd{l}")
        return hf, zf, f, _mm(f, p["ffn_w_down"][l], f"mm_down{l}", res=h)

    hf0, zf0, f0, h2 = ffn_fwd(h1, 0)
    hn1 = _rmsnorm_fwd(h2, row(p["attn_norm_g"][1]), "norm_a1")
    zc = _mm(hn1, p["c_w_in"][0], "mm_c_in")
    lng, lnb, bst, ws = row(p["c_ln_g"][0]), row(p["c_ln_b"][0]), tr(p["c_b_s"][0]), p["c_w_s"][0]
    y1 = _sgu_fwd(zc, lng, lnb, ws, bst)
    h3 = _mm(y1, p["c_w_out"][0], "mm_c_out", res=h2)
    hf1, zf1, f1, h4 = ffn_fwd(h3, 1)
    lsum, dh4, dh4b, dgfin = _loss_head(h4, tgt, row(p["final_norm_g"]))

    g = {}

    def ffn_bwd(dh, dhb, h_in, hf, zf, f, l):
        d_down = _mm_tn(f, dhb, f"mmt_down{l}")
        df = _mm(dhb, tr(p["ffn_w_down"][l]), f"mmb_down{l}")
        dzg, dzu, dwg, dwu, dbg, dbu = _convglu_bwd(zf, df, p["ffn_conv_w"][l], row(p["ffn_conv_b"][l]), f"convglu_bwd{l}")
        d_up = _mm_tn(hf, [dzg, dzu], f"mmt_up{l}")
        dhf = _mm([dzg, dzu], tr(p["ffn_w_up"][l]), f"mmb_up{l}")
        dh_in, dh_in_b, dgf = _rmsnorm_bwd(h_in, dhf, row(p["ffn_norm_g"][l]), dh, f"norm_f{l}_bwd")
        return dh_in, dh_in_b, dict(ffn_w_down=d_down, ffn_w_up=d_up, ffn_norm_g=dgf[0],
                                    ffn_conv_w=jnp.concatenate([dwg, dwu], axis=1), ffn_conv_b=jnp.concatenate([dbg, dbu], axis=1)[0])

    dh3, dh3b, gf1 = ffn_bwd(dh4, dh4b, h3, hf1, zf1, f1, 1)
    g["c_w_out"] = _mm_tn(y1, dh3b, "mmt_c_out")[None]
    dy1 = _mm(dh3b, tr(p["c_w_out"][0]), "mmb_c_out")
    dzc, dws, dbst, dlng, dlnb = _sgu_bwd(zc, dy1, lng, lnb, ws, bst)
    g["c_w_s"], g["c_b_s"], g["c_ln_g"], g["c_ln_b"] = dws[None], tr(dbst)[None], dlng, dlnb
    g["c_w_in"] = _mm_tn(hn1, dzc, "mmt_c_in")[None]
    dhn1 = _mm(dzc, tr(p["c_w_in"][0]), "mmb_c_in")
    dh2, dh2b, dga1 = _rmsnorm_bwd(h2, dhn1, row(p["attn_norm_g"][1]), dh3, "norm_a1_bwd")
    dh1, dh1b, gf0 = ffn_bwd(dh2, dh2b, h1, hf0, zf0, f0, 0)
    for k in gf0:
        g[k] = jnp.stack([gf0[k], gf1[k]])
    g["ab_w_out"] = _mm_tn([ya, yb], dh1b, "mmt_ab_out")[None]
    dycat = _mm(dh1b, tr(p["ab_w_out"][0]), "mmb_ab_out")
    (dqb, dkb, dvb, dbias_t), late_slots = _attention_bwd(z0, bias_t, dycat, plan=exchange(LATE_BIG, g) if exchange else None)
    dqa, dka, dva, dga = _retention_bwd(z0, *tab, r, dycat, st)
    dz0 = [dqa, dka, dva, dga, dqb, dkb, dvb]
    g["ab_w_in"] = _mm_tn(hn0, dz0, "mmt_ab_in")[None]
    slots = {}
    if exchange:
        dhn0, first_slots = _mm(dz0, tr(p["ab_w_in"][0]), "mmb_ab_in", plan=exchange(FIRST_BIG, g))
        slots = dict(first=first_slots[0], late=late_slots[0])
    else:
        dhn0 = _mm(dz0, tr(p["ab_w_in"][0]), "mmb_ab_in")
    gx, _, dga0 = _rmsnorm_bwd(x, dhn0, row(p["attn_norm_g"][0]), dh1, "norm_a0_bwd")
    g["ab_rel_bias"] = _bias_grad(_bias_bands(dbias_t))[None, :, :N_REL]
    g["attn_norm_g"] = jnp.stack([dga0[0], dga1[0]])
    g["final_norm_g"] = dgfin[0]
    return lsum[0, 0], gx, g, slots


FIRST_BIG = [("ab_w_in", 2), ("ab_w_out", 1)]
LATE_BIG = [("c_w_in", 2), ("c_w_out", 1), ("ffn_w_up", 2), ("ffn_w_down", 1)]
BIG = FIRST_BIG + LATE_BIG
PACK_ROWS = 32 * COPY_PARTS
SMALL_SHARDED = [("c_ln_g", 1), ("c_ln_b", 1), ("ffn_conv_w", 2)]
REPLICATED = ["attn_norm_g", "ffn_norm_g", "ab_rel_bias", "c_w_s", "c_b_s", "ffn_conv_b", "final_norm_g"]


def _rows_of(n_elems):
    return -(-n_elems // PACK_W)


def _flat_rows(a):
    f = a.reshape(-1)
    rows = _rows_of(f.shape[0])
    return jnp.pad(f, (0, rows * PACK_W - f.shape[0])).reshape(rows, PACK_W)


def _pad_rows(a, mult):
    extra = (-a.shape[0]) % mult
    return jnp.pad(a, ((0, extra), (0, 0))) if extra else a


def _pack(arrs, mult):
    return _pad_rows(jnp.concatenate([_flat_rows(a) for a in arrs], axis=0), mult)


def _unpack(buf, shapes):
    out, r = [], 0
    for shp in shapes:
        n = math.prod(shp)
        rows = _rows_of(n)
        out.append(buf[r:r + rows].reshape(-1)[:n].reshape(shp))
        r += rows
    return out


def _shard_major(full, axis):
    shp = full.shape
    split = full.reshape(shp[:axis] + (N_CHIPS, shp[axis] // N_CHIPS) + shp[axis + 1:])
    return jnp.moveaxis(split, axis, 0)


def _from_shards(sh, axis):
    m = jnp.moveaxis(sh, 0, axis)
    shp = m.shape
    return m.reshape(shp[:axis] + (shp[axis] * shp[axis + 1],) + shp[axis + 2:])


def _as_bf16_pairs(a):
    return lax.bitcast_convert_type(a.astype(F32), BF16)


def _from_bf16_pairs(a):
    return lax.bitcast_convert_type(a, F32)


def kernel(x, attn_norm_g, ffn_norm_g, ab_w_in, ab_w_out, ab_rel_bias, c_w_in, c_ln_g, c_ln_b, c_w_s, c_b_s, c_w_out, ffn_w_up, ffn_conv_w, ffn_conv_b, ffn_w_down, final_norm_g, loss_target, m_attn_norm_g, m_ffn_norm_g, m_ab_w_in, m_ab_w_out, m_ab_rel_bias, m_c_w_in, m_c_ln_g, m_c_ln_b, m_c_w_s, m_c_b_s, m_c_w_out, m_ffn_w_up, m_ffn_conv_w, m_ffn_conv_b, m_ffn_w_down, m_final_norm_g, v_attn_norm_g, v_ffn_norm_g, v_ab_w_in, v_ab_w_out, v_ab_rel_bias, v_c_w_in, v_c_ln_g, v_c_ln_b, v_c_w_s, v_c_b_s, v_c_w_out, v_ffn_w_up, v_ffn_conv_w, v_ffn_conv_b, v_ffn_w_down, v_final_norm_g):
    w = dict(attn_norm_g=attn_norm_g, ffn_norm_g=ffn_norm_g, ab_w_in=ab_w_in, ab_w_out=ab_w_out, ab_rel_bias=ab_rel_bias,
             c_w_in=c_w_in, c_ln_g=c_ln_g, c_ln_b=c_ln_b, c_w_s=c_w_s, c_b_s=c_b_s, c_w_out=c_w_out, ffn_w_up=ffn_w_up,
             ffn_conv_w=ffn_conv_w, ffn_conv_b=ffn_conv_b, ffn_w_down=ffn_w_down, final_norm_g=final_norm_g)
    m = dict(attn_norm_g=m_attn_norm_g, ffn_norm_g=m_ffn_norm_g, ab_w_in=m_ab_w_in, ab_w_out=m_ab_w_out,
             ab_rel_bias=m_ab_rel_bias, c_w_in=m_c_w_in, c_ln_g=m_c_ln_g, c_ln_b=m_c_ln_b, c_w_s=m_c_w_s, c_b_s=m_c_b_s,
             c_w_out=m_c_w_out, ffn_w_up=m_ffn_w_up, ffn_conv_w=m_ffn_conv_w, ffn_conv_b=m_ffn_conv_b,
             ffn_w_down=m_ffn_w_down, final_norm_g=m_final_norm_g)
    v = dict(attn_norm_g=v_attn_norm_g, ffn_norm_g=v_ffn_norm_g, ab_w_in=v_ab_w_in, ab_w_out=v_ab_w_out,
             ab_rel_bias=v_ab_rel_bias, c_w_in=v_c_w_in, c_ln_g=v_c_ln_g, c_ln_b=v_c_ln_b, c_w_s=v_c_w_s, c_b_s=v_c_b_s,
             c_w_out=v_c_w_out, ffn_w_up=v_ffn_w_up, ffn_conv_w=v_ffn_conv_w, ffn_conv_b=v_ffn_conv_b,
             ffn_w_down=v_ffn_w_down, final_norm_g=v_final_norm_g)
    names = list(w)
    chip = 2 * lax.axis_index("x") + lax.axis_index("y")

    def gathered_weights(group, small, send):
        shapes = [a.shape for a in send]

        def finish(gathered):
            parts = [_unpack(gathered[s], shapes) for s in range(N_CHIPS)]
            stack = lambda i: jnp.stack([parts[s][i] for s in range(N_CHIPS)])
            out = {n: _from_shards(stack(i), axis) for i, (n, axis) in enumerate(group)}
            out.update({n: _from_shards(_from_bf16_pairs(stack(len(group) + i)), axis) for i, (n, axis) in enumerate(small)})
            return out

        return _GatherPlan(_pack(send, PACK_ROWS)), finish

    plan0, finish0 = gathered_weights(FIRST_BIG, [], [w[n].astype(BF16) for n, _ in FIRST_BIG])
    full = {n: w[n] for n in REPLICATED}
    full.update(finish0(_run_plan(plan0, "gather_first")[0]))
    late = gathered_weights(LATE_BIG, SMALL_SHARDED,
                            [w[n].astype(BF16) for n, _ in LATE_BIG] + [_as_bf16_pairs(w[n]) for n, _ in SMALL_SHARDED])

    def exchange(group, g):
        parts = [_shard_major(g[n], axis).reshape(N_CHIPS, -1, PACK_W) for n, axis in group]
        return _ExchangePlan(jnp.concatenate(parts, axis=1).astype(BF16), True)

    lsum, grad_x, g, slots = _local_step(x[0], loss_target[0], full, late=late, exchange=exchange)
    loss = lax.psum(0.5 * lsum, ("x", "y", "c"))

    big_grads, big_outs = {}, [{}, {}, {}]
    for key, group in (("late", LATE_BIG), ("first", FIRST_BIG)):
        gshard = _run_plan(_SwapPlan(_sum_slots(slots[key], f"sum_{key}")), f"swap_{key}")[0]
        pack_group = lambda d: jnp.concatenate([d[n].reshape(-1, PACK_W) for n, _ in group], axis=0)
        res = _adamw(pack_group(w), gshard, pack_group(m), pack_group(v), f"adamw_{key}")
        shapes = [w[n].shape for n, _ in group]
        big_grads.update(zip([n for n, _ in group], _unpack(gshard, shapes)))
        for k in range(3):
            big_outs[k].update(zip([n for n, _ in group], _unpack(res[k], shapes)))

    small_names = REPLICATED + [n for n, _ in SMALL_SHARDED]
    gsmall = _pack([g[n] for n in small_names], 32)
    gsum = _sum_slots(_run_plan(_ExchangePlan(gsmall, False), "exchange_small")[0], "sum_small")
    gsmall_full = dict(zip(small_names, _unpack(gsum, [g[n].shape for n in small_names])))
    for n, axis in SMALL_SHARDED:
        size = w[n].shape[axis]
        gsmall_full[n] = lax.dynamic_slice_in_dim(gsmall_full[n], chip * size, size, axis)
    pack_small = lambda d: _pack([d[n] for n in small_names], 8)
    small_out = _adamw(pack_small(w), pack_small(gsmall_full), pack_small(m), pack_small(v), "adamw_small")
    small_shapes = [w[n].shape for n in small_names]

    outs = [{**big_grads, **gsmall_full}]
    for k in range(3):
        outs.append({**big_outs[k], **dict(zip(small_names, _unpack(small_out[k], small_shapes)))})
    return (loss, grad_x[None], *[o[n] for o in outs for n in names])
```

```python
import functools
import math

import jax
import jax.numpy as jnp
from jax import lax
from jax.experimental import pallas as pl
from jax.experimental.pallas import tpu as pltpu

F32 = jnp.float32
BF16 = jnp.bfloat16
S = jax.ShapeDtypeStruct
MESH = pl.DeviceIdType.MESH

D_MODEL = 1024
CHUNK = 64
EPS = 1e-6
NEG_INF = -1e30
RET_HEADS, RET_QK, RET_V = 4, 128, 256
ATT_HEADS, ATT_D, ATT_PAST, MAX_REL = 8, 64, 8, 128
BAND = (ATT_PAST + 1) * CHUNK
PADK = ATT_PAST * CHUNK
SGU_BLOCK, SGU_GROUPS, SGU_WIDTH = 128, 8, 2048
SGU_GW = SGU_WIDTH // SGU_GROUPS
FFN_HIDDEN = 2816
N_REL = 2 * MAX_REL + 1
RET_SCALE = RET_QK ** -0.5
ATT_SCALE = ATT_D ** -0.5
ADAM_LR, ADAM_B1, ADAM_B2, ADAM_EPS, ADAM_WD, ADAM_STEP = 0.001, 0.9, 0.999, 1e-08, 0.01, 10

V7X_VMEM_BYTES = 64 * 1024 * 1024
VMEM_LIMIT = V7X_VMEM_BYTES * 7 // 8
LANES = 128
PACK_W = 1024
N_CHIPS = 4
N_DEV = 8

GELU_C = math.sqrt(2.0 / math.pi)
GELU_A = 0.044715


def _cparams(*sem):
    return pltpu.CompilerParams(dimension_semantics=tuple(sem) if sem else None, vmem_limit_bytes=VMEM_LIMIT)


def _tile(n, target, unit=LANES):
    best = None
    for t in range(unit, min(n, target) + 1, unit):
        if n % t == 0:
            best = t
    return best if best is not None else n


def _gelu(x):
    t = jnp.tanh(GELU_C * (x + GELU_A * x * x * x))
    return 0.5 * x * (1.0 + t)


def _gelu_and_grad(x):
    x2 = x * x
    t = jnp.tanh(GELU_C * (x + GELU_A * x2 * x))
    g = 0.5 * x * (1.0 + t)
    dg = 0.5 * (1.0 + t) + 0.5 * x * (1.0 - t * t) * (GELU_C * (1.0 + 3.0 * GELU_A * x2))
    return g, dg


def _sigmoid(x):
    return 1.0 / (1.0 + jnp.exp(-x))


def _dot(a, b):
    return jnp.dot(a, b, preferred_element_type=F32)


def _dot_nt(a, b):
    return lax.dot_general(a, b, (((1,), (1,)), ((), ())), preferred_element_type=F32)


def _dot_tn(a, b):
    return lax.dot_general(a, b, (((0,), (0,)), ((), ())), preferred_element_type=F32)


def _rmsnorm_fwd(x, g, name):
    T, D = x.shape
    tr = _tile(T, 512, 8)

    def body(x_ref, g_ref, o_ref):
        xv = x_ref[...]
        r = lax.rsqrt(jnp.mean(xv * xv, axis=-1, keepdims=True) + EPS)
        o_ref[...] = (xv * r * g_ref[...]).astype(o_ref.dtype)

    return pl.pallas_call(
        body, grid=(T // tr,),
        in_specs=[pl.BlockSpec((tr, D), lambda i: (i, 0)), pl.BlockSpec((1, D), lambda i: (0, 0))],
        out_specs=pl.BlockSpec((tr, D), lambda i: (i, 0)),
        out_shape=S((T, D), BF16), name=name, compiler_params=_cparams("parallel"))(x, g)


def _pieces(a):
    return list(a) if isinstance(a, (list, tuple)) else [a]


def _piece_layout(widths, tile):
    out, s = [], 0
    for w in widths:
        out.append((s, w // tile))
        s += w // tile
    return out


def _common_tile(widths, target):
    return _tile(functools.reduce(math.gcd, widths), target)


def _mm(a, b, name, res=None, out_dtype=F32, plan=None, norm_g=None, norm_bwd=None):
    pieces = _pieces(a)
    M = pieces[0].shape[0]
    widths = [p.shape[1] for p in pieces]
    K, N = sum(widths), b.shape[1]
    rowwise = norm_g is not None or norm_bwd is not None
    tm, tn, tk = _tile(M, 512 if norm_bwd is not None else 1024, 8), N if rowwise else _tile(N, 1408), _common_tile(widths, 1536)
    nk, npc = K // tk, len(pieces)
    layout = _piece_layout(widths, tk)
    tile = pl.BlockSpec((tm, tn), lambda i, j, k: (i, j))
    vec = pl.BlockSpec((1, tn), lambda i, j, k: (0, j))
    extra_in, extra_specs = [], []
    if res is not None:
        extra_in, extra_specs = [res], [tile]
    if norm_g is not None:
        extra_in, extra_specs = extra_in + [norm_g], extra_specs + [vec]
    if norm_bwd is not None:
        extra_in, extra_specs = extra_in + list(norm_bwd), extra_specs + [tile, vec, tile]
    n_extra = len(extra_in)
    if norm_bwd is not None:
        out_shape, out_specs = [S((M, N), F32), S((M, N), BF16), S((1, N), F32)], [tile, tile, vec]
    elif norm_g is not None:
        out_shape, out_specs = [S((M, N), out_dtype), S((M, N), BF16)], [tile, tile]
    else:
        out_shape, out_specs = [S((M, N), out_dtype)], [tile]

    def body(*refs):
        a_refs, b_ref = refs[:npc], refs[npc]
        ext = list(refs[npc + 1:npc + 1 + n_extra])
        outs = refs[npc + 1 + n_extra:npc + 1 + n_extra + len(out_shape)]
        first_rows = pl.program_id(0) == 0

        def finish(v):
            if norm_bwd is not None:
                x_ref, g_ref, dres_ref = ext[-3:]
                dx_ref, dxb_ref, dg_ref = outs

                @pl.when(first_rows)
                def _():
                    dg_ref[...] = jnp.zeros_like(dg_ref)

                xv = x_ref[...]
                r = lax.rsqrt(jnp.mean(xv * xv, axis=-1, keepdims=True) + EPS)
                xh = xv * r
                dg_ref[...] += jnp.sum(v * xh, axis=0, keepdims=True)
                dxh = v * g_ref[...]
                dx = dres_ref[...] + r * (dxh - xh * jnp.mean(dxh * xh, axis=-1, keepdims=True))
                dx_ref[...] = dx
                dxb_ref[...] = dx.astype(BF16)
                return
            if res is not None:
                v = v + ext[0][...]
            outs[0][...] = v.astype(outs[0].dtype)
            if norm_g is not None:
                r = lax.rsqrt(jnp.mean(v * v, axis=-1, keepdims=True) + EPS)
                outs[1][...] = (v * r * ext[-1][...]).astype(BF16)

        if nk == 1:
            finish(_dot(a_refs[0][...], b_ref[...]))
            return
        acc = refs[-1]
        k = pl.program_id(2)
        for a_ref, (s, c) in zip(a_refs, layout):
            def add(a_ref=a_ref):
                acc[...] += _dot(a_ref[...], b_ref[...])

            if s == 0:
                @pl.when(k == 0)
                def _(a_ref=a_ref):
                    acc[...] = _dot(a_ref[...], b_ref[...])

                if c > 1:
                    pl.when((k > 0) & (k < c))(add)
            else:
                pl.when((k >= s) & (k < s + c))(add)

        @pl.when(k == nk - 1)
        def _():
            finish(acc[...])

    in_specs = [pl.BlockSpec((tm, tk), lambda i, j, k, s=s, c=c: (i, jnp.clip(k - s, 0, c - 1))) for s, c in layout]
    in_specs.append(pl.BlockSpec((tk, tn), lambda i, j, k: (k, j)))
    outs, extra = _call(
        body, grid=(M // tm, N // tn, nk), in_specs=in_specs + extra_specs, out_specs=out_specs, out_shape=out_shape,
        scratch_shapes=[pltpu.VMEM((tm, tn), F32)] if nk > 1 else [],
        name=name, sem=("arbitrary",) * 3 if norm_bwd is not None else ("parallel", "parallel", "arbitrary"),
        args=pieces + [b] + extra_in, plan=plan)
    outs = outs[0] if len(outs) == 1 else tuple(outs)
    return outs if plan is None else (outs, extra)


def _mm_tn(a, g, name):
    ap, gp = _pieces(a), _pieces(g)
    T = ap[0].shape[0]
    aw, gw = [p.shape[1] for p in ap], [p.shape[1] for p in gp]
    tm, tn, tt = _common_tile(aw, 1408), _common_tile(gw, 1408), _tile(T, 1024, 8)
    alay, glay = _piece_layout(aw, tm), _piece_layout(gw, tn)
    na = len(ap)

    def inside(idx, s, c, single):
        return None if single else (idx >= s) & (idx < s + c)

    def body(*refs):
        a_refs, g_refs, o_ref = refs[:na], refs[na:-1], refs[-1]
        i, j = pl.program_id(0), pl.program_id(1)

        @pl.when(pl.program_id(2) == 0)
        def _():
            o_ref[...] = jnp.zeros_like(o_ref)

        for a_ref, (sa, ca) in zip(a_refs, alay):
            for g_ref, (sg, cg) in zip(g_refs, glay):
                def add(a_ref=a_ref, g_ref=g_ref):
                    o_ref[...] += _dot_tn(a_ref[...], g_ref[...])

                conds = [c for c in (inside(i, sa, ca, na == 1), inside(j, sg, cg, len(gp) == 1)) if c is not None]
                if not conds:
                    add()
                else:
                    pl.when(functools.reduce(lambda u, v: u & v, conds))(add)

    def spec(tile, lay, single, axis):
        s, c = lay

        def index(i, j, k):
            idx = (i, j)[axis]
            if single:
                return (k, idx)
            on = (idx >= s) & (idx < s + c)
            return (jnp.where(on, k, 0), jnp.clip(idx - s, 0, c - 1))

        return pl.BlockSpec((tt, tile), index)

    in_specs = [spec(tm, lay, na == 1, 0) for lay in alay] + [spec(tn, lay, len(gp) == 1, 1) for lay in glay]
    return pl.pallas_call(
        body, grid=(sum(aw) // tm, sum(gw) // tn, T // tt), in_specs=in_specs,
        out_specs=pl.BlockSpec((tm, tn), lambda i, j, k: (i, j)),
        out_shape=S((sum(aw), sum(gw)), F32), name=name,
        compiler_params=_cparams("parallel", "parallel", "arbitrary"))(*ap, *gp)


def _rotate(x, c2, s2):
    return x * c2 + pltpu.roll(x, RET_QK // 2, 1) * s2


def _unrotate(d, c2, s2):
    return d * c2 - pltpu.roll(d, RET_QK // 2, 1) * s2


def _ret_specs(RB, blockmap):
    q = pl.BlockSpec((RB, RET_QK), lambda h, n: (blockmap(n), h))
    k = pl.BlockSpec((RB, RET_QK), lambda h, n: (blockmap(n), RET_HEADS + h))
    v = pl.BlockSpec((RB, RET_V), lambda h, n: (blockmap(n), RET_HEADS + h))
    g = pl.BlockSpec((RB, RET_V), lambda h, n: (blockmap(n), 2 * RET_HEADS + h))
    tab = pl.BlockSpec((RB, RET_QK), lambda h, n: (blockmap(n), 0))
    return q, k, v, g, tab


def _ret_decay_specs():
    return [pl.BlockSpec((None, CHUNK, CHUNK), lambda h, n: (h, 0, 0)),
            pl.BlockSpec((None, CHUNK, RET_QK), lambda h, n: (h, 0, 0)),
            pl.BlockSpec((None, CHUNK, RET_QK), lambda h, n: (h, 0, 0)),
            pl.BlockSpec((None, 1, RET_V), lambda h, n: (h, 0, 0))]


def _retention_fwd(z, c2, s2, dintra, qdec, kdec, cdec):
    T = z.shape[0]
    RB = min(512, T)
    nch, nb = RB // CHUNK, T // RB

    def body(q_ref, k_ref, v_ref, g_ref, c2_ref, s2_ref, di_ref, qd_ref, kd_ref, cd_ref, ya_ref, r_ref, st_ref, state):
        @pl.when(pl.program_id(1) == 0)
        def _():
            state[...] = jnp.zeros_like(state)

        dmat, qdv, kdv, cdv = di_ref[...], qd_ref[...], kd_ref[...], cd_ref[...]
        for c in range(nch):
            rows = slice(c * CHUNK, (c + 1) * CHUNK)
            c2v, s2v = c2_ref[rows, :], s2_ref[rows, :]
            qr = _rotate(q_ref[rows, :], c2v, s2v)
            kr = _rotate(k_ref[rows, :], c2v, s2v) * RET_SCALE
            vb = v_ref[rows, :].astype(BF16)
            sm = _dot_nt(qr.astype(BF16), kr.astype(BF16)) * dmat
            sb = state[...].astype(BF16)
            st_ref[c] = sb
            o = _dot(sm.astype(BF16), vb) + _dot((qr * qdv).astype(BF16), sb)
            state[...] = state[...] * cdv + _dot_tn((kr * kdv).astype(BF16), vb)
            r_ref[rows, :] = o
            mu = jnp.mean(o, axis=-1, keepdims=True)
            oc = o - mu
            rn = oc * lax.rsqrt(jnp.mean(oc * oc, axis=-1, keepdims=True) + EPS)
            gv = g_ref[rows, :]
            ya_ref[rows, :] = (gv * _sigmoid(gv) * rn).astype(BF16)

    q, k, v, g, tab = _ret_specs(RB, lambda n: n)
    wide = pl.BlockSpec((RB, RET_V), lambda h, n: (n, h))
    return pl.pallas_call(
        body, grid=(RET_HEADS, nb),
        in_specs=[q, k, v, g, tab, tab] + _ret_decay_specs(),
        out_specs=[wide, wide, pl.BlockSpec((None, nch, RET_QK, RET_V), lambda h, n: (h, n, 0, 0))],
        out_shape=[S((T, RET_HEADS * RET_V), BF16), S((T, RET_HEADS * RET_V), F32),
                   S((RET_HEADS, T // CHUNK, RET_QK, RET_V), BF16)],
        scratch_shapes=[pltpu.VMEM((RET_QK, RET_V), F32)], name="retention_fwd",
        compiler_params=_cparams("parallel", "arbitrary"))(z, z, z, z, c2, s2, dintra, qdec, kdec, cdec)


def _retention_bwd(z, c2, s2, dintra, qdec, kdec, cdec, r, dycat, st):
    T = z.shape[0]
    RB = min(512, T)
    nch, nb = RB // CHUNK, T // RB

    def body(q_ref, k_ref, v_ref, g_ref, c2_ref, s2_ref, di_ref, qd_ref, kd_ref, cd_ref, r_ref, dy_ref, st_ref,
             dq_ref, dk_ref, dv_ref, dg_ref, dstate):
        @pl.when(pl.program_id(1) == 0)
        def _():
            dstate[...] = jnp.zeros_like(dstate)

        dmat, qdv, kdv, cdv = di_ref[...], qd_ref[...], kd_ref[...], cd_ref[...]
        for c in reversed(range(nch)):
            rows = slice(c * CHUNK, (c + 1) * CHUNK)
            c2v, s2v = c2_ref[rows, :], s2_ref[rows, :]
            qr = _rotate(q_ref[rows, :], c2v, s2v)
            kr = _rotate(k_ref[rows, :], c2v, s2v) * RET_SCALE
            qb, kb = qr.astype(BF16), kr.astype(BF16)
            vb = v_ref[rows, :].astype(BF16)
            o, gv, dy = r_ref[rows, :], g_ref[rows, :], dy_ref[rows, :]
            mu = jnp.mean(o, axis=-1, keepdims=True)
            oc = o - mu
            rstd = lax.rsqrt(jnp.mean(oc * oc, axis=-1, keepdims=True) + EPS)
            rn = oc * rstd
            sg = _sigmoid(gv)
            dg_ref[rows, :] = (dy * rn * (sg * (1.0 + gv * (1.0 - sg)))).astype(BF16)
            drn = dy * (gv * sg)
            do = rstd * (drn - jnp.mean(drn, axis=-1, keepdims=True) - rn * jnp.mean(drn * rn, axis=-1, keepdims=True))
            dob = do.astype(BF16)
            sm = (_dot_nt(qb, kb) * dmat).astype(BF16)
            kdb = (kr * kdv).astype(BF16)
            dsb = dstate[...].astype(BF16)
            dv_ref[rows, :] = (_dot_tn(sm, dob) + _dot(kdb, dsb)).astype(BF16)
            ds = (_dot_nt(dob, vb) * dmat).astype(BF16)
            dqr = _dot(ds, kb) + _dot_nt(dob, st_ref[c]) * qdv
            dkr = (_dot_tn(ds, qb) + _dot_nt(vb, dsb) * kdv) * RET_SCALE
            dstate[...] = dstate[...] * cdv + _dot_tn((qr * qdv).astype(BF16), dob)
            dq_ref[rows, :] = _unrotate(dqr, c2v, s2v).astype(BF16)
            dk_ref[rows, :] = _unrotate(dkr, c2v, s2v).astype(BF16)

    rev = lambda n: nb - 1 - n
    q, k, v, g, tab = _ret_specs(RB, rev)
    wide = pl.BlockSpec((RB, RET_V), lambda h, n: (rev(n), h))
    narrow = pl.BlockSpec((RB, RET_QK), lambda h, n: (rev(n), h))
    return pl.pallas_call(
        body, grid=(RET_HEADS, nb),
        in_specs=[q, k, v, g, tab, tab] + _ret_decay_specs() + [
            wide, wide, pl.BlockSpec((None, nch, RET_QK, RET_V), lambda h, n: (h, rev(n), 0, 0))],
        out_specs=[narrow, narrow, wide, wide],
        out_shape=[S((T, RET_HEADS * RET_QK), BF16), S((T, RET_HEADS * RET_QK), BF16),
                   S((T, RET_HEADS * RET_V), BF16), S((T, RET_HEADS * RET_V), BF16)],
        scratch_shapes=[pltpu.VMEM((RET_QK, RET_V), F32)], name="retention_bwd",
        compiler_params=_cparams("parallel", "arbitrary"))(z, z, z, z, c2, s2, dintra, qdec, kdec, cdec, r, dycat, st)


def _rel_index(i):
    r = lax.broadcasted_iota(jnp.int32, (3 * LANES, 5 * LANES), 0)
    j = lax.broadcasted_iota(jnp.int32, (3 * LANES, 5 * LANES), 1)
    idx = jnp.clip(i + PADK - j, -MAX_REL, MAX_REL) + MAX_REL
    return (r == idx).astype(BF16)


def _split3(v):
    hi = v.astype(BF16)
    r1 = v - hi.astype(F32)
    mid = r1.astype(BF16)
    lo = (r1 - mid.astype(F32)).astype(BF16)
    return hi, mid, lo


def _bias_build(rb):
    rbp = jnp.pad(rb, ((0, 0), (0, 3 * LANES - N_REL)))

    def body(rb_ref, o_ref):
        e = _rel_index(pl.program_id(0))
        hi, mid, lo = _split3(rb_ref[...])
        o_ref[...] = _dot(hi, e) + _dot(mid, e) + _dot(lo, e)

    return pl.pallas_call(
        body, grid=(CHUNK,), in_specs=[pl.BlockSpec((ATT_HEADS, 3 * LANES), lambda i: (0, 0))],
        out_specs=pl.BlockSpec((None, ATT_HEADS, 5 * LANES), lambda i: (i, 0, 0)),
        out_shape=S((CHUNK, ATT_HEADS, 5 * LANES), F32), name="bias_build",
        compiler_params=_cparams("parallel"))(rbp)


ATT_RB = 512
ATT_QT = 256
ATT_CPT = ATT_QT // CHUNK
ATT_KT = ATT_QT + PADK
ATT_QCOL = (2 * RET_HEADS * RET_QK + 2 * RET_HEADS * RET_V) // LANES
ATT_KCOL = ATT_QCOL + ATT_HEADS * ATT_D // LANES
ATT_VCOL = ATT_KCOL + ATT_HEADS * ATT_D // LANES


def _bias_grad(dbt):
    def body(d_ref, o_ref):
        @pl.when(pl.program_id(0) == 0)
        def _():
            o_ref[...] = jnp.zeros_like(o_ref)

        e = _rel_index(pl.program_id(0))
        d = d_ref[0]
        for ci in range(1, ATT_CPT):
            d = d + d_ref[ci]
        hi, mid, lo = _split3(d)
        o_ref[...] += _dot_nt(hi, e) + _dot_nt(mid, e) + _dot_nt(lo, e)

    return pl.pallas_call(
        body, grid=(CHUNK,),
        in_specs=[pl.BlockSpec((ATT_CPT, None, ATT_HEADS, 5 * LANES), lambda i: (0, i, 0, 0))],
        out_specs=pl.BlockSpec((ATT_HEADS, 3 * LANES), lambda i: (0, 0)),
        out_shape=S((ATT_HEADS, 3 * LANES), F32), name="bias_grad",
        compiler_params=_cparams("arbitrary"))(dbt)


def _bias_tiles(bias):
    parts = [jnp.pad(bias, ((0, 0), (0, 0), (CHUNK * ci, ATT_KT - BAND - CHUNK * ci)), constant_values=NEG_INF)
             for ci in range(ATT_CPT)]
    return jnp.stack(parts, axis=1).reshape(ATT_HEADS, ATT_QT, ATT_KT)


def _bias_bands(dbias_tiles):
    d = dbias_tiles.reshape(ATT_HEADS, ATT_CPT, CHUNK, ATT_KT)
    bands = jnp.stack([d[:, ci, :, CHUNK * ci:CHUNK * ci + BAND] for ci in range(ATT_CPT)])
    return jnp.pad(jnp.transpose(bands, (0, 2, 1, 3)), ((0, 0), (0, 0), (0, 0), (0, 5 * LANES - BAND)))


def _att_fill(kw, vw, klo, khi, vlo, vhi):
    kw[0:ATT_RB, :] = klo[...].astype(BF16)
    kw[ATT_RB:, :] = khi[...].astype(BF16)
    vw[0:ATT_RB, :] = vlo[...].astype(BF16)
    vw[ATT_RB:, :] = vhi[...].astype(BF16)


def _att_probs(qm, kwin, bias, first_key):
    s = _dot_nt(qm, kwin) * ATT_SCALE + bias
    col = lax.broadcasted_iota(jnp.int32, (ATT_QT, ATT_KT), 1)
    s = jnp.where(col + first_key >= 0, s, NEG_INF)
    p = jnp.exp(s - jnp.max(s, axis=-1, keepdims=True))
    return p / jnp.sum(p, axis=-1, keepdims=True)


def _att_in_specs(nq):
    qn = lambda n: jnp.minimum(n, nq - 1)
    blk = lambda col, back: pl.BlockSpec((ATT_RB, LANES), lambda hp, n: (jnp.maximum(qn(n) - back, 0), col + hp))
    return [blk(ATT_QCOL, 0), blk(ATT_KCOL, 1), blk(ATT_KCOL, 0), blk(ATT_VCOL, 1), blk(ATT_VCOL, 0),
            pl.BlockSpec((2, ATT_QT, ATT_KT), lambda hp, n: (hp, 0, 0))]


def _attention_fwd(z, bias_t, plan=None):
    T = z.shape[0]
    nq = T // ATT_RB

    def body(q_ref, klo, khi, vlo, vhi, b_ref, o_ref, kw, vw):
        _att_fill(kw, vw, klo, khi, vlo, vhi)
        lane = lax.broadcasted_iota(jnp.int32, (ATT_QT, LANES), 1)
        n = pl.program_id(1)
        for t in range(ATT_RB // ATT_QT):
            rows = slice(t * ATT_QT, (t + 1) * ATT_QT)
            win = slice(t * ATT_QT, t * ATT_QT + ATT_KT)
            qc = q_ref[rows, :]
            outs = []
            for e in range(2):
                qm = jnp.where((lane >= ATT_D) == (e == 1), qc, 0.0).astype(BF16)
                p = _att_probs(qm, kw[win, :], b_ref[e], (n - 1) * ATT_RB + t * ATT_QT)
                outs.append(_dot(p.astype(BF16), vw[win, :]))
            o_ref[rows, :] = jnp.where(lane < ATT_D, outs[0], outs[1]).astype(BF16)

    (yb,), extra = _call(
        body, grid=(ATT_HEADS // 2, nq), in_specs=_att_in_specs(nq),
        out_specs=[pl.BlockSpec((ATT_RB, LANES), lambda hp, n: (n, hp))],
        out_shape=[S((T, ATT_HEADS * ATT_D), BF16)],
        scratch_shapes=[pltpu.VMEM((2 * ATT_RB, LANES), BF16), pltpu.VMEM((2 * ATT_RB, LANES), BF16)],
        name="attention_fwd", sem=("parallel", "parallel"), args=(z, z, z, z, z, bias_t), plan=plan)
    return yb, extra


def _attention_bwd(z, bias_t, dycat, plan=None):
    T = z.shape[0]
    nq = T // ATT_RB
    dycol = RET_HEADS * RET_V // LANES

    def body(q_ref, klo, khi, vlo, vhi, b_ref, dy_ref, dq_ref, dk_ref, dv_ref, db_ref, kw, vw, dkw, dvw):
        n = pl.program_id(1)

        @pl.when(n == 0)
        def _():
            dkw[...] = jnp.zeros_like(dkw)
            dvw[...] = jnp.zeros_like(dvw)
            db_ref[...] = jnp.zeros_like(db_ref)

        @pl.when(n > 0)
        def _():
            dkw[0:ATT_RB, :] = dkw[ATT_RB:, :]
            dvw[0:ATT_RB, :] = dvw[ATT_RB:, :]
            dkw[ATT_RB:, :] = jnp.zeros((ATT_RB, LANES), F32)
            dvw[ATT_RB:, :] = jnp.zeros((ATT_RB, LANES), F32)

        @pl.when(n < nq)
        def _():
            _att_fill(kw, vw, klo, khi, vlo, vhi)
            lane = lax.broadcasted_iota(jnp.int32, (ATT_QT, LANES), 1)
            for t in range(ATT_RB // ATT_QT):
                rows = slice(t * ATT_QT, (t + 1) * ATT_QT)
                win = slice(t * ATT_QT, t * ATT_QT + ATT_KT)
                qc, dyc = q_ref[rows, :], dy_ref[rows, :]
                kwin, vwin = kw[win, :], vw[win, :]
                dq = jnp.zeros((ATT_QT, LANES), F32)
                for e in range(2):
                    mine = (lane >= ATT_D) == (e == 1)
                    qm = jnp.where(mine, qc, 0.0).astype(BF16)
                    dom = jnp.where(mine, dyc, 0.0).astype(BF16)
                    p = _att_probs(qm, kwin, b_ref[e], (n - 1) * ATT_RB + t * ATT_QT)
                    dp = _dot_nt(dom, vwin)
                    ds = p * (dp - jnp.sum(dp * p, axis=-1, keepdims=True))
                    db_ref[e] += ds
                    dsb = (ds * ATT_SCALE).astype(BF16)
                    dq = dq + jnp.where(mine, _dot(dsb, kwin), 0.0)
                    dkw[win, :] += _dot_tn(dsb, qm)
                    dvw[win, :] += _dot_tn(p.astype(BF16), dom)
                dq_ref[rows, :] = dq.astype(BF16)

        dk_ref[...] = dkw[0:ATT_RB, :].astype(BF16)
        dv_ref[...] = dvw[0:ATT_RB, :].astype(BF16)

    qn = lambda n: jnp.minimum(n, nq - 1)
    out_kv = pl.BlockSpec((ATT_RB, LANES), lambda hp, n: (jnp.maximum(n - 1, 0), hp))
    return _call(
        body, grid=(ATT_HEADS // 2, nq + 1),
        in_specs=_att_in_specs(nq) + [pl.BlockSpec((ATT_RB, LANES), lambda hp, n: (qn(n), dycol + hp))],
        out_specs=[pl.BlockSpec((ATT_RB, LANES), lambda hp, n: (qn(n), hp)), out_kv, out_kv,
                   pl.BlockSpec((2, ATT_QT, ATT_KT), lambda hp, n: (hp, 0, 0))],
        out_shape=[S((T, ATT_HEADS * ATT_D), BF16), S((T, ATT_HEADS * ATT_D), BF16),
                   S((T, ATT_HEADS * ATT_D), BF16), S((ATT_HEADS, ATT_QT, ATT_KT), F32)],
        scratch_shapes=[pltpu.VMEM((2 * ATT_RB, LANES), BF16), pltpu.VMEM((2 * ATT_RB, LANES), BF16),
                        pltpu.VMEM((2 * ATT_RB, LANES), F32), pltpu.VMEM((2 * ATT_RB, LANES), F32)],
        name="attention_bwd", sem=("parallel", "arbitrary"), args=(z, z, z, z, z, bias_t, dycat), plan=plan)


HALO = 8


def _conv_specs(tb, tc, nct, T):
    per = tb // HALO
    last = T // HALO - 1

    def at(half):
        off = half * nct
        return [pl.BlockSpec((HALO, tc), lambda j, i: (jnp.maximum(i * per - 1, 0), j + off)),
                pl.BlockSpec((tb, tc), lambda j, i: (i, j + off)),
                pl.BlockSpec((HALO, tc), lambda j, i: (jnp.minimum((i + 1) * per, last), j + off))]

    return at(0), at(1)


def _causal_conv(ext, w_ref, b_ref):
    zc = w_ref[0:1, :] * pltpu.roll(ext, 2, 0) + w_ref[1:2, :] * pltpu.roll(ext, 1, 0) + w_ref[2:3, :] * ext + b_ref[...]
    return zc[HALO:]


def _convglu_fwd(z, cw, cb, name):
    T = z.shape[0]
    tb, tc = _tile(T, 1024, 8), 256
    nct = FFN_HIDDEN // tc

    def body(gp_ref, g_ref, up_ref, u_ref, wg_ref, wu_ref, bg_ref, bu_ref, o_ref):
        first = pl.program_id(1) == 0

        def conv(p_ref, blk_ref, w_ref, b_ref):
            prev = jnp.where(first, 0.0, p_ref[...])
            return _causal_conv(jnp.concatenate([prev, blk_ref[...]], axis=0), w_ref, b_ref)

        o_ref[...] = (_gelu(conv(gp_ref, g_ref, wg_ref, bg_ref)) * conv(up_ref, u_ref, wu_ref, bu_ref)).astype(BF16)

    (gp, gb, _), (up, ub, _) = _conv_specs(tb, tc, nct, T)
    wspec = lambda off, rows: pl.BlockSpec((rows, tc), lambda j, i: (0, j + off))
    return pl.pallas_call(
        body, grid=(nct, T // tb),
        in_specs=[gp, gb, up, ub, wspec(0, 3), wspec(nct, 3), wspec(0, 1), wspec(nct, 1)],
        out_specs=pl.BlockSpec((tb, tc), lambda j, i: (i, j)),
        out_shape=S((T, FFN_HIDDEN), BF16), name=name,
        compiler_params=_cparams("parallel", "parallel"))(z, z, z, z, cw, cw, cb, cb)


def _convglu_bwd(z, df, cw, cb, name):
    T = z.shape[0]
    tb, tc = _tile(T, 1024, 8), 256
    nct = FFN_HIDDEN // tc
    nrb = T // tb

    def body(gp_ref, g_ref, gn_ref, up_ref, u_ref, un_ref, df_ref, dfn_ref, wg_ref, wu_ref, bg_ref, bu_ref,
             dzg_ref, dzu_ref, dwg_ref, dwu_ref, dbg_ref, dbu_ref):
        i = pl.program_id(1)
        first, last = i == 0, i == nrb - 1

        @pl.when(first)
        def _():
            for ref in (dwg_ref, dwu_ref, dbg_ref, dbu_ref):
                ref[...] = jnp.zeros_like(ref)

        def ext_of(p_ref, blk_ref, n_ref):
            return jnp.concatenate([jnp.where(first, 0.0, p_ref[...]), blk_ref[...], n_ref[...]], axis=0)

        gext, uext = ext_of(gp_ref, g_ref, gn_ref), ext_of(up_ref, u_ref, un_ref)
        gc, uc = _causal_conv(gext, wg_ref, bg_ref), _causal_conv(uext, wu_ref, bu_ref)
        dfe = jnp.concatenate([df_ref[...], jnp.where(last, 0.0, dfn_ref[...])], axis=0)
        ge, gd = _gelu_and_grad(gc)
        dgc, duc = dfe * uc * gd, dfe * ge
        n = tb + HALO

        def back(d, ext, w_ref, dz_ref, dw_ref, db_ref):
            dz = w_ref[2:3, :] * d + w_ref[1:2, :] * pltpu.roll(d, n - 1, 0) + w_ref[0:1, :] * pltpu.roll(d, n - 2, 0)
            dz_ref[...] = dz[:tb].astype(BF16)
            dblk = d[:tb]
            db_ref[...] += jnp.sum(dblk, axis=0, keepdims=True)
            taps = [pltpu.roll(ext, 2, 0)[HALO:HALO + tb], pltpu.roll(ext, 1, 0)[HALO:HALO + tb], ext[HALO:HALO + tb]]
            for k, t in enumerate(taps):
                dw_ref[k:k + 1, :] += jnp.sum(dblk * t, axis=0, keepdims=True)

        back(dgc, gext, wg_ref, dzg_ref, dwg_ref, dbg_ref)
        back(duc, uext, wu_ref, dzu_ref, dwu_ref, dbu_ref)

    gspecs, uspecs = _conv_specs(tb, tc, nct, T)
    per = tb // HALO
    dfs = [pl.BlockSpec((tb, tc), lambda j, i: (i, j)),
           pl.BlockSpec((HALO, tc), lambda j, i: (jnp.minimum((i + 1) * per, T // HALO - 1), j))]
    wspec = lambda off, rows: pl.BlockSpec((rows, tc), lambda j, i: (0, j + off))
    acc = lambda rows: pl.BlockSpec((rows, tc), lambda j, i: (0, j))
    blk = pl.BlockSpec((tb, tc), lambda j, i: (i, j))
    return pl.pallas_call(
        body, grid=(nct, nrb),
        in_specs=gspecs + uspecs + dfs + [wspec(0, 3), wspec(nct, 3), wspec(0, 1), wspec(nct, 1)],
        out_specs=[blk, blk, acc(3), acc(3), acc(1), acc(1)],
        out_shape=[S((T, FFN_HIDDEN), BF16), S((T, FFN_HIDDEN), BF16), S((3, FFN_HIDDEN), F32), S((3, FFN_HIDDEN), F32),
                   S((1, FFN_HIDDEN), F32), S((1, FFN_HIDDEN), F32)],
        name=name, compiler_params=_cparams("parallel", "arbitrary"))(z, z, z, z, z, z, df, df, cw, cw, cb, cb)


SGU_RB = 256


def _sgu_weights(ws_ref):
    i = lax.broadcasted_iota(jnp.int32, (SGU_BLOCK, SGU_BLOCK), 0)
    j = lax.broadcasted_iota(jnp.int32, (SGU_BLOCK, SGU_BLOCK), 1)
    mask = (j < CHUNK) | (i >= CHUNK)
    return mask, [jnp.where(mask, ws_ref[g], 0.0).astype(BF16) for g in range(SGU_GROUPS)]


def _sgu_norm(zv, lng, lnb):
    mu = jnp.mean(zv, axis=-1, keepdims=True)
    vc = zv - mu
    rstd = lax.rsqrt(jnp.mean(vc * vc, axis=-1, keepdims=True) + EPS)
    vh = vc * rstd
    return vh, rstd, vh * lng + lnb


def _sgu_fwd(zpre, lng, lnb, ws, bst):
    T = zpre.shape[0]
    nb = SGU_RB // SGU_BLOCK

    def body(z_ref, lng_ref, lnb_ref, ws_ref, bst_ref, o_ref):
        _, wm = _sgu_weights(ws_ref)
        u = _gelu(z_ref[:, :SGU_WIDTH])
        _, _, vn = _sgu_norm(_gelu(z_ref[:, SGU_WIDTH:]), lng_ref[...], lnb_ref[...])
        vnb = vn.astype(BF16)
        for b in range(nb):
            rows = slice(b * SGU_BLOCK, (b + 1) * SGU_BLOCK)
            for g in range(SGU_GROUPS):
                cols = slice(g * SGU_GW, (g + 1) * SGU_GW)
                mixed = _dot(wm[g], vnb[rows, cols]) + bst_ref[:, g:g + 1]
                o_ref[rows, cols] = (u[rows, cols] * mixed).astype(BF16)

    vec = pl.BlockSpec((1, SGU_WIDTH), lambda i: (0, 0))
    return pl.pallas_call(
        body, grid=(T // SGU_RB,),
        in_specs=[pl.BlockSpec((SGU_RB, 2 * SGU_WIDTH), lambda i: (i, 0)), vec, vec,
                  pl.BlockSpec((SGU_GROUPS, SGU_BLOCK, SGU_BLOCK), lambda i: (0, 0, 0)),
                  pl.BlockSpec((SGU_BLOCK, SGU_GROUPS), lambda i: (0, 0))],
        out_specs=pl.BlockSpec((SGU_RB, SGU_WIDTH), lambda i: (i, 0)),
        out_shape=S((T, SGU_WIDTH), BF16), name="sgu_fwd", compiler_params=_cparams("parallel"))(zpre, lng, lnb, ws, bst)


def _sgu_bwd(zpre, dy, lng, lnb, ws, bst):
    T = zpre.shape[0]
    nb = SGU_RB // SGU_BLOCK

    def body(z_ref, dy_ref, lng_ref, lnb_ref, ws_ref, bst_ref, dz_ref, dws_ref, dbst_ref, dlng_ref, dlnb_ref, dvn):
        @pl.when(pl.program_id(0) == 0)
        def _():
            for ref in (dws_ref, dbst_ref, dlng_ref, dlnb_ref):
                ref[...] = jnp.zeros_like(ref)

        mask, wm = _sgu_weights(ws_ref)
        u, ud = _gelu_and_grad(z_ref[:, :SGU_WIDTH])
        v, vd = _gelu_and_grad(z_ref[:, SGU_WIDTH:])
        vh, rstd, vn = _sgu_norm(v, lng_ref[...], lnb_ref[...])
        vnb = vn.astype(BF16)
        lane8 = lax.broadcasted_iota(jnp.int32, (SGU_BLOCK, SGU_GROUPS), 1)
        dbs = jnp.zeros((SGU_BLOCK, SGU_GROUPS), F32)
        for b in range(nb):
            rows = slice(b * SGU_BLOCK, (b + 1) * SGU_BLOCK)
            for g in range(SGU_GROUPS):
                cols = slice(g * SGU_GW, (g + 1) * SGU_GW)
                vg = vnb[rows, cols]
                mixed = _dot(wm[g], vg) + bst_ref[:, g:g + 1]
                dyv = dy_ref[rows, cols]
                dz_ref[rows, cols] = (dyv * mixed * ud[rows, cols]).astype(BF16)
                dmix = dyv * u[rows, cols]
                dmb = dmix.astype(BF16)
                dvn[rows, cols] = _dot_tn(wm[g], dmb)
                dws_ref[g] += jnp.where(mask, _dot_nt(dmb, vg), 0.0)
                dbs = dbs + jnp.where(lane8 == g, jnp.sum(dmix, axis=-1, keepdims=True), 0.0)
        dbst_ref[...] += dbs
        dvnv = dvn[...]
        dlng_ref[...] += jnp.sum(dvnv * vh, axis=0, keepdims=True)
        dlnb_ref[...] += jnp.sum(dvnv, axis=0, keepdims=True)
        dvh = dvnv * lng_ref[...]
        dv = rstd * (dvh - jnp.mean(dvh, axis=-1, keepdims=True) - vh * jnp.mean(dvh * vh, axis=-1, keepdims=True))
        dz_ref[:, SGU_WIDTH:] = (dv * vd).astype(BF16)

    vec = pl.BlockSpec((1, SGU_WIDTH), lambda i: (0, 0))
    wsp = pl.BlockSpec((SGU_GROUPS, SGU_BLOCK, SGU_BLOCK), lambda i: (0, 0, 0))
    bsp = pl.BlockSpec((SGU_BLOCK, SGU_GROUPS), lambda i: (0, 0))
    return pl.pallas_call(
        body, grid=(T // SGU_RB,),
        in_specs=[pl.BlockSpec((SGU_RB, 2 * SGU_WIDTH), lambda i: (i, 0)),
                  pl.BlockSpec((SGU_RB, SGU_WIDTH), lambda i: (i, 0)), vec, vec, wsp, bsp],
        out_specs=[pl.BlockSpec((SGU_RB, 2 * SGU_WIDTH), lambda i: (i, 0)), wsp, bsp, vec, vec],
        out_shape=[S((T, 2 * SGU_WIDTH), BF16), S((SGU_GROUPS, SGU_BLOCK, SGU_BLOCK), F32),
                   S((SGU_BLOCK, SGU_GROUPS), F32), S((1, SGU_WIDTH), F32), S((1, SGU_WIDTH), F32)],
        scratch_shapes=[pltpu.VMEM((SGU_RB, SGU_WIDTH), F32)], name="sgu_bwd",
        compiler_params=_cparams("arbitrary"))(zpre, dy, lng, lnb, ws, bst)


def _loss_head(h, tgt, g):
    T, D = h.shape
    tr = _tile(T, 512, 8)

    def body(h_ref, t_ref, g_ref, ls_ref, dh_ref, dhb_ref, dg_ref):
        @pl.when(pl.program_id(0) == 0)
        def _():
            ls_ref[...] = jnp.zeros_like(ls_ref)
            dg_ref[...] = jnp.zeros_like(dg_ref)

        hv = h_ref[...]
        r = lax.rsqrt(jnp.mean(hv * hv, axis=-1, keepdims=True) + EPS)
        xh = hv * r
        diff = xh * g_ref[...] - t_ref[...]
        per_row = jnp.mean(diff * diff, axis=-1, keepdims=True)
        ls_ref[...] += jnp.sum(per_row, axis=0, keepdims=True)
        dy = diff * (1.0 / D)
        dg_ref[...] += jnp.sum(dy * xh, axis=0, keepdims=True)
        dxh = dy * g_ref[...]
        dh = r * (dxh - xh * jnp.mean(dxh * xh, axis=-1, keepdims=True))
        dh_ref[...] = dh
        dhb_ref[...] = dh.astype(BF16)

    row = pl.BlockSpec((tr, D), lambda i: (i, 0))
    vec = pl.BlockSpec((1, D), lambda i: (0, 0))
    return pl.pallas_call(
        body, grid=(T // tr,), in_specs=[row, row, vec],
        out_specs=[pl.BlockSpec((1, LANES), lambda i: (0, 0)), row, row, vec],
        out_shape=[S((1, LANES), F32), S((T, D), F32), S((T, D), BF16), S((1, D), F32)],
        name="loss_head", compiler_params=_cparams("arbitrary"))(h, tgt, g)


ANY = pl.BlockSpec(memory_space=pl.ANY)
COPY_PARTS = 4
SWAP_PARTS = 8
DMA = pltpu.SemaphoreType.DMA


def _place():
    return lax.axis_index("x"), lax.axis_index("y"), lax.axis_index("c")


def _nparts(rows, unit, want):
    n = want
    while n > 1 and rows % (unit * n):
        n //= 2
    return n


def _row_unit(dtype):
    return 16 if jnp.dtype(dtype).itemsize == 2 else 8


def _remote(src, dst, send_sems, recv_sems, k, to):
    return pltpu.make_async_remote_copy(src_ref=src, dst_ref=dst, send_sem=send_sems.at[k], recv_sem=recv_sems.at[k],
                                        device_id=to, device_id_type=MESH)


class _GatherPlan:
    def __init__(self, p):
        R, W = p.shape
        self.srcs = [p]
        self.Rh = R // 2
        self.unit = _row_unit(p.dtype)
        self.n = _nparts(self.Rh, self.unit, COPY_PARTS)
        self.out_shapes = [S((N_CHIPS, R, W), p.dtype)]
        n = self.n
        self.scratch = [DMA((3 * n,)), DMA((3 * n,)), DMA((3 * n,)), DMA((3 * n,))]
        self.has_relay = True

    def _ops(self, srcs, outs, sems):
        (p_ref,), (out_ref,), (ici_s, ici_r, rel_s, rel_r) = srcs, outs, sems
        x, y, c = _place()
        me, sibling = (x, y, c), (x, y, 1 - c)
        chips = [(1 - x, y), (x, 1 - y), (1 - x, 1 - y)]
        n, rp, Rh, unit = self.n, self.Rh // self.n, self.Rh, self.unit

        def part(px, py, pc, k):
            return out_ref.at[2 * px + py, pl.ds(pl.multiple_of(pc * Rh + k * rp, unit), rp), :]

        def mine(k):
            return p_ref.at[pl.ds(pl.multiple_of(c * Rh + k * rp, unit), rp), :]

        send, arrive, relay, relayed = [], [], [], []
        for j, chip in enumerate(chips):
            for k in range(n):
                s = j * n + k
                send.append(_remote(mine(k), part(x, y, c, k), ici_s, ici_r, s, (*chip, c)))
                arrive.append(_remote(mine(k), part(*chip, c, k), ici_s, ici_r, s, me))
                relay.append(_remote(part(*chip, c, k), part(*chip, c, k), rel_s, rel_r, s, sibling))
                relayed.append(_remote(part(*chip, c, k), part(*chip, 1 - c, k), rel_s, rel_r, s, me))
        return send, arrive, relay, relayed

    def start(self, *refs):
        for cp in self._ops(*refs)[0]:
            cp.start()

    def relay(self, *refs):
        _, arrive, relay, _ = self._ops(*refs)
        for a, r in zip(arrive, relay):
            a.wait_recv()
            r.start()

    def finish(self, *refs):
        send, _, relay, relayed = self._ops(*refs)
        for cp in relayed:
            cp.wait_recv()
        for cp in send + relay:
            cp.wait_send()


class _ExchangePlan:
    def __init__(self, src, per_target):
        self.srcs = [src]
        self.per_target = per_target
        self.Rh = src.shape[1] // 2 if per_target else src.shape[0]
        self.unit = _row_unit(src.dtype)
        self.n = _nparts(self.Rh, self.unit, COPY_PARTS)
        self.out_shapes = [S((N_DEV, self.Rh, src.shape[-1]), src.dtype)]
        self.scratch = [DMA((N_DEV - 2 + self.n,)), DMA((N_DEV - 2 + self.n,))]
        self.has_relay = False

    def _ops(self, srcs, outs, sems):
        (src_ref,), (out_ref,), (ss, rs) = srcs, outs, sems
        x, y, c = _place()
        n, rp, Rh, unit = self.n, self.Rh // self.n, self.Rh, self.unit

        def ident(px, py, pc):
            return 4 * px + 2 * py + pc

        def block_for(px, py, pc, r0, rows):
            if self.per_target:
                return src_ref.at[2 * px + py, pl.ds(pl.multiple_of(pc * Rh + r0, unit), rows), :]
            return src_ref.at[pl.ds(r0, rows), :]

        def slot(d, r0, rows):
            return out_ref.at[d, pl.ds(r0, rows), :]

        me = ident(x, y, c)
        send, arrive = [], []
        for k in range(1, N_DEV):
            peer = (x ^ ((k >> 2) & 1), y ^ ((k >> 1) & 1), c ^ (k & 1))
            pieces = [(N_DEV - 2 + q, q * rp, rp) for q in range(n)] if k == 1 else [(k - 2, 0, Rh)]
            for sem, r0, rows in pieces:
                send.append(_remote(block_for(*peer, r0, rows), slot(me, r0, rows), ss, rs, sem, peer))
                arrive.append(_remote(block_for(*peer, r0, rows), slot(ident(*peer), r0, rows), ss, rs, sem, peer))
        return send, arrive

    def start(self, *refs):
        for cp in self._ops(*refs)[0]:
            cp.start()

    def finish(self, *refs):
        send, arrive = self._ops(*refs)
        for cp in arrive:
            cp.wait_recv()
        for cp in send:
            cp.wait_send()


class _SwapPlan:
    def __init__(self, h):
        self.srcs = [h]
        self.Rh, W = h.shape
        self.n = _nparts(self.Rh, _row_unit(h.dtype), SWAP_PARTS)
        self.out_shapes = [S((self.Rh, W), h.dtype)]
        self.scratch = [DMA((self.n,)), DMA((self.n,))]
        self.has_relay = False

    def _ops(self, srcs, outs, sems):
        (h_ref,), (out_ref,), (ss, rs) = srcs, outs, sems
        x, y, c = _place()
        rp = self.Rh // self.n
        part = lambda ref, k: ref.at[pl.ds(k * rp, rp), :]
        copies = [_remote(part(h_ref, k), part(out_ref, k), ss, rs, k, (x, y, 1 - c)) for k in range(self.n)]
        return copies, copies

    start = _ExchangePlan.start
    finish = _ExchangePlan.finish


def _run_plan(plan, name):
    ni, no = len(plan.srcs), len(plan.out_shapes)

    def body(*refs):
        parts = (refs[:ni], refs[ni:ni + no], refs[ni + no:])
        plan.start(*parts)
        if plan.has_relay:
            plan.relay(*parts)
        plan.finish(*parts)

    return pl.pallas_call(body, out_shape=plan.out_shapes, in_specs=[ANY] * ni, out_specs=[ANY] * no,
                          scratch_shapes=plan.scratch, name=name)(*plan.srcs)


def _call(body, *, grid, in_specs, out_specs, out_shape, name, sem, args, scratch_shapes=(), plan=None):
    if plan is None:
        return pl.pallas_call(body, grid=grid, in_specs=in_specs, out_specs=out_specs, out_shape=out_shape,
                              scratch_shapes=list(scratch_shapes), name=name, compiler_params=_cparams(*sem))(*args), None
    n_in, n_out, n_scr = len(in_specs), len(out_shape), len(scratch_shapes)
    pi, po = len(plan.srcs), len(plan.out_shapes)
    total = math.prod(grid)

    def wrapped(*refs):
        a, refs = refs[:n_in], refs[n_in:]
        pa, refs = refs[:pi], refs[pi:]
        o, refs = refs[:n_out], refs[n_out:]
        pout, refs = refs[:po], refs[po:]
        scr, psem = refs[:n_scr], refs[n_scr:]
        step = 0
        for d, gsize in enumerate(grid):
            step = step * gsize + pl.program_id(d)

        @pl.when(step == 0)
        def _():
            plan.start(pa, pout, psem)

        body(*a, *o, *scr)
        if plan.has_relay:
            @pl.when(step == (3 * total) // 4)
            def _():
                plan.relay(pa, pout, psem)

        @pl.when(step == total - 1)
        def _():
            plan.finish(pa, pout, psem)

    outs = pl.pallas_call(
        wrapped, grid=grid, in_specs=list(in_specs) + [ANY] * pi, out_specs=list(out_specs) + [ANY] * po,
        out_shape=list(out_shape) + plan.out_shapes, scratch_shapes=list(scratch_shapes) + plan.scratch, name=name,
        compiler_params=_cparams(*["arbitrary"] * len(grid)))(*args, *plan.srcs)
    return outs[:n_out], outs[n_out:]


SMEM = pl.BlockSpec(memory_space=pltpu.SMEM)


def _sum_slots(buf, own, me, name):
    n, R, W = buf.shape
    tr = _tile(R, 256, 8)

    def body(me_ref, b_ref, own_ref, o_ref):
        acc = None
        for s in range(n):
            blk = jnp.where(me_ref[0] == s, own_ref[...], b_ref[s]).astype(F32)
            acc = blk if acc is None else acc + blk
        o_ref[...] = acc

    return pl.pallas_call(
        body, grid=(R // tr,),
        in_specs=[SMEM, pl.BlockSpec((n, tr, W), lambda i: (0, i, 0)), pl.BlockSpec((tr, W), lambda i: (i, 0))],
        out_specs=pl.BlockSpec((tr, W), lambda i: (i, 0)), out_shape=S((R, W), F32), name=name,
        compiler_params=_cparams("parallel"))(me, buf, own)


def _adamw_update(wv, gv, mv, vv):
    mn = ADAM_B1 * mv + (1.0 - ADAM_B1) * gv
    vn = ADAM_B2 * vv + (1.0 - ADAM_B2) * (gv * gv)
    m_hat = mn / (1.0 - ADAM_B1 ** ADAM_STEP)
    v_hat = vn / (1.0 - ADAM_B2 ** ADAM_STEP)
    return -ADAM_LR * (m_hat / (jnp.sqrt(v_hat) + ADAM_EPS) + ADAM_WD * wv), mn, vn


def _adamw(w, g, m, v, name):
    R, W = w.shape
    tr = _tile(R, 256, 8)

    def body(w_ref, g_ref, m_ref, v_ref, d_ref, mo_ref, vo_ref):
        d_ref[...], mo_ref[...], vo_ref[...] = _adamw_update(w_ref[...], g_ref[...], m_ref[...], v_ref[...])

    blk = pl.BlockSpec((tr, W), lambda i: (i, 0))
    return pl.pallas_call(
        body, grid=(R // tr,), in_specs=[blk] * 4, out_specs=[blk] * 3, out_shape=[S((R, W), F32)] * 3, name=name,
        compiler_params=_cparams("parallel"))(w, g, m, v)


def _adamw_halves(w, mine, other, core, m, v, name):
    R, W = w.shape
    Rh = R // 2
    tr = _tile(Rh, 256, 8)
    nbh = Rh // tr

    def body(c_ref, w_ref, a_ref, b_ref, m_ref, v_ref, g_ref, d_ref, mo_ref, vo_ref):
        gv = jnp.where(pl.program_id(0) // nbh == c_ref[0], a_ref[...], b_ref[...])
        g_ref[...] = gv
        d_ref[...], mo_ref[...], vo_ref[...] = _adamw_update(w_ref[...], gv, m_ref[...], v_ref[...])

    blk = pl.BlockSpec((tr, W), lambda i: (i, 0))
    half = pl.BlockSpec((tr, W), lambda i: (i % nbh, 0))
    return pl.pallas_call(
        body, grid=(R // tr,), in_specs=[SMEM, blk, half, half, blk, blk], out_specs=[blk] * 4,
        out_shape=[S((R, W), F32)] * 4, name=name, compiler_params=_cparams("parallel"))(core, w, mine, other, m, v)


def _tables(T):
    f32 = F32
    half = RET_QK // 2
    inv = 1.0 / (10000.0 ** jnp.linspace(0.0, 1.0, half, dtype=f32))
    ang = jnp.arange(T).astype(f32)[:, None] * inv[None, :]
    cos, sin = jnp.cos(ang), jnp.sin(ang)
    c2 = jnp.concatenate([cos, cos], axis=-1)
    s2 = jnp.concatenate([-sin, sin], axis=-1)
    log_g = jnp.log1p(-jnp.exp2(-5.0 - jnp.arange(RET_HEADS, dtype=f32)))
    idx = jnp.arange(CHUNK, dtype=f32)
    dintra = jnp.exp(log_g[:, None, None] * jnp.abs(idx[:, None] - idx[None, :]))
    kdec = jnp.exp(log_g[None, :] * (CHUNK - 1 - idx)[:, None]).T
    qdec = jnp.exp(log_g[None, :] * (idx + 1.0)[:, None]).T
    cdec = jnp.exp(log_g * CHUNK)
    bc = lambda a, w: jnp.broadcast_to(a[:, :, None], (RET_HEADS, a.shape[1], w))
    return c2, s2, dintra, bc(qdec, RET_QK), bc(kdec, RET_QK), jnp.broadcast_to(cdec[:, None, None], (RET_HEADS, 1, RET_V))


def _local_step(x, tgt, p, late=None, exchange=None):
    T = x.shape[0]
    tab = _tables(T)
    row = lambda a: a.reshape(1, -1)
    tr = lambda w: jnp.transpose(w)

    hn0 = _rmsnorm_fwd(x, row(p["attn_norm_g"][0]), "norm_a0")
    z0 = _mm(hn0, p["ab_w_in"][0], "mm_ab_in")
    ya, r, st = _retention_fwd(z0, *tab)
    bias_t = _bias_tiles(jnp.transpose(_bias_build(p["ab_rel_bias"][0]), (1, 0, 2))[:, :, :BAND])
    yb, late_out = _attention_fwd(z0, bias_t, plan=late[0] if late else None)
    if late:
        p = {**p, **late[1](late_out[0])}
    h1, hf0 = _mm([ya, yb], p["ab_w_out"][0], "mm_ab_out", res=x, norm_g=row(p["ffn_norm_g"][0]))

    def ffn_fwd(h, hf, l, next_g):
        zf = _mm(hf, p["ffn_w_up"][l], f"mm_up{l}")
        f = _convglu_fwd(zf, p["ffn_conv_w"][l], row(p["ffn_conv_b"][l]), f"convglu_fwd{l}")
        return zf, f, _mm(f, p["ffn_w_down"][l], f"mm_down{l}", res=h, norm_g=next_g)

    zf0, f0, (h2, hn1) = ffn_fwd(h1, hf0, 0, row(p["attn_norm_g"][1]))
    zc = _mm(hn1, p["c_w_in"][0], "mm_c_in")
    lng, lnb, bst, ws = row(p["c_ln_g"][0]), row(p["c_ln_b"][0]), tr(p["c_b_s"][0]), p["c_w_s"][0]
    y1 = _sgu_fwd(zc, lng, lnb, ws, bst)
    h3, hf1 = _mm(y1, p["c_w_out"][0], "mm_c_out", res=h2, norm_g=row(p["ffn_norm_g"][1]))
    zf1, f1, h4 = ffn_fwd(h3, hf1, 1, None)
    lsum, dh4, dh4b, dgfin = _loss_head(h4, tgt, row(p["final_norm_g"]))

    g = {}

    def ffn_bwd(dh, dhb, h_in, hf, zf, f, l):
        d_down = _mm_tn(f, dhb, f"mmt_down{l}")
        df = _mm(dhb, tr(p["ffn_w_down"][l]), f"mmb_down{l}")
        dzg, dzu, dwg, dwu, dbg, dbu = _convglu_bwd(zf, df, p["ffn_conv_w"][l], row(p["ffn_conv_b"][l]), f"convglu_bwd{l}")
        d_up = _mm_tn(hf, [dzg, dzu], f"mmt_up{l}")
        dh_in, dh_in_b, dgf = _mm([dzg, dzu], tr(p["ffn_w_up"][l]), f"mmb_up{l}",
                                  norm_bwd=(h_in, row(p["ffn_norm_g"][l]), dh))
        return dh_in, dh_in_b, dict(ffn_w_down=d_down, ffn_w_up=d_up, ffn_norm_g=dgf[0],
                                    ffn_conv_w=jnp.concatenate([dwg, dwu], axis=1), ffn_conv_b=jnp.concatenate([dbg, dbu], axis=1)[0])

    dh3, dh3b, gf1 = ffn_bwd(dh4, dh4b, h3, hf1, zf1, f1, 1)
    g["c_w_out"] = _mm_tn(y1, dh3b, "mmt_c_out")[None]
    dy1 = _mm(dh3b, tr(p["c_w_out"][0]), "mmb_c_out")
    dzc, dws, dbst, dlng, dlnb = _sgu_bwd(zc, dy1, lng, lnb, ws, bst)
    g["c_w_s"], g["c_b_s"], g["c_ln_g"], g["c_ln_b"] = dws[None], tr(dbst)[None], dlng, dlnb
    g["c_w_in"] = _mm_tn(hn1, dzc, "mmt_c_in")[None]
    dh2, dh2b, dga1 = _mm(dzc, tr(p["c_w_in"][0]), "mmb_c_in", norm_bwd=(h2, row(p["attn_norm_g"][1]), dh3))
    dh1, dh1b, gf0 = ffn_bwd(dh2, dh2b, h1, hf0, zf0, f0, 0)
    for k in gf0:
        g[k] = jnp.stack([gf0[k], gf1[k]])
    g["ab_w_out"] = _mm_tn([ya, yb], dh1b, "mmt_ab_out")[None]
    dycat = _mm(dh1b, tr(p["ab_w_out"][0]), "mmb_ab_out")
    (dqb, dkb, dvb, dbias_t), late_slots = _attention_bwd(z0, bias_t, dycat, plan=exchange(LATE_BIG, g) if exchange else None)
    dqa, dka, dva, dga = _retention_bwd(z0, *tab, r, dycat, st)
    dz0 = [dqa, dka, dva, dga, dqb, dkb, dvb]
    g["ab_w_in"] = _mm_tn(hn0, dz0, "mmt_ab_in")[None]
    slots = {}
    first_norm = (x, row(p["attn_norm_g"][0]), dh1)
    if exchange:
        (gx, _, dga0), first_slots = _mm(dz0, tr(p["ab_w_in"][0]), "mmb_ab_in", plan=exchange(FIRST_BIG, g), norm_bwd=first_norm)
        slots = dict(first=first_slots[0], late=late_slots[0])
    else:
        gx, _, dga0 = _mm(dz0, tr(p["ab_w_in"][0]), "mmb_ab_in", norm_bwd=first_norm)
    g["ab_rel_bias"] = _bias_grad(_bias_bands(dbias_t))[None, :, :N_REL]
    g["attn_norm_g"] = jnp.stack([dga0[0], dga1[0]])
    g["final_norm_g"] = dgfin[0]
    return lsum[0, 0], gx, g, slots


FIRST_BIG = [("ab_w_in", 2), ("ab_w_out", 1)]
LATE_BIG = [("c_w_in", 2), ("c_w_out", 1), ("ffn_w_up", 2), ("ffn_w_down", 1)]
BIG = FIRST_BIG + LATE_BIG
PACK_ROWS = 32 * COPY_PARTS
SMALL_SHARDED = [("c_ln_g", 1), ("c_ln_b", 1), ("ffn_conv_w", 2)]
REPLICATED = ["attn_norm_g", "ffn_norm_g", "ab_rel_bias", "c_w_s", "c_b_s", "ffn_conv_b", "final_norm_g"]


def _rows_of(n_elems):
    return -(-n_elems // PACK_W)


def _flat_rows(a):
    f = a.reshape(-1)
    rows = _rows_of(f.shape[0])
    return jnp.pad(f, (0, rows * PACK_W - f.shape[0])).reshape(rows, PACK_W)


def _pad_rows(a, mult):
    extra = (-a.shape[0]) % mult
    return jnp.pad(a, ((0, extra), (0, 0))) if extra else a


def _pack(arrs, mult):
    return _pad_rows(jnp.concatenate([_flat_rows(a) for a in arrs], axis=0), mult)


def _unpack(buf, shapes):
    out, r = [], 0
    for shp in shapes:
        n = math.prod(shp)
        rows = _rows_of(n)
        out.append(buf[r:r + rows].reshape(-1)[:n].reshape(shp))
        r += rows
    return out


def _shard_major(full, axis):
    shp = full.shape
    split = full.reshape(shp[:axis] + (N_CHIPS, shp[axis] // N_CHIPS) + shp[axis + 1:])
    return jnp.moveaxis(split, axis, 0)


def _from_shards(sh, axis):
    m = jnp.moveaxis(sh, 0, axis)
    shp = m.shape
    return m.reshape(shp[:axis] + (shp[axis] * shp[axis + 1],) + shp[axis + 2:])


def _as_bf16_pairs(a):
    return lax.bitcast_convert_type(a.astype(F32), BF16)


def _from_bf16_pairs(a):
    return lax.bitcast_convert_type(a, F32)


def kernel(x, attn_norm_g, ffn_norm_g, ab_w_in, ab_w_out, ab_rel_bias, c_w_in, c_ln_g, c_ln_b, c_w_s, c_b_s, c_w_out, ffn_w_up, ffn_conv_w, ffn_conv_b, ffn_w_down, final_norm_g, loss_target, m_attn_norm_g, m_ffn_norm_g, m_ab_w_in, m_ab_w_out, m_ab_rel_bias, m_c_w_in, m_c_ln_g, m_c_ln_b, m_c_w_s, m_c_b_s, m_c_w_out, m_ffn_w_up, m_ffn_conv_w, m_ffn_conv_b, m_ffn_w_down, m_final_norm_g, v_attn_norm_g, v_ffn_norm_g, v_ab_w_in, v_ab_w_out, v_ab_rel_bias, v_c_w_in, v_c_ln_g, v_c_ln_b, v_c_w_s, v_c_b_s, v_c_w_out, v_ffn_w_up, v_ffn_conv_w, v_ffn_conv_b, v_ffn_w_down, v_final_norm_g):
    w = dict(attn_norm_g=attn_norm_g, ffn_norm_g=ffn_norm_g, ab_w_in=ab_w_in, ab_w_out=ab_w_out, ab_rel_bias=ab_rel_bias,
             c_w_in=c_w_in, c_ln_g=c_ln_g, c_ln_b=c_ln_b, c_w_s=c_w_s, c_b_s=c_b_s, c_w_out=c_w_out, ffn_w_up=ffn_w_up,
             ffn_conv_w=ffn_conv_w, ffn_conv_b=ffn_conv_b, ffn_w_down=ffn_w_down, final_norm_g=final_norm_g)
    m = dict(attn_norm_g=m_attn_norm_g, ffn_norm_g=m_ffn_norm_g, ab_w_in=m_ab_w_in, ab_w_out=m_ab_w_out,
             ab_rel_bias=m_ab_rel_bias, c_w_in=m_c_w_in, c_ln_g=m_c_ln_g, c_ln_b=m_c_ln_b, c_w_s=m_c_w_s, c_b_s=m_c_b_s,
             c_w_out=m_c_w_out, ffn_w_up=m_ffn_w_up, ffn_conv_w=m_ffn_conv_w, ffn_conv_b=m_ffn_conv_b,
             ffn_w_down=m_ffn_w_down, final_norm_g=m_final_norm_g)
    v = dict(attn_norm_g=v_attn_norm_g, ffn_norm_g=v_ffn_norm_g, ab_w_in=v_ab_w_in, ab_w_out=v_ab_w_out,
             ab_rel_bias=v_ab_rel_bias, c_w_in=v_c_w_in, c_ln_g=v_c_ln_g, c_ln_b=v_c_ln_b, c_w_s=v_c_w_s, c_b_s=v_c_b_s,
             c_w_out=v_c_w_out, ffn_w_up=v_ffn_w_up, ffn_conv_w=v_ffn_conv_w, ffn_conv_b=v_ffn_conv_b,
             ffn_w_down=v_ffn_w_down, final_norm_g=v_final_norm_g)
    names = list(w)
    chip = 2 * lax.axis_index("x") + lax.axis_index("y")

    def gathered_weights(group, small, send):
        shapes = [a.shape for a in send]
        pack = _pack(send, PACK_ROWS)

        def finish(gathered):
            gathered = lax.dynamic_update_slice(gathered, pack[None], (chip, 0, 0))
            parts = [_unpack(gathered[s], shapes) for s in range(N_CHIPS)]
            stack = lambda i: jnp.stack([parts[s][i] for s in range(N_CHIPS)])
            out = {n: _from_shards(stack(i), axis) for i, (n, axis) in enumerate(group)}
            out.update({n: _from_shards(_from_bf16_pairs(stack(len(group) + i)), axis) for i, (n, axis) in enumerate(small)})
            return out

        return _GatherPlan(pack), finish

    plan0, finish0 = gathered_weights(FIRST_BIG, [], [w[n].astype(BF16) for n, _ in FIRST_BIG])
    full = {n: w[n] for n in REPLICATED}
    full.update(finish0(_run_plan(plan0, "gather_first")[0]))
    late = gathered_weights(LATE_BIG, SMALL_SHARDED,
                            [w[n].astype(BF16) for n, _ in LATE_BIG] + [_as_bf16_pairs(w[n]) for n, _ in SMALL_SHARDED])

    core = lax.axis_index("c")
    core_arr = core.reshape(1).astype(jnp.int32)
    me_arr = (2 * chip + core).reshape(1).astype(jnp.int32)
    own_blocks = {}

    def exchange(group, g):
        parts = [_shard_major(g[n], axis).reshape(N_CHIPS, -1, PACK_W) for n, axis in group]
        gb = jnp.concatenate(parts, axis=1).astype(BF16)
        rh = gb.shape[1] // 2
        own_blocks[group[0][0]] = lax.dynamic_slice(gb, (chip, core * rh, 0), (1, rh, PACK_W))[0]
        return _ExchangePlan(gb, True)

    lsum, grad_x, g, slots = _local_step(x[0], loss_target[0], full, late=late, exchange=exchange)
    loss = lax.psum(0.5 * lsum, ("x", "y", "c"))

    big_grads, big_outs = {}, [{}, {}, {}]
    for key, group in (("late", LATE_BIG), ("first", FIRST_BIG)):
        mine = _sum_slots(slots[key], own_blocks[group[0][0]], me_arr, f"sum_{key}")
        other = _run_plan(_SwapPlan(mine), f"swap_{key}")[0]
        pack_group = lambda d: jnp.concatenate([d[n].reshape(-1, PACK_W) for n, _ in group], axis=0)
        res = _adamw_halves(pack_group(w), mine, other, core_arr, pack_group(m), pack_group(v), f"adamw_{key}")
        shapes = [w[n].shape for n, _ in group]
        big_grads.update(zip([n for n, _ in group], _unpack(res[0], shapes)))
        for k in range(3):
            big_outs[k].update(zip([n for n, _ in group], _unpack(res[k + 1], shapes)))

    small_names = REPLICATED + [n for n, _ in SMALL_SHARDED]
    gsmall = _pack([g[n] for n in small_names], 32)
    gsum = _sum_slots(_run_plan(_ExchangePlan(gsmall, False), "exchange_small")[0], gsmall, me_arr, "sum_small")
    gsmall_full = dict(zip(small_names, _unpack(gsum, [g[n].shape for n in small_names])))
    for n, axis in SMALL_SHARDED:
        size = w[n].shape[axis]
        gsmall_full[n] = lax.dynamic_slice_in_dim(gsmall_full[n], chip * size, size, axis)
    pack_small = lambda d: _pack([d[n] for n in small_names], 8)
    small_out = _adamw(pack_small(w), pack_small(gsmall_full), pack_small(m), pack_small(v), "adamw_small")
    small_shapes = [w[n].shape for n in small_names]

    outs = [{**big_grads, **gsmall_full}]
    for k in range(3):
        outs.append({**big_outs[k], **dict(zip(small_names, _unpack(small_out[k], small_shapes)))})
    return (loss, grad_x[None], *[o[n] for o in outs for n in names])
```

```python
import functools
import math

import jax
import jax.numpy as jnp
from jax import lax
from jax.experimental import pallas as pl
from jax.experimental.pallas import tpu as pltpu

F32 = jnp.float32
BF16 = jnp.bfloat16
S = jax.ShapeDtypeStruct
MESH = pl.DeviceIdType.MESH

D_MODEL = 1024
CHUNK = 64
EPS = 1e-6
NEG_INF = -1e30
RET_HEADS, RET_QK, RET_V = 4, 128, 256
ATT_HEADS, ATT_D, ATT_PAST, MAX_REL = 8, 64, 8, 128
BAND = (ATT_PAST + 1) * CHUNK
PADK = ATT_PAST * CHUNK
SGU_BLOCK, SGU_GROUPS, SGU_WIDTH = 128, 8, 2048
SGU_GW = SGU_WIDTH // SGU_GROUPS
FFN_HIDDEN = 2816
N_REL = 2 * MAX_REL + 1
RET_SCALE = RET_QK ** -0.5
ATT_SCALE = ATT_D ** -0.5
ADAM_LR, ADAM_B1, ADAM_B2, ADAM_EPS, ADAM_WD, ADAM_STEP = 0.001, 0.9, 0.999, 1e-08, 0.01, 10

V7X_VMEM_BYTES = 64 * 1024 * 1024
VMEM_LIMIT = V7X_VMEM_BYTES * 7 // 8
LANES = 128
PACK_W = 1024
N_CHIPS = 4
N_DEV = 8

GELU_C = math.sqrt(2.0 / math.pi)
GELU_A = 0.044715


def _cparams(*sem):
    return pltpu.CompilerParams(dimension_semantics=tuple(sem) if sem else None, vmem_limit_bytes=VMEM_LIMIT)


def _tile(n, target, unit=LANES):
    best = None
    for t in range(unit, min(n, target) + 1, unit):
        if n % t == 0:
            best = t
    return best if best is not None else n


def _gelu(x):
    t = jnp.tanh(GELU_C * (x + GELU_A * x * x * x))
    return 0.5 * x * (1.0 + t)


def _gelu_and_grad(x):
    x2 = x * x
    t = jnp.tanh(GELU_C * (x + GELU_A * x2 * x))
    g = 0.5 * x * (1.0 + t)
    dg = 0.5 * (1.0 + t) + 0.5 * x * (1.0 - t * t) * (GELU_C * (1.0 + 3.0 * GELU_A * x2))
    return g, dg


def _sigmoid(x):
    return 1.0 / (1.0 + jnp.exp(-x))


def _dot(a, b):
    return jnp.dot(a, b, preferred_element_type=F32)


def _dot_nt(a, b):
    return lax.dot_general(a, b, (((1,), (1,)), ((), ())), preferred_element_type=F32)


def _dot_tn(a, b):
    return lax.dot_general(a, b, (((0,), (0,)), ((), ())), preferred_element_type=F32)


def _rmsnorm_fwd(x, g, name):
    T, D = x.shape
    tr = _tile(T, 512, 8)

    def body(x_ref, g_ref, o_ref):
        xv = x_ref[...]
        r = lax.rsqrt(jnp.mean(xv * xv, axis=-1, keepdims=True) + EPS)
        o_ref[...] = (xv * r * g_ref[...]).astype(o_ref.dtype)

    return pl.pallas_call(
        body, grid=(T // tr,),
        in_specs=[pl.BlockSpec((tr, D), lambda i: (i, 0)), pl.BlockSpec((1, D), lambda i: (0, 0))],
        out_specs=pl.BlockSpec((tr, D), lambda i: (i, 0)),
        out_shape=S((T, D), BF16), name=name, compiler_params=_cparams("parallel"))(x, g)


def _pieces(a):
    return list(a) if isinstance(a, (list, tuple)) else [a]


def _piece_layout(widths, tile):
    out, s = [], 0
    for w in widths:
        out.append((s, w // tile))
        s += w // tile
    return out


def _common_tile(widths, target):
    return _tile(functools.reduce(math.gcd, widths), target)


def _rmsnorm_bwd(x, dy, g, dres, name):
    T, D = x.shape
    tr = _tile(T, 512, 8)

    def body(x_ref, dy_ref, g_ref, dres_ref, dx_ref, dxb_ref, dg_ref):
        @pl.when(pl.program_id(0) == 0)
        def _():
            dg_ref[...] = jnp.zeros_like(dg_ref)

        xv = x_ref[...]
        r = lax.rsqrt(jnp.mean(xv * xv, axis=-1, keepdims=True) + EPS)
        xh = xv * r
        dyv = dy_ref[...]
        dg_ref[...] += jnp.sum(dyv * xh, axis=0, keepdims=True)
        dxh = dyv * g_ref[...]
        dx = dres_ref[...] + r * (dxh - xh * jnp.mean(dxh * xh, axis=-1, keepdims=True))
        dx_ref[...] = dx
        dxb_ref[...] = dx.astype(BF16)

    row = pl.BlockSpec((tr, D), lambda i: (i, 0))
    vec = pl.BlockSpec((1, D), lambda i: (0, 0))
    return pl.pallas_call(
        body, grid=(T // tr,), in_specs=[row, row, vec, row], out_specs=[row, row, vec],
        out_shape=[S((T, D), F32), S((T, D), BF16), S((1, D), F32)], name=name,
        compiler_params=_cparams("arbitrary"))(x, dy, g, dres)


def _mm(a, b, name, res=None, out_dtype=F32, plan=None, norm_g=None, form="kn", row0=0, rows=None):
    pieces = _pieces(a)
    M = pieces[0].shape[0]
    widths = [p.shape[1] for p in pieces]
    K = sum(widths)
    N = {"kn": lambda: b.shape[1], "kn4": lambda: N_CHIPS * b.shape[2], "nk": lambda: b.shape[0], "nk4": lambda: rows}[form]()
    tm = _tile(M, 1024, 8)
    tn = N if norm_g is not None else (b.shape[2] if form == "kn4" else _tile(N, 1408))
    tk = b.shape[2] if form == "nk4" else _common_tile(widths, 1536)
    assert all(w % tk == 0 for w in widths) and row0 % (tk if form == "kn4" else tn) == 0
    nk, npc = K // tk, len(pieces)
    layout = _piece_layout(widths, tk)
    tile = pl.BlockSpec((tm, tn), lambda i, j, k: (i, j))
    vec = pl.BlockSpec((1, tn), lambda i, j, k: (0, j))
    b_spec = {"kn": lambda: pl.BlockSpec((tk, tn), lambda i, j, k: (k, j)),
              "kn4": lambda: pl.BlockSpec((None, tk, tn), lambda i, j, k: (j, row0 // tk + k, 0)),
              "nk": lambda: pl.BlockSpec((tn, tk), lambda i, j, k: (j, k)),
              "nk4": lambda: pl.BlockSpec((None, tn, tk), lambda i, j, k: (k, row0 // tn + j, 0))}[form]()
    dot = _dot if form in ("kn", "kn4") else _dot_nt
    extra_in, extra_specs = [], []
    if res is not None:
        extra_in, extra_specs = [res], [tile]
    if norm_g is not None:
        extra_in, extra_specs = extra_in + [norm_g], extra_specs + [vec]
    n_extra = len(extra_in)
    if norm_g is not None:
        out_shape, out_specs = [S((M, N), out_dtype), S((M, N), BF16)], [tile, tile]
    else:
        out_shape, out_specs = [S((M, N), out_dtype)], [tile]

    def body(*refs):
        a_refs, b_ref = refs[:npc], refs[npc]
        ext = list(refs[npc + 1:npc + 1 + n_extra])
        outs = refs[npc + 1 + n_extra:npc + 1 + n_extra + len(out_shape)]

        def finish(v):
            if res is not None:
                v = v + ext[0][...]
            outs[0][...] = v.astype(outs[0].dtype)
            if norm_g is not None:
                r = lax.rsqrt(jnp.mean(v * v, axis=-1, keepdims=True) + EPS)
                outs[1][...] = (v * r * ext[-1][...]).astype(BF16)

        if nk == 1:
            finish(dot(a_refs[0][...], b_ref[...]))
            return
        acc = refs[-1]
        k = pl.program_id(2)
        for a_ref, (s, c) in zip(a_refs, layout):
            def add(a_ref=a_ref):
                acc[...] += dot(a_ref[...], b_ref[...])

            if s == 0:
                @pl.when(k == 0)
                def _(a_ref=a_ref):
                    acc[...] = dot(a_ref[...], b_ref[...])

                if c > 1:
                    pl.when((k > 0) & (k < c))(add)
            else:
                pl.when((k >= s) & (k < s + c))(add)

        @pl.when(k == nk - 1)
        def _():
            finish(acc[...])

    in_specs = [pl.BlockSpec((tm, tk), lambda i, j, k, s=s, c=c: (i, jnp.clip(k - s, 0, c - 1))) for s, c in layout]
    outs, extra = _call(
        body, grid=(M // tm, N // tn, nk), in_specs=in_specs + [b_spec] + extra_specs, out_specs=out_specs,
        out_shape=out_shape, scratch_shapes=[pltpu.VMEM((tm, tn), F32)] if nk > 1 else [],
        name=name, sem=("parallel", "parallel", "arbitrary"), args=pieces + [b] + extra_in, plan=plan)
    outs = outs[0] if len(outs) == 1 else tuple(outs)
    return outs if plan is None else (outs, extra)


def _mm_tn(a, g, name, out_dtype=F32):
    ap, gp = _pieces(a), _pieces(g)
    T = ap[0].shape[0]
    aw, gw = [p.shape[1] for p in ap], [p.shape[1] for p in gp]
    tm, tn, tt = _common_tile(aw, 1408), _common_tile(gw, 1408), _tile(T, 1024, 8)
    alay, glay = _piece_layout(aw, tm), _piece_layout(gw, tn)
    na = len(ap)

    def inside(idx, s, c, single):
        return None if single else (idx >= s) & (idx < s + c)

    narrow = out_dtype != F32
    nt = T // tt

    def body(*refs):
        a_refs, g_refs = refs[:na], refs[na:na + len(gp)]
        o_ref = refs[na + len(gp)]
        acc = refs[-1] if narrow else o_ref
        i, j, k = pl.program_id(0), pl.program_id(1), pl.program_id(2)

        @pl.when(k == 0)
        def _():
            acc[...] = jnp.zeros_like(acc)

        for a_ref, (sa, ca) in zip(a_refs, alay):
            for g_ref, (sg, cg) in zip(g_refs, glay):
                def add(a_ref=a_ref, g_ref=g_ref):
                    acc[...] += _dot_tn(a_ref[...], g_ref[...])

                conds = [c for c in (inside(i, sa, ca, na == 1), inside(j, sg, cg, len(gp) == 1)) if c is not None]
                if not conds:
                    add()
                else:
                    pl.when(functools.reduce(lambda u, v: u & v, conds))(add)

        if narrow:
            @pl.when(k == nt - 1)
            def _():
                o_ref[...] = acc[...].astype(out_dtype)

    def spec(tile, lay, single, axis):
        s, c = lay

        def index(i, j, k):
            idx = (i, j)[axis]
            if single:
                return (k, idx)
            on = (idx >= s) & (idx < s + c)
            return (jnp.where(on, k, 0), jnp.clip(idx - s, 0, c - 1))

        return pl.BlockSpec((tt, tile), index)

    in_specs = [spec(tm, lay, na == 1, 0) for lay in alay] + [spec(tn, lay, len(gp) == 1, 1) for lay in glay]
    return pl.pallas_call(
        body, grid=(sum(aw) // tm, sum(gw) // tn, nt), in_specs=in_specs,
        out_specs=pl.BlockSpec((tm, tn), lambda i, j, k: (i, j)),
        out_shape=S((sum(aw), sum(gw)), out_dtype), scratch_shapes=[pltpu.VMEM((tm, tn), F32)] if narrow else [],
        name=name, compiler_params=_cparams("parallel", "parallel", "arbitrary"))(*ap, *gp)


def _rotate(x, c2, s2):
    return x * c2 + pltpu.roll(x, RET_QK // 2, 1) * s2


def _unrotate(d, c2, s2):
    return d * c2 - pltpu.roll(d, RET_QK // 2, 1) * s2


def _ret_specs(RB, blockmap):
    q = pl.BlockSpec((RB, RET_QK), lambda h, n: (blockmap(n), h))
    k = pl.BlockSpec((RB, RET_QK), lambda h, n: (blockmap(n), RET_HEADS + h))
    v = pl.BlockSpec((RB, RET_V), lambda h, n: (blockmap(n), RET_HEADS + h))
    g = pl.BlockSpec((RB, RET_V), lambda h, n: (blockmap(n), 2 * RET_HEADS + h))
    tab = pl.BlockSpec((RB, RET_QK), lambda h, n: (blockmap(n), 0))
    return q, k, v, g, tab


def _ret_decay_specs():
    return [pl.BlockSpec((None, CHUNK, CHUNK), lambda h, n: (h, 0, 0)),
            pl.BlockSpec((None, CHUNK, RET_QK), lambda h, n: (h, 0, 0)),
            pl.BlockSpec((None, CHUNK, RET_QK), lambda h, n: (h, 0, 0)),
            pl.BlockSpec((None, 1, RET_V), lambda h, n: (h, 0, 0))]


def _retention_fwd(z, c2, s2, dintra, qdec, kdec, cdec):
    T = z.shape[0]
    RB = min(512, T)
    nch, nb = RB // CHUNK, T // RB

    def body(q_ref, k_ref, v_ref, g_ref, c2_ref, s2_ref, di_ref, qd_ref, kd_ref, cd_ref, ya_ref, r_ref, st_ref, state):
        @pl.when(pl.program_id(1) == 0)
        def _():
            state[...] = jnp.zeros_like(state)

        dmat, qdv, kdv, cdv = di_ref[...], qd_ref[...], kd_ref[...], cd_ref[...]
        for c in range(nch):
            rows = slice(c * CHUNK, (c + 1) * CHUNK)
            c2v, s2v = c2_ref[rows, :], s2_ref[rows, :]
            qr = _rotate(q_ref[rows, :], c2v, s2v)
            kr = _rotate(k_ref[rows, :], c2v, s2v) * RET_SCALE
            vb = v_ref[rows, :].astype(BF16)
            sm = _dot_nt(qr.astype(BF16), kr.astype(BF16)) * dmat
            sb = state[...].astype(BF16)
            st_ref[c] = sb
            o = _dot(sm.astype(BF16), vb) + _dot((qr * qdv).astype(BF16), sb)
            state[...] = state[...] * cdv + _dot_tn((kr * kdv).astype(BF16), vb)
            r_ref[rows, :] = o
            mu = jnp.mean(o, axis=-1, keepdims=True)
            oc = o - mu
            rn = oc * lax.rsqrt(jnp.mean(oc * oc, axis=-1, keepdims=True) + EPS)
            gv = g_ref[rows, :]
            ya_ref[rows, :] = (gv * _sigmoid(gv) * rn).astype(BF16)

    q, k, v, g, tab = _ret_specs(RB, lambda n: n)
    wide = pl.BlockSpec((RB, RET_V), lambda h, n: (n, h))
    return pl.pallas_call(
        body, grid=(RET_HEADS, nb),
        in_specs=[q, k, v, g, tab, tab] + _ret_decay_specs(),
        out_specs=[wide, wide, pl.BlockSpec((None, nch, RET_QK, RET_V), lambda h, n: (h, n, 0, 0))],
        out_shape=[S((T, RET_HEADS * RET_V), BF16), S((T, RET_HEADS * RET_V), F32),
                   S((RET_HEADS, T // CHUNK, RET_QK, RET_V), BF16)],
        scratch_shapes=[pltpu.VMEM((RET_QK, RET_V), F32)], name="retention_fwd",
        compiler_params=_cparams("parallel", "arbitrary"))(z, z, z, z, c2, s2, dintra, qdec, kdec, cdec)


def _retention_bwd(z, c2, s2, dintra, qdec, kdec, cdec, r, dycat, st):
    T = z.shape[0]
    RB = min(512, T)
    nch, nb = RB // CHUNK, T // RB

    def body(q_ref, k_ref, v_ref, g_ref, c2_ref, s2_ref, di_ref, qd_ref, kd_ref, cd_ref, r_ref, dy_ref, st_ref,
             dq_ref, dk_ref, dv_ref, dg_ref, dstate):
        @pl.when(pl.program_id(1) == 0)
        def _():
            dstate[...] = jnp.zeros_like(dstate)

        dmat, qdv, kdv, cdv = di_ref[...], qd_ref[...], kd_ref[...], cd_ref[...]
        for c in reversed(range(nch)):
            rows = slice(c * CHUNK, (c + 1) * CHUNK)
            c2v, s2v = c2_ref[rows, :], s2_ref[rows, :]
            qr = _rotate(q_ref[rows, :], c2v, s2v)
            kr = _rotate(k_ref[rows, :], c2v, s2v) * RET_SCALE
            qb, kb = qr.astype(BF16), kr.astype(BF16)
            vb = v_ref[rows, :].astype(BF16)
            o, gv, dy = r_ref[rows, :], g_ref[rows, :], dy_ref[rows, :]
            mu = jnp.mean(o, axis=-1, keepdims=True)
            oc = o - mu
            rstd = lax.rsqrt(jnp.mean(oc * oc, axis=-1, keepdims=True) + EPS)
            rn = oc * rstd
            sg = _sigmoid(gv)
            dg_ref[rows, :] = (dy * rn * (sg * (1.0 + gv * (1.0 - sg)))).astype(BF16)
            drn = dy * (gv * sg)
            do = rstd * (drn - jnp.mean(drn, axis=-1, keepdims=True) - rn * jnp.mean(drn * rn, axis=-1, keepdims=True))
            dob = do.astype(BF16)
            sm = (_dot_nt(qb, kb) * dmat).astype(BF16)
            kdb = (kr * kdv).astype(BF16)
            dsb = dstate[...].astype(BF16)
            dv_ref[rows, :] = (_dot_tn(sm, dob) + _dot(kdb, dsb)).astype(BF16)
            ds = (_dot_nt(dob, vb) * dmat).astype(BF16)
            dqr = _dot(ds, kb) + _dot_nt(dob, st_ref[c]) * qdv
            dkr = (_dot_tn(ds, qb) + _dot_nt(vb, dsb) * kdv) * RET_SCALE
            dstate[...] = dstate[...] * cdv + _dot_tn((qr * qdv).astype(BF16), dob)
            dq_ref[rows, :] = _unrotate(dqr, c2v, s2v).astype(BF16)
            dk_ref[rows, :] = _unrotate(dkr, c2v, s2v).astype(BF16)

    rev = lambda n: nb - 1 - n
    q, k, v, g, tab = _ret_specs(RB, rev)
    wide = pl.BlockSpec((RB, RET_V), lambda h, n: (rev(n), h))
    narrow = pl.BlockSpec((RB, RET_QK), lambda h, n: (rev(n), h))
    return pl.pallas_call(
        body, grid=(RET_HEADS, nb),
        in_specs=[q, k, v, g, tab, tab] + _ret_decay_specs() + [
            wide, wide, pl.BlockSpec((None, nch, RET_QK, RET_V), lambda h, n: (h, rev(n), 0, 0))],
        out_specs=[narrow, narrow, wide, wide],
        out_shape=[S((T, RET_HEADS * RET_QK), BF16), S((T, RET_HEADS * RET_QK), BF16),
                   S((T, RET_HEADS * RET_V), BF16), S((T, RET_HEADS * RET_V), BF16)],
        scratch_shapes=[pltpu.VMEM((RET_QK, RET_V), F32)], name="retention_bwd",
        compiler_params=_cparams("parallel", "arbitrary"))(z, z, z, z, c2, s2, dintra, qdec, kdec, cdec, r, dycat, st)


def _rel_index(i):
    r = lax.broadcasted_iota(jnp.int32, (3 * LANES, 5 * LANES), 0)
    j = lax.broadcasted_iota(jnp.int32, (3 * LANES, 5 * LANES), 1)
    idx = jnp.clip(i + PADK - j, -MAX_REL, MAX_REL) + MAX_REL
    return (r == idx).astype(BF16)


def _split3(v):
    hi = v.astype(BF16)
    r1 = v - hi.astype(F32)
    mid = r1.astype(BF16)
    lo = (r1 - mid.astype(F32)).astype(BF16)
    return hi, mid, lo


def _bias_build(rb):
    rbp = jnp.pad(rb, ((0, 0), (0, 3 * LANES - N_REL)))

    def body(rb_ref, o_ref):
        e = _rel_index(pl.program_id(0))
        hi, mid, lo = _split3(rb_ref[...])
        o_ref[...] = _dot(hi, e) + _dot(mid, e) + _dot(lo, e)

    return pl.pallas_call(
        body, grid=(CHUNK,), in_specs=[pl.BlockSpec((ATT_HEADS, 3 * LANES), lambda i: (0, 0))],
        out_specs=pl.BlockSpec((None, ATT_HEADS, 5 * LANES), lambda i: (i, 0, 0)),
        out_shape=S((CHUNK, ATT_HEADS, 5 * LANES), F32), name="bias_build",
        compiler_params=_cparams("parallel"))(rbp)


ATT_RB = 512
ATT_QT = 256
ATT_CPT = ATT_QT // CHUNK
ATT_KT = ATT_QT + PADK
ATT_QCOL = (2 * RET_HEADS * RET_QK + 2 * RET_HEADS * RET_V) // LANES
ATT_KCOL = ATT_QCOL + ATT_HEADS * ATT_D // LANES
ATT_VCOL = ATT_KCOL + ATT_HEADS * ATT_D // LANES


def _bias_grad(dbt):
    def body(d_ref, o_ref):
        @pl.when(pl.program_id(0) == 0)
        def _():
            o_ref[...] = jnp.zeros_like(o_ref)

        e = _rel_index(pl.program_id(0))
        d = d_ref[0]
        for ci in range(1, ATT_CPT):
            d = d + d_ref[ci]
        hi, mid, lo = _split3(d)
        o_ref[...] += _dot_nt(hi, e) + _dot_nt(mid, e) + _dot_nt(lo, e)

    return pl.pallas_call(
        body, grid=(CHUNK,),
        in_specs=[pl.BlockSpec((ATT_CPT, None, ATT_HEADS, 5 * LANES), lambda i: (0, i, 0, 0))],
        out_specs=pl.BlockSpec((ATT_HEADS, 3 * LANES), lambda i: (0, 0)),
        out_shape=S((ATT_HEADS, 3 * LANES), F32), name="bias_grad",
        compiler_params=_cparams("arbitrary"))(dbt)


def _bias_tiles(bias):
    parts = [jnp.pad(bias, ((0, 0), (0, 0), (CHUNK * ci, ATT_KT - BAND - CHUNK * ci)), constant_values=NEG_INF)
             for ci in range(ATT_CPT)]
    return jnp.stack(parts, axis=1).reshape(ATT_HEADS, ATT_QT, ATT_KT)


def _bias_bands(dbias_tiles):
    d = dbias_tiles.reshape(ATT_HEADS, ATT_CPT, CHUNK, ATT_KT)
    bands = jnp.stack([d[:, ci, :, CHUNK * ci:CHUNK * ci + BAND] for ci in range(ATT_CPT)])
    return jnp.pad(jnp.transpose(bands, (0, 2, 1, 3)), ((0, 0), (0, 0), (0, 0), (0, 5 * LANES - BAND)))


def _att_fill(kw, vw, klo, khi, vlo, vhi):
    kw[0:ATT_RB, :] = klo[...].astype(BF16)
    kw[ATT_RB:, :] = khi[...].astype(BF16)
    vw[0:ATT_RB, :] = vlo[...].astype(BF16)
    vw[ATT_RB:, :] = vhi[...].astype(BF16)


def _att_probs(qm, kwin, bias, first_key):
    s = _dot_nt(qm, kwin) * ATT_SCALE + bias
    col = lax.broadcasted_iota(jnp.int32, (ATT_QT, ATT_KT), 1)
    s = jnp.where(col + first_key >= 0, s, NEG_INF)
    p = jnp.exp(s - jnp.max(s, axis=-1, keepdims=True))
    return p / jnp.sum(p, axis=-1, keepdims=True)


def _att_in_specs(nq):
    qn = lambda n: jnp.minimum(n, nq - 1)
    blk = lambda col, back: pl.BlockSpec((ATT_RB, LANES), lambda hp, n: (jnp.maximum(qn(n) - back, 0), col + hp))
    return [blk(ATT_QCOL, 0), blk(ATT_KCOL, 1), blk(ATT_KCOL, 0), blk(ATT_VCOL, 1), blk(ATT_VCOL, 0),
            pl.BlockSpec((2, ATT_QT, ATT_KT), lambda hp, n: (hp, 0, 0))]


def _attention_fwd(z, bias_t, plan=None):
    T = z.shape[0]
    nq = T // ATT_RB

    def body(q_ref, klo, khi, vlo, vhi, b_ref, o_ref, kw, vw):
        _att_fill(kw, vw, klo, khi, vlo, vhi)
        lane = lax.broadcasted_iota(jnp.int32, (ATT_QT, LANES), 1)
        n = pl.program_id(1)
        for t in range(ATT_RB // ATT_QT):
            rows = slice(t * ATT_QT, (t + 1) * ATT_QT)
            win = slice(t * ATT_QT, t * ATT_QT + ATT_KT)
            qc = q_ref[rows, :]
            outs = []
            for e in range(2):
                qm = jnp.where((lane >= ATT_D) == (e == 1), qc, 0.0).astype(BF16)
                p = _att_probs(qm, kw[win, :], b_ref[e], (n - 1) * ATT_RB + t * ATT_QT)
                outs.append(_dot(p.astype(BF16), vw[win, :]))
            o_ref[rows, :] = jnp.where(lane < ATT_D, outs[0], outs[1]).astype(BF16)

    (yb,), extra = _call(
        body, grid=(ATT_HEADS // 2, nq), in_specs=_att_in_specs(nq),
        out_specs=[pl.BlockSpec((ATT_RB, LANES), lambda hp, n: (n, hp))],
        out_shape=[S((T, ATT_HEADS * ATT_D), BF16)],
        scratch_shapes=[pltpu.VMEM((2 * ATT_RB, LANES), BF16), pltpu.VMEM((2 * ATT_RB, LANES), BF16)],
        name="attention_fwd", sem=("parallel", "parallel"), args=(z, z, z, z, z, bias_t), plan=plan)
    return yb, extra


def _attention_bwd(z, bias_t, dycat, plan=None):
    T = z.shape[0]
    nq = T // ATT_RB
    dycol = RET_HEADS * RET_V // LANES

    def body(q_ref, klo, khi, vlo, vhi, b_ref, dy_ref, dq_ref, dk_ref, dv_ref, db_ref, kw, vw, dkw, dvw):
        n = pl.program_id(1)

        @pl.when(n == 0)
        def _():
            dkw[...] = jnp.zeros_like(dkw)
            dvw[...] = jnp.zeros_like(dvw)
            db_ref[...] = jnp.zeros_like(db_ref)

        @pl.when(n > 0)
        def _():
            dkw[0:ATT_RB, :] = dkw[ATT_RB:, :]
            dvw[0:ATT_RB, :] = dvw[ATT_RB:, :]
            dkw[ATT_RB:, :] = jnp.zeros((ATT_RB, LANES), F32)
            dvw[ATT_RB:, :] = jnp.zeros((ATT_RB, LANES), F32)

        @pl.when(n < nq)
        def _():
            _att_fill(kw, vw, klo, khi, vlo, vhi)
            lane = lax.broadcasted_iota(jnp.int32, (ATT_QT, LANES), 1)
            for t in range(ATT_RB // ATT_QT):
                rows = slice(t * ATT_QT, (t + 1) * ATT_QT)
                win = slice(t * ATT_QT, t * ATT_QT + ATT_KT)
                qc, dyc = q_ref[rows, :], dy_ref[rows, :]
                kwin, vwin = kw[win, :], vw[win, :]
                dq = jnp.zeros((ATT_QT, LANES), F32)
                for e in range(2):
                    mine = (lane >= ATT_D) == (e == 1)
                    qm = jnp.where(mine, qc, 0.0).astype(BF16)
                    dom = jnp.where(mine, dyc, 0.0).astype(BF16)
                    p = _att_probs(qm, kwin, b_ref[e], (n - 1) * ATT_RB + t * ATT_QT)
                    dp = _dot_nt(dom, vwin)
                    ds = p * (dp - jnp.sum(dp * p, axis=-1, keepdims=True))
                    db_ref[e] += ds
                    dsb = (ds * ATT_SCALE).astype(BF16)
                    dq = dq + jnp.where(mine, _dot(dsb, kwin), 0.0)
                    dkw[win, :] += _dot_tn(dsb, qm)
                    dvw[win, :] += _dot_tn(p.astype(BF16), dom)
                dq_ref[rows, :] = dq.astype(BF16)

        dk_ref[...] = dkw[0:ATT_RB, :].astype(BF16)
        dv_ref[...] = dvw[0:ATT_RB, :].astype(BF16)

    qn = lambda n: jnp.minimum(n, nq - 1)
    out_kv = pl.BlockSpec((ATT_RB, LANES), lambda hp, n: (jnp.maximum(n - 1, 0), hp))
    return _call(
        body, grid=(ATT_HEADS // 2, nq + 1),
        in_specs=_att_in_specs(nq) + [pl.BlockSpec((ATT_RB, LANES), lambda hp, n: (qn(n), dycol + hp))],
        out_specs=[pl.BlockSpec((ATT_RB, LANES), lambda hp, n: (qn(n), hp)), out_kv, out_kv,
                   pl.BlockSpec((2, ATT_QT, ATT_KT), lambda hp, n: (hp, 0, 0))],
        out_shape=[S((T, ATT_HEADS * ATT_D), BF16), S((T, ATT_HEADS * ATT_D), BF16),
                   S((T, ATT_HEADS * ATT_D), BF16), S((ATT_HEADS, ATT_QT, ATT_KT), F32)],
        scratch_shapes=[pltpu.VMEM((2 * ATT_RB, LANES), BF16), pltpu.VMEM((2 * ATT_RB, LANES), BF16),
                        pltpu.VMEM((2 * ATT_RB, LANES), F32), pltpu.VMEM((2 * ATT_RB, LANES), F32)],
        name="attention_bwd", sem=("parallel", "arbitrary"), args=(z, z, z, z, z, bias_t, dycat), plan=plan)


HALO = 8


def _conv_specs(tb, tc, nct, T):
    per = tb // HALO
    last = T // HALO - 1

    def at(half):
        off = half * nct
        return [pl.BlockSpec((HALO, tc), lambda j, i: (jnp.maximum(i * per - 1, 0), j + off)),
                pl.BlockSpec((tb, tc), lambda j, i: (i, j + off)),
                pl.BlockSpec((HALO, tc), lambda j, i: (jnp.minimum((i + 1) * per, last), j + off))]

    return at(0), at(1)


def _causal_conv(ext, w_ref, b_ref):
    zc = w_ref[0:1, :] * pltpu.roll(ext, 2, 0) + w_ref[1:2, :] * pltpu.roll(ext, 1, 0) + w_ref[2:3, :] * ext + b_ref[...]
    return zc[HALO:]


def _convglu_fwd(z, cw, cb, name):
    T = z.shape[0]
    tb, tc = _tile(T, 1024, 8), 256
    nct = FFN_HIDDEN // tc

    def body(gp_ref, g_ref, up_ref, u_ref, wg_ref, wu_ref, bg_ref, bu_ref, o_ref):
        first = pl.program_id(1) == 0

        def conv(p_ref, blk_ref, w_ref, b_ref):
            prev = jnp.where(first, 0.0, p_ref[...])
            return _causal_conv(jnp.concatenate([prev, blk_ref[...]], axis=0), w_ref, b_ref)

        o_ref[...] = (_gelu(conv(gp_ref, g_ref, wg_ref, bg_ref)) * conv(up_ref, u_ref, wu_ref, bu_ref)).astype(BF16)

    (gp, gb, _), (up, ub, _) = _conv_specs(tb, tc, nct, T)
    wspec = lambda off, rows: pl.BlockSpec((rows, tc), lambda j, i: (0, j + off))
    return pl.pallas_call(
        body, grid=(nct, T // tb),
        in_specs=[gp, gb, up, ub, wspec(0, 3), wspec(nct, 3), wspec(0, 1), wspec(nct, 1)],
        out_specs=pl.BlockSpec((tb, tc), lambda j, i: (i, j)),
        out_shape=S((T, FFN_HIDDEN), BF16), name=name,
        compiler_params=_cparams("parallel", "parallel"))(z, z, z, z, cw, cw, cb, cb)


def _convglu_bwd(z, df, cw, cb, name):
    T = z.shape[0]
    tb, tc = _tile(T, 1024, 8), 256
    nct = FFN_HIDDEN // tc
    nrb = T // tb

    def body(gp_ref, g_ref, gn_ref, up_ref, u_ref, un_ref, df_ref, dfn_ref, wg_ref, wu_ref, bg_ref, bu_ref,
             dzg_ref, dzu_ref, dwg_ref, dwu_ref, dbg_ref, dbu_ref):
        i = pl.program_id(1)
        first, last = i == 0, i == nrb - 1

        @pl.when(first)
        def _():
            for ref in (dwg_ref, dwu_ref, dbg_ref, dbu_ref):
                ref[...] = jnp.zeros_like(ref)

        def ext_of(p_ref, blk_ref, n_ref):
            return jnp.concatenate([jnp.where(first, 0.0, p_ref[...]), blk_ref[...], n_ref[...]], axis=0)

        gext, uext = ext_of(gp_ref, g_ref, gn_ref), ext_of(up_ref, u_ref, un_ref)
        gc, uc = _causal_conv(gext, wg_ref, bg_ref), _causal_conv(uext, wu_ref, bu_ref)
        dfe = jnp.concatenate([df_ref[...], jnp.where(last, 0.0, dfn_ref[...])], axis=0)
        ge, gd = _gelu_and_grad(gc)
        dgc, duc = dfe * uc * gd, dfe * ge
        n = tb + HALO

        def back(d, ext, w_ref, dz_ref, dw_ref, db_ref):
            dz = w_ref[2:3, :] * d + w_ref[1:2, :] * pltpu.roll(d, n - 1, 0) + w_ref[0:1, :] * pltpu.roll(d, n - 2, 0)
            dz_ref[...] = dz[:tb].astype(BF16)
            dblk = d[:tb]
            db_ref[...] += jnp.sum(dblk, axis=0, keepdims=True)
            taps = [pltpu.roll(ext, 2, 0)[HALO:HALO + tb], pltpu.roll(ext, 1, 0)[HALO:HALO + tb], ext[HALO:HALO + tb]]
            for k, t in enumerate(taps):
                dw_ref[k:k + 1, :] += jnp.sum(dblk * t, axis=0, keepdims=True)

        back(dgc, gext, wg_ref, dzg_ref, dwg_ref, dbg_ref)
        back(duc, uext, wu_ref, dzu_ref, dwu_ref, dbu_ref)

    gspecs, uspecs = _conv_specs(tb, tc, nct, T)
    per = tb // HALO
    dfs = [pl.BlockSpec((tb, tc), lambda j, i: (i, j)),
           pl.BlockSpec((HALO, tc), lambda j, i: (jnp.minimum((i + 1) * per, T // HALO - 1), j))]
    wspec = lambda off, rows: pl.BlockSpec((rows, tc), lambda j, i: (0, j + off))
    acc = lambda rows: pl.BlockSpec((rows, tc), lambda j, i: (0, j))
    blk = pl.BlockSpec((tb, tc), lambda j, i: (i, j))
    return pl.pallas_call(
        body, grid=(nct, nrb),
        in_specs=gspecs + uspecs + dfs + [wspec(0, 3), wspec(nct, 3), wspec(0, 1), wspec(nct, 1)],
        out_specs=[blk, blk, acc(3), acc(3), acc(1), acc(1)],
        out_shape=[S((T, FFN_HIDDEN), BF16), S((T, FFN_HIDDEN), BF16), S((3, FFN_HIDDEN), F32), S((3, FFN_HIDDEN), F32),
                   S((1, FFN_HIDDEN), F32), S((1, FFN_HIDDEN), F32)],
        name=name, compiler_params=_cparams("parallel", "arbitrary"))(z, z, z, z, z, z, df, df, cw, cw, cb, cb)


SGU_RB = 256


def _sgu_weights(ws_ref):
    i = lax.broadcasted_iota(jnp.int32, (SGU_BLOCK, SGU_BLOCK), 0)
    j = lax.broadcasted_iota(jnp.int32, (SGU_BLOCK, SGU_BLOCK), 1)
    mask = (j < CHUNK) | (i >= CHUNK)
    return mask, [jnp.where(mask, ws_ref[g], 0.0).astype(BF16) for g in range(SGU_GROUPS)]


def _sgu_norm(zv, lng, lnb):
    mu = jnp.mean(zv, axis=-1, keepdims=True)
    vc = zv - mu
    rstd = lax.rsqrt(jnp.mean(vc * vc, axis=-1, keepdims=True) + EPS)
    vh = vc * rstd
    return vh, rstd, vh * lng + lnb


def _sgu_fwd(zpre, lng, lnb, ws, bst):
    T = zpre.shape[0]
    nb = SGU_RB // SGU_BLOCK

    def body(z_ref, lng_ref, lnb_ref, ws_ref, bst_ref, o_ref):
        _, wm = _sgu_weights(ws_ref)
        u = _gelu(z_ref[:, :SGU_WIDTH])
        _, _, vn = _sgu_norm(_gelu(z_ref[:, SGU_WIDTH:]), lng_ref[...], lnb_ref[...])
        vnb = vn.astype(BF16)
        for b in range(nb):
            rows = slice(b * SGU_BLOCK, (b + 1) * SGU_BLOCK)
            for g in range(SGU_GROUPS):
                cols = slice(g * SGU_GW, (g + 1) * SGU_GW)
                mixed = _dot(wm[g], vnb[rows, cols]) + bst_ref[:, g:g + 1]
                o_ref[rows, cols] = (u[rows, cols] * mixed).astype(BF16)

    vec = pl.BlockSpec((1, SGU_WIDTH), lambda i: (0, 0))
    return pl.pallas_call(
        body, grid=(T // SGU_RB,),
        in_specs=[pl.BlockSpec((SGU_RB, 2 * SGU_WIDTH), lambda i: (i, 0)), vec, vec,
                  pl.BlockSpec((SGU_GROUPS, SGU_BLOCK, SGU_BLOCK), lambda i: (0, 0, 0)),
                  pl.BlockSpec((SGU_BLOCK, SGU_GROUPS), lambda i: (0, 0))],
        out_specs=pl.BlockSpec((SGU_RB, SGU_WIDTH), lambda i: (i, 0)),
        out_shape=S((T, SGU_WIDTH), BF16), name="sgu_fwd", compiler_params=_cparams("parallel"))(zpre, lng, lnb, ws, bst)


def _sgu_bwd(zpre, dy, lng, lnb, ws, bst):
    T = zpre.shape[0]
    nb = SGU_RB // SGU_BLOCK

    def body(z_ref, dy_ref, lng_ref, lnb_ref, ws_ref, bst_ref, dz_ref, dws_ref, dbst_ref, dlng_ref, dlnb_ref, dvn):
        @pl.when(pl.program_id(0) == 0)
        def _():
            for ref in (dws_ref, dbst_ref, dlng_ref, dlnb_ref):
                ref[...] = jnp.zeros_like(ref)

        mask, wm = _sgu_weights(ws_ref)
        u, ud = _gelu_and_grad(z_ref[:, :SGU_WIDTH])
        v, vd = _gelu_and_grad(z_ref[:, SGU_WIDTH:])
        vh, rstd, vn = _sgu_norm(v, lng_ref[...], lnb_ref[...])
        vnb = vn.astype(BF16)
        lane8 = lax.broadcasted_iota(jnp.int32, (SGU_BLOCK, SGU_GROUPS), 1)
        dbs = jnp.zeros((SGU_BLOCK, SGU_GROUPS), F32)
        for b in range(nb):
            rows = slice(b * SGU_BLOCK, (b + 1) * SGU_BLOCK)
            for g in range(SGU_GROUPS):
                cols = slice(g * SGU_GW, (g + 1) * SGU_GW)
                vg = vnb[rows, cols]
                mixed = _dot(wm[g], vg) + bst_ref[:, g:g + 1]
                dyv = dy_ref[rows, cols]
                dz_ref[rows, cols] = (dyv * mixed * ud[rows, cols]).astype(BF16)
                dmix = dyv * u[rows, cols]
                dmb = dmix.astype(BF16)
                dvn[rows, cols] = _dot_tn(wm[g], dmb)
                dws_ref[g] += jnp.where(mask, _dot_nt(dmb, vg), 0.0)
                dbs = dbs + jnp.where(lane8 == g, jnp.sum(dmix, axis=-1, keepdims=True), 0.0)
        dbst_ref[...] += dbs
        dvnv = dvn[...]
        dlng_ref[...] += jnp.sum(dvnv * vh, axis=0, keepdims=True)
        dlnb_ref[...] += jnp.sum(dvnv, axis=0, keepdims=True)
        dvh = dvnv * lng_ref[...]
        dv = rstd * (dvh - jnp.mean(dvh, axis=-1, keepdims=True) - vh * jnp.mean(dvh * vh, axis=-1, keepdims=True))
        dz_ref[:, SGU_WIDTH:] = (dv * vd).astype(BF16)

    vec = pl.BlockSpec((1, SGU_WIDTH), lambda i: (0, 0))
    wsp = pl.BlockSpec((SGU_GROUPS, SGU_BLOCK, SGU_BLOCK), lambda i: (0, 0, 0))
    bsp = pl.BlockSpec((SGU_BLOCK, SGU_GROUPS), lambda i: (0, 0))
    return pl.pallas_call(
        body, grid=(T // SGU_RB,),
        in_specs=[pl.BlockSpec((SGU_RB, 2 * SGU_WIDTH), lambda i: (i, 0)),
                  pl.BlockSpec((SGU_RB, SGU_WIDTH), lambda i: (i, 0)), vec, vec, wsp, bsp],
        out_specs=[pl.BlockSpec((SGU_RB, 2 * SGU_WIDTH), lambda i: (i, 0)), wsp, bsp, vec, vec],
        out_shape=[S((T, 2 * SGU_WIDTH), BF16), S((SGU_GROUPS, SGU_BLOCK, SGU_BLOCK), F32),
                   S((SGU_BLOCK, SGU_GROUPS), F32), S((1, SGU_WIDTH), F32), S((1, SGU_WIDTH), F32)],
        scratch_shapes=[pltpu.VMEM((SGU_RB, SGU_WIDTH), F32)], name="sgu_bwd",
        compiler_params=_cparams("arbitrary"))(zpre, dy, lng, lnb, ws, bst)


def _loss_head(h, tgt, g):
    T, D = h.shape
    tr = _tile(T, 512, 8)

    def body(h_ref, t_ref, g_ref, ls_ref, dh_ref, dhb_ref, dg_ref):
        @pl.when(pl.program_id(0) == 0)
        def _():
            ls_ref[...] = jnp.zeros_like(ls_ref)
            dg_ref[...] = jnp.zeros_like(dg_ref)

        hv = h_ref[...]
        r = lax.rsqrt(jnp.mean(hv * hv, axis=-1, keepdims=True) + EPS)
        xh = hv * r
        diff = xh * g_ref[...] - t_ref[...]
        per_row = jnp.mean(diff * diff, axis=-1, keepdims=True)
        ls_ref[...] += jnp.sum(per_row, axis=0, keepdims=True)
        dy = diff * (1.0 / D)
        dg_ref[...] += jnp.sum(dy * xh, axis=0, keepdims=True)
        dxh = dy * g_ref[...]
        dh = r * (dxh - xh * jnp.mean(dxh * xh, axis=-1, keepdims=True))
        dh_ref[...] = dh
        dhb_ref[...] = dh.astype(BF16)

    row = pl.BlockSpec((tr, D), lambda i: (i, 0))
    vec = pl.BlockSpec((1, D), lambda i: (0, 0))
    return pl.pallas_call(
        body, grid=(T // tr,), in_specs=[row, row, vec],
        out_specs=[pl.BlockSpec((1, LANES), lambda i: (0, 0)), row, row, vec],
        out_shape=[S((1, LANES), F32), S((T, D), F32), S((T, D), BF16), S((1, D), F32)],
        name="loss_head", compiler_params=_cparams("arbitrary"))(h, tgt, g)


ANY = pl.BlockSpec(memory_space=pl.ANY)
COPY_PARTS = 4
SWAP_PARTS = 8
DMA = pltpu.SemaphoreType.DMA


def _place():
    return lax.axis_index("x"), lax.axis_index("y"), lax.axis_index("c")


def _nparts(rows, unit, want):
    n = want
    while n > 1 and rows % (unit * n):
        n //= 2
    return n


def _row_unit(dtype):
    return 16 if jnp.dtype(dtype).itemsize == 2 else 8


def _remote(src, dst, send_sems, recv_sems, k, to):
    return pltpu.make_async_remote_copy(src_ref=src, dst_ref=dst, send_sem=send_sems.at[k], recv_sem=recv_sems.at[k],
                                        device_id=to, device_id_type=MESH)


def _sem_ranges(counts):
    first, total = [], 0
    for c in counts:
        first.append(total)
        total += c
    return first, total


class _Gather:
    def __init__(self, shards):
        self.srcs = list(shards)
        self.halves = [a.shape[0] // 2 for a in shards]
        self.units = [_row_unit(a.dtype) for a in shards]
        self.parts = [_nparts(h, u, COPY_PARTS) for h, u in zip(self.halves, self.units)]
        self.first, total = _sem_ranges([3 * n for n in self.parts])
        self.out_shapes = [S((N_CHIPS,) + a.shape, a.dtype) for a in shards]
        self.scratch = [DMA((total,))] * 4
        self.has_relay = True

    def _ops(self, srcs, outs, sems):
        ici_s, ici_r, rel_s, rel_r = sems
        x, y, c = _place()
        me, sibling = (x, y, c), (x, y, 1 - c)
        chips = [(1 - x, y), (x, 1 - y), (1 - x, 1 - y)]
        send, arrive, relay, relayed = [], [], [], []
        for p_ref, out_ref, Rh, unit, n, base in zip(srcs, outs, self.halves, self.units, self.parts, self.first):
            rp = Rh // n

            def part(px, py, pc, k, out_ref=out_ref, Rh=Rh, unit=unit, rp=rp):
                return out_ref.at[2 * px + py, pl.ds(pl.multiple_of(pc * Rh + k * rp, unit), rp), :]

            def mine(k, p_ref=p_ref, Rh=Rh, unit=unit, rp=rp):
                return p_ref.at[pl.ds(pl.multiple_of(c * Rh + k * rp, unit), rp), :]

            for j, chip in enumerate(chips):
                for k in range(n):
                    s = base + j * n + k
                    send.append(_remote(mine(k), part(x, y, c, k), ici_s, ici_r, s, (*chip, c)))
                    arrive.append(_remote(mine(k), part(*chip, c, k), ici_s, ici_r, s, me))
                    relay.append(_remote(part(*chip, c, k), part(*chip, c, k), rel_s, rel_r, s, sibling))
                    relayed.append(_remote(part(*chip, c, k), part(*chip, 1 - c, k), rel_s, rel_r, s, me))
        return send, arrive, relay, relayed

    def start(self, *refs):
        for cp in self._ops(*refs)[0]:
            cp.start()

    def relay(self, *refs):
        _, arrive, relay, _ = self._ops(*refs)
        for a, r in zip(arrive, relay):
            a.wait_recv()
            r.start()

    def finish(self, *refs):
        send, _, relay, relayed = self._ops(*refs)
        for cp in relayed:
            cp.wait_recv()
        for cp in send + relay:
            cp.wait_send()


class _Exchange:
    def __init__(self, items):
        self.srcs = [a for a, _ in items]
        self.kinds = [k for _, k in items]
        self.blocks = []
        for a, kind in items:
            R, Ccols = a.shape
            self.blocks.append({"cols": (R // 2, Ccols // N_CHIPS), "rows": (R // (2 * N_CHIPS), Ccols), "all": (R, Ccols)}[kind])
        self.units = [_row_unit(a.dtype) for a in self.srcs]
        self.parts = [_nparts(b[0], u, COPY_PARTS) for b, u in zip(self.blocks, self.units)]
        self.first, total = _sem_ranges([N_DEV - 2 + n for n in self.parts])
        self.out_shapes = [S((N_DEV,) + b, a.dtype) for a, b in zip(self.srcs, self.blocks)]
        self.scratch = [DMA((total,))] * 2
        self.has_relay = False

    def _ops(self, srcs, outs, sems):
        ss, rs = sems
        x, y, c = _place()
        send, arrive = [], []
        for src_ref, out_ref, kind, (Rb, Cb), unit, n, base in zip(srcs, outs, self.kinds, self.blocks, self.units, self.parts,
                                                                  self.first):
            rp = Rb // n

            def block_for(px, py, pc, r0, rows, src_ref=src_ref, kind=kind, Rb=Rb, Cb=Cb, unit=unit):
                if kind == "cols":
                    return src_ref.at[pl.ds(pl.multiple_of(pc * Rb + r0, unit), rows),
                                      pl.ds(pl.multiple_of((2 * px + py) * Cb, LANES), Cb)]
                if kind == "rows":
                    return src_ref.at[pl.ds(pl.multiple_of((2 * (2 * px + py) + pc) * Rb + r0, unit), rows), :]
                return src_ref.at[pl.ds(r0, rows), :]

            def slot(d, r0, rows, out_ref=out_ref):
                return out_ref.at[d, pl.ds(r0, rows), :]

            me = 4 * x + 2 * y + c
            for k in range(1, N_DEV):
                peer = (x ^ ((k >> 2) & 1), y ^ ((k >> 1) & 1), c ^ (k & 1))
                pieces = [(N_DEV - 2 + q, q * rp, rp) for q in range(n)] if k == 1 else [(k - 2, 0, Rb)]
                for sem, r0, rows in pieces:
                    send.append(_remote(block_for(*peer, r0, rows), slot(me, r0, rows), ss, rs, base + sem, peer))
                    arrive.append(_remote(block_for(*peer, r0, rows), slot(4 * peer[0] + 2 * peer[1] + peer[2], r0, rows),
                                          ss, rs, base + sem, peer))
        return send, arrive

    def start(self, *refs):
        for cp in self._ops(*refs)[0]:
            cp.start()

    def finish(self, *refs):
        send, arrive = self._ops(*refs)
        for cp in arrive:
            cp.wait_recv()
        for cp in send:
            cp.wait_send()


class _Swap:
    def __init__(self, halves):
        self.srcs = list(halves)
        self.parts = [_nparts(a.shape[0], _row_unit(a.dtype), SWAP_PARTS) for a in halves]
        self.first, total = _sem_ranges(self.parts)
        self.out_shapes = [S(a.shape, a.dtype) for a in halves]
        self.scratch = [DMA((total,))] * 2
        self.has_relay = False

    def _ops(self, srcs, outs, sems):
        ss, rs = sems
        x, y, c = _place()
        copies = []
        for h_ref, out_ref, n, base in zip(srcs, outs, self.parts, self.first):
            rp = h_ref.shape[0] // n
            for k in range(n):
                rows = pl.ds(k * rp, rp)
                copies.append(_remote(h_ref.at[rows, :], out_ref.at[rows, :], ss, rs, base + k, (x, y, 1 - c)))
        return copies, copies

    start = _Exchange.start
    finish = _Exchange.finish


def _run_plan(plan, name):
    ni, no = len(plan.srcs), len(plan.out_shapes)

    def body(*refs):
        parts = (refs[:ni], refs[ni:ni + no], refs[ni + no:])
        plan.start(*parts)
        if plan.has_relay:
            plan.relay(*parts)
        plan.finish(*parts)

    return pl.pallas_call(body, out_shape=plan.out_shapes, in_specs=[ANY] * ni, out_specs=[ANY] * no,
                          scratch_shapes=plan.scratch, name=name)(*plan.srcs)


def _call(body, *, grid, in_specs, out_specs, out_shape, name, sem, args, scratch_shapes=(), plan=None):
    if plan is None:
        return pl.pallas_call(body, grid=grid, in_specs=in_specs, out_specs=out_specs, out_shape=out_shape,
                              scratch_shapes=list(scratch_shapes), name=name, compiler_params=_cparams(*sem))(*args), None
    n_in, n_out, n_scr = len(in_specs), len(out_shape), len(scratch_shapes)
    pi, po = len(plan.srcs), len(plan.out_shapes)
    total = math.prod(grid)

    def wrapped(*refs):
        a, refs = refs[:n_in], refs[n_in:]
        pa, refs = refs[:pi], refs[pi:]
        o, refs = refs[:n_out], refs[n_out:]
        pout, refs = refs[:po], refs[po:]
        scr, psem = refs[:n_scr], refs[n_scr:]
        step = 0
        for d, gsize in enumerate(grid):
            step = step * gsize + pl.program_id(d)

        @pl.when(step == 0)
        def _():
            plan.start(pa, pout, psem)

        body(*a, *o, *scr)
        if plan.has_relay:
            @pl.when(step == (3 * total) // 4)
            def _():
                plan.relay(pa, pout, psem)

        @pl.when(step == total - 1)
        def _():
            plan.finish(pa, pout, psem)

    outs = pl.pallas_call(
        wrapped, grid=grid, in_specs=list(in_specs) + [ANY] * pi, out_specs=list(out_specs) + [ANY] * po,
        out_shape=list(out_shape) + plan.out_shapes, scratch_shapes=list(scratch_shapes) + plan.scratch, name=name,
        compiler_params=_cparams(*["arbitrary"] * len(grid)))(*args, *plan.srcs)
    return outs[:n_out], outs[n_out:]


SMEM = pl.BlockSpec(memory_space=pltpu.SMEM)


def _sum_slots(buf, own, me, name):
    n, R, W = buf.shape
    tr = _tile(R, 256, 8)

    def body(me_ref, b_ref, own_ref, o_ref):
        acc = None
        for s in range(n):
            blk = jnp.where(me_ref[0] == s, own_ref[...], b_ref[s]).astype(F32)
            acc = blk if acc is None else acc + blk
        o_ref[...] = acc

    return pl.pallas_call(
        body, grid=(R // tr,),
        in_specs=[SMEM, pl.BlockSpec((n, tr, W), lambda i: (0, i, 0)), pl.BlockSpec((tr, W), lambda i: (i, 0))],
        out_specs=pl.BlockSpec((tr, W), lambda i: (i, 0)), out_shape=S((R, W), F32), name=name,
        compiler_params=_cparams("parallel"))(me, buf, own)


def _adamw_update(wv, gv, mv, vv):
    mn = ADAM_B1 * mv + (1.0 - ADAM_B1) * gv
    vn = ADAM_B2 * vv + (1.0 - ADAM_B2) * (gv * gv)
    m_hat = mn / (1.0 - ADAM_B1 ** ADAM_STEP)
    v_hat = vn / (1.0 - ADAM_B2 ** ADAM_STEP)
    return -ADAM_LR * (m_hat / (jnp.sqrt(v_hat) + ADAM_EPS) + ADAM_WD * wv), mn, vn


def _adamw(w, g, m, v, name):
    R, W = w.shape
    tr = _tile(R, 256, 8)

    def body(w_ref, g_ref, m_ref, v_ref, d_ref, mo_ref, vo_ref):
        d_ref[...], mo_ref[...], vo_ref[...] = _adamw_update(w_ref[...], g_ref[...], m_ref[...], v_ref[...])

    blk = pl.BlockSpec((tr, W), lambda i: (i, 0))
    return pl.pallas_call(
        body, grid=(R // tr,), in_specs=[blk] * 4, out_specs=[blk] * 3, out_shape=[S((R, W), F32)] * 3, name=name,
        compiler_params=_cparams("parallel"))(w, g, m, v)


def _adamw_shard(w, halves, core, m, v, name):
    L, R, C = w.shape
    Rh = R // 2
    tr = _tile(Rh, 256, 8)
    nbh = Rh // tr

    def body(c_ref, w_ref, m_ref, v_ref, *rest):
        pairs, (g_ref, d_ref, mo_ref, vo_ref) = rest[:2 * L], rest[2 * L:]
        l, i = pl.program_id(0), pl.program_id(1)
        mine_rows = i // nbh == c_ref[0]
        gv = None
        for lp in range(L):
            cand = jnp.where(mine_rows, pairs[2 * lp][...], pairs[2 * lp + 1][...])
            gv = cand if gv is None else jnp.where(l == lp, cand, gv)
        g_ref[...] = gv
        d_ref[...], mo_ref[...], vo_ref[...] = _adamw_update(w_ref[...], gv, m_ref[...], v_ref[...])

    blk = pl.BlockSpec((None, tr, C), lambda l, i: (l, i, 0))
    half = lambda lp: pl.BlockSpec((tr, C), lambda l, i: (jnp.where(l == lp, i % nbh, 0), 0))
    return pl.pallas_call(
        body, grid=(L, R // tr), in_specs=[SMEM, blk, blk, blk] + [half(lp) for lp in range(L) for _ in range(2)],
        out_specs=[blk] * 4, out_shape=[S((L, R, C), F32)] * 4, name=name,
        compiler_params=_cparams("parallel", "parallel"))(core, w, m, v, *[h for pair in halves for h in pair])


def _tables(T):
    f32 = F32
    half = RET_QK // 2
    inv = 1.0 / (10000.0 ** jnp.linspace(0.0, 1.0, half, dtype=f32))
    ang = jnp.arange(T).astype(f32)[:, None] * inv[None, :]
    cos, sin = jnp.cos(ang), jnp.sin(ang)
    c2 = jnp.concatenate([cos, cos], axis=-1)
    s2 = jnp.concatenate([-sin, sin], axis=-1)
    log_g = jnp.log1p(-jnp.exp2(-5.0 - jnp.arange(RET_HEADS, dtype=f32)))
    idx = jnp.arange(CHUNK, dtype=f32)
    dintra = jnp.exp(log_g[:, None, None] * jnp.abs(idx[:, None] - idx[None, :]))
    kdec = jnp.exp(log_g[None, :] * (CHUNK - 1 - idx)[:, None]).T
    qdec = jnp.exp(log_g[None, :] * (idx + 1.0)[:, None]).T
    cdec = jnp.exp(log_g * CHUNK)
    bc = lambda a, w: jnp.broadcast_to(a[:, :, None], (RET_HEADS, a.shape[1], w))
    return c2, s2, dintra, bc(qdec, RET_QK), bc(kdec, RET_QK), jnp.broadcast_to(cdec[:, None, None], (RET_HEADS, 1, RET_V))


def _first_forms(g4):
    return {"ab_w_in4": g4["ab_w_in"],
            "ab_w_inT": jnp.transpose(g4["ab_w_in"], (0, 2, 1)).reshape(-1, D_MODEL),
            "ab_w_out": g4["ab_w_out"].reshape(-1, D_MODEL)}


def _late_forms(g4):
    wd = g4["ffn_w_down"]
    per = wd.shape[1] // 2
    return {"c_w_in4": g4["c_w_in"], "c_w_out": g4["c_w_out"].reshape(-1, D_MODEL), "ffn_w_up4": g4["ffn_w_up"],
            "ffn_w_down": [wd[:, l * per:(l + 1) * per].reshape(-1, D_MODEL) for l in range(2)]}


def _local_step(x, tgt, p, late=None, exchange=False):
    T = x.shape[0]
    tab = _tables(T)
    row = lambda a: a.reshape(1, -1)
    tr = lambda w: jnp.transpose(w)
    width = D_MODEL

    hn0 = _rmsnorm_fwd(x, row(p["attn_norm_g"][0]), "norm_a0")
    z0 = _mm(hn0, p["ab_w_in4"], "mm_ab_in", form="kn4")
    ya, r, st = _retention_fwd(z0, *tab)
    bias_t = _bias_tiles(jnp.transpose(_bias_build(p["ab_rel_bias"][0]), (1, 0, 2))[:, :, :BAND])
    yb, late_out = _attention_fwd(z0, bias_t, plan=late[0] if late else None)
    if late:
        p = {**p, **late[1](late_out)}
    h1, hf0 = _mm([ya, yb], p["ab_w_out"], "mm_ab_out", res=x, norm_g=row(p["ffn_norm_g"][0]))

    def ffn_fwd(h, hf, l, next_g):
        zf = _mm(hf, p["ffn_w_up4"], f"mm_up{l}", form="kn4", row0=l * width)
        f = _convglu_fwd(zf, p["ffn_conv_w"][l], row(p["ffn_conv_b"][l]), f"convglu_fwd{l}")
        return zf, f, _mm(f, p["ffn_w_down"][l], f"mm_down{l}", res=h, norm_g=next_g)

    zf0, f0, (h2, hn1) = ffn_fwd(h1, hf0, 0, row(p["attn_norm_g"][1]))
    zc = _mm(hn1, p["c_w_in4"], "mm_c_in", form="kn4")
    lng, lnb, bst, ws = row(p["c_ln_g"][0]), row(p["c_ln_b"][0]), tr(p["c_b_s"][0]), p["c_w_s"][0]
    y1 = _sgu_fwd(zc, lng, lnb, ws, bst)
    h3, hf1 = _mm(y1, p["c_w_out"], "mm_c_out", res=h2, norm_g=row(p["ffn_norm_g"][1]))
    zf1, f1, h4 = ffn_fwd(h3, hf1, 1, None)
    lsum, dh4, dh4b, dgfin = _loss_head(h4, tgt, row(p["final_norm_g"]))

    g, big = {}, {}

    def ffn_bwd(dh, dhb, h_in, hf, zf, f, l):
        big[f"ffn_w_down{l}"] = _mm_tn(f, dhb, f"mmt_down{l}", out_dtype=BF16)
        df = _mm(dhb, p["ffn_w_down"][l], f"mmb_down{l}", form="nk")
        dzg, dzu, dwg, dwu, dbg, dbu = _convglu_bwd(zf, df, p["ffn_conv_w"][l], row(p["ffn_conv_b"][l]), f"convglu_bwd{l}")
        big[f"ffn_w_up{l}"] = _mm_tn(hf, [dzg, dzu], f"mmt_up{l}", out_dtype=BF16)
        dhf = _mm([dzg, dzu], p["ffn_w_up4"], f"mmb_up{l}", form="nk4", row0=l * width, rows=width)
        dh_in, dh_in_b, dgf = _rmsnorm_bwd(h_in, dhf, row(p["ffn_norm_g"][l]), dh, f"norm_f{l}_bwd")
        return dh_in, dh_in_b, dict(ffn_norm_g=dgf[0], ffn_conv_w=jnp.concatenate([dwg, dwu], axis=1),
                                    ffn_conv_b=jnp.concatenate([dbg, dbu], axis=1)[0])

    dh3, dh3b, gf1 = ffn_bwd(dh4, dh4b, h3, hf1, zf1, f1, 1)
    big["c_w_out"] = _mm_tn(y1, dh3b, "mmt_c_out", out_dtype=BF16)
    dy1 = _mm(dh3b, p["c_w_out"], "mmb_c_out", form="nk")
    dzc, dws, dbst, dlng, dlnb = _sgu_bwd(zc, dy1, lng, lnb, ws, bst)
    g["c_w_s"], g["c_b_s"], g["c_ln_g"], g["c_ln_b"] = dws[None], tr(dbst)[None], dlng, dlnb
    big["c_w_in"] = _mm_tn(hn1, dzc, "mmt_c_in", out_dtype=BF16)
    dhn1 = _mm(dzc, p["c_w_in4"], "mmb_c_in", form="nk4", rows=width)
    dh2, dh2b, dga1 = _rmsnorm_bwd(h2, dhn1, row(p["attn_norm_g"][1]), dh3, "norm_a1_bwd")
    dh1, dh1b, gf0 = ffn_bwd(dh2, dh2b, h1, hf0, zf0, f0, 0)
    for k in gf0:
        g[k] = jnp.stack([gf0[k], gf1[k]])
    big["ab_w_out"] = _mm_tn([ya, yb], dh1b, "mmt_ab_out", out_dtype=BF16)
    dycat = _mm(dh1b, p["ab_w_out"], "mmb_ab_out", form="nk")
    late_plan = _Exchange([(big[n], kind) for n, kind in LATE_ITEMS]) if exchange else None
    (dqb, dkb, dvb, dbias_t), late_slots = _attention_bwd(z0, bias_t, dycat, plan=late_plan)
    dqa, dka, dva, dga = _retention_bwd(z0, *tab, r, dycat, st)
    dz0 = [dqa, dka, dva, dga, dqb, dkb, dvb]
    big["ab_w_in"] = _mm_tn(hn0, dz0, "mmt_ab_in", out_dtype=BF16)
    slots = {}
    if exchange:
        dhn0, first_slots = _mm(dz0, p["ab_w_inT"], "mmb_ab_in", plan=_Exchange([(big[n], kind) for n, kind in FIRST_ITEMS]))
        slots = dict(first=first_slots, late=late_slots)
    else:
        dhn0 = _mm(dz0, p["ab_w_inT"], "mmb_ab_in")
    gx, _, dga0 = _rmsnorm_bwd(x, dhn0, row(p["attn_norm_g"][0]), dh1, "norm_a0_bwd")
    g["ab_rel_bias"] = _bias_grad(_bias_bands(dbias_t))[None, :, :N_REL]
    g["attn_norm_g"] = jnp.stack([dga0[0], dga1[0]])
    g["final_norm_g"] = dgfin[0]
    return lsum[0, 0], gx, g, big, slots


FIRST_BIG = ["ab_w_in", "ab_w_out"]
LATE_BIG = ["c_w_in", "c_w_out", "ffn_w_up", "ffn_w_down"]
BIG = FIRST_BIG + LATE_BIG
FIRST_ITEMS = [("ab_w_in", "cols"), ("ab_w_out", "rows")]
LATE_ITEMS = [("c_w_in", "cols"), ("c_w_out", "rows"), ("ffn_w_up0", "cols"), ("ffn_w_up1", "cols"),
              ("ffn_w_down0", "rows"), ("ffn_w_down1", "rows")]
LAYERS_OF = {"ab_w_in": ["ab_w_in"], "ab_w_out": ["ab_w_out"], "c_w_in": ["c_w_in"], "c_w_out": ["c_w_out"],
             "ffn_w_up": ["ffn_w_up0", "ffn_w_up1"], "ffn_w_down": ["ffn_w_down0", "ffn_w_down1"]}
SMALL_SHARDED = [("c_ln_g", 1), ("c_ln_b", 1), ("ffn_conv_w", 2)]
REPLICATED = ["attn_norm_g", "ffn_norm_g", "ab_rel_bias", "c_w_s", "c_b_s", "ffn_conv_b", "final_norm_g"]


def _rows_of(n_elems):
    return -(-n_elems // PACK_W)


def _flat_rows(a):
    f = a.reshape(-1)
    rows = _rows_of(f.shape[0])
    return jnp.pad(f, (0, rows * PACK_W - f.shape[0])).reshape(rows, PACK_W)


def _pad_rows(a, mult):
    extra = (-a.shape[0]) % mult
    return jnp.pad(a, ((0, extra), (0, 0))) if extra else a


def _pack(arrs, mult):
    return _pad_rows(jnp.concatenate([_flat_rows(a) for a in arrs], axis=0), mult)


def _unpack(buf, shapes):
    out, r = [], 0
    for shp in shapes:
        n = math.prod(shp)
        rows = _rows_of(n)
        out.append(buf[r:r + rows].reshape(-1)[:n].reshape(shp))
        r += rows
    return out


def _from_shards(sh, axis):
    m = jnp.moveaxis(sh, 0, axis)
    shp = m.shape
    return m.reshape(shp[:axis] + (shp[axis] * shp[axis + 1],) + shp[axis + 2:])


def _as_bf16_pairs(a):
    return lax.bitcast_convert_type(a.astype(F32), BF16)


def _from_bf16_pairs(a):
    return lax.bitcast_convert_type(a, F32)


def kernel(x, attn_norm_g, ffn_norm_g, ab_w_in, ab_w_out, ab_rel_bias, c_w_in, c_ln_g, c_ln_b, c_w_s, c_b_s, c_w_out, ffn_w_up, ffn_conv_w, ffn_conv_b, ffn_w_down, final_norm_g, loss_target, m_attn_norm_g, m_ffn_norm_g, m_ab_w_in, m_ab_w_out, m_ab_rel_bias, m_c_w_in, m_c_ln_g, m_c_ln_b, m_c_w_s, m_c_b_s, m_c_w_out, m_ffn_w_up, m_ffn_conv_w, m_ffn_conv_b, m_ffn_w_down, m_final_norm_g, v_attn_norm_g, v_ffn_norm_g, v_ab_w_in, v_ab_w_out, v_ab_rel_bias, v_c_w_in, v_c_ln_g, v_c_ln_b, v_c_w_s, v_c_b_s, v_c_w_out, v_ffn_w_up, v_ffn_conv_w, v_ffn_conv_b, v_ffn_w_down, v_final_norm_g):
    w = dict(attn_norm_g=attn_norm_g, ffn_norm_g=ffn_norm_g, ab_w_in=ab_w_in, ab_w_out=ab_w_out, ab_rel_bias=ab_rel_bias,
             c_w_in=c_w_in, c_ln_g=c_ln_g, c_ln_b=c_ln_b, c_w_s=c_w_s, c_b_s=c_b_s, c_w_out=c_w_out, ffn_w_up=ffn_w_up,
             ffn_conv_w=ffn_conv_w, ffn_conv_b=ffn_conv_b, ffn_w_down=ffn_w_down, final_norm_g=final_norm_g)
    m = dict(attn_norm_g=m_attn_norm_g, ffn_norm_g=m_ffn_norm_g, ab_w_in=m_ab_w_in, ab_w_out=m_ab_w_out,
             ab_rel_bias=m_ab_rel_bias, c_w_in=m_c_w_in, c_ln_g=m_c_ln_g, c_ln_b=m_c_ln_b, c_w_s=m_c_w_s, c_b_s=m_c_b_s,
             c_w_out=m_c_w_out, ffn_w_up=m_ffn_w_up, ffn_conv_w=m_ffn_conv_w, ffn_conv_b=m_ffn_conv_b,
             ffn_w_down=m_ffn_w_down, final_norm_g=m_final_norm_g)
    v = dict(attn_norm_g=v_attn_norm_g, ffn_norm_g=v_ffn_norm_g, ab_w_in=v_ab_w_in, ab_w_out=v_ab_w_out,
             ab_rel_bias=v_ab_rel_bias, c_w_in=v_c_w_in, c_ln_g=v_c_ln_g, c_ln_b=v_c_ln_b, c_w_s=v_c_w_s, c_b_s=v_c_b_s,
             c_w_out=v_c_w_out, ffn_w_up=v_ffn_w_up, ffn_conv_w=v_ffn_conv_w, ffn_conv_b=v_ffn_conv_b,
             ffn_w_down=v_ffn_w_down, final_norm_g=v_final_norm_g)
    names = list(w)
    chip = 2 * lax.axis_index("x") + lax.axis_index("y")

    core = lax.axis_index("c")
    core_arr = core.reshape(1).astype(jnp.int32)
    me_arr = (2 * chip + core).reshape(1).astype(jnp.int32)
    two_d = lambda a: a.reshape(-1, a.shape[-1])
    with_own = lambda gathered, own: lax.dynamic_update_slice(gathered, own[None], (chip, 0, 0))

    send_first = [two_d(w[n]).astype(BF16) for n in FIRST_BIG]
    got_first = _run_plan(_Gather(send_first), "gather_first")
    full = {n: w[n] for n in REPLICATED}
    full.update(_first_forms({n: with_own(got, own) for n, got, own in zip(FIRST_BIG, got_first, send_first)}))
    small_send = [_as_bf16_pairs(w[n]) for n, _ in SMALL_SHARDED]
    send_late = [two_d(w[n]).astype(BF16) for n in LATE_BIG] + [_pack(small_send, 32)]

    def finish_late(got):
        whole = [with_own(a, own) for a, own in zip(got, send_late)]
        forms = _late_forms(dict(zip(LATE_BIG, whole)))
        parts = [_unpack(whole[-1][s], [a.shape for a in small_send]) for s in range(N_CHIPS)]
        for i, (n, axis) in enumerate(SMALL_SHARDED):
            forms[n] = _from_shards(_from_bf16_pairs(jnp.stack([parts[s][i] for s in range(N_CHIPS)])), axis)
        return forms

    lsum, grad_x, g, big, slots = _local_step(x[0], loss_target[0], full, late=(_Gather(send_late), finish_late), exchange=True)
    loss = lax.psum(0.5 * lsum, ("x", "y", "c"))

    def own_block(a, kind):
        rows, cols = a.shape
        if kind == "cols":
            return lax.dynamic_slice(a, (core * (rows // 2), chip * (cols // N_CHIPS)), (rows // 2, cols // N_CHIPS))
        per = rows // N_DEV
        return lax.dynamic_slice(a, ((2 * chip + core) * per, 0), (per, cols))

    reduced = {}
    for key, items in (("late", LATE_ITEMS), ("first", FIRST_ITEMS)):
        halves = [_sum_slots(got, own_block(big[n], kind), me_arr, f"sum_{n}") for (n, kind), got in zip(items, slots[key])]
        others = _run_plan(_Swap(halves), f"swap_{key}")
        reduced.update({n: (h, o) for (n, _), h, o in zip(items, halves, others)})
    big_outs = [{}, {}, {}, {}]
    for n in BIG:
        res = _adamw_shard(w[n], [reduced[layer] for layer in LAYERS_OF[n]], core_arr, m[n], v[n], f"adamw_{n}")
        for k in range(4):
            big_outs[k][n] = res[k]

    small_names = REPLICATED + [n for n, _ in SMALL_SHARDED]
    gsmall = _pack([g[n] for n in small_names], 32)
    gsum = _sum_slots(_run_plan(_Exchange([(gsmall, "all")]), "exchange_small")[0], gsmall, me_arr, "sum_small")
    gsmall_full = dict(zip(small_names, _unpack(gsum, [g[n].shape for n in small_names])))
    for n, axis in SMALL_SHARDED:
        size = w[n].shape[axis]
        gsmall_full[n] = lax.dynamic_slice_in_dim(gsmall_full[n], chip * size, size, axis)
    pack_small = lambda d: _pack([d[n] for n in small_names], 8)
    small_out = _adamw(pack_small(w), pack_small(gsmall_full), pack_small(m), pack_small(v), "adamw_small")
    small_shapes = [w[n].shape for n in small_names]

    outs = [{**big_outs[0], **gsmall_full}]
    for k in range(3):
        outs.append({**big_outs[k + 1], **dict(zip(small_names, _unpack(small_out[k], small_shapes)))})
    return (loss, grad_x[None], *[o[n] for o in outs for n in names])
```

```python
import functools
import math

import jax
import jax.numpy as jnp
from jax import lax
from jax.experimental import pallas as pl
from jax.experimental.pallas import tpu as pltpu

F32 = jnp.float32
BF16 = jnp.bfloat16
S = jax.ShapeDtypeStruct
MESH = pl.DeviceIdType.MESH

D_MODEL = 1024
CHUNK = 64
EPS = 1e-6
NEG_INF = -1e30
RET_HEADS, RET_QK, RET_V = 4, 128, 256
ATT_HEADS, ATT_D, ATT_PAST, MAX_REL = 8, 64, 8, 128
BAND = (ATT_PAST + 1) * CHUNK
PADK = ATT_PAST * CHUNK
SGU_BLOCK, SGU_GROUPS, SGU_WIDTH = 128, 8, 2048
SGU_GW = SGU_WIDTH // SGU_GROUPS
FFN_HIDDEN = 2816
N_REL = 2 * MAX_REL + 1
RET_SCALE = RET_QK ** -0.5
ATT_SCALE = ATT_D ** -0.5
ADAM_LR, ADAM_B1, ADAM_B2, ADAM_EPS, ADAM_WD, ADAM_STEP = 0.001, 0.9, 0.999, 1e-08, 0.01, 10

V7X_VMEM_BYTES = 64 * 1024 * 1024
VMEM_LIMIT = V7X_VMEM_BYTES * 7 // 8
LANES = 128
PACK_W = 1024
N_CHIPS = 4
N_DEV = 8

GELU_C = math.sqrt(2.0 / math.pi)
GELU_A = 0.044715


def _cparams(*sem):
    return pltpu.CompilerParams(dimension_semantics=tuple(sem) if sem else None, vmem_limit_bytes=VMEM_LIMIT)


def _tile(n, target, unit=LANES):
    best = None
    for t in range(unit, min(n, target) + 1, unit):
        if n % t == 0:
            best = t
    return best if best is not None else n


def _gelu(x):
    t = jnp.tanh(GELU_C * (x + GELU_A * x * x * x))
    return 0.5 * x * (1.0 + t)


def _gelu_and_grad(x):
    x2 = x * x
    t = jnp.tanh(GELU_C * (x + GELU_A * x2 * x))
    g = 0.5 * x * (1.0 + t)
    dg = 0.5 * (1.0 + t) + 0.5 * x * (1.0 - t * t) * (GELU_C * (1.0 + 3.0 * GELU_A * x2))
    return g, dg


def _sigmoid(x):
    return 1.0 / (1.0 + jnp.exp(-x))


def _dot(a, b):
    return jnp.dot(a, b, preferred_element_type=F32)


def _dot_nt(a, b):
    return lax.dot_general(a, b, (((1,), (1,)), ((), ())), preferred_element_type=F32)


def _dot_tn(a, b):
    return lax.dot_general(a, b, (((0,), (0,)), ((), ())), preferred_element_type=F32)


def _rmsnorm_fwd(x, g, name):
    T, D = x.shape
    tr = _tile(T, 512, 8)

    def body(x_ref, g_ref, o_ref):
        xv = x_ref[...]
        r = lax.rsqrt(jnp.mean(xv * xv, axis=-1, keepdims=True) + EPS)
        o_ref[...] = (xv * r * g_ref[...]).astype(o_ref.dtype)

    return pl.pallas_call(
        body, grid=(T // tr,),
        in_specs=[pl.BlockSpec((tr, D), lambda i: (i, 0)), pl.BlockSpec((1, D), lambda i: (0, 0))],
        out_specs=pl.BlockSpec((tr, D), lambda i: (i, 0)),
        out_shape=S((T, D), BF16), name=name, compiler_params=_cparams("parallel"))(x, g)


def _pieces(a):
    return list(a) if isinstance(a, (list, tuple)) else [a]


def _piece_layout(widths, tile):
    out, s = [], 0
    for w in widths:
        out.append((s, w // tile))
        s += w // tile
    return out


def _common_tile(widths, target):
    return _tile(functools.reduce(math.gcd, widths), target)


def _rmsnorm_bwd(x, dy, g, dres, name):
    T, D = x.shape
    tr = _tile(T, 512, 8)

    def body(x_ref, dy_ref, g_ref, dres_ref, dx_ref, dxb_ref, dg_ref):
        @pl.when(pl.program_id(0) == 0)
        def _():
            dg_ref[...] = jnp.zeros_like(dg_ref)

        xv = x_ref[...]
        r = lax.rsqrt(jnp.mean(xv * xv, axis=-1, keepdims=True) + EPS)
        xh = xv * r
        dyv = dy_ref[...].astype(F32)
        dg_ref[...] += jnp.sum(dyv * xh, axis=0, keepdims=True)
        dxh = dyv * g_ref[...]
        dx = dres_ref[...] + r * (dxh - xh * jnp.mean(dxh * xh, axis=-1, keepdims=True))
        dx_ref[...] = dx
        dxb_ref[...] = dx.astype(BF16)

    row = pl.BlockSpec((tr, D), lambda i: (i, 0))
    vec = pl.BlockSpec((1, D), lambda i: (0, 0))
    return pl.pallas_call(
        body, grid=(T // tr,), in_specs=[row, row, vec, row], out_specs=[row, row, vec],
        out_shape=[S((T, D), F32), S((T, D), BF16), S((1, D), F32)], name=name,
        compiler_params=_cparams("arbitrary"))(x, dy, g, dres)


def _mm(a, b, name, res=None, out_dtype=F32, plan=None, norm_g=None, form="kn", row0=0, rows=None):
    pieces = _pieces(a)
    M = pieces[0].shape[0]
    widths = [p.shape[1] for p in pieces]
    K = sum(widths)
    N = {"kn": lambda: b.shape[1], "kn4": lambda: N_CHIPS * b.shape[2], "nk": lambda: b.shape[0], "nk4": lambda: rows}[form]()
    tn = N if norm_g is not None else (b.shape[2] if form == "kn4" else _tile(N, 1408))
    tk = b.shape[2] if form == "nk4" else _common_tile(widths, 1536)
    tm = _tile(M, 2048 if (K == tk and res is None and norm_g is None) else 1024, 8)
    assert all(w % tk == 0 for w in widths) and row0 % (tk if form == "kn4" else tn) == 0
    nk, npc = K // tk, len(pieces)
    layout = _piece_layout(widths, tk)
    tile = pl.BlockSpec((tm, tn), lambda i, j, k: (i, j))
    vec = pl.BlockSpec((1, tn), lambda i, j, k: (0, j))
    b_spec = {"kn": lambda: pl.BlockSpec((tk, tn), lambda i, j, k: (k, j)),
              "kn4": lambda: pl.BlockSpec((None, tk, tn), lambda i, j, k: (j, row0 // tk + k, 0)),
              "nk": lambda: pl.BlockSpec((tn, tk), lambda i, j, k: (j, k)),
              "nk4": lambda: pl.BlockSpec((None, tn, tk), lambda i, j, k: (k, row0 // tn + j, 0))}[form]()
    dot = _dot if form in ("kn", "kn4") else _dot_nt
    extra_in, extra_specs = [], []
    if res is not None:
        extra_in, extra_specs = [res], [tile]
    if norm_g is not None:
        extra_in, extra_specs = extra_in + [norm_g], extra_specs + [vec]
    n_extra = len(extra_in)
    if norm_g is not None:
        out_shape, out_specs = [S((M, N), out_dtype), S((M, N), BF16)], [tile, tile]
    else:
        out_shape, out_specs = [S((M, N), out_dtype)], [tile]

    def body(*refs):
        a_refs, b_ref = refs[:npc], refs[npc]
        ext = list(refs[npc + 1:npc + 1 + n_extra])
        outs = refs[npc + 1 + n_extra:npc + 1 + n_extra + len(out_shape)]

        def finish(v):
            if res is not None:
                v = v + ext[0][...]
            outs[0][...] = v.astype(outs[0].dtype)
            if norm_g is not None:
                r = lax.rsqrt(jnp.mean(v * v, axis=-1, keepdims=True) + EPS)
                outs[1][...] = (v * r * ext[-1][...]).astype(BF16)

        if nk == 1:
            finish(dot(a_refs[0][...], b_ref[...]))
            return
        acc = refs[-1]
        k = pl.program_id(2)
        for a_ref, (s, c) in zip(a_refs, layout):
            def add(a_ref=a_ref):
                acc[...] += dot(a_ref[...], b_ref[...])

            if s == 0:
                @pl.when(k == 0)
                def _(a_ref=a_ref):
                    acc[...] = dot(a_ref[...], b_ref[...])

                if c > 1:
                    pl.when((k > 0) & (k < c))(add)
            else:
                pl.when((k >= s) & (k < s + c))(add)

        @pl.when(k == nk - 1)
        def _():
            finish(acc[...])

    in_specs = [pl.BlockSpec((tm, tk), lambda i, j, k, s=s, c=c: (i, jnp.clip(k - s, 0, c - 1))) for s, c in layout]
    outs, extra = _call(
        body, grid=(M // tm, N // tn, nk), in_specs=in_specs + [b_spec] + extra_specs, out_specs=out_specs,
        out_shape=out_shape, scratch_shapes=[pltpu.VMEM((tm, tn), F32)] if nk > 1 else [],
        name=name, sem=("parallel", "parallel", "arbitrary"), args=pieces + [b] + extra_in, plan=plan)
    outs = outs[0] if len(outs) == 1 else tuple(outs)
    return outs if plan is None else (outs, extra)


def _mm_tn(a, g, name, out_dtype=F32):
    ap, gp = _pieces(a), _pieces(g)
    T = ap[0].shape[0]
    aw, gw = [p.shape[1] for p in ap], [p.shape[1] for p in gp]
    tm, tn, tt = _common_tile(aw, 1408), _common_tile(gw, 1408), _tile(T, 1024, 8)
    alay, glay = _piece_layout(aw, tm), _piece_layout(gw, tn)
    na = len(ap)

    def inside(idx, s, c, single):
        return None if single else (idx >= s) & (idx < s + c)

    narrow = out_dtype != F32
    nt = T // tt

    def body(*refs):
        a_refs, g_refs = refs[:na], refs[na:na + len(gp)]
        o_ref = refs[na + len(gp)]
        acc = refs[-1] if narrow else o_ref
        i, j, k = pl.program_id(0), pl.program_id(1), pl.program_id(2)

        @pl.when(k == 0)
        def _():
            acc[...] = jnp.zeros_like(acc)

        for a_ref, (sa, ca) in zip(a_refs, alay):
            for g_ref, (sg, cg) in zip(g_refs, glay):
                def add(a_ref=a_ref, g_ref=g_ref):
                    acc[...] += _dot_tn(a_ref[...], g_ref[...])

                conds = [c for c in (inside(i, sa, ca, na == 1), inside(j, sg, cg, len(gp) == 1)) if c is not None]
                if not conds:
                    add()
                else:
                    pl.when(functools.reduce(lambda u, v: u & v, conds))(add)

        if narrow:
            @pl.when(k == nt - 1)
            def _():
                o_ref[...] = acc[...].astype(out_dtype)

    def spec(tile, lay, single, axis):
        s, c = lay

        def index(i, j, k):
            idx = (i, j)[axis]
            if single:
                return (k, idx)
            on = (idx >= s) & (idx < s + c)
            return (jnp.where(on, k, 0), jnp.clip(idx - s, 0, c - 1))

        return pl.BlockSpec((tt, tile), index)

    in_specs = [spec(tm, lay, na == 1, 0) for lay in alay] + [spec(tn, lay, len(gp) == 1, 1) for lay in glay]
    return pl.pallas_call(
        body, grid=(sum(aw) // tm, sum(gw) // tn, nt), in_specs=in_specs,
        out_specs=pl.BlockSpec((tm, tn), lambda i, j, k: (i, j)),
        out_shape=S((sum(aw), sum(gw)), out_dtype), scratch_shapes=[pltpu.VMEM((tm, tn), F32)] if narrow else [],
        name=name, compiler_params=_cparams("parallel", "parallel", "arbitrary"))(*ap, *gp)


def _rotate(x, c2, s2):
    return x * c2 + pltpu.roll(x, RET_QK // 2, 1) * s2


def _unrotate(d, c2, s2):
    return d * c2 - pltpu.roll(d, RET_QK // 2, 1) * s2


def _ret_specs(RB, blockmap):
    q = pl.BlockSpec((RB, RET_QK), lambda h, n: (blockmap(n), h))
    k = pl.BlockSpec((RB, RET_QK), lambda h, n: (blockmap(n), RET_HEADS + h))
    v = pl.BlockSpec((RB, RET_V), lambda h, n: (blockmap(n), RET_HEADS + h))
    g = pl.BlockSpec((RB, RET_V), lambda h, n: (blockmap(n), 2 * RET_HEADS + h))
    tab = pl.BlockSpec((RB, RET_QK), lambda h, n: (blockmap(n), 0))
    return q, k, v, g, tab


def _ret_decay_specs():
    return [pl.BlockSpec((None, CHUNK, CHUNK), lambda h, n: (h, 0, 0)),
            pl.BlockSpec((None, CHUNK, RET_QK), lambda h, n: (h, 0, 0)),
            pl.BlockSpec((None, CHUNK, RET_QK), lambda h, n: (h, 0, 0)),
            pl.BlockSpec((None, 1, RET_V), lambda h, n: (h, 0, 0))]


def _retention_fwd(z, c2, s2, dintra, qdec, kdec, cdec):
    T = z.shape[0]
    RB = min(512, T)
    nch, nb = RB // CHUNK, T // RB

    def body(q_ref, k_ref, v_ref, g_ref, c2_ref, s2_ref, di_ref, qd_ref, kd_ref, cd_ref, ya_ref, r_ref, st_ref, state):
        @pl.when(pl.program_id(1) == 0)
        def _():
            state[...] = jnp.zeros_like(state)

        dmat, qdv, kdv, cdv = di_ref[...], qd_ref[...], kd_ref[...], cd_ref[...]
        for c in range(nch):
            rows = slice(c * CHUNK, (c + 1) * CHUNK)
            c2v, s2v = c2_ref[rows, :], s2_ref[rows, :]
            qr = _rotate(q_ref[rows, :], c2v, s2v)
            kr = _rotate(k_ref[rows, :], c2v, s2v) * RET_SCALE
            vb = v_ref[rows, :].astype(BF16)
            sm = _dot_nt(qr.astype(BF16), kr.astype(BF16)) * dmat
            sb = state[...].astype(BF16)
            st_ref[c] = sb
            o = _dot(sm.astype(BF16), vb) + _dot((qr * qdv).astype(BF16), sb)
            state[...] = state[...] * cdv + _dot_tn((kr * kdv).astype(BF16), vb)
            r_ref[rows, :] = o
            mu = jnp.mean(o, axis=-1, keepdims=True)
            oc = o - mu
            rn = oc * lax.rsqrt(jnp.mean(oc * oc, axis=-1, keepdims=True) + EPS)
            gv = g_ref[rows, :]
            ya_ref[rows, :] = (gv * _sigmoid(gv) * rn).astype(BF16)

    q, k, v, g, tab = _ret_specs(RB, lambda n: n)
    wide = pl.BlockSpec((RB, RET_V), lambda h, n: (n, h))
    return pl.pallas_call(
        body, grid=(RET_HEADS, nb),
        in_specs=[q, k, v, g, tab, tab] + _ret_decay_specs(),
        out_specs=[wide, wide, pl.BlockSpec((None, nch, RET_QK, RET_V), lambda h, n: (h, n, 0, 0))],
        out_shape=[S((T, RET_HEADS * RET_V), BF16), S((T, RET_HEADS * RET_V), F32),
                   S((RET_HEADS, T // CHUNK, RET_QK, RET_V), BF16)],
        scratch_shapes=[pltpu.VMEM((RET_QK, RET_V), F32)], name="retention_fwd",
        compiler_params=_cparams("parallel", "arbitrary"))(z, z, z, z, c2, s2, dintra, qdec, kdec, cdec)


def _retention_bwd(z, c2, s2, dintra, qdec, kdec, cdec, r, dycat, st):
    T = z.shape[0]
    RB = min(512, T)
    nch, nb = RB // CHUNK, T // RB

    def body(q_ref, k_ref, v_ref, g_ref, c2_ref, s2_ref, di_ref, qd_ref, kd_ref, cd_ref, r_ref, dy_ref, st_ref,
             dq_ref, dk_ref, dv_ref, dg_ref, dstate):
        @pl.when(pl.program_id(1) == 0)
        def _():
            dstate[...] = jnp.zeros_like(dstate)

        dmat, qdv, kdv, cdv = di_ref[...], qd_ref[...], kd_ref[...], cd_ref[...]
        for c in reversed(range(nch)):
            rows = slice(c * CHUNK, (c + 1) * CHUNK)
            c2v, s2v = c2_ref[rows, :], s2_ref[rows, :]
            qr = _rotate(q_ref[rows, :], c2v, s2v)
            kr = _rotate(k_ref[rows, :], c2v, s2v) * RET_SCALE
            qb, kb = qr.astype(BF16), kr.astype(BF16)
            vb = v_ref[rows, :].astype(BF16)
            o, gv, dy = r_ref[rows, :], g_ref[rows, :], dy_ref[rows, :]
            mu = jnp.mean(o, axis=-1, keepdims=True)
            oc = o - mu
            rstd = lax.rsqrt(jnp.mean(oc * oc, axis=-1, keepdims=True) + EPS)
            rn = oc * rstd
            sg = _sigmoid(gv)
            dg_ref[rows, :] = (dy * rn * (sg * (1.0 + gv * (1.0 - sg)))).astype(BF16)
            drn = dy * (gv * sg)
            do = rstd * (drn - jnp.mean(drn, axis=-1, keepdims=True) - rn * jnp.mean(drn * rn, axis=-1, keepdims=True))
            dob = do.astype(BF16)
            sm = (_dot_nt(qb, kb) * dmat).astype(BF16)
            kdb = (kr * kdv).astype(BF16)
            dsb = dstate[...].astype(BF16)
            dv_ref[rows, :] = (_dot_tn(sm, dob) + _dot(kdb, dsb)).astype(BF16)
            ds = (_dot_nt(dob, vb) * dmat).astype(BF16)
            dqr = _dot(ds, kb) + _dot_nt(dob, st_ref[c]) * qdv
            dkr = (_dot_tn(ds, qb) + _dot_nt(vb, dsb) * kdv) * RET_SCALE
            dstate[...] = dstate[...] * cdv + _dot_tn((qr * qdv).astype(BF16), dob)
            dq_ref[rows, :] = _unrotate(dqr, c2v, s2v).astype(BF16)
            dk_ref[rows, :] = _unrotate(dkr, c2v, s2v).astype(BF16)

    rev = lambda n: nb - 1 - n
    q, k, v, g, tab = _ret_specs(RB, rev)
    wide = pl.BlockSpec((RB, RET_V), lambda h, n: (rev(n), h))
    narrow = pl.BlockSpec((RB, RET_QK), lambda h, n: (rev(n), h))
    return pl.pallas_call(
        body, grid=(RET_HEADS, nb),
        in_specs=[q, k, v, g, tab, tab] + _ret_decay_specs() + [
            wide, wide, pl.BlockSpec((None, nch, RET_QK, RET_V), lambda h, n: (h, rev(n), 0, 0))],
        out_specs=[narrow, narrow, wide, wide],
        out_shape=[S((T, RET_HEADS * RET_QK), BF16), S((T, RET_HEADS * RET_QK), BF16),
                   S((T, RET_HEADS * RET_V), BF16), S((T, RET_HEADS * RET_V), BF16)],
        scratch_shapes=[pltpu.VMEM((RET_QK, RET_V), F32)], name="retention_bwd",
        compiler_params=_cparams("parallel", "arbitrary"))(z, z, z, z, c2, s2, dintra, qdec, kdec, cdec, r, dycat, st)


def _rel_index(i):
    r = lax.broadcasted_iota(jnp.int32, (3 * LANES, 5 * LANES), 0)
    j = lax.broadcasted_iota(jnp.int32, (3 * LANES, 5 * LANES), 1)
    idx = jnp.clip(i + PADK - j, -MAX_REL, MAX_REL) + MAX_REL
    return (r == idx).astype(BF16)


def _split3(v):
    hi = v.astype(BF16)
    r1 = v - hi.astype(F32)
    mid = r1.astype(BF16)
    lo = (r1 - mid.astype(F32)).astype(BF16)
    return hi, mid, lo


def _bias_build(rb):
    rbp = jnp.pad(rb, ((0, 0), (0, 3 * LANES - N_REL)))

    def body(rb_ref, o_ref):
        e = _rel_index(pl.program_id(0))
        hi, mid, lo = _split3(rb_ref[...])
        o_ref[...] = _dot(hi, e) + _dot(mid, e) + _dot(lo, e)

    return pl.pallas_call(
        body, grid=(CHUNK,), in_specs=[pl.BlockSpec((ATT_HEADS, 3 * LANES), lambda i: (0, 0))],
        out_specs=pl.BlockSpec((None, ATT_HEADS, 5 * LANES), lambda i: (i, 0, 0)),
        out_shape=S((CHUNK, ATT_HEADS, 5 * LANES), F32), name="bias_build",
        compiler_params=_cparams("parallel"))(rbp)


ATT_RB = 512
ATT_QT = 256
ATT_CPT = ATT_QT // CHUNK
ATT_KT = ATT_QT + PADK
ATT_QCOL = (2 * RET_HEADS * RET_QK + 2 * RET_HEADS * RET_V) // LANES
ATT_KCOL = ATT_QCOL + ATT_HEADS * ATT_D // LANES
ATT_VCOL = ATT_KCOL + ATT_HEADS * ATT_D // LANES


def _bias_grad(dbt):
    def body(d_ref, o_ref):
        @pl.when(pl.program_id(0) == 0)
        def _():
            o_ref[...] = jnp.zeros_like(o_ref)

        e = _rel_index(pl.program_id(0))
        d = d_ref[0]
        for ci in range(1, ATT_CPT):
            d = d + d_ref[ci]
        hi, mid, lo = _split3(d)
        o_ref[...] += _dot_nt(hi, e) + _dot_nt(mid, e) + _dot_nt(lo, e)

    return pl.pallas_call(
        body, grid=(CHUNK,),
        in_specs=[pl.BlockSpec((ATT_CPT, None, ATT_HEADS, 5 * LANES), lambda i: (0, i, 0, 0))],
        out_specs=pl.BlockSpec((ATT_HEADS, 3 * LANES), lambda i: (0, 0)),
        out_shape=S((ATT_HEADS, 3 * LANES), F32), name="bias_grad",
        compiler_params=_cparams("arbitrary"))(dbt)


def _bias_tiles(bias):
    parts = [jnp.pad(bias, ((0, 0), (0, 0), (CHUNK * ci, ATT_KT - BAND - CHUNK * ci)), constant_values=NEG_INF)
             for ci in range(ATT_CPT)]
    return jnp.stack(parts, axis=1).reshape(ATT_HEADS, ATT_QT, ATT_KT)


def _bias_bands(dbias_tiles):
    d = dbias_tiles.reshape(ATT_HEADS, ATT_CPT, CHUNK, ATT_KT)
    bands = jnp.stack([d[:, ci, :, CHUNK * ci:CHUNK * ci + BAND] for ci in range(ATT_CPT)])
    return jnp.pad(jnp.transpose(bands, (0, 2, 1, 3)), ((0, 0), (0, 0), (0, 0), (0, 5 * LANES - BAND)))


def _att_fill(kw, vw, klo, khi, vlo, vhi):
    kw[0:ATT_RB, :] = klo[...].astype(BF16)
    kw[ATT_RB:, :] = khi[...].astype(BF16)
    vw[0:ATT_RB, :] = vlo[...].astype(BF16)
    vw[ATT_RB:, :] = vhi[...].astype(BF16)


def _att_probs(qm, kwin, bias, first_key):
    s = _dot_nt(qm, kwin) * ATT_SCALE + bias
    col = lax.broadcasted_iota(jnp.int32, (ATT_QT, ATT_KT), 1)
    s = jnp.where(col + first_key >= 0, s, NEG_INF)
    p = jnp.exp(s - jnp.max(s, axis=-1, keepdims=True))
    return p / jnp.sum(p, axis=-1, keepdims=True)


def _att_in_specs(nq):
    qn = lambda n: jnp.minimum(n, nq - 1)
    blk = lambda col, back: pl.BlockSpec((ATT_RB, LANES), lambda hp, n: (jnp.maximum(qn(n) - back, 0), col + hp))
    return [blk(ATT_QCOL, 0), blk(ATT_KCOL, 1), blk(ATT_KCOL, 0), blk(ATT_VCOL, 1), blk(ATT_VCOL, 0),
            pl.BlockSpec((2, ATT_QT, ATT_KT), lambda hp, n: (hp, 0, 0))]


def _attention_fwd(z, bias_t, plan=None):
    T = z.shape[0]
    nq = T // ATT_RB

    def body(q_ref, klo, khi, vlo, vhi, b_ref, o_ref, kw, vw):
        _att_fill(kw, vw, klo, khi, vlo, vhi)
        lane = lax.broadcasted_iota(jnp.int32, (ATT_QT, LANES), 1)
        n = pl.program_id(1)
        for t in range(ATT_RB // ATT_QT):
            rows = slice(t * ATT_QT, (t + 1) * ATT_QT)
            win = slice(t * ATT_QT, t * ATT_QT + ATT_KT)
            qc = q_ref[rows, :]
            outs = []
            for e in range(2):
                qm = jnp.where((lane >= ATT_D) == (e == 1), qc, 0.0).astype(BF16)
                p = _att_probs(qm, kw[win, :], b_ref[e], (n - 1) * ATT_RB + t * ATT_QT)
                outs.append(_dot(p.astype(BF16), vw[win, :]))
            o_ref[rows, :] = jnp.where(lane < ATT_D, outs[0], outs[1]).astype(BF16)

    (yb,), extra = _call(
        body, grid=(ATT_HEADS // 2, nq), in_specs=_att_in_specs(nq),
        out_specs=[pl.BlockSpec((ATT_RB, LANES), lambda hp, n: (n, hp))],
        out_shape=[S((T, ATT_HEADS * ATT_D), BF16)],
        scratch_shapes=[pltpu.VMEM((2 * ATT_RB, LANES), BF16), pltpu.VMEM((2 * ATT_RB, LANES), BF16)],
        name="attention_fwd", sem=("parallel", "parallel"), args=(z, z, z, z, z, bias_t), plan=plan)
    return yb, extra


def _attention_bwd(z, bias_t, dycat, plan=None):
    T = z.shape[0]
    nq = T // ATT_RB
    dycol = RET_HEADS * RET_V // LANES

    def body(q_ref, klo, khi, vlo, vhi, b_ref, dy_ref, dq_ref, dk_ref, dv_ref, db_ref, kw, vw, dkw, dvw):
        n = pl.program_id(1)

        @pl.when(n == 0)
        def _():
            dkw[...] = jnp.zeros_like(dkw)
            dvw[...] = jnp.zeros_like(dvw)
            db_ref[...] = jnp.zeros_like(db_ref)

        @pl.when(n > 0)
        def _():
            dkw[0:ATT_RB, :] = dkw[ATT_RB:, :]
            dvw[0:ATT_RB, :] = dvw[ATT_RB:, :]
            dkw[ATT_RB:, :] = jnp.zeros((ATT_RB, LANES), F32)
            dvw[ATT_RB:, :] = jnp.zeros((ATT_RB, LANES), F32)

        @pl.when(n < nq)
        def _():
            _att_fill(kw, vw, klo, khi, vlo, vhi)
            lane = lax.broadcasted_iota(jnp.int32, (ATT_QT, LANES), 1)
            for t in range(ATT_RB // ATT_QT):
                rows = slice(t * ATT_QT, (t + 1) * ATT_QT)
                win = slice(t * ATT_QT, t * ATT_QT + ATT_KT)
                qc, dyc = q_ref[rows, :], dy_ref[rows, :]
                kwin, vwin = kw[win, :], vw[win, :]
                dq = jnp.zeros((ATT_QT, LANES), F32)
                for e in range(2):
                    mine = (lane >= ATT_D) == (e == 1)
                    qm = jnp.where(mine, qc, 0.0).astype(BF16)
                    dom = jnp.where(mine, dyc, 0.0).astype(BF16)
                    p = _att_probs(qm, kwin, b_ref[e], (n - 1) * ATT_RB + t * ATT_QT)
                    dp = _dot_nt(dom, vwin)
                    ds = p * (dp - jnp.sum(dp * p, axis=-1, keepdims=True))
                    db_ref[e] += ds
                    dsb = (ds * ATT_SCALE).astype(BF16)
                    dq = dq + jnp.where(mine, _dot(dsb, kwin), 0.0)
                    dkw[win, :] += _dot_tn(dsb, qm)
                    dvw[win, :] += _dot_tn(p.astype(BF16), dom)
                dq_ref[rows, :] = dq.astype(BF16)

        dk_ref[...] = dkw[0:ATT_RB, :].astype(BF16)
        dv_ref[...] = dvw[0:ATT_RB, :].astype(BF16)

    qn = lambda n: jnp.minimum(n, nq - 1)
    out_kv = pl.BlockSpec((ATT_RB, LANES), lambda hp, n: (jnp.maximum(n - 1, 0), hp))
    return _call(
        body, grid=(ATT_HEADS // 2, nq + 1),
        in_specs=_att_in_specs(nq) + [pl.BlockSpec((ATT_RB, LANES), lambda hp, n: (qn(n), dycol + hp))],
        out_specs=[pl.BlockSpec((ATT_RB, LANES), lambda hp, n: (qn(n), hp)), out_kv, out_kv,
                   pl.BlockSpec((2, ATT_QT, ATT_KT), lambda hp, n: (hp, 0, 0))],
        out_shape=[S((T, ATT_HEADS * ATT_D), BF16), S((T, ATT_HEADS * ATT_D), BF16),
                   S((T, ATT_HEADS * ATT_D), BF16), S((ATT_HEADS, ATT_QT, ATT_KT), F32)],
        scratch_shapes=[pltpu.VMEM((2 * ATT_RB, LANES), BF16), pltpu.VMEM((2 * ATT_RB, LANES), BF16),
                        pltpu.VMEM((2 * ATT_RB, LANES), F32), pltpu.VMEM((2 * ATT_RB, LANES), F32)],
        name="attention_bwd", sem=("parallel", "arbitrary"), args=(z, z, z, z, z, bias_t, dycat), plan=plan)


HALO = 8


def _causal_conv(ext, w_ref, b_ref):
    back2, back1 = pltpu.roll(ext, 2, 0), pltpu.roll(ext, 1, 0)
    zc = w_ref[0:1, :] * back2 + w_ref[1:2, :] * back1 + w_ref[2:3, :] * ext + b_ref[...]
    return zc[HALO:], back2, back1


def _ffn_down(z, cw, cb, wd, res, name, norm_g=None):
    T = z.shape[0]
    D = wd.shape[1]
    tb, tc = _tile(T, 1024, 8), 256
    nct = FFN_HIDDEN // tc
    per = tb // HALO

    def body(gp_ref, g_ref, up_ref, u_ref, wg_ref, wu_ref, bg_ref, bu_ref, wd_ref, res_ref, *rest):
        ng_ref = rest[0] if norm_g is not None else None
        f_ref, gc_ref, uc_ref, o_ref = rest[norm_g is not None:][:4]
        hn_ref = rest[5] if norm_g is not None else None
        acc = rest[-1]
        first, j = pl.program_id(0) == 0, pl.program_id(1)

        def conv(p_ref, blk_ref, w_ref, b_ref):
            prev = jnp.where(first, 0.0, p_ref[...])
            return _causal_conv(jnp.concatenate([prev, blk_ref[...]], axis=0), w_ref, b_ref)[0]

        gc, uc = conv(gp_ref, g_ref, wg_ref, bg_ref), conv(up_ref, u_ref, wu_ref, bu_ref)
        gc_ref[...] = gc.astype(BF16)
        uc_ref[...] = uc.astype(BF16)
        f = (_gelu(gc) * uc).astype(BF16)
        f_ref[...] = f
        p = _dot(f, wd_ref[...])

        @pl.when(j == 0)
        def _():
            acc[...] = p

        @pl.when(j > 0)
        def _():
            acc[...] += p

        @pl.when(j == nct - 1)
        def _():
            v = acc[...] + res_ref[...]
            o_ref[...] = v
            if norm_g is not None:
                r = lax.rsqrt(jnp.mean(v * v, axis=-1, keepdims=True) + EPS)
                hn_ref[...] = (v * r * ng_ref[...]).astype(BF16)

    def zspecs(off):
        return [pl.BlockSpec((HALO, tc), lambda i, j: (jnp.maximum(i * per - 1, 0), j + off)),
                pl.BlockSpec((tb, tc), lambda i, j: (i, j + off))]

    wspec = lambda off, rows: pl.BlockSpec((rows, tc), lambda i, j: (0, j + off))
    row = pl.BlockSpec((tb, D), lambda i, j: (i, 0))
    vec = pl.BlockSpec((1, D), lambda i, j: (0, 0))
    normed = norm_g is not None
    return pl.pallas_call(
        body, grid=(T // tb, nct),
        in_specs=zspecs(0) + zspecs(nct) + [wspec(0, 3), wspec(nct, 3), wspec(0, 1), wspec(nct, 1),
                                            pl.BlockSpec((tc, D), lambda i, j: (j, 0)), row] + [vec] * normed,
        out_specs=[pl.BlockSpec((tb, tc), lambda i, j: (i, j))] * 3 + [row] + [row] * normed,
        out_shape=[S((T, FFN_HIDDEN), BF16)] * 3 + [S((T, D), F32)] + [S((T, D), BF16)] * normed,
        scratch_shapes=[pltpu.VMEM((tb, D), F32)], name=name,
        compiler_params=_cparams("parallel", "arbitrary"))(z, z, z, z, cw, cw, cb, cb, wd, res, *([norm_g] * normed))


HALO16 = 16


def _convglu_bwd(z, gc, uc, df, cw, name):
    T = z.shape[0]
    tb, tc = _tile(T, 1024, 8), 256
    nct = FFN_HIDDEN // tc
    nrb = T // tb

    def body(g_ref, u_ref, gc_ref, gcn_ref, uc_ref, ucn_ref, df_ref, dfn_ref, wg_ref, wu_ref,
             dzg_ref, dzu_ref, dwg_ref, dwu_ref, dbg_ref, dbu_ref):
        i = pl.program_id(1)
        first, last = i == 0, i == nrb - 1

        @pl.when(first)
        def _():
            for ref in (dwg_ref, dwu_ref, dbg_ref, dbu_ref):
                ref[...] = jnp.zeros_like(ref)

        ext = lambda blk_ref, n_ref: jnp.concatenate([blk_ref[...].astype(F32), n_ref[...].astype(F32)[0:HALO]], axis=0)
        gcv, ucv = ext(gc_ref, gcn_ref), ext(uc_ref, ucn_ref)
        dfe = jnp.concatenate([df_ref[...], jnp.where(last, 0.0, dfn_ref[...])], axis=0)
        ge, gd = _gelu_and_grad(gcv)
        dgc, duc = dfe * ucv * gd, dfe * ge
        n = tb + HALO

        def back(d, z_ref, w_ref, dz_ref, dw_ref, db_ref):
            ahead1, ahead2 = pltpu.roll(d, n - 1, 0), pltpu.roll(d, n - 2, 0)
            dz_ref[...] = (w_ref[2:3, :] * d + w_ref[1:2, :] * ahead1 + w_ref[0:1, :] * ahead2)[:tb].astype(BF16)
            db_ref[...] += jnp.sum(d[:tb], axis=0, keepdims=True)
            zv = z_ref[...]
            for k, dk in enumerate((ahead2, ahead1, d)):
                dw_ref[k:k + 1, :] += jnp.sum(dk[:tb] * zv, axis=0, keepdims=True)

        back(dgc, g_ref, wg_ref, dzg_ref, dwg_ref, dbg_ref)
        back(duc, u_ref, wu_ref, dzu_ref, dwu_ref, dbu_ref)

    blk = pl.BlockSpec((tb, tc), lambda j, i: (i, j))
    after = lambda rows: pl.BlockSpec((rows, tc), lambda j, i: (jnp.minimum((i + 1) * (tb // rows), T // rows - 1), j))
    zspec = lambda off: pl.BlockSpec((tb, tc), lambda j, i: (i, j + off))
    wspec = lambda off: pl.BlockSpec((3, tc), lambda j, i: (0, j + off))
    acc = lambda rows: pl.BlockSpec((rows, tc), lambda j, i: (0, j))
    return pl.pallas_call(
        body, grid=(nct, nrb),
        in_specs=[zspec(0), zspec(nct), blk, after(HALO16), blk, after(HALO16), blk, after(HALO), wspec(0), wspec(nct)],
        out_specs=[blk, blk, acc(3), acc(3), acc(1), acc(1)],
        out_shape=[S((T, FFN_HIDDEN), BF16), S((T, FFN_HIDDEN), BF16), S((3, FFN_HIDDEN), F32), S((3, FFN_HIDDEN), F32),
                   S((1, FFN_HIDDEN), F32), S((1, FFN_HIDDEN), F32)],
        name=name, compiler_params=_cparams("parallel", "arbitrary"))(z, z, gc, gc, uc, uc, df, df, cw, cw)


SGU_RB = 256


def _sgu_weights(ws_ref):
    i = lax.broadcasted_iota(jnp.int32, (SGU_BLOCK, SGU_BLOCK), 0)
    j = lax.broadcasted_iota(jnp.int32, (SGU_BLOCK, SGU_BLOCK), 1)
    mask = (j < CHUNK) | (i >= CHUNK)
    return mask, [jnp.where(mask, ws_ref[g], 0.0).astype(BF16) for g in range(SGU_GROUPS)]


def _sgu_norm(zv, lng, lnb):
    mu = jnp.mean(zv, axis=-1, keepdims=True)
    vc = zv - mu
    rstd = lax.rsqrt(jnp.mean(vc * vc, axis=-1, keepdims=True) + EPS)
    vh = vc * rstd
    return vh, rstd, vh * lng + lnb


def _sgu_fwd(zpre, lng, lnb, ws, bst):
    T = zpre.shape[0]
    nb = SGU_RB // SGU_BLOCK

    def body(z_ref, lng_ref, lnb_ref, ws_ref, bst_ref, o_ref):
        _, wm = _sgu_weights(ws_ref)
        u = _gelu(z_ref[:, :SGU_WIDTH])
        _, _, vn = _sgu_norm(_gelu(z_ref[:, SGU_WIDTH:]), lng_ref[...], lnb_ref[...])
        vnb = vn.astype(BF16)
        for b in range(nb):
            rows = slice(b * SGU_BLOCK, (b + 1) * SGU_BLOCK)
            for g in range(SGU_GROUPS):
                cols = slice(g * SGU_GW, (g + 1) * SGU_GW)
                mixed = _dot(wm[g], vnb[rows, cols]) + bst_ref[:, g:g + 1]
                o_ref[rows, cols] = (u[rows, cols] * mixed).astype(BF16)

    vec = pl.BlockSpec((1, SGU_WIDTH), lambda i: (0, 0))
    return pl.pallas_call(
        body, grid=(T // SGU_RB,),
        in_specs=[pl.BlockSpec((SGU_RB, 2 * SGU_WIDTH), lambda i: (i, 0)), vec, vec,
                  pl.BlockSpec((SGU_GROUPS, SGU_BLOCK, SGU_BLOCK), lambda i: (0, 0, 0)),
                  pl.BlockSpec((SGU_BLOCK, SGU_GROUPS), lambda i: (0, 0))],
        out_specs=pl.BlockSpec((SGU_RB, SGU_WIDTH), lambda i: (i, 0)),
        out_shape=S((T, SGU_WIDTH), BF16), name="sgu_fwd", compiler_params=_cparams("parallel"))(zpre, lng, lnb, ws, bst)


def _sgu_bwd(zpre, dy, lng, lnb, ws, bst):
    T = zpre.shape[0]
    nb = SGU_RB // SGU_BLOCK

    def body(z_ref, dy_ref, lng_ref, lnb_ref, ws_ref, bst_ref, dz_ref, dws_ref, dbst_ref, dlng_ref, dlnb_ref, dvn):
        @pl.when(pl.program_id(0) == 0)
        def _():
            for ref in (dws_ref, dbst_ref, dlng_ref, dlnb_ref):
                ref[...] = jnp.zeros_like(ref)

        mask, wm = _sgu_weights(ws_ref)
        u, ud = _gelu_and_grad(z_ref[:, :SGU_WIDTH])
        v, vd = _gelu_and_grad(z_ref[:, SGU_WIDTH:])
        vh, rstd, vn = _sgu_norm(v, lng_ref[...], lnb_ref[...])
        vnb = vn.astype(BF16)
        lane8 = lax.broadcasted_iota(jnp.int32, (SGU_BLOCK, SGU_GROUPS), 1)
        dbs = jnp.zeros((SGU_BLOCK, SGU_GROUPS), F32)
        for b in range(nb):
            rows = slice(b * SGU_BLOCK, (b + 1) * SGU_BLOCK)
            for g in range(SGU_GROUPS):
                cols = slice(g * SGU_GW, (g + 1) * SGU_GW)
                vg = vnb[rows, cols]
                mixed = _dot(wm[g], vg) + bst_ref[:, g:g + 1]
                dyv = dy_ref[rows, cols]
                dz_ref[rows, cols] = (dyv * mixed * ud[rows, cols]).astype(BF16)
                dmix = dyv * u[rows, cols]
                dmb = dmix.astype(BF16)
                dvn[rows, cols] = _dot_tn(wm[g], dmb)
                dws_ref[g] += jnp.where(mask, _dot_nt(dmb, vg), 0.0)
                dbs = dbs + jnp.where(lane8 == g, jnp.sum(dmix, axis=-1, keepdims=True), 0.0)
        dbst_ref[...] += dbs
        dvnv = dvn[...]
        dlng_ref[...] += jnp.sum(dvnv * vh, axis=0, keepdims=True)
        dlnb_ref[...] += jnp.sum(dvnv, axis=0, keepdims=True)
        dvh = dvnv * lng_ref[...]
        dv = rstd * (dvh - jnp.mean(dvh, axis=-1, keepdims=True) - vh * jnp.mean(dvh * vh, axis=-1, keepdims=True))
        dz_ref[:, SGU_WIDTH:] = (dv * vd).astype(BF16)

    vec = pl.BlockSpec((1, SGU_WIDTH), lambda i: (0, 0))
    wsp = pl.BlockSpec((SGU_GROUPS, SGU_BLOCK, SGU_BLOCK), lambda i: (0, 0, 0))
    bsp = pl.BlockSpec((SGU_BLOCK, SGU_GROUPS), lambda i: (0, 0))
    return pl.pallas_call(
        body, grid=(T // SGU_RB,),
        in_specs=[pl.BlockSpec((SGU_RB, 2 * SGU_WIDTH), lambda i: (i, 0)),
                  pl.BlockSpec((SGU_RB, SGU_WIDTH), lambda i: (i, 0)), vec, vec, wsp, bsp],
        out_specs=[pl.BlockSpec((SGU_RB, 2 * SGU_WIDTH), lambda i: (i, 0)), wsp, bsp, vec, vec],
        out_shape=[S((T, 2 * SGU_WIDTH), BF16), S((SGU_GROUPS, SGU_BLOCK, SGU_BLOCK), F32),
                   S((SGU_BLOCK, SGU_GROUPS), F32), S((1, SGU_WIDTH), F32), S((1, SGU_WIDTH), F32)],
        scratch_shapes=[pltpu.VMEM((SGU_RB, SGU_WIDTH), F32)], name="sgu_bwd",
        compiler_params=_cparams("arbitrary"))(zpre, dy, lng, lnb, ws, bst)


def _loss_head(h, tgt, g):
    T, D = h.shape
    tr = _tile(T, 512, 8)

    def body(h_ref, t_ref, g_ref, ls_ref, dh_ref, dhb_ref, dg_ref):
        @pl.when(pl.program_id(0) == 0)
        def _():
            ls_ref[...] = jnp.zeros_like(ls_ref)
            dg_ref[...] = jnp.zeros_like(dg_ref)

        hv = h_ref[...]
        r = lax.rsqrt(jnp.mean(hv * hv, axis=-1, keepdims=True) + EPS)
        xh = hv * r
        diff = xh * g_ref[...] - t_ref[...]
        per_row = jnp.mean(diff * diff, axis=-1, keepdims=True)
        ls_ref[...] += jnp.sum(per_row, axis=0, keepdims=True)
        dy = diff * (1.0 / D)
        dg_ref[...] += jnp.sum(dy * xh, axis=0, keepdims=True)
        dxh = dy * g_ref[...]
        dh = r * (dxh - xh * jnp.mean(dxh * xh, axis=-1, keepdims=True))
        dh_ref[...] = dh
        dhb_ref[...] = dh.astype(BF16)

    row = pl.BlockSpec((tr, D), lambda i: (i, 0))
    vec = pl.BlockSpec((1, D), lambda i: (0, 0))
    return pl.pallas_call(
        body, grid=(T // tr,), in_specs=[row, row, vec],
        out_specs=[pl.BlockSpec((1, LANES), lambda i: (0, 0)), row, row, vec],
        out_shape=[S((1, LANES), F32), S((T, D), F32), S((T, D), BF16), S((1, D), F32)],
        name="loss_head", compiler_params=_cparams("arbitrary"))(h, tgt, g)


ANY = pl.BlockSpec(memory_space=pl.ANY)
COPY_PARTS = 4
SWAP_PARTS = 8
DMA = pltpu.SemaphoreType.DMA


def _place():
    return lax.axis_index("x"), lax.axis_index("y"), lax.axis_index("c")


def _nparts(rows, unit, want):
    n = want
    while n > 1 and rows % (unit * n):
        n //= 2
    return n


def _row_unit(dtype):
    return 16 if jnp.dtype(dtype).itemsize == 2 else 8


def _remote(src, dst, send_sems, recv_sems, k, to):
    return pltpu.make_async_remote_copy(src_ref=src, dst_ref=dst, send_sem=send_sems.at[k], recv_sem=recv_sems.at[k],
                                        device_id=to, device_id_type=MESH)


def _sem_ranges(counts):
    first, total = [], 0
    for c in counts:
        first.append(total)
        total += c
    return first, total


class _Gather:
    def __init__(self, shards):
        self.srcs = list(shards)
        self.halves = [a.shape[0] // 2 for a in shards]
        self.units = [_row_unit(a.dtype) for a in shards]
        self.parts = [_nparts(h, u, COPY_PARTS) for h, u in zip(self.halves, self.units)]
        self.first, total = _sem_ranges([3 * n for n in self.parts])
        self.out_shapes = [S((N_CHIPS,) + a.shape, a.dtype) for a in shards]
        self.scratch = [DMA((total,))] * 4
        self.has_relay = True

    def _ops(self, srcs, outs, sems):
        ici_s, ici_r, rel_s, rel_r = sems
        x, y, c = _place()
        me, sibling = (x, y, c), (x, y, 1 - c)
        chips = [(1 - x, y), (x, 1 - y), (1 - x, 1 - y)]
        send, arrive, relay, relayed = [], [], [], []
        for p_ref, out_ref, Rh, unit, n, base in zip(srcs, outs, self.halves, self.units, self.parts, self.first):
            rp = Rh // n

            def part(px, py, pc, k, out_ref=out_ref, Rh=Rh, unit=unit, rp=rp):
                return out_ref.at[2 * px + py, pl.ds(pl.multiple_of(pc * Rh + k * rp, unit), rp), :]

            def mine(k, p_ref=p_ref, Rh=Rh, unit=unit, rp=rp):
                return p_ref.at[pl.ds(pl.multiple_of(c * Rh + k * rp, unit), rp), :]

            for j, chip in enumerate(chips):
                for k in range(n):
                    s = base + j * n + k
                    send.append(_remote(mine(k), part(x, y, c, k), ici_s, ici_r, s, (*chip, c)))
                    arrive.append(_remote(mine(k), part(*chip, c, k), ici_s, ici_r, s, me))
                    relay.append(_remote(part(*chip, c, k), part(*chip, c, k), rel_s, rel_r, s, sibling))
                    relayed.append(_remote(part(*chip, c, k), part(*chip, 1 - c, k), rel_s, rel_r, s, me))
        return send, arrive, relay, relayed

    def start(self, *refs):
        for cp in self._ops(*refs)[0]:
            cp.start()

    def relay(self, *refs):
        _, arrive, relay, _ = self._ops(*refs)
        for a, r in zip(arrive, relay):
            a.wait_recv()
            r.start()

    def finish(self, *refs):
        send, _, relay, relayed = self._ops(*refs)
        for cp in relayed:
            cp.wait_recv()
        for cp in send + relay:
            cp.wait_send()


class _Exchange:
    def __init__(self, items):
        self.srcs = [a for a, _ in items]
        self.kinds = [k for _, k in items]
        self.blocks = []
        for a, kind in items:
            R, Ccols = a.shape
            self.blocks.append({"cols": (R // 2, Ccols // N_CHIPS), "rows": (R // (2 * N_CHIPS), Ccols), "all": (R, Ccols)}[kind])
        self.units = [_row_unit(a.dtype) for a in self.srcs]
        self.parts = [_nparts(b[0], u, COPY_PARTS) for b, u in zip(self.blocks, self.units)]
        self.first, total = _sem_ranges([N_DEV - 2 + n for n in self.parts])
        self.out_shapes = [S((N_DEV,) + b, a.dtype) for a, b in zip(self.srcs, self.blocks)]
        self.scratch = [DMA((total,))] * 2
        self.has_relay = False

    def _ops(self, srcs, outs, sems):
        ss, rs = sems
        x, y, c = _place()
        send, arrive = [], []
        for src_ref, out_ref, kind, (Rb, Cb), unit, n, base in zip(srcs, outs, self.kinds, self.blocks, self.units, self.parts,
                                                                  self.first):
            rp = Rb // n

            def block_for(px, py, pc, r0, rows, src_ref=src_ref, kind=kind, Rb=Rb, Cb=Cb, unit=unit):
                if kind == "cols":
                    return src_ref.at[pl.ds(pl.multiple_of(pc * Rb + r0, unit), rows),
                                      pl.ds(pl.multiple_of((2 * px + py) * Cb, LANES), Cb)]
                if kind == "rows":
                    return src_ref.at[pl.ds(pl.multiple_of((2 * (2 * px + py) + pc) * Rb + r0, unit), rows), :]
                return src_ref.at[pl.ds(r0, rows), :]

            def slot(d, r0, rows, out_ref=out_ref):
                return out_ref.at[d, pl.ds(r0, rows), :]

            me = 4 * x + 2 * y + c
            for k in range(1, N_DEV):
                peer = (x ^ ((k >> 2) & 1), y ^ ((k >> 1) & 1), c ^ (k & 1))
                pieces = [(N_DEV - 2 + q, q * rp, rp) for q in range(n)] if k == 1 else [(k - 2, 0, Rb)]
                for sem, r0, rows in pieces:
                    send.append(_remote(block_for(*peer, r0, rows), slot(me, r0, rows), ss, rs, base + sem, peer))
                    arrive.append(_remote(block_for(*peer, r0, rows), slot(4 * peer[0] + 2 * peer[1] + peer[2], r0, rows),
                                          ss, rs, base + sem, peer))
        return send, arrive

    def start(self, *refs):
        for cp in self._ops(*refs)[0]:
            cp.start()

    def finish(self, *refs):
        send, arrive = self._ops(*refs)
        for cp in arrive:
            cp.wait_recv()
        for cp in send:
            cp.wait_send()


class _Swap:
    def __init__(self, halves):
        self.srcs = list(halves)
        self.parts = [_nparts(a.shape[0], _row_unit(a.dtype), SWAP_PARTS) for a in halves]
        self.first, total = _sem_ranges(self.parts)
        self.out_shapes = [S(a.shape, a.dtype) for a in halves]
        self.scratch = [DMA((total,))] * 2
        self.has_relay = False

    def _ops(self, srcs, outs, sems):
        ss, rs = sems
        x, y, c = _place()
        copies = []
        for h_ref, out_ref, n, base in zip(srcs, outs, self.parts, self.first):
            rp = h_ref.shape[0] // n
            for k in range(n):
                rows = pl.ds(k * rp, rp)
                copies.append(_remote(h_ref.at[rows, :], out_ref.at[rows, :], ss, rs, base + k, (x, y, 1 - c)))
        return copies, copies

    start = _Exchange.start
    finish = _Exchange.finish


def _run_plan(plan, name):
    ni, no = len(plan.srcs), len(plan.out_shapes)

    def body(*refs):
        parts = (refs[:ni], refs[ni:ni + no], refs[ni + no:])
        plan.start(*parts)
        if plan.has_relay:
            plan.relay(*parts)
        plan.finish(*parts)

    return pl.pallas_call(body, out_shape=plan.out_shapes, in_specs=[ANY] * ni, out_specs=[ANY] * no,
                          scratch_shapes=plan.scratch, name=name)(*plan.srcs)


def _call(body, *, grid, in_specs, out_specs, out_shape, name, sem, args, scratch_shapes=(), plan=None):
    if plan is None:
        return pl.pallas_call(body, grid=grid, in_specs=in_specs, out_specs=out_specs, out_shape=out_shape,
                              scratch_shapes=list(scratch_shapes), name=name, compiler_params=_cparams(*sem))(*args), None
    n_in, n_out, n_scr = len(in_specs), len(out_shape), len(scratch_shapes)
    pi, po = len(plan.srcs), len(plan.out_shapes)
    total = math.prod(grid)

    def wrapped(*refs):
        a, refs = refs[:n_in], refs[n_in:]
        pa, refs = refs[:pi], refs[pi:]
        o, refs = refs[:n_out], refs[n_out:]
        pout, refs = refs[:po], refs[po:]
        scr, psem = refs[:n_scr], refs[n_scr:]
        step = 0
        for d, gsize in enumerate(grid):
            step = step * gsize + pl.program_id(d)

        @pl.when(step == 0)
        def _():
            plan.start(pa, pout, psem)

        body(*a, *o, *scr)
        if plan.has_relay:
            @pl.when(step == (3 * total) // 4)
            def _():
                plan.relay(pa, pout, psem)

        @pl.when(step == total - 1)
        def _():
            plan.finish(pa, pout, psem)

    outs = pl.pallas_call(
        wrapped, grid=grid, in_specs=list(in_specs) + [ANY] * pi, out_specs=list(out_specs) + [ANY] * po,
        out_shape=list(out_shape) + plan.out_shapes, scratch_shapes=list(scratch_shapes) + plan.scratch, name=name,
        compiler_params=_cparams(*["arbitrary"] * len(grid)))(*args, *plan.srcs)
    return outs[:n_out], outs[n_out:]


SMEM = pl.BlockSpec(memory_space=pltpu.SMEM)


def _sum_slots(buf, own, me, name):
    n, R, W = buf.shape
    tr = _tile(R, 256, 8)

    def body(me_ref, b_ref, own_ref, o_ref):
        acc = None
        for s in range(n):
            blk = jnp.where(me_ref[0] == s, own_ref[...], b_ref[s]).astype(F32)
            acc = blk if acc is None else acc + blk
        o_ref[...] = acc

    return pl.pallas_call(
        body, grid=(R // tr,),
        in_specs=[SMEM, pl.BlockSpec((n, tr, W), lambda i: (0, i, 0)), pl.BlockSpec((tr, W), lambda i: (i, 0))],
        out_specs=pl.BlockSpec((tr, W), lambda i: (i, 0)), out_shape=S((R, W), F32), name=name,
        compiler_params=_cparams("parallel"))(me, buf, own)


def _adamw_update(wv, gv, mv, vv):
    mn = ADAM_B1 * mv + (1.0 - ADAM_B1) * gv
    vn = ADAM_B2 * vv + (1.0 - ADAM_B2) * (gv * gv)
    m_hat = mn / (1.0 - ADAM_B1 ** ADAM_STEP)
    v_hat = vn / (1.0 - ADAM_B2 ** ADAM_STEP)
    return -ADAM_LR * (m_hat / (jnp.sqrt(v_hat) + ADAM_EPS) + ADAM_WD * wv), mn, vn


def _adamw(w, g, m, v, name):
    R, W = w.shape
    tr = _tile(R, 256, 8)

    def body(w_ref, g_ref, m_ref, v_ref, d_ref, mo_ref, vo_ref):
        d_ref[...], mo_ref[...], vo_ref[...] = _adamw_update(w_ref[...], g_ref[...], m_ref[...], v_ref[...])

    blk = pl.BlockSpec((tr, W), lambda i: (i, 0))
    return pl.pallas_call(
        body, grid=(R // tr,), in_specs=[blk] * 4, out_specs=[blk] * 3, out_shape=[S((R, W), F32)] * 3, name=name,
        compiler_params=_cparams("parallel"))(w, g, m, v)


def _adamw_shard(w, halves, core, m, v, name):
    L, R, C = w.shape
    Rh = R // 2
    tr = _tile(Rh, 256, 8)
    nbh = Rh // tr

    def body(c_ref, w_ref, m_ref, v_ref, *rest):
        pairs, (g_ref, d_ref, mo_ref, vo_ref) = rest[:2 * L], rest[2 * L:]
        l, i = pl.program_id(0), pl.program_id(1)
        mine_rows = i // nbh == c_ref[0]
        gv = None
        for lp in range(L):
            cand = jnp.where(mine_rows, pairs[2 * lp][...], pairs[2 * lp + 1][...])
            gv = cand if gv is None else jnp.where(l == lp, cand, gv)
        g_ref[...] = gv
        d_ref[...], mo_ref[...], vo_ref[...] = _adamw_update(w_ref[...], gv, m_ref[...], v_ref[...])

    blk = pl.BlockSpec((None, tr, C), lambda l, i: (l, i, 0))
    half = lambda lp: pl.BlockSpec((tr, C), lambda l, i: (jnp.where(l == lp, i % nbh, 0), 0))
    return pl.pallas_call(
        body, grid=(L, R // tr), in_specs=[SMEM, blk, blk, blk] + [half(lp) for lp in range(L) for _ in range(2)],
        out_specs=[blk] * 4, out_shape=[S((L, R, C), F32)] * 4, name=name,
        compiler_params=_cparams("parallel", "parallel"))(core, w, m, v, *[h for pair in halves for h in pair])


def _tables(T):
    f32 = F32
    half = RET_QK // 2
    inv = 1.0 / (10000.0 ** jnp.linspace(0.0, 1.0, half, dtype=f32))
    ang = jnp.arange(T).astype(f32)[:, None] * inv[None, :]
    cos, sin = jnp.cos(ang), jnp.sin(ang)
    c2 = jnp.concatenate([cos, cos], axis=-1)
    s2 = jnp.concatenate([-sin, sin], axis=-1)
    log_g = jnp.log1p(-jnp.exp2(-5.0 - jnp.arange(RET_HEADS, dtype=f32)))
    idx = jnp.arange(CHUNK, dtype=f32)
    dintra = jnp.exp(log_g[:, None, None] * jnp.abs(idx[:, None] - idx[None, :]))
    kdec = jnp.exp(log_g[None, :] * (CHUNK - 1 - idx)[:, None]).T
    qdec = jnp.exp(log_g[None, :] * (idx + 1.0)[:, None]).T
    cdec = jnp.exp(log_g * CHUNK)
    bc = lambda a, w: jnp.broadcast_to(a[:, :, None], (RET_HEADS, a.shape[1], w))
    return c2, s2, dintra, bc(qdec, RET_QK), bc(kdec, RET_QK), jnp.broadcast_to(cdec[:, None, None], (RET_HEADS, 1, RET_V))


def _first_forms(g4):
    return {"ab_w_in4": g4["ab_w_in"],
            "ab_w_inT": jnp.transpose(g4["ab_w_in"], (0, 2, 1)).reshape(-1, D_MODEL),
            "ab_w_out": g4["ab_w_out"].reshape(-1, D_MODEL)}


def _late_forms(g4):
    wd = g4["ffn_w_down"]
    per = wd.shape[1] // 2
    return {"c_w_in4": g4["c_w_in"], "c_w_out": g4["c_w_out"].reshape(-1, D_MODEL), "ffn_w_up4": g4["ffn_w_up"],
            "ffn_w_down": [wd[:, l * per:(l + 1) * per].reshape(-1, D_MODEL) for l in range(2)]}


def _local_step(x, tgt, p, late=None, exchange=False):
    T = x.shape[0]
    tab = _tables(T)
    row = lambda a: a.reshape(1, -1)
    tr = lambda w: jnp.transpose(w)
    width = D_MODEL

    hn0 = _rmsnorm_fwd(x, row(p["attn_norm_g"][0]), "norm_a0")
    z0 = _mm(hn0, p["ab_w_in4"], "mm_ab_in", form="kn4")
    ya, r, st = _retention_fwd(z0, *tab)
    bias_t = _bias_tiles(jnp.transpose(_bias_build(p["ab_rel_bias"][0]), (1, 0, 2))[:, :, :BAND])
    yb, late_out = _attention_fwd(z0, bias_t, plan=late[0] if late else None)
    if late:
        p = {**p, **late[1](late_out)}
    h1, hf0 = _mm([ya, yb], p["ab_w_out"], "mm_ab_out", res=x, norm_g=row(p["ffn_norm_g"][0]))

    def ffn_fwd(h, hf, l, next_g):
        zf = _mm(hf, p["ffn_w_up4"], f"mm_up{l}", form="kn4", row0=l * width)
        f, gc, uc, *out = _ffn_down(zf, p["ffn_conv_w"][l], row(p["ffn_conv_b"][l]), p["ffn_w_down"][l], h, f"ffn_down{l}",
                                    norm_g=next_g)
        return (zf, f, gc, uc), (tuple(out) if next_g is not None else out[0])

    kept0, (h2, hn1) = ffn_fwd(h1, hf0, 0, row(p["attn_norm_g"][1]))
    zc = _mm(hn1, p["c_w_in4"], "mm_c_in", form="kn4")
    lng, lnb, bst, ws = row(p["c_ln_g"][0]), row(p["c_ln_b"][0]), tr(p["c_b_s"][0]), p["c_w_s"][0]
    y1 = _sgu_fwd(zc, lng, lnb, ws, bst)
    h3, hf1 = _mm(y1, p["c_w_out"], "mm_c_out", res=h2, norm_g=row(p["ffn_norm_g"][1]))
    kept1, h4 = ffn_fwd(h3, hf1, 1, None)
    lsum, dh4, dh4b, dgfin = _loss_head(h4, tgt, row(p["final_norm_g"]))

    g, big = {}, {}

    def ffn_bwd(dh, dhb, h_in, hf, kept, l):
        zf, f, gc, uc = kept
        big[f"ffn_w_down{l}"] = _mm_tn(f, dhb, f"mmt_down{l}", out_dtype=BF16)
        df = _mm(dhb, p["ffn_w_down"][l], f"mmb_down{l}", form="nk")
        dzg, dzu, dwg, dwu, dbg, dbu = _convglu_bwd(zf, gc, uc, df, p["ffn_conv_w"][l], f"convglu_bwd{l}")
        big[f"ffn_w_up{l}"] = _mm_tn(hf, [dzg, dzu], f"mmt_up{l}", out_dtype=BF16)
        dhf = _mm([dzg, dzu], p["ffn_w_up4"], f"mmb_up{l}", form="nk4", row0=l * width, rows=width, out_dtype=BF16)
        dh_in, dh_in_b, dgf = _rmsnorm_bwd(h_in, dhf, row(p["ffn_norm_g"][l]), dh, f"norm_f{l}_bwd")
        return dh_in, dh_in_b, dict(ffn_norm_g=dgf[0], ffn_conv_w=jnp.concatenate([dwg, dwu], axis=1),
                                    ffn_conv_b=jnp.concatenate([dbg, dbu], axis=1)[0])

    dh3, dh3b, gf1 = ffn_bwd(dh4, dh4b, h3, hf1, kept1, 1)
    big["c_w_out"] = _mm_tn(y1, dh3b, "mmt_c_out", out_dtype=BF16)
    dy1 = _mm(dh3b, p["c_w_out"], "mmb_c_out", form="nk")
    dzc, dws, dbst, dlng, dlnb = _sgu_bwd(zc, dy1, lng, lnb, ws, bst)
    g["c_w_s"], g["c_b_s"], g["c_ln_g"], g["c_ln_b"] = dws[None], tr(dbst)[None], dlng, dlnb
    big["c_w_in"] = _mm_tn(hn1, dzc, "mmt_c_in", out_dtype=BF16)
    dhn1 = _mm(dzc, p["c_w_in4"], "mmb_c_in", form="nk4", rows=width, out_dtype=BF16)
    dh2, dh2b, dga1 = _rmsnorm_bwd(h2, dhn1, row(p["attn_norm_g"][1]), dh3, "norm_a1_bwd")
    dh1, dh1b, gf0 = ffn_bwd(dh2, dh2b, h1, hf0, kept0, 0)
    for k in gf0:
        g[k] = jnp.stack([gf0[k], gf1[k]])
    big["ab_w_out"] = _mm_tn([ya, yb], dh1b, "mmt_ab_out", out_dtype=BF16)
    dycat = _mm(dh1b, p["ab_w_out"], "mmb_ab_out", form="nk")
    late_plan = _Exchange([(big[n], kind) for n, kind in LATE_ITEMS]) if exchange else None
    (dqb, dkb, dvb, dbias_t), late_slots = _attention_bwd(z0, bias_t, dycat, plan=late_plan)
    dqa, dka, dva, dga = _retention_bwd(z0, *tab, r, dycat, st)
    dz0 = [dqa, dka, dva, dga, dqb, dkb, dvb]
    big["ab_w_in"] = _mm_tn(hn0, dz0, "mmt_ab_in", out_dtype=BF16)
    slots = {}
    if exchange:
        dhn0, first_slots = _mm(dz0, p["ab_w_inT"], "mmb_ab_in", out_dtype=BF16,
                                plan=_Exchange([(big[n], kind) for n, kind in FIRST_ITEMS]))
        slots = dict(first=first_slots, late=late_slots)
    else:
        dhn0 = _mm(dz0, p["ab_w_inT"], "mmb_ab_in", out_dtype=BF16)
    gx, _, dga0 = _rmsnorm_bwd(x, dhn0, row(p["attn_norm_g"][0]), dh1, "norm_a0_bwd")
    g["ab_rel_bias"] = _bias_grad(_bias_bands(dbias_t))[None, :, :N_REL]
    g["attn_norm_g"] = jnp.stack([dga0[0], dga1[0]])
    g["final_norm_g"] = dgfin[0]
    return lsum[0, 0], gx, g, big, slots


FIRST_BIG = ["ab_w_in", "ab_w_out"]
LATE_BIG = ["c_w_in", "c_w_out", "ffn_w_up", "ffn_w_down"]
BIG = FIRST_BIG + LATE_BIG
FIRST_ITEMS = [("ab_w_in", "cols"), ("ab_w_out", "rows")]
LATE_ITEMS = [("c_w_in", "cols"), ("c_w_out", "rows"), ("ffn_w_up0", "cols"), ("ffn_w_up1", "cols"),
              ("ffn_w_down0", "rows"), ("ffn_w_down1", "rows")]
LAYERS_OF = {"ab_w_in": ["ab_w_in"], "ab_w_out": ["ab_w_out"], "c_w_in": ["c_w_in"], "c_w_out": ["c_w_out"],
             "ffn_w_up": ["ffn_w_up0", "ffn_w_up1"], "ffn_w_down": ["ffn_w_down0", "ffn_w_down1"]}
SMALL_SHARDED = [("c_ln_g", 1), ("c_ln_b", 1), ("ffn_conv_w", 2)]
REPLICATED = ["attn_norm_g", "ffn_norm_g", "ab_rel_bias", "c_w_s", "c_b_s", "ffn_conv_b", "final_norm_g"]


def _rows_of(n_elems):
    return -(-n_elems // PACK_W)


def _flat_rows(a):
    f = a.reshape(-1)
    rows = _rows_of(f.shape[0])
    return jnp.pad(f, (0, rows * PACK_W - f.shape[0])).reshape(rows, PACK_W)


def _pad_rows(a, mult):
    extra = (-a.shape[0]) % mult
    return jnp.pad(a, ((0, extra), (0, 0))) if extra else a


def _pack(arrs, mult):
    return _pad_rows(jnp.concatenate([_flat_rows(a) for a in arrs], axis=0), mult)


def _unpack(buf, shapes):
    out, r = [], 0
    for shp in shapes:
        n = math.prod(shp)
        rows = _rows_of(n)
        out.append(buf[r:r + rows].reshape(-1)[:n].reshape(shp))
        r += rows
    return out


def _from_shards(sh, axis):
    m = jnp.moveaxis(sh, 0, axis)
    shp = m.shape
    return m.reshape(shp[:axis] + (shp[axis] * shp[axis + 1],) + shp[axis + 2:])


def _as_bf16_pairs(a):
    return lax.bitcast_convert_type(a.astype(F32), BF16)


def _from_bf16_pairs(a):
    return lax.bitcast_convert_type(a, F32)


def kernel(x, attn_norm_g, ffn_norm_g, ab_w_in, ab_w_out, ab_rel_bias, c_w_in, c_ln_g, c_ln_b, c_w_s, c_b_s, c_w_out, ffn_w_up, ffn_conv_w, ffn_conv_b, ffn_w_down, final_norm_g, loss_target, m_attn_norm_g, m_ffn_norm_g, m_ab_w_in, m_ab_w_out, m_ab_rel_bias, m_c_w_in, m_c_ln_g, m_c_ln_b, m_c_w_s, m_c_b_s, m_c_w_out, m_ffn_w_up, m_ffn_conv_w, m_ffn_conv_b, m_ffn_w_down, m_final_norm_g, v_attn_norm_g, v_ffn_norm_g, v_ab_w_in, v_ab_w_out, v_ab_rel_bias, v_c_w_in, v_c_ln_g, v_c_ln_b, v_c_w_s, v_c_b_s, v_c_w_out, v_ffn_w_up, v_ffn_conv_w, v_ffn_conv_b, v_ffn_w_down, v_final_norm_g):
    w = dict(attn_norm_g=attn_norm_g, ffn_norm_g=ffn_norm_g, ab_w_in=ab_w_in, ab_w_out=ab_w_out, ab_rel_bias=ab_rel_bias,
             c_w_in=c_w_in, c_ln_g=c_ln_g, c_ln_b=c_ln_b, c_w_s=c_w_s, c_b_s=c_b_s, c_w_out=c_w_out, ffn_w_up=ffn_w_up,
             ffn_conv_w=ffn_conv_w, ffn_conv_b=ffn_conv_b, ffn_w_down=ffn_w_down, final_norm_g=final_norm_g)
    m = dict(attn_norm_g=m_attn_norm_g, ffn_norm_g=m_ffn_norm_g, ab_w_in=m_ab_w_in, ab_w_out=m_ab_w_out,
             ab_rel_bias=m_ab_rel_bias, c_w_in=m_c_w_in, c_ln_g=m_c_ln_g, c_ln_b=m_c_ln_b, c_w_s=m_c_w_s, c_b_s=m_c_b_s,
             c_w_out=m_c_w_out, ffn_w_up=m_ffn_w_up, ffn_conv_w=m_ffn_conv_w, ffn_conv_b=m_ffn_conv_b,
             ffn_w_down=m_ffn_w_down, final_norm_g=m_final_norm_g)
    v = dict(attn_norm_g=v_attn_norm_g, ffn_norm_g=v_ffn_norm_g, ab_w_in=v_ab_w_in, ab_w_out=v_ab_w_out,
             ab_rel_bias=v_ab_rel_bias, c_w_in=v_c_w_in, c_ln_g=v_c_ln_g, c_ln_b=v_c_ln_b, c_w_s=v_c_w_s, c_b_s=v_c_b_s,
             c_w_out=v_c_w_out, ffn_w_up=v_ffn_w_up, ffn_conv_w=v_ffn_conv_w, ffn_conv_b=v_ffn_conv_b,
             ffn_w_down=v_ffn_w_down, final_norm_g=v_final_norm_g)
    names = list(w)
    chip = 2 * lax.axis_index("x") + lax.axis_index("y")

    core = lax.axis_index("c")
    core_arr = core.reshape(1).astype(jnp.int32)
    me_arr = (2 * chip + core).reshape(1).astype(jnp.int32)
    two_d = lambda a: a.reshape(-1, a.shape[-1])
    with_own = lambda gathered, own: lax.dynamic_update_slice(gathered, own[None], (chip, 0, 0))

    send_first = [two_d(w[n]).astype(BF16) for n in FIRST_BIG]
    got_first = _run_plan(_Gather(send_first), "gather_first")
    full = {n: w[n] for n in REPLICATED}
    full.update(_first_forms({n: with_own(got, own) for n, got, own in zip(FIRST_BIG, got_first, send_first)}))
    small_send = [_as_bf16_pairs(w[n]) for n, _ in SMALL_SHARDED]
    send_late = [two_d(w[n]).astype(BF16) for n in LATE_BIG] + [_pack(small_send, 32)]

    def finish_late(got):
        whole = [with_own(a, own) for a, own in zip(got, send_late)]
        forms = _late_forms(dict(zip(LATE_BIG, whole)))
        parts = [_unpack(whole[-1][s], [a.shape for a in small_send]) for s in range(N_CHIPS)]
        for i, (n, axis) in enumerate(SMALL_SHARDED):
            forms[n] = _from_shards(_from_bf16_pairs(jnp.stack([parts[s][i] for s in range(N_CHIPS)])), axis)
        return forms

    lsum, grad_x, g, big, slots = _local_step(x[0], loss_target[0], full, late=(_Gather(send_late), finish_late), exchange=True)
    loss = lax.psum(0.5 * lsum, ("x", "y", "c"))

    def own_block(a, kind):
        rows, cols = a.shape
        if kind == "cols":
            return lax.dynamic_slice(a, (core * (rows // 2), chip * (cols // N_CHIPS)), (rows // 2, cols // N_CHIPS))
        per = rows // N_DEV
        return lax.dynamic_slice(a, ((2 * chip + core) * per, 0), (per, cols))

    reduced = {}
    for key, items in (("late", LATE_ITEMS), ("first", FIRST_ITEMS)):
        halves = [_sum_slots(got, own_block(big[n], kind), me_arr, f"sum_{n}") for (n, kind), got in zip(items, slots[key])]
        others = _run_plan(_Swap(halves), f"swap_{key}")
        reduced.update({n: (h, o) for (n, _), h, o in zip(items, halves, others)})
    big_outs = [{}, {}, {}, {}]
    for n in BIG:
        res = _adamw_shard(w[n], [reduced[layer] for layer in LAYERS_OF[n]], core_arr, m[n], v[n], f"adamw_{n}")
        for k in range(4):
            big_outs[k][n] = res[k]

    small_names = REPLICATED + [n for n, _ in SMALL_SHARDED]
    gsmall = _pack([g[n] for n in small_names], 32)
    gsum = _sum_slots(_run_plan(_Exchange([(gsmall, "all")]), "exchange_small")[0], gsmall, me_arr, "sum_small")
    gsmall_full = dict(zip(small_names, _unpack(gsum, [g[n].shape for n in small_names])))
    for n, axis in SMALL_SHARDED:
        size = w[n].shape[axis]
        gsmall_full[n] = lax.dynamic_slice_in_dim(gsmall_full[n], chip * size, size, axis)
    pack_small = lambda d: _pack([d[n] for n in small_names], 8)
    small_out = _adamw(pack_small(w), pack_small(gsmall_full), pack_small(m), pack_small(v), "adamw_small")
    small_shapes = [w[n].shape for n in small_names]

    outs = [{**big_outs[0], **gsmall_full}]
    for k in range(3):
        outs.append({**big_outs[k + 1], **dict(zip(small_names, _unpack(small_out[k], small_shapes)))})
    return (loss, grad_x[None], *[o[n] for o in outs for n in names])
```

```python
import functools
import math

import jax
import jax.numpy as jnp
from jax import lax
from jax.experimental import pallas as pl
from jax.experimental.pallas import tpu as pltpu

F32 = jnp.float32
BF16 = jnp.bfloat16
S = jax.ShapeDtypeStruct
MESH = pl.DeviceIdType.MESH

D_MODEL = 1024
CHUNK = 64
EPS = 1e-6
NEG_INF = -1e30
RET_HEADS, RET_QK, RET_V = 4, 128, 256
ATT_HEADS, ATT_D, ATT_PAST, MAX_REL = 8, 64, 8, 128
BAND = (ATT_PAST + 1) * CHUNK
PADK = ATT_PAST * CHUNK
SGU_BLOCK, SGU_GROUPS, SGU_WIDTH = 128, 8, 2048
SGU_GW = SGU_WIDTH // SGU_GROUPS
FFN_HIDDEN = 2816
N_REL = 2 * MAX_REL + 1
RET_SCALE = RET_QK ** -0.5
ATT_SCALE = ATT_D ** -0.5
ADAM_LR, ADAM_B1, ADAM_B2, ADAM_EPS, ADAM_WD, ADAM_STEP = 0.001, 0.9, 0.999, 1e-08, 0.01, 10

V7X_VMEM_BYTES = 64 * 1024 * 1024
VMEM_LIMIT = V7X_VMEM_BYTES * 7 // 8
MM_TILE_BUDGET = V7X_VMEM_BYTES * 11 // 16
LANES = 128
PACK_W = 1024
N_CHIPS = 4
N_DEV = 8

GELU_C = math.sqrt(2.0 / math.pi)
GELU_A = 0.044715


def _cparams(*sem):
    return pltpu.CompilerParams(dimension_semantics=tuple(sem) if sem else None, vmem_limit_bytes=VMEM_LIMIT)


def _tile(n, target, unit=LANES):
    best = None
    for t in range(unit, min(n, target) + 1, unit):
        if n % t == 0:
            best = t
    return best if best is not None else n


def _gelu(x):
    t = jnp.tanh(GELU_C * (x + GELU_A * x * x * x))
    return 0.5 * x * (1.0 + t)


def _gelu_and_grad(x):
    x2 = x * x
    t = jnp.tanh(GELU_C * (x + GELU_A * x2 * x))
    g = 0.5 * x * (1.0 + t)
    dg = 0.5 * (1.0 + t) + 0.5 * x * (1.0 - t * t) * (GELU_C * (1.0 + 3.0 * GELU_A * x2))
    return g, dg


def _sigmoid(x):
    return 1.0 / (1.0 + jnp.exp(-x))


def _dot(a, b):
    return jnp.dot(a, b, preferred_element_type=F32)


def _dot_nt(a, b):
    return lax.dot_general(a, b, (((1,), (1,)), ((), ())), preferred_element_type=F32)


def _dot_tn(a, b):
    return lax.dot_general(a, b, (((0,), (0,)), ((), ())), preferred_element_type=F32)


def _rmsnorm_fwd(x, g, name):
    T, D = x.shape
    tr = _tile(T, 512, 8)

    def body(x_ref, g_ref, o_ref):
        xv = x_ref[...]
        r = lax.rsqrt(jnp.mean(xv * xv, axis=-1, keepdims=True) + EPS)
        o_ref[...] = (xv * r * g_ref[...]).astype(o_ref.dtype)

    return pl.pallas_call(
        body, grid=(T // tr,),
        in_specs=[pl.BlockSpec((tr, D), lambda i: (i, 0)), pl.BlockSpec((1, D), lambda i: (0, 0))],
        out_specs=pl.BlockSpec((tr, D), lambda i: (i, 0)),
        out_shape=S((T, D), BF16), name=name, compiler_params=_cparams("parallel"))(x, g)


def _pieces(a):
    return list(a) if isinstance(a, (list, tuple)) else [a]


def _piece_layout(widths, tile):
    out, s = [], 0
    for w in widths:
        out.append((s, w // tile))
        s += w // tile
    return out


def _common_tile(widths, target):
    return _tile(functools.reduce(math.gcd, widths), target)


def _rmsnorm_bwd(x, dy, g, dres, name):
    T, D = x.shape
    tr = _tile(T, 512, 8)

    def body(x_ref, dy_ref, g_ref, dres_ref, dx_ref, dxb_ref, dg_ref):
        @pl.when(pl.program_id(0) == 0)
        def _():
            dg_ref[...] = jnp.zeros_like(dg_ref)

        xv = x_ref[...]
        r = lax.rsqrt(jnp.mean(xv * xv, axis=-1, keepdims=True) + EPS)
        xh = xv * r
        dyv = dy_ref[...].astype(F32)
        dg_ref[...] += jnp.sum(dyv * xh, axis=0, keepdims=True)
        dxh = dyv * g_ref[...]
        dx = dres_ref[...] + r * (dxh - xh * jnp.mean(dxh * xh, axis=-1, keepdims=True))
        dx_ref[...] = dx
        dxb_ref[...] = dx.astype(BF16)

    row = pl.BlockSpec((tr, D), lambda i: (i, 0))
    vec = pl.BlockSpec((1, D), lambda i: (0, 0))
    return pl.pallas_call(
        body, grid=(T // tr,), in_specs=[row, row, vec, row], out_specs=[row, row, vec],
        out_shape=[S((T, D), F32), S((T, D), BF16), S((1, D), F32)], name=name,
        compiler_params=_cparams("arbitrary"))(x, dy, g, dres)


def _mm(a, b, name, res=None, out_dtype=F32, plan=None, norm_g=None, form="kn", row0=0, rows=None):
    pieces = _pieces(a)
    M = pieces[0].shape[0]
    widths = [p.shape[1] for p in pieces]
    K = sum(widths)
    N = {"kn": lambda: b.shape[1], "kn4": lambda: N_CHIPS * b.shape[2], "nk": lambda: b.shape[0], "nk4": lambda: rows}[form]()
    tn = N if norm_g is not None else (b.shape[2] if form == "kn4" else _tile(N, 1408))
    tk = b.shape[2] if form == "nk4" else _common_tile(widths, 1536)
    def vmem_bytes(rows):
        out_bytes = jnp.dtype(out_dtype).itemsize + 2 * (norm_g is not None)
        blocks = len(pieces) * rows * tk * 2 + tk * tn * 2 + rows * tn * (4 * (res is not None) + out_bytes)
        return 2 * blocks + rows * tn * 4 * (K != tk)

    tm = _tile(M, 2048 if vmem_bytes(2048) <= MM_TILE_BUDGET else 1024, 8)
    assert all(w % tk == 0 for w in widths) and row0 % (tk if form == "kn4" else tn) == 0
    nk, npc = K // tk, len(pieces)
    layout = _piece_layout(widths, tk)
    tile = pl.BlockSpec((tm, tn), lambda i, j, k: (i, j))
    vec = pl.BlockSpec((1, tn), lambda i, j, k: (0, j))
    b_spec = {"kn": lambda: pl.BlockSpec((tk, tn), lambda i, j, k: (k, j)),
              "kn4": lambda: pl.BlockSpec((None, tk, tn), lambda i, j, k: (j, row0 // tk + k, 0)),
              "nk": lambda: pl.BlockSpec((tn, tk), lambda i, j, k: (j, k)),
              "nk4": lambda: pl.BlockSpec((None, tn, tk), lambda i, j, k: (k, row0 // tn + j, 0))}[form]()
    dot = _dot if form in ("kn", "kn4") else _dot_nt
    extra_in, extra_specs = [], []
    if res is not None:
        extra_in, extra_specs = [res], [tile]
    if norm_g is not None:
        extra_in, extra_specs = extra_in + [norm_g], extra_specs + [vec]
    n_extra = len(extra_in)
    if norm_g is not None:
        out_shape, out_specs = [S((M, N), out_dtype), S((M, N), BF16)], [tile, tile]
    else:
        out_shape, out_specs = [S((M, N), out_dtype)], [tile]

    def body(*refs):
        a_refs, b_ref = refs[:npc], refs[npc]
        ext = list(refs[npc + 1:npc + 1 + n_extra])
        outs = refs[npc + 1 + n_extra:npc + 1 + n_extra + len(out_shape)]

        def finish(v):
            if res is not None:
                v = v + ext[0][...]
            outs[0][...] = v.astype(outs[0].dtype)
            if norm_g is not None:
                r = lax.rsqrt(jnp.mean(v * v, axis=-1, keepdims=True) + EPS)
                outs[1][...] = (v * r * ext[-1][...]).astype(BF16)

        if nk == 1:
            finish(dot(a_refs[0][...], b_ref[...]))
            return
        acc = refs[-1]
        k = pl.program_id(2)
        for a_ref, (s, c) in zip(a_refs, layout):
            def add(a_ref=a_ref):
                acc[...] += dot(a_ref[...], b_ref[...])

            if s == 0:
                @pl.when(k == 0)
                def _(a_ref=a_ref):
                    acc[...] = dot(a_ref[...], b_ref[...])

                if c > 1:
                    pl.when((k > 0) & (k < c))(add)
            else:
                pl.when((k >= s) & (k < s + c))(add)

        @pl.when(k == nk - 1)
        def _():
            finish(acc[...])

    in_specs = [pl.BlockSpec((tm, tk), lambda i, j, k, s=s, c=c: (i, jnp.clip(k - s, 0, c - 1))) for s, c in layout]
    outs, extra = _call(
        body, grid=(M // tm, N // tn, nk), in_specs=in_specs + [b_spec] + extra_specs, out_specs=out_specs,
        out_shape=out_shape, scratch_shapes=[pltpu.VMEM((tm, tn), F32)] if nk > 1 else [],
        name=name, sem=("parallel", "parallel", "arbitrary"), args=pieces + [b] + extra_in, plan=plan)
    outs = outs[0] if len(outs) == 1 else tuple(outs)
    return outs if plan is None else (outs, extra)


def _mm_tn(a, g, name, out_dtype=F32):
    ap, gp = _pieces(a), _pieces(g)
    T = ap[0].shape[0]
    aw, gw = [p.shape[1] for p in ap], [p.shape[1] for p in gp]
    tm, tn = _common_tile(aw, 1408), _common_tile(gw, 1408)
    narrow = out_dtype != F32
    tall = 2 * 2 * 2048 * (len(ap) * tm + len(gp) * tn) + tm * tn * (2 * jnp.dtype(out_dtype).itemsize + 4 * narrow)
    tt = _tile(T, 2048 if tall <= MM_TILE_BUDGET else 1024, 8)
    alay, glay = _piece_layout(aw, tm), _piece_layout(gw, tn)
    na = len(ap)

    def inside(idx, s, c, single):
        return None if single else (idx >= s) & (idx < s + c)

    nt = T // tt

    def body(*refs):
        a_refs, g_refs = refs[:na], refs[na:na + len(gp)]
        o_ref = refs[na + len(gp)]
        acc = refs[-1] if narrow else o_ref
        i, j, k = pl.program_id(0), pl.program_id(1), pl.program_id(2)

        @pl.when(k == 0)
        def _():
            acc[...] = jnp.zeros_like(acc)

        for a_ref, (sa, ca) in zip(a_refs, alay):
            for g_ref, (sg, cg) in zip(g_refs, glay):
                def add(a_ref=a_ref, g_ref=g_ref):
                    acc[...] += _dot_tn(a_ref[...], g_ref[...])

                conds = [c for c in (inside(i, sa, ca, na == 1), inside(j, sg, cg, len(gp) == 1)) if c is not None]
                if not conds:
                    add()
                else:
                    pl.when(functools.reduce(lambda u, v: u & v, conds))(add)

        if narrow:
            @pl.when(k == nt - 1)
            def _():
                o_ref[...] = acc[...].astype(out_dtype)

    def spec(tile, lay, single, axis):
        s, c = lay

        def index(i, j, k):
            idx = (i, j)[axis]
            if single:
                return (k, idx)
            on = (idx >= s) & (idx < s + c)
            return (jnp.where(on, k, 0), jnp.clip(idx - s, 0, c - 1))

        return pl.BlockSpec((tt, tile), index)

    in_specs = [spec(tm, lay, na == 1, 0) for lay in alay] + [spec(tn, lay, len(gp) == 1, 1) for lay in glay]
    return pl.pallas_call(
        body, grid=(sum(aw) // tm, sum(gw) // tn, nt), in_specs=in_specs,
        out_specs=pl.BlockSpec((tm, tn), lambda i, j, k: (i, j)),
        out_shape=S((sum(aw), sum(gw)), out_dtype), scratch_shapes=[pltpu.VMEM((tm, tn), F32)] if narrow else [],
        name=name, compiler_params=_cparams("parallel", "parallel", "arbitrary"))(*ap, *gp)


def _rotate(x, c2, s2):
    return x * c2 + pltpu.roll(x, RET_QK // 2, 1) * s2


def _unrotate(d, c2, s2):
    return d * c2 - pltpu.roll(d, RET_QK // 2, 1) * s2


RET_PAIR = 2
RET_STEPS = RET_HEADS // RET_PAIR


def _ret_specs(RB, blockmap):
    qk, vg = RET_PAIR * RET_QK, RET_PAIR * RET_V
    q = pl.BlockSpec((RB, qk), lambda h, n: (blockmap(n), h))
    k = pl.BlockSpec((RB, qk), lambda h, n: (blockmap(n), RET_STEPS + h))
    v = pl.BlockSpec((RB, vg), lambda h, n: (blockmap(n), RET_STEPS + h))
    g = pl.BlockSpec((RB, vg), lambda h, n: (blockmap(n), 2 * RET_STEPS + h))
    tab = pl.BlockSpec((RB, RET_QK), lambda h, n: (blockmap(n), 0))
    return q, k, v, g, tab


def _ret_decay_specs():
    return [pl.BlockSpec((RET_PAIR, CHUNK, CHUNK), lambda h, n: (h, 0, 0)),
            pl.BlockSpec((RET_PAIR, CHUNK, RET_QK), lambda h, n: (h, 0, 0)),
            pl.BlockSpec((RET_PAIR, CHUNK, RET_QK), lambda h, n: (h, 0, 0)),
            pl.BlockSpec((RET_PAIR, 1, RET_V), lambda h, n: (h, 0, 0))]


def _ret_cols(e):
    return slice(e * RET_QK, (e + 1) * RET_QK), slice(e * RET_V, (e + 1) * RET_V)


def _retention_fwd(z, c2, s2, dintra, qdec, kdec, cdec):
    T = z.shape[0]
    RB = min(512, T)
    nch, nb = RB // CHUNK, T // RB

    def body(q_ref, k_ref, v_ref, g_ref, c2_ref, s2_ref, di_ref, qd_ref, kd_ref, cd_ref, ya_ref, r_ref, st_ref, state):
        @pl.when(pl.program_id(1) == 0)
        def _():
            state[...] = jnp.zeros_like(state)

        for c in range(nch):
            rows = slice(c * CHUNK, (c + 1) * CHUNK)
            c2v, s2v = c2_ref[rows, :], s2_ref[rows, :]
            for e in range(RET_PAIR):
                qk, vg = _ret_cols(e)
                dmat, qdv, kdv, cdv = di_ref[e], qd_ref[e], kd_ref[e], cd_ref[e]
                qr = _rotate(q_ref[rows, qk], c2v, s2v)
                kr = _rotate(k_ref[rows, qk], c2v, s2v) * RET_SCALE
                vb = v_ref[rows, vg].astype(BF16)
                sm = _dot_nt(qr.astype(BF16), kr.astype(BF16)) * dmat
                sb = state[e].astype(BF16)
                st_ref[e, c] = sb
                o = _dot(sm.astype(BF16), vb) + _dot((qr * qdv).astype(BF16), sb)
                state[e] = state[e] * cdv + _dot_tn((kr * kdv).astype(BF16), vb)
                r_ref[rows, vg] = o
                mu = jnp.mean(o, axis=-1, keepdims=True)
                oc = o - mu
                rn = oc * lax.rsqrt(jnp.mean(oc * oc, axis=-1, keepdims=True) + EPS)
                gv = g_ref[rows, vg]
                ya_ref[rows, vg] = (gv * _sigmoid(gv) * rn).astype(BF16)

    q, k, v, g, tab = _ret_specs(RB, lambda n: n)
    wide = pl.BlockSpec((RB, RET_PAIR * RET_V), lambda h, n: (n, h))
    return pl.pallas_call(
        body, grid=(RET_STEPS, nb),
        in_specs=[q, k, v, g, tab, tab] + _ret_decay_specs(),
        out_specs=[wide, wide, pl.BlockSpec((RET_PAIR, nch, RET_QK, RET_V), lambda h, n: (h, n, 0, 0))],
        out_shape=[S((T, RET_HEADS * RET_V), BF16), S((T, RET_HEADS * RET_V), F32),
                   S((RET_HEADS, T // CHUNK, RET_QK, RET_V), BF16)],
        scratch_shapes=[pltpu.VMEM((RET_PAIR, RET_QK, RET_V), F32)], name="retention_fwd",
        compiler_params=_cparams("parallel", "arbitrary"))(z, z, z, z, c2, s2, dintra, qdec, kdec, cdec)


def _retention_bwd(z, c2, s2, dintra, qdec, kdec, cdec, r, dycat, st):
    T = z.shape[0]
    RB = min(512, T)
    nch, nb = RB // CHUNK, T // RB

    def body(q_ref, k_ref, v_ref, g_ref, c2_ref, s2_ref, di_ref, qd_ref, kd_ref, cd_ref, r_ref, dy_ref, st_ref,
             dq_ref, dk_ref, dv_ref, dg_ref, dstate):
        @pl.when(pl.program_id(1) == 0)
        def _():
            dstate[...] = jnp.zeros_like(dstate)

        for c in reversed(range(nch)):
            rows = slice(c * CHUNK, (c + 1) * CHUNK)
            c2v, s2v = c2_ref[rows, :], s2_ref[rows, :]
            for e in range(RET_PAIR):
                qk, vg = _ret_cols(e)
                dmat, qdv, kdv, cdv = di_ref[e], qd_ref[e], kd_ref[e], cd_ref[e]
                qr = _rotate(q_ref[rows, qk], c2v, s2v)
                kr = _rotate(k_ref[rows, qk], c2v, s2v) * RET_SCALE
                qb, kb = qr.astype(BF16), kr.astype(BF16)
                vb = v_ref[rows, vg].astype(BF16)
                o, gv, dy = r_ref[rows, vg], g_ref[rows, vg], dy_ref[rows, vg]
                mu = jnp.mean(o, axis=-1, keepdims=True)
                oc = o - mu
                rstd = lax.rsqrt(jnp.mean(oc * oc, axis=-1, keepdims=True) + EPS)
                rn = oc * rstd
                sg = _sigmoid(gv)
                dg_ref[rows, vg] = (dy * rn * (sg * (1.0 + gv * (1.0 - sg)))).astype(BF16)
                drn = dy * (gv * sg)
                do = rstd * (drn - jnp.mean(drn, axis=-1, keepdims=True) - rn * jnp.mean(drn * rn, axis=-1, keepdims=True))
                dob = do.astype(BF16)
                sm = (_dot_nt(qb, kb) * dmat).astype(BF16)
                kdb = (kr * kdv).astype(BF16)
                dsb = dstate[e].astype(BF16)
                dv_ref[rows, vg] = (_dot_tn(sm, dob) + _dot(kdb, dsb)).astype(BF16)
                ds = (_dot_nt(dob, vb) * dmat).astype(BF16)
                dqr = _dot(ds, kb) + _dot_nt(dob, st_ref[e, c]) * qdv
                dkr = (_dot_tn(ds, qb) + _dot_nt(vb, dsb) * kdv) * RET_SCALE
                dstate[e] = dstate[e] * cdv + _dot_tn((qr * qdv).astype(BF16), dob)
                dq_ref[rows, qk] = _unrotate(dqr, c2v, s2v).astype(BF16)
                dk_ref[rows, qk] = _unrotate(dkr, c2v, s2v).astype(BF16)

    rev = lambda n: nb - 1 - n
    q, k, v, g, tab = _ret_specs(RB, rev)
    wide = pl.BlockSpec((RB, RET_PAIR * RET_V), lambda h, n: (rev(n), h))
    narrow = pl.BlockSpec((RB, RET_PAIR * RET_QK), lambda h, n: (rev(n), h))
    return pl.pallas_call(
        body, grid=(RET_STEPS, nb),
        in_specs=[q, k, v, g, tab, tab] + _ret_decay_specs() + [
            wide, wide, pl.BlockSpec((RET_PAIR, nch, RET_QK, RET_V), lambda h, n: (h, rev(n), 0, 0))],
        out_specs=[narrow, narrow, wide, wide],
        out_shape=[S((T, RET_HEADS * RET_QK), BF16), S((T, RET_HEADS * RET_QK), BF16),
                   S((T, RET_HEADS * RET_V), BF16), S((T, RET_HEADS * RET_V), BF16)],
        scratch_shapes=[pltpu.VMEM((RET_PAIR, RET_QK, RET_V), F32)], name="retention_bwd",
        compiler_params=_cparams("parallel", "arbitrary"))(z, z, z, z, c2, s2, dintra, qdec, kdec, cdec, r, dycat, st)


def _rel_index(i):
    r = lax.broadcasted_iota(jnp.int32, (3 * LANES, 5 * LANES), 0)
    j = lax.broadcasted_iota(jnp.int32, (3 * LANES, 5 * LANES), 1)
    idx = jnp.clip(i + PADK - j, -MAX_REL, MAX_REL) + MAX_REL
    return (r == idx).astype(BF16)


def _split3(v):
    hi = v.astype(BF16)
    r1 = v - hi.astype(F32)
    mid = r1.astype(BF16)
    lo = (r1 - mid.astype(F32)).astype(BF16)
    return hi, mid, lo


def _bias_build(rb):
    rbp = jnp.pad(rb, ((0, 0), (0, 3 * LANES - N_REL)))

    def body(rb_ref, o_ref):
        e = _rel_index(pl.program_id(0))
        hi, mid, lo = _split3(rb_ref[...])
        o_ref[...] = _dot(hi, e) + _dot(mid, e) + _dot(lo, e)

    return pl.pallas_call(
        body, grid=(CHUNK,), in_specs=[pl.BlockSpec((ATT_HEADS, 3 * LANES), lambda i: (0, 0))],
        out_specs=pl.BlockSpec((None, ATT_HEADS, 5 * LANES), lambda i: (i, 0, 0)),
        out_shape=S((CHUNK, ATT_HEADS, 5 * LANES), F32), name="bias_build",
        compiler_params=_cparams("parallel"))(rbp)


ATT_RB = 512
ATT_QT = 256
ATT_CPT = ATT_QT // CHUNK
ATT_KT = ATT_QT + PADK
ATT_QCOL = (2 * RET_HEADS * RET_QK + 2 * RET_HEADS * RET_V) // LANES
ATT_KCOL = ATT_QCOL + ATT_HEADS * ATT_D // LANES
ATT_VCOL = ATT_KCOL + ATT_HEADS * ATT_D // LANES


def _bias_grad(dbt):
    def body(d_ref, o_ref):
        @pl.when(pl.program_id(0) == 0)
        def _():
            o_ref[...] = jnp.zeros_like(o_ref)

        e = _rel_index(pl.program_id(0))
        d = d_ref[0]
        for ci in range(1, ATT_CPT):
            d = d + d_ref[ci]
        hi, mid, lo = _split3(d)
        o_ref[...] += _dot_nt(hi, e) + _dot_nt(mid, e) + _dot_nt(lo, e)

    return pl.pallas_call(
        body, grid=(CHUNK,),
        in_specs=[pl.BlockSpec((ATT_CPT, None, ATT_HEADS, 5 * LANES), lambda i: (0, i, 0, 0))],
        out_specs=pl.BlockSpec((ATT_HEADS, 3 * LANES), lambda i: (0, 0)),
        out_shape=S((ATT_HEADS, 3 * LANES), F32), name="bias_grad",
        compiler_params=_cparams("arbitrary"))(dbt)


def _bias_tiles(bias):
    parts = [jnp.pad(bias, ((0, 0), (0, 0), (CHUNK * ci, ATT_KT - BAND - CHUNK * ci)), constant_values=NEG_INF)
             for ci in range(ATT_CPT)]
    return jnp.stack(parts, axis=1).reshape(ATT_HEADS, ATT_QT, ATT_KT)


def _bias_bands(dbias_tiles):
    d = dbias_tiles.reshape(ATT_HEADS, ATT_CPT, CHUNK, ATT_KT)
    bands = jnp.stack([d[:, ci, :, CHUNK * ci:CHUNK * ci + BAND] for ci in range(ATT_CPT)])
    return jnp.pad(jnp.transpose(bands, (0, 2, 1, 3)), ((0, 0), (0, 0), (0, 0), (0, 5 * LANES - BAND)))


def _att_fill(kw, vw, klo, khi, vlo, vhi):
    kw[0:ATT_RB, :] = klo[...].astype(BF16)
    kw[ATT_RB:, :] = khi[...].astype(BF16)
    vw[0:ATT_RB, :] = vlo[...].astype(BF16)
    vw[ATT_RB:, :] = vhi[...].astype(BF16)


def _att_probs(qm, kwin, bias, first_key):
    s = _dot_nt(qm, kwin) * ATT_SCALE + bias
    col = lax.broadcasted_iota(jnp.int32, (ATT_QT, ATT_KT), 1)
    s = jnp.where(col + first_key >= 0, s, NEG_INF)
    p = jnp.exp(s - jnp.max(s, axis=-1, keepdims=True))
    return p / jnp.sum(p, axis=-1, keepdims=True)


def _att_in_specs(nq):
    qn = lambda n: jnp.minimum(n, nq - 1)
    blk = lambda col, back: pl.BlockSpec((ATT_RB, LANES), lambda hp, n: (jnp.maximum(qn(n) - back, 0), col + hp))
    return [blk(ATT_QCOL, 0), blk(ATT_KCOL, 1), blk(ATT_KCOL, 0), blk(ATT_VCOL, 1), blk(ATT_VCOL, 0),
            pl.BlockSpec((2, ATT_QT, ATT_KT), lambda hp, n: (hp, 0, 0))]


def _attention_fwd(z, bias_t, plan=None):
    T = z.shape[0]
    nq = T // ATT_RB

    def body(q_ref, klo, khi, vlo, vhi, b_ref, o_ref, kw, vw):
        _att_fill(kw, vw, klo, khi, vlo, vhi)
        lane = lax.broadcasted_iota(jnp.int32, (ATT_QT, LANES), 1)
        n = pl.program_id(1)
        for t in range(ATT_RB // ATT_QT):
            rows = slice(t * ATT_QT, (t + 1) * ATT_QT)
            win = slice(t * ATT_QT, t * ATT_QT + ATT_KT)
            qc = q_ref[rows, :]
            outs = []
            for e in range(2):
                qm = jnp.where((lane >= ATT_D) == (e == 1), qc, 0.0).astype(BF16)
                p = _att_probs(qm, kw[win, :], b_ref[e], (n - 1) * ATT_RB + t * ATT_QT)
                outs.append(_dot(p.astype(BF16), vw[win, :]))
            o_ref[rows, :] = jnp.where(lane < ATT_D, outs[0], outs[1]).astype(BF16)

    (yb,), extra = _call(
        body, grid=(ATT_HEADS // 2, nq), in_specs=_att_in_specs(nq),
        out_specs=[pl.BlockSpec((ATT_RB, LANES), lambda hp, n: (n, hp))],
        out_shape=[S((T, ATT_HEADS * ATT_D), BF16)],
        scratch_shapes=[pltpu.VMEM((2 * ATT_RB, LANES), BF16), pltpu.VMEM((2 * ATT_RB, LANES), BF16)],
        name="attention_fwd", sem=("parallel", "parallel"), args=(z, z, z, z, z, bias_t), plan=plan)
    return yb, extra


def _attention_bwd(z, bias_t, dycat, plan=None):
    T = z.shape[0]
    nq = T // ATT_RB
    dycol = RET_HEADS * RET_V // LANES

    def body(q_ref, klo, khi, vlo, vhi, b_ref, dy_ref, dq_ref, dk_ref, dv_ref, db_ref, kw, vw, dkw, dvw):
        n = pl.program_id(1)

        @pl.when(n == 0)
        def _():
            dkw[...] = jnp.zeros_like(dkw)
            dvw[...] = jnp.zeros_like(dvw)
            db_ref[...] = jnp.zeros_like(db_ref)

        @pl.when(n > 0)
        def _():
            dkw[0:ATT_RB, :] = dkw[ATT_RB:, :]
            dvw[0:ATT_RB, :] = dvw[ATT_RB:, :]
            dkw[ATT_RB:, :] = jnp.zeros((ATT_RB, LANES), F32)
            dvw[ATT_RB:, :] = jnp.zeros((ATT_RB, LANES), F32)

        @pl.when(n < nq)
        def _():
            _att_fill(kw, vw, klo, khi, vlo, vhi)
            lane = lax.broadcasted_iota(jnp.int32, (ATT_QT, LANES), 1)
            for t in range(ATT_RB // ATT_QT):
                rows = slice(t * ATT_QT, (t + 1) * ATT_QT)
                win = slice(t * ATT_QT, t * ATT_QT + ATT_KT)
                qc, dyc = q_ref[rows, :], dy_ref[rows, :]
                kwin, vwin = kw[win, :], vw[win, :]
                dq = jnp.zeros((ATT_QT, LANES), F32)
                for e in range(2):
                    mine = (lane >= ATT_D) == (e == 1)
                    qm = jnp.where(mine, qc, 0.0).astype(BF16)
                    dom = jnp.where(mine, dyc, 0.0).astype(BF16)
                    p = _att_probs(qm, kwin, b_ref[e], (n - 1) * ATT_RB + t * ATT_QT)
                    dp = _dot_nt(dom, vwin)
                    ds = p * (dp - jnp.sum(dp * p, axis=-1, keepdims=True))
                    db_ref[e] += ds
                    dsb = (ds * ATT_SCALE).astype(BF16)
                    dq = dq + jnp.where(mine, _dot(dsb, kwin), 0.0)
                    dkw[win, :] += _dot_tn(dsb, qm)
                    dvw[win, :] += _dot_tn(p.astype(BF16), dom)
                dq_ref[rows, :] = dq.astype(BF16)

        dk_ref[...] = dkw[0:ATT_RB, :].astype(BF16)
        dv_ref[...] = dvw[0:ATT_RB, :].astype(BF16)

    qn = lambda n: jnp.minimum(n, nq - 1)
    out_kv = pl.BlockSpec((ATT_RB, LANES), lambda hp, n: (jnp.maximum(n - 1, 0), hp))
    return _call(
        body, grid=(ATT_HEADS // 2, nq + 1),
        in_specs=_att_in_specs(nq) + [pl.BlockSpec((ATT_RB, LANES), lambda hp, n: (qn(n), dycol + hp))],
        out_specs=[pl.BlockSpec((ATT_RB, LANES), lambda hp, n: (qn(n), hp)), out_kv, out_kv,
                   pl.BlockSpec((2, ATT_QT, ATT_KT), lambda hp, n: (hp, 0, 0))],
        out_shape=[S((T, ATT_HEADS * ATT_D), BF16), S((T, ATT_HEADS * ATT_D), BF16),
                   S((T, ATT_HEADS * ATT_D), BF16), S((ATT_HEADS, ATT_QT, ATT_KT), F32)],
        scratch_shapes=[pltpu.VMEM((2 * ATT_RB, LANES), BF16), pltpu.VMEM((2 * ATT_RB, LANES), BF16),
                        pltpu.VMEM((2 * ATT_RB, LANES), F32), pltpu.VMEM((2 * ATT_RB, LANES), F32)],
        name="attention_bwd", sem=("parallel", "arbitrary"), args=(z, z, z, z, z, bias_t, dycat), plan=plan)


HALO = 8


def _causal_conv(ext, w_ref, b_ref):
    back2, back1 = pltpu.roll(ext, 2, 0), pltpu.roll(ext, 1, 0)
    zc = w_ref[0:1, :] * back2 + w_ref[1:2, :] * back1 + w_ref[2:3, :] * ext + b_ref[...]
    return zc[HALO:], back2, back1


def _ffn_down(z, cw, cb, wd, res, name, norm_g=None):
    T = z.shape[0]
    D = wd.shape[1]
    tb, tc = _tile(T, 1024, 8), 256
    nct = FFN_HIDDEN // tc
    per = tb // HALO

    def body(gp_ref, g_ref, up_ref, u_ref, wg_ref, wu_ref, bg_ref, bu_ref, wd_ref, res_ref, *rest):
        ng_ref = rest[0] if norm_g is not None else None
        f_ref, gc_ref, uc_ref, o_ref = rest[norm_g is not None:][:4]
        hn_ref = rest[5] if norm_g is not None else None
        acc = rest[-1]
        first, j = pl.program_id(0) == 0, pl.program_id(1)

        def conv(p_ref, blk_ref, w_ref, b_ref):
            prev = jnp.where(first, 0.0, p_ref[...])
            return _causal_conv(jnp.concatenate([prev, blk_ref[...]], axis=0), w_ref, b_ref)[0]

        gc, uc = conv(gp_ref, g_ref, wg_ref, bg_ref), conv(up_ref, u_ref, wu_ref, bu_ref)
        gc_ref[...] = gc.astype(BF16)
        uc_ref[...] = uc.astype(BF16)
        f = (_gelu(gc) * uc).astype(BF16)
        f_ref[...] = f
        p = _dot(f, wd_ref[...])

        @pl.when(j == 0)
        def _():
            acc[...] = p

        @pl.when(j > 0)
        def _():
            acc[...] += p

        @pl.when(j == nct - 1)
        def _():
            v = acc[...] + res_ref[...]
            o_ref[...] = v
            if norm_g is not None:
                r = lax.rsqrt(jnp.mean(v * v, axis=-1, keepdims=True) + EPS)
                hn_ref[...] = (v * r * ng_ref[...]).astype(BF16)

    def zspecs(off):
        return [pl.BlockSpec((HALO, tc), lambda i, j: (jnp.maximum(i * per - 1, 0), j + off)),
                pl.BlockSpec((tb, tc), lambda i, j: (i, j + off))]

    wspec = lambda off, rows: pl.BlockSpec((rows, tc), lambda i, j: (0, j + off))
    row = pl.BlockSpec((tb, D), lambda i, j: (i, 0))
    vec = pl.BlockSpec((1, D), lambda i, j: (0, 0))
    normed = norm_g is not None
    return pl.pallas_call(
        body, grid=(T // tb, nct),
        in_specs=zspecs(0) + zspecs(nct) + [wspec(0, 3), wspec(nct, 3), wspec(0, 1), wspec(nct, 1),
                                            pl.BlockSpec((tc, D), lambda i, j: (j, 0)), row] + [vec] * normed,
        out_specs=[pl.BlockSpec((tb, tc), lambda i, j: (i, j))] * 3 + [row] + [row] * normed,
        out_shape=[S((T, FFN_HIDDEN), BF16)] * 3 + [S((T, D), F32)] + [S((T, D), BF16)] * normed,
        scratch_shapes=[pltpu.VMEM((tb, D), F32)], name=name,
        compiler_params=_cparams("parallel", "arbitrary"))(z, z, z, z, cw, cw, cb, cb, wd, res, *([norm_g] * normed))


HALO16 = 16


def _convglu_bwd(z, gc, uc, df, cw, name):
    T = z.shape[0]
    tb, tc = _tile(T, 1024, 8), 256
    nct = FFN_HIDDEN // tc
    nrb = T // tb

    def body(g_ref, u_ref, gc_ref, gcn_ref, uc_ref, ucn_ref, df_ref, dfn_ref, wg_ref, wu_ref,
             dzg_ref, dzu_ref, dwg_ref, dwu_ref, dbg_ref, dbu_ref):
        i = pl.program_id(1)
        first, last = i == 0, i == nrb - 1

        @pl.when(first)
        def _():
            for ref in (dwg_ref, dwu_ref, dbg_ref, dbu_ref):
                ref[...] = jnp.zeros_like(ref)

        ext = lambda blk_ref, n_ref: jnp.concatenate([blk_ref[...].astype(F32), n_ref[...].astype(F32)[0:HALO]], axis=0)
        gcv, ucv = ext(gc_ref, gcn_ref), ext(uc_ref, ucn_ref)
        dfe = jnp.concatenate([df_ref[...], jnp.where(last, 0.0, dfn_ref[...])], axis=0)
        ge, gd = _gelu_and_grad(gcv)
        dgc, duc = dfe * ucv * gd, dfe * ge
        n = tb + HALO

        def back(d, z_ref, w_ref, dz_ref, dw_ref, db_ref):
            ahead1, ahead2 = pltpu.roll(d, n - 1, 0), pltpu.roll(d, n - 2, 0)
            dz_ref[...] = (w_ref[2:3, :] * d + w_ref[1:2, :] * ahead1 + w_ref[0:1, :] * ahead2)[:tb].astype(BF16)
            db_ref[...] += jnp.sum(d[:tb], axis=0, keepdims=True)
            zv = z_ref[...]
            for k, dk in enumerate((ahead2, ahead1, d)):
                dw_ref[k:k + 1, :] += jnp.sum(dk[:tb] * zv, axis=0, keepdims=True)

        back(dgc, g_ref, wg_ref, dzg_ref, dwg_ref, dbg_ref)
        back(duc, u_ref, wu_ref, dzu_ref, dwu_ref, dbu_ref)

    blk = pl.BlockSpec((tb, tc), lambda j, i: (i, j))
    after = lambda rows: pl.BlockSpec((rows, tc), lambda j, i: (jnp.minimum((i + 1) * (tb // rows), T // rows - 1), j))
    zspec = lambda off: pl.BlockSpec((tb, tc), lambda j, i: (i, j + off))
    wspec = lambda off: pl.BlockSpec((3, tc), lambda j, i: (0, j + off))
    acc = lambda rows: pl.BlockSpec((rows, tc), lambda j, i: (0, j))
    return pl.pallas_call(
        body, grid=(nct, nrb),
        in_specs=[zspec(0), zspec(nct), blk, after(HALO16), blk, after(HALO16), blk, after(HALO), wspec(0), wspec(nct)],
        out_specs=[blk, blk, acc(3), acc(3), acc(1), acc(1)],
        out_shape=[S((T, FFN_HIDDEN), BF16), S((T, FFN_HIDDEN), BF16), S((3, FFN_HIDDEN), F32), S((3, FFN_HIDDEN), F32),
                   S((1, FFN_HIDDEN), F32), S((1, FFN_HIDDEN), F32)],
        name=name, compiler_params=_cparams("parallel", "arbitrary"))(z, z, gc, gc, uc, uc, df, df, cw, cw)


SGU_RB = 256


def _sgu_weights(ws_ref):
    i = lax.broadcasted_iota(jnp.int32, (SGU_BLOCK, SGU_BLOCK), 0)
    j = lax.broadcasted_iota(jnp.int32, (SGU_BLOCK, SGU_BLOCK), 1)
    mask = (j < CHUNK) | (i >= CHUNK)
    return mask, [jnp.where(mask, ws_ref[g], 0.0).astype(BF16) for g in range(SGU_GROUPS)]


def _sgu_norm(zv, lng, lnb):
    mu = jnp.mean(zv, axis=-1, keepdims=True)
    vc = zv - mu
    rstd = lax.rsqrt(jnp.mean(vc * vc, axis=-1, keepdims=True) + EPS)
    vh = vc * rstd
    return vh, rstd, vh * lng + lnb


def _sgu_fwd(zpre, lng, lnb, ws, bst):
    T = zpre.shape[0]
    nb = SGU_RB // SGU_BLOCK

    def body(z_ref, lng_ref, lnb_ref, ws_ref, bst_ref, o_ref):
        _, wm = _sgu_weights(ws_ref)
        u = _gelu(z_ref[:, :SGU_WIDTH])
        _, _, vn = _sgu_norm(_gelu(z_ref[:, SGU_WIDTH:]), lng_ref[...], lnb_ref[...])
        vnb = vn.astype(BF16)
        for b in range(nb):
            rows = slice(b * SGU_BLOCK, (b + 1) * SGU_BLOCK)
            for g in range(SGU_GROUPS):
                cols = slice(g * SGU_GW, (g + 1) * SGU_GW)
                mixed = _dot(wm[g], vnb[rows, cols]) + bst_ref[:, g:g + 1]
                o_ref[rows, cols] = (u[rows, cols] * mixed).astype(BF16)

    vec = pl.BlockSpec((1, SGU_WIDTH), lambda i: (0, 0))
    return pl.pallas_call(
        body, grid=(T // SGU_RB,),
        in_specs=[pl.BlockSpec((SGU_RB, 2 * SGU_WIDTH), lambda i: (i, 0)), vec, vec,
                  pl.BlockSpec((SGU_GROUPS, SGU_BLOCK, SGU_BLOCK), lambda i: (0, 0, 0)),
                  pl.BlockSpec((SGU_BLOCK, SGU_GROUPS), lambda i: (0, 0))],
        out_specs=pl.BlockSpec((SGU_RB, SGU_WIDTH), lambda i: (i, 0)),
        out_shape=S((T, SGU_WIDTH), BF16), name="sgu_fwd", compiler_params=_cparams("parallel"))(zpre, lng, lnb, ws, bst)


def _sgu_bwd(zpre, dy, lng, lnb, ws, bst):
    T = zpre.shape[0]
    nb = SGU_RB // SGU_BLOCK

    def body(z_ref, dy_ref, lng_ref, lnb_ref, ws_ref, bst_ref, dz_ref, dws_ref, dbst_ref, dlng_ref, dlnb_ref, dvn):
        @pl.when(pl.program_id(0) == 0)
        def _():
            for ref in (dws_ref, dbst_ref, dlng_ref, dlnb_ref):
                ref[...] = jnp.zeros_like(ref)

        mask, wm = _sgu_weights(ws_ref)
        u, ud = _gelu_and_grad(z_ref[:, :SGU_WIDTH])
        v, vd = _gelu_and_grad(z_ref[:, SGU_WIDTH:])
        vh, rstd, vn = _sgu_norm(v, lng_ref[...], lnb_ref[...])
        vnb = vn.astype(BF16)
        lane8 = lax.broadcasted_iota(jnp.int32, (SGU_BLOCK, SGU_GROUPS), 1)
        dbs = jnp.zeros((SGU_BLOCK, SGU_GROUPS), F32)
        for b in range(nb):
            rows = slice(b * SGU_BLOCK, (b + 1) * SGU_BLOCK)
            for g in range(SGU_GROUPS):
                cols = slice(g * SGU_GW, (g + 1) * SGU_GW)
                vg = vnb[rows, cols]
                mixed = _dot(wm[g], vg) + bst_ref[:, g:g + 1]
                dyv = dy_ref[rows, cols]
                dz_ref[rows, cols] = (dyv * mixed * ud[rows, cols]).astype(BF16)
                dmix = dyv * u[rows, cols]
                dmb = dmix.astype(BF16)
                dvn[rows, cols] = _dot_tn(wm[g], dmb)
                dws_ref[g] += jnp.where(mask, _dot_nt(dmb, vg), 0.0)
                dbs = dbs + jnp.where(lane8 == g, jnp.sum(dmix, axis=-1, keepdims=True), 0.0)
        dbst_ref[...] += dbs
        dvnv = dvn[...]
        dlng_ref[...] += jnp.sum(dvnv * vh, axis=0, keepdims=True)
        dlnb_ref[...] += jnp.sum(dvnv, axis=0, keepdims=True)
        dvh = dvnv * lng_ref[...]
        dv = rstd * (dvh - jnp.mean(dvh, axis=-1, keepdims=True) - vh * jnp.mean(dvh * vh, axis=-1, keepdims=True))
        dz_ref[:, SGU_WIDTH:] = (dv * vd).astype(BF16)

    vec = pl.BlockSpec((1, SGU_WIDTH), lambda i: (0, 0))
    wsp = pl.BlockSpec((SGU_GROUPS, SGU_BLOCK, SGU_BLOCK), lambda i: (0, 0, 0))
    bsp = pl.BlockSpec((SGU_BLOCK, SGU_GROUPS), lambda i: (0, 0))
    return pl.pallas_call(
        body, grid=(T // SGU_RB,),
        in_specs=[pl.BlockSpec((SGU_RB, 2 * SGU_WIDTH), lambda i: (i, 0)),
                  pl.BlockSpec((SGU_RB, SGU_WIDTH), lambda i: (i, 0)), vec, vec, wsp, bsp],
        out_specs=[pl.BlockSpec((SGU_RB, 2 * SGU_WIDTH), lambda i: (i, 0)), wsp, bsp, vec, vec],
        out_shape=[S((T, 2 * SGU_WIDTH), BF16), S((SGU_GROUPS, SGU_BLOCK, SGU_BLOCK), F32),
                   S((SGU_BLOCK, SGU_GROUPS), F32), S((1, SGU_WIDTH), F32), S((1, SGU_WIDTH), F32)],
        scratch_shapes=[pltpu.VMEM((SGU_RB, SGU_WIDTH), F32)], name="sgu_bwd",
        compiler_params=_cparams("arbitrary"))(zpre, dy, lng, lnb, ws, bst)


def _loss_head(h, tgt, g):
    T, D = h.shape
    tr = _tile(T, 512, 8)

    def body(h_ref, t_ref, g_ref, ls_ref, dh_ref, dhb_ref, dg_ref):
        @pl.when(pl.program_id(0) == 0)
        def _():
            ls_ref[...] = jnp.zeros_like(ls_ref)
            dg_ref[...] = jnp.zeros_like(dg_ref)

        hv = h_ref[...]
        r = lax.rsqrt(jnp.mean(hv * hv, axis=-1, keepdims=True) + EPS)
        xh = hv * r
        diff = xh * g_ref[...] - t_ref[...]
        per_row = jnp.mean(diff * diff, axis=-1, keepdims=True)
        ls_ref[...] += jnp.sum(per_row, axis=0, keepdims=True)
        dy = diff * (1.0 / D)
        dg_ref[...] += jnp.sum(dy * xh, axis=0, keepdims=True)
        dxh = dy * g_ref[...]
        dh = r * (dxh - xh * jnp.mean(dxh * xh, axis=-1, keepdims=True))
        dh_ref[...] = dh
        dhb_ref[...] = dh.astype(BF16)

    row = pl.BlockSpec((tr, D), lambda i: (i, 0))
    vec = pl.BlockSpec((1, D), lambda i: (0, 0))
    return pl.pallas_call(
        body, grid=(T // tr,), in_specs=[row, row, vec],
        out_specs=[pl.BlockSpec((1, LANES), lambda i: (0, 0)), row, row, vec],
        out_shape=[S((1, LANES), F32), S((T, D), F32), S((T, D), BF16), S((1, D), F32)],
        name="loss_head", compiler_params=_cparams("arbitrary"))(h, tgt, g)


ANY = pl.BlockSpec(memory_space=pl.ANY)
COPY_PARTS = 4
SWAP_PARTS = 8
DMA = pltpu.SemaphoreType.DMA


def _place():
    return lax.axis_index("x"), lax.axis_index("y"), lax.axis_index("c")


def _nparts(rows, unit, want):
    n = want
    while n > 1 and rows % (unit * n):
        n //= 2
    return n


def _row_unit(dtype):
    return 16 if jnp.dtype(dtype).itemsize == 2 else 8


def _remote(src, dst, send_sems, recv_sems, k, to):
    return pltpu.make_async_remote_copy(src_ref=src, dst_ref=dst, send_sem=send_sems.at[k], recv_sem=recv_sems.at[k],
                                        device_id=to, device_id_type=MESH)


def _sem_ranges(counts):
    first, total = [], 0
    for c in counts:
        first.append(total)
        total += c
    return first, total


class _Gather:
    def __init__(self, shards):
        self.srcs = list(shards)
        self.halves = [a.shape[0] // 2 for a in shards]
        self.units = [_row_unit(a.dtype) for a in shards]
        self.parts = [_nparts(h, u, COPY_PARTS) for h, u in zip(self.halves, self.units)]
        self.first, total = _sem_ranges([3 * n for n in self.parts])
        self.out_shapes = [S((N_CHIPS,) + a.shape, a.dtype) for a in shards]
        self.scratch = [DMA((total,))] * 4
        self.has_relay = True

    def _ops(self, srcs, outs, sems):
        ici_s, ici_r, rel_s, rel_r = sems
        x, y, c = _place()
        me, sibling = (x, y, c), (x, y, 1 - c)
        chips = [(1 - x, y), (x, 1 - y), (1 - x, 1 - y)]
        send, arrive, relay, relayed = [], [], [], []
        for p_ref, out_ref, Rh, unit, n, base in zip(srcs, outs, self.halves, self.units, self.parts, self.first):
            rp = Rh // n

            def part(px, py, pc, k, out_ref=out_ref, Rh=Rh, unit=unit, rp=rp):
                return out_ref.at[2 * px + py, pl.ds(pl.multiple_of(pc * Rh + k * rp, unit), rp), :]

            def mine(k, p_ref=p_ref, Rh=Rh, unit=unit, rp=rp):
                return p_ref.at[pl.ds(pl.multiple_of(c * Rh + k * rp, unit), rp), :]

            for j, chip in enumerate(chips):
                for k in range(n):
                    s = base + j * n + k
                    send.append(_remote(mine(k), part(x, y, c, k), ici_s, ici_r, s, (*chip, c)))
                    arrive.append(_remote(mine(k), part(*chip, c, k), ici_s, ici_r, s, me))
                    relay.append(_remote(part(*chip, c, k), part(*chip, c, k), rel_s, rel_r, s, sibling))
                    relayed.append(_remote(part(*chip, c, k), part(*chip, 1 - c, k), rel_s, rel_r, s, me))
        return send, arrive, relay, relayed

    def start(self, *refs):
        for cp in self._ops(*refs)[0]:
            cp.start()

    def relay(self, *refs):
        _, arrive, relay, _ = self._ops(*refs)
        for a, r in zip(arrive, relay):
            a.wait_recv()
            r.start()

    def finish(self, *refs):
        send, _, relay, relayed = self._ops(*refs)
        for cp in relayed:
            cp.wait_recv()
        for cp in send + relay:
            cp.wait_send()


class _Exchange:
    def __init__(self, items):
        self.srcs = [a for a, _ in items]
        self.kinds = [k for _, k in items]
        self.blocks = []
        for a, kind in items:
            R, Ccols = a.shape
            self.blocks.append({"cols": (R // 2, Ccols // N_CHIPS), "rows": (R // (2 * N_CHIPS), Ccols), "all": (R, Ccols)}[kind])
        self.units = [_row_unit(a.dtype) for a in self.srcs]
        self.parts = [_nparts(b[0], u, COPY_PARTS) for b, u in zip(self.blocks, self.units)]
        self.first, total = _sem_ranges([N_DEV - 2 + n for n in self.parts])
        self.out_shapes = [S((N_DEV,) + b, a.dtype) for a, b in zip(self.srcs, self.blocks)]
        self.scratch = [DMA((total,))] * 2
        self.has_relay = False

    def _ops(self, srcs, outs, sems):
        ss, rs = sems
        x, y, c = _place()
        send, arrive = [], []
        for src_ref, out_ref, kind, (Rb, Cb), unit, n, base in zip(srcs, outs, self.kinds, self.blocks, self.units, self.parts,
                                                                  self.first):
            rp = Rb // n

            def block_for(px, py, pc, r0, rows, src_ref=src_ref, kind=kind, Rb=Rb, Cb=Cb, unit=unit):
                if kind == "cols":
                    return src_ref.at[pl.ds(pl.multiple_of(pc * Rb + r0, unit), rows),
                                      pl.ds(pl.multiple_of((2 * px + py) * Cb, LANES), Cb)]
                if kind == "rows":
                    return src_ref.at[pl.ds(pl.multiple_of((2 * (2 * px + py) + pc) * Rb + r0, unit), rows), :]
                return src_ref.at[pl.ds(r0, rows), :]

            def slot(d, r0, rows, out_ref=out_ref):
                return out_ref.at[d, pl.ds(r0, rows), :]

            me = 4 * x + 2 * y + c
            for k in range(1, N_DEV):
                peer = (x ^ ((k >> 2) & 1), y ^ ((k >> 1) & 1), c ^ (k & 1))
                pieces = [(N_DEV - 2 + q, q * rp, rp) for q in range(n)] if k == 1 else [(k - 2, 0, Rb)]
                for sem, r0, rows in pieces:
                    send.append(_remote(block_for(*peer, r0, rows), slot(me, r0, rows), ss, rs, base + sem, peer))
                    arrive.append(_remote(block_for(*peer, r0, rows), slot(4 * peer[0] + 2 * peer[1] + peer[2], r0, rows),
                                          ss, rs, base + sem, peer))
        return send, arrive

    def start(self, *refs):
        for cp in self._ops(*refs)[0]:
            cp.start()

    def finish(self, *refs):
        send, arrive = self._ops(*refs)
        for cp in arrive:
            cp.wait_recv()
        for cp in send:
            cp.wait_send()


class _Swap:
    def __init__(self, halves):
        self.srcs = list(halves)
        self.parts = [_nparts(a.shape[0], _row_unit(a.dtype), SWAP_PARTS) for a in halves]
        self.first, total = _sem_ranges(self.parts)
        self.out_shapes = [S(a.shape, a.dtype) for a in halves]
        self.scratch = [DMA((total,))] * 2
        self.has_relay = False

    def _ops(self, srcs, outs, sems):
        ss, rs = sems
        x, y, c = _place()
        copies = []
        for h_ref, out_ref, n, base in zip(srcs, outs, self.parts, self.first):
            rp = h_ref.shape[0] // n
            for k in range(n):
                rows = pl.ds(k * rp, rp)
                copies.append(_remote(h_ref.at[rows, :], out_ref.at[rows, :], ss, rs, base + k, (x, y, 1 - c)))
        return copies, copies

    start = _Exchange.start
    finish = _Exchange.finish


def _run_plan(plan, name):
    ni, no = len(plan.srcs), len(plan.out_shapes)

    def body(*refs):
        parts = (refs[:ni], refs[ni:ni + no], refs[ni + no:])
        plan.start(*parts)
        if plan.has_relay:
            plan.relay(*parts)
        plan.finish(*parts)

    return pl.pallas_call(body, out_shape=plan.out_shapes, in_specs=[ANY] * ni, out_specs=[ANY] * no,
                          scratch_shapes=plan.scratch, name=name)(*plan.srcs)


def _call(body, *, grid, in_specs, out_specs, out_shape, name, sem, args, scratch_shapes=(), plan=None):
    if plan is None:
        return pl.pallas_call(body, grid=grid, in_specs=in_specs, out_specs=out_specs, out_shape=out_shape,
                              scratch_shapes=list(scratch_shapes), name=name, compiler_params=_cparams(*sem))(*args), None
    n_in, n_out, n_scr = len(in_specs), len(out_shape), len(scratch_shapes)
    pi, po = len(plan.srcs), len(plan.out_shapes)
    total = math.prod(grid)

    def wrapped(*refs):
        a, refs = refs[:n_in], refs[n_in:]
        pa, refs = refs[:pi], refs[pi:]
        o, refs = refs[:n_out], refs[n_out:]
        pout, refs = refs[:po], refs[po:]
        scr, psem = refs[:n_scr], refs[n_scr:]
        step = 0
        for d, gsize in enumerate(grid):
            step = step * gsize + pl.program_id(d)

        @pl.when(step == 0)
        def _():
            plan.start(pa, pout, psem)

        body(*a, *o, *scr)
        if plan.has_relay:
            @pl.when(step == (3 * total) // 4)
            def _():
                plan.relay(pa, pout, psem)

        @pl.when(step == total - 1)
        def _():
            plan.finish(pa, pout, psem)

    outs = pl.pallas_call(
        wrapped, grid=grid, in_specs=list(in_specs) + [ANY] * pi, out_specs=list(out_specs) + [ANY] * po,
        out_shape=list(out_shape) + plan.out_shapes, scratch_shapes=list(scratch_shapes) + plan.scratch, name=name,
        compiler_params=_cparams(*["arbitrary"] * len(grid)))(*args, *plan.srcs)
    return outs[:n_out], outs[n_out:]


SMEM = pl.BlockSpec(memory_space=pltpu.SMEM)


def _sum_slots(buf, own, me, name):
    n, R, W = buf.shape
    tr = _tile(R, 256, 8)

    def body(me_ref, b_ref, own_ref, o_ref):
        acc = None
        for s in range(n):
            blk = jnp.where(me_ref[0] == s, own_ref[...], b_ref[s]).astype(F32)
            acc = blk if acc is None else acc + blk
        o_ref[...] = acc

    return pl.pallas_call(
        body, grid=(R // tr,),
        in_specs=[SMEM, pl.BlockSpec((n, tr, W), lambda i: (0, i, 0)), pl.BlockSpec((tr, W), lambda i: (i, 0))],
        out_specs=pl.BlockSpec((tr, W), lambda i: (i, 0)), out_shape=S((R, W), F32), name=name,
        compiler_params=_cparams("parallel"))(me, buf, own)


def _adamw_update(wv, gv, mv, vv):
    mn = ADAM_B1 * mv + (1.0 - ADAM_B1) * gv
    vn = ADAM_B2 * vv + (1.0 - ADAM_B2) * (gv * gv)
    m_hat = mn / (1.0 - ADAM_B1 ** ADAM_STEP)
    v_hat = vn / (1.0 - ADAM_B2 ** ADAM_STEP)
    return -ADAM_LR * (m_hat / (jnp.sqrt(v_hat) + ADAM_EPS) + ADAM_WD * wv), mn, vn


def _adamw(w, g, m, v, name):
    R, W = w.shape
    tr = _tile(R, 256, 8)

    def body(w_ref, g_ref, m_ref, v_ref, d_ref, mo_ref, vo_ref):
        d_ref[...], mo_ref[...], vo_ref[...] = _adamw_update(w_ref[...], g_ref[...], m_ref[...], v_ref[...])

    blk = pl.BlockSpec((tr, W), lambda i: (i, 0))
    return pl.pallas_call(
        body, grid=(R // tr,), in_specs=[blk] * 4, out_specs=[blk] * 3, out_shape=[S((R, W), F32)] * 3, name=name,
        compiler_params=_cparams("parallel"))(w, g, m, v)


def _adamw_shard(w, halves, core, m, v, name):
    L, R, C = w.shape
    Rh = R // 2
    tr = _tile(Rh, 256, 8)
    nbh = Rh // tr

    def body(c_ref, w_ref, m_ref, v_ref, *rest):
        pairs, (g_ref, d_ref, mo_ref, vo_ref) = rest[:2 * L], rest[2 * L:]
        l, i = pl.program_id(0), pl.program_id(1)
        mine_rows = i // nbh == c_ref[0]
        gv = None
        for lp in range(L):
            cand = jnp.where(mine_rows, pairs[2 * lp][...], pairs[2 * lp + 1][...])
            gv = cand if gv is None else jnp.where(l == lp, cand, gv)
        g_ref[...] = gv
        d_ref[...], mo_ref[...], vo_ref[...] = _adamw_update(w_ref[...], gv, m_ref[...], v_ref[...])

    blk = pl.BlockSpec((None, tr, C), lambda l, i: (l, i, 0))
    half = lambda lp: pl.BlockSpec((tr, C), lambda l, i: (jnp.where(l == lp, i % nbh, 0), 0))
    return pl.pallas_call(
        body, grid=(L, R // tr), in_specs=[SMEM, blk, blk, blk] + [half(lp) for lp in range(L) for _ in range(2)],
        out_specs=[blk] * 4, out_shape=[S((L, R, C), F32)] * 4, name=name,
        compiler_params=_cparams("parallel", "parallel"))(core, w, m, v, *[h for pair in halves for h in pair])


def _tables(T):
    f32 = F32
    half = RET_QK // 2
    inv = 1.0 / (10000.0 ** jnp.linspace(0.0, 1.0, half, dtype=f32))
    ang = jnp.arange(T).astype(f32)[:, None] * inv[None, :]
    cos, sin = jnp.cos(ang), jnp.sin(ang)
    c2 = jnp.concatenate([cos, cos], axis=-1)
    s2 = jnp.concatenate([-sin, sin], axis=-1)
    log_g = jnp.log1p(-jnp.exp2(-5.0 - jnp.arange(RET_HEADS, dtype=f32)))
    idx = jnp.arange(CHUNK, dtype=f32)
    dintra = jnp.exp(log_g[:, None, None] * jnp.abs(idx[:, None] - idx[None, :]))
    kdec = jnp.exp(log_g[None, :] * (CHUNK - 1 - idx)[:, None]).T
    qdec = jnp.exp(log_g[None, :] * (idx + 1.0)[:, None]).T
    cdec = jnp.exp(log_g * CHUNK)
    bc = lambda a, w: jnp.broadcast_to(a[:, :, None], (RET_HEADS, a.shape[1], w))
    return c2, s2, dintra, bc(qdec, RET_QK), bc(kdec, RET_QK), jnp.broadcast_to(cdec[:, None, None], (RET_HEADS, 1, RET_V))


def _first_forms(g4):
    return {"ab_w_in4": g4["ab_w_in"],
            "ab_w_inT": jnp.transpose(g4["ab_w_in"], (0, 2, 1)).reshape(-1, D_MODEL),
            "ab_w_out": g4["ab_w_out"].reshape(-1, D_MODEL)}


def _late_forms(g4):
    wd = g4["ffn_w_down"]
    per = wd.shape[1] // 2
    return {"c_w_in4": g4["c_w_in"], "c_w_out": g4["c_w_out"].reshape(-1, D_MODEL), "ffn_w_up4": g4["ffn_w_up"],
            "ffn_w_down": [wd[:, l * per:(l + 1) * per].reshape(-1, D_MODEL) for l in range(2)]}


def _local_step(x, tgt, p, late=None, exchange=False):
    T = x.shape[0]
    tab = _tables(T)
    row = lambda a: a.reshape(1, -1)
    tr = lambda w: jnp.transpose(w)
    width = D_MODEL

    hn0 = _rmsnorm_fwd(x, row(p["attn_norm_g"][0]), "norm_a0")
    z0 = _mm(hn0, p["ab_w_in4"], "mm_ab_in", form="kn4")
    ya, r, st = _retention_fwd(z0, *tab)
    bias_t = _bias_tiles(jnp.transpose(_bias_build(p["ab_rel_bias"][0]), (1, 0, 2))[:, :, :BAND])
    yb, late_out = _attention_fwd(z0, bias_t, plan=late[0] if late else None)
    if late:
        p = {**p, **late[1](late_out)}
    h1, hf0 = _mm([ya, yb], p["ab_w_out"], "mm_ab_out", res=x, norm_g=row(p["ffn_norm_g"][0]))

    def ffn_fwd(h, hf, l, next_g):
        zf = _mm(hf, p["ffn_w_up4"], f"mm_up{l}", form="kn4", row0=l * width)
        f, gc, uc, *out = _ffn_down(zf, p["ffn_conv_w"][l], row(p["ffn_conv_b"][l]), p["ffn_w_down"][l], h, f"ffn_down{l}",
                                    norm_g=next_g)
        return (zf, f, gc, uc), (tuple(out) if next_g is not None else out[0])

    kept0, (h2, hn1) = ffn_fwd(h1, hf0, 0, row(p["attn_norm_g"][1]))
    zc = _mm(hn1, p["c_w_in4"], "mm_c_in", form="kn4")
    lng, lnb, bst, ws = row(p["c_ln_g"][0]), row(p["c_ln_b"][0]), tr(p["c_b_s"][0]), p["c_w_s"][0]
    y1 = _sgu_fwd(zc, lng, lnb, ws, bst)
    h3, hf1 = _mm(y1, p["c_w_out"], "mm_c_out", res=h2, norm_g=row(p["ffn_norm_g"][1]))
    kept1, h4 = ffn_fwd(h3, hf1, 1, None)
    lsum, dh4, dh4b, dgfin = _loss_head(h4, tgt, row(p["final_norm_g"]))

    g, big = {}, {}

    def ffn_bwd(dh, dhb, h_in, hf, kept, l):
        zf, f, gc, uc = kept
        big[f"ffn_w_down{l}"] = _mm_tn(f, dhb, f"mmt_down{l}", out_dtype=BF16)
        df = _mm(dhb, p["ffn_w_down"][l], f"mmb_down{l}", form="nk")
        dzg, dzu, dwg, dwu, dbg, dbu = _convglu_bwd(zf, gc, uc, df, p["ffn_conv_w"][l], f"convglu_bwd{l}")
        big[f"ffn_w_up{l}"] = _mm_tn(hf, [dzg, dzu], f"mmt_up{l}", out_dtype=BF16)
        dhf = _mm([dzg, dzu], p["ffn_w_up4"], f"mmb_up{l}", form="nk4", row0=l * width, rows=width, out_dtype=BF16)
        dh_in, dh_in_b, dgf = _rmsnorm_bwd(h_in, dhf, row(p["ffn_norm_g"][l]), dh, f"norm_f{l}_bwd")
        return dh_in, dh_in_b, dict(ffn_norm_g=dgf[0], ffn_conv_w=jnp.concatenate([dwg, dwu], axis=1),
                                    ffn_conv_b=jnp.concatenate([dbg, dbu], axis=1)[0])

    dh3, dh3b, gf1 = ffn_bwd(dh4, dh4b, h3, hf1, kept1, 1)
    big["c_w_out"] = _mm_tn(y1, dh3b, "mmt_c_out", out_dtype=BF16)
    dy1 = _mm(dh3b, p["c_w_out"], "mmb_c_out", form="nk")
    dzc, dws, dbst, dlng, dlnb = _sgu_bwd(zc, dy1, lng, lnb, ws, bst)
    g["c_w_s"], g["c_b_s"], g["c_ln_g"], g["c_ln_b"] = dws[None], tr(dbst)[None], dlng, dlnb
    big["c_w_in"] = _mm_tn(hn1, dzc, "mmt_c_in", out_dtype=BF16)
    dhn1 = _mm(dzc, p["c_w_in4"], "mmb_c_in", form="nk4", rows=width, out_dtype=BF16)
    dh2, dh2b, dga1 = _rmsnorm_bwd(h2, dhn1, row(p["attn_norm_g"][1]), dh3, "norm_a1_bwd")
    dh1, dh1b, gf0 = ffn_bwd(dh2, dh2b, h1, hf0, kept0, 0)
    for k in gf0:
        g[k] = jnp.stack([gf0[k], gf1[k]])
    big["ab_w_out"] = _mm_tn([ya, yb], dh1b, "mmt_ab_out", out_dtype=BF16)
    dycat = _mm(dh1b, p["ab_w_out"], "mmb_ab_out", form="nk")
    late_plan = _Exchange([(big[n], kind) for n, kind in LATE_ITEMS]) if exchange else None
    (dqb, dkb, dvb, dbias_t), late_slots = _attention_bwd(z0, bias_t, dycat, plan=late_plan)
    dqa, dka, dva, dga = _retention_bwd(z0, *tab, r, dycat, st)
    dz0 = [dqa, dka, dva, dga, dqb, dkb, dvb]
    big["ab_w_in"] = _mm_tn(hn0, dz0, "mmt_ab_in", out_dtype=BF16)
    slots = {}
    if exchange:
        dhn0, first_slots = _mm(dz0, p["ab_w_inT"], "mmb_ab_in", out_dtype=BF16,
                                plan=_Exchange([(big[n], kind) for n, kind in FIRST_ITEMS]))
        slots = dict(first=first_slots, late=late_slots)
    else:
        dhn0 = _mm(dz0, p["ab_w_inT"], "mmb_ab_in", out_dtype=BF16)
    gx, _, dga0 = _rmsnorm_bwd(x, dhn0, row(p["attn_norm_g"][0]), dh1, "norm_a0_bwd")
    g["ab_rel_bias"] = _bias_grad(_bias_bands(dbias_t))[None, :, :N_REL]
    g["attn_norm_g"] = jnp.stack([dga0[0], dga1[0]])
    g["final_norm_g"] = dgfin[0]
    return lsum[0, 0], gx, g, big, slots


FIRST_BIG = ["ab_w_in", "ab_w_out"]
LATE_BIG = ["c_w_in", "c_w_out", "ffn_w_up", "ffn_w_down"]
BIG = FIRST_BIG + LATE_BIG
FIRST_ITEMS = [("ab_w_in", "cols"), ("ab_w_out", "rows")]
LATE_ITEMS = [("c_w_in", "cols"), ("c_w_out", "rows"), ("ffn_w_up0", "cols"), ("ffn_w_up1", "cols"),
              ("ffn_w_down0", "rows"), ("ffn_w_down1", "rows")]
LAYERS_OF = {"ab_w_in": ["ab_w_in"], "ab_w_out": ["ab_w_out"], "c_w_in": ["c_w_in"], "c_w_out": ["c_w_out"],
             "ffn_w_up": ["ffn_w_up0", "ffn_w_up1"], "ffn_w_down": ["ffn_w_down0", "ffn_w_down1"]}
SMALL_SHARDED = [("c_ln_g", 1), ("c_ln_b", 1), ("ffn_conv_w", 2)]
REPLICATED = ["attn_norm_g", "ffn_norm_g", "ab_rel_bias", "c_w_s", "c_b_s", "ffn_conv_b", "final_norm_g"]


def _rows_of(n_elems):
    return -(-n_elems // PACK_W)


def _flat_rows(a):
    f = a.reshape(-1)
    rows = _rows_of(f.shape[0])
    return jnp.pad(f, (0, rows * PACK_W - f.shape[0])).reshape(rows, PACK_W)


def _pad_rows(a, mult):
    extra = (-a.shape[0]) % mult
    return jnp.pad(a, ((0, extra), (0, 0))) if extra else a


def _pack(arrs, mult):
    return _pad_rows(jnp.concatenate([_flat_rows(a) for a in arrs], axis=0), mult)


def _unpack(buf, shapes):
    out, r = [], 0
    for shp in shapes:
        n = math.prod(shp)
        rows = _rows_of(n)
        out.append(buf[r:r + rows].reshape(-1)[:n].reshape(shp))
        r += rows
    return out


def _from_shards(sh, axis):
    m = jnp.moveaxis(sh, 0, axis)
    shp = m.shape
    return m.reshape(shp[:axis] + (shp[axis] * shp[axis + 1],) + shp[axis + 2:])


def _as_bf16_pairs(a):
    return lax.bitcast_convert_type(a.astype(F32), BF16)


def _from_bf16_pairs(a):
    return lax.bitcast_convert_type(a, F32)


def kernel(x, attn_norm_g, ffn_norm_g, ab_w_in, ab_w_out, ab_rel_bias, c_w_in, c_ln_g, c_ln_b, c_w_s, c_b_s, c_w_out, ffn_w_up, ffn_conv_w, ffn_conv_b, ffn_w_down, final_norm_g, loss_target, m_attn_norm_g, m_ffn_norm_g, m_ab_w_in, m_ab_w_out, m_ab_rel_bias, m_c_w_in, m_c_ln_g, m_c_ln_b, m_c_w_s, m_c_b_s, m_c_w_out, m_ffn_w_up, m_ffn_conv_w, m_ffn_conv_b, m_ffn_w_down, m_final_norm_g, v_attn_norm_g, v_ffn_norm_g, v_ab_w_in, v_ab_w_out, v_ab_rel_bias, v_c_w_in, v_c_ln_g, v_c_ln_b, v_c_w_s, v_c_b_s, v_c_w_out, v_ffn_w_up, v_ffn_conv_w, v_ffn_conv_b, v_ffn_w_down, v_final_norm_g):
    w = dict(attn_norm_g=attn_norm_g, ffn_norm_g=ffn_norm_g, ab_w_in=ab_w_in, ab_w_out=ab_w_out, ab_rel_bias=ab_rel_bias,
             c_w_in=c_w_in, c_ln_g=c_ln_g, c_ln_b=c_ln_b, c_w_s=c_w_s, c_b_s=c_b_s, c_w_out=c_w_out, ffn_w_up=ffn_w_up,
             ffn_conv_w=ffn_conv_w, ffn_conv_b=ffn_conv_b, ffn_w_down=ffn_w_down, final_norm_g=final_norm_g)
    m = dict(attn_norm_g=m_attn_norm_g, ffn_norm_g=m_ffn_norm_g, ab_w_in=m_ab_w_in, ab_w_out=m_ab_w_out,
             ab_rel_bias=m_ab_rel_bias, c_w_in=m_c_w_in, c_ln_g=m_c_ln_g, c_ln_b=m_c_ln_b, c_w_s=m_c_w_s, c_b_s=m_c_b_s,
             c_w_out=m_c_w_out, ffn_w_up=m_ffn_w_up, ffn_conv_w=m_ffn_conv_w, ffn_conv_b=m_ffn_conv_b,
             ffn_w_down=m_ffn_w_down, final_norm_g=m_final_norm_g)
    v = dict(attn_norm_g=v_attn_norm_g, ffn_norm_g=v_ffn_norm_g, ab_w_in=v_ab_w_in, ab_w_out=v_ab_w_out,
             ab_rel_bias=v_ab_rel_bias, c_w_in=v_c_w_in, c_ln_g=v_c_ln_g, c_ln_b=v_c_ln_b, c_w_s=v_c_w_s, c_b_s=v_c_b_s,
             c_w_out=v_c_w_out, ffn_w_up=v_ffn_w_up, ffn_conv_w=v_ffn_conv_w, ffn_conv_b=v_ffn_conv_b,
             ffn_w_down=v_ffn_w_down, final_norm_g=v_final_norm_g)
    names = list(w)
    chip = 2 * lax.axis_index("x") + lax.axis_index("y")

    core = lax.axis_index("c")
    core_arr = core.reshape(1).astype(jnp.int32)
    me_arr = (2 * chip + core).reshape(1).astype(jnp.int32)
    two_d = lambda a: a.reshape(-1, a.shape[-1])
    with_own = lambda gathered, own: lax.dynamic_update_slice(gathered, own[None], (chip, 0, 0))

    send_first = [two_d(w[n]).astype(BF16) for n in FIRST_BIG]
    got_first = _run_plan(_Gather(send_first), "gather_first")
    full = {n: w[n] for n in REPLICATED}
    full.update(_first_forms({n: with_own(got, own) for n, got, own in zip(FIRST_BIG, got_first, send_first)}))
    small_send = [_as_bf16_pairs(w[n]) for n, _ in SMALL_SHARDED]
    send_late = [two_d(w[n]).astype(BF16) for n in LATE_BIG] + [_pack(small_send, 32)]

    def finish_late(got):
        whole = [with_own(a, own) for a, own in zip(got, send_late)]
        forms = _late_forms(dict(zip(LATE_BIG, whole)))
        parts = [_unpack(whole[-1][s], [a.shape for a in small_send]) for s in range(N_CHIPS)]
        for i, (n, axis) in enumerate(SMALL_SHARDED):
            forms[n] = _from_shards(_from_bf16_pairs(jnp.stack([parts[s][i] for s in range(N_CHIPS)])), axis)
        return forms

    lsum, grad_x, g, big, slots = _local_step(x[0], loss_target[0], full, late=(_Gather(send_late), finish_late), exchange=True)
    loss = lax.psum(0.5 * lsum, ("x", "y", "c"))

    def own_block(a, kind):
        rows, cols = a.shape
        if kind == "cols":
            return lax.dynamic_slice(a, (core * (rows // 2), chip * (cols // N_CHIPS)), (rows // 2, cols // N_CHIPS))
        per = rows // N_DEV
        return lax.dynamic_slice(a, ((2 * chip + core) * per, 0), (per, cols))

    reduced = {}
    for key, items in (("late", LATE_ITEMS), ("first", FIRST_ITEMS)):
        halves = [_sum_slots(got, own_block(big[n], kind), me_arr, f"sum_{n}") for (n, kind), got in zip(items, slots[key])]
        others = _run_plan(_Swap(halves), f"swap_{key}")
        reduced.update({n: (h, o) for (n, _), h, o in zip(items, halves, others)})
    big_outs = [{}, {}, {}, {}]
    for n in BIG:
        res = _adamw_shard(w[n], [reduced[layer] for layer in LAYERS_OF[n]], core_arr, m[n], v[n], f"adamw_{n}")
        for k in range(4):
            big_outs[k][n] = res[k]

    small_names = REPLICATED + [n for n, _ in SMALL_SHARDED]
    gsmall = _pack([g[n] for n in small_names], 32)
    gsum = _sum_slots(_run_plan(_Exchange([(gsmall, "all")]), "exchange_small")[0], gsmall, me_arr, "sum_small")
    gsmall_full = dict(zip(small_names, _unpack(gsum, [g[n].shape for n in small_names])))
    for n, axis in SMALL_SHARDED:
        size = w[n].shape[axis]
        gsmall_full[n] = lax.dynamic_slice_in_dim(gsmall_full[n], chip * size, size, axis)
    pack_small = lambda d: _pack([d[n] for n in small_names], 8)
    small_out = _adamw(pack_small(w), pack_small(gsmall_full), pack_small(m), pack_small(v), "adamw_small")
    small_shapes = [w[n].shape for n in small_names]

    outs = [{**big_outs[0], **gsmall_full}]
    for k in range(3):
        outs.append({**big_outs[k + 1], **dict(zip(small_names, _unpack(small_out[k], small_shapes)))})
    return (loss, grad_x[None], *[o[n] for o in outs for n in names])
```

```python
import functools
import math

import jax
import jax.numpy as jnp
from jax import lax
from jax.experimental import pallas as pl
from jax.experimental.pallas import tpu as pltpu

F32 = jnp.float32
BF16 = jnp.bfloat16
S = jax.ShapeDtypeStruct
MESH = pl.DeviceIdType.MESH

D_MODEL = 1024
CHUNK = 64
EPS = 1e-6
NEG_INF = -1e30
RET_HEADS, RET_QK, RET_V = 4, 128, 256
ATT_HEADS, ATT_D, ATT_PAST, MAX_REL = 8, 64, 8, 128
BAND = (ATT_PAST + 1) * CHUNK
PADK = ATT_PAST * CHUNK
SGU_BLOCK, SGU_GROUPS, SGU_WIDTH = 128, 8, 2048
SGU_GW = SGU_WIDTH // SGU_GROUPS
FFN_HIDDEN = 2816
N_REL = 2 * MAX_REL + 1
RET_SCALE = RET_QK ** -0.5
ATT_SCALE = ATT_D ** -0.5
ADAM_LR, ADAM_B1, ADAM_B2, ADAM_EPS, ADAM_WD, ADAM_STEP = 0.001, 0.9, 0.999, 1e-08, 0.01, 10

V7X_VMEM_BYTES = 64 * 1024 * 1024
VMEM_LIMIT = V7X_VMEM_BYTES * 7 // 8
MM_TILE_BUDGET = V7X_VMEM_BYTES * 11 // 16
LANES = 128
PACK_W = 1024
N_CHIPS = 4
N_DEV = 8

GELU_C = math.sqrt(2.0 / math.pi)
GELU_A = 0.044715


def _cparams(*sem):
    return pltpu.CompilerParams(dimension_semantics=tuple(sem) if sem else None, vmem_limit_bytes=VMEM_LIMIT)


def _tile(n, target, unit=LANES):
    best = None
    for t in range(unit, min(n, target) + 1, unit):
        if n % t == 0:
            best = t
    return best if best is not None else n


def _gelu(x):
    t = jnp.tanh(GELU_C * (x + GELU_A * x * x * x))
    return 0.5 * x * (1.0 + t)


def _gelu_and_grad(x):
    x2 = x * x
    t = jnp.tanh(GELU_C * (x + GELU_A * x2 * x))
    g = 0.5 * x * (1.0 + t)
    dg = 0.5 * (1.0 + t) + 0.5 * x * (1.0 - t * t) * (GELU_C * (1.0 + 3.0 * GELU_A * x2))
    return g, dg


def _sigmoid(x):
    return 1.0 / (1.0 + jnp.exp(-x))


def _dot(a, b):
    return jnp.dot(a, b, preferred_element_type=F32)


def _dot_nt(a, b):
    return lax.dot_general(a, b, (((1,), (1,)), ((), ())), preferred_element_type=F32)


def _dot_tn(a, b):
    return lax.dot_general(a, b, (((0,), (0,)), ((), ())), preferred_element_type=F32)


def _rmsnorm_fwd(x, g, name, plan=None):
    T, D = x.shape
    tr = _tile(T, 512, 8)

    def body(x_ref, g_ref, o_ref):
        xv = x_ref[...]
        r = lax.rsqrt(jnp.mean(xv * xv, axis=-1, keepdims=True) + EPS)
        o_ref[...] = (xv * r * g_ref[...]).astype(o_ref.dtype)

    (out,), extra = _call(
        body, grid=(T // tr,),
        in_specs=[pl.BlockSpec((tr, D), lambda i: (i, 0)), pl.BlockSpec((1, D), lambda i: (0, 0))],
        out_specs=[pl.BlockSpec((tr, D), lambda i: (i, 0))],
        out_shape=[S((T, D), BF16)], name=name, sem=("parallel",), args=(x, g), plan=plan)
    return out, extra


def _pieces(a):
    return list(a) if isinstance(a, (list, tuple)) else [a]


def _piece_layout(widths, tile):
    out, s = [], 0
    for w in widths:
        out.append((s, w // tile))
        s += w // tile
    return out


def _common_tile(widths, target):
    return _tile(functools.reduce(math.gcd, widths), target)


def _rmsnorm_bwd(x, dy, g, dres, name):
    T, D = x.shape
    tr = _tile(T, 512, 8)

    def body(x_ref, dy_ref, g_ref, dres_ref, dx_ref, dxb_ref, dg_ref):
        @pl.when(pl.program_id(0) == 0)
        def _():
            dg_ref[...] = jnp.zeros_like(dg_ref)

        xv = x_ref[...]
        r = lax.rsqrt(jnp.mean(xv * xv, axis=-1, keepdims=True) + EPS)
        xh = xv * r
        dyv = dy_ref[...].astype(F32)
        dg_ref[...] += jnp.sum(dyv * xh, axis=0, keepdims=True)
        dxh = dyv * g_ref[...]
        dx = dres_ref[...] + r * (dxh - xh * jnp.mean(dxh * xh, axis=-1, keepdims=True))
        dx_ref[...] = dx
        dxb_ref[...] = dx.astype(BF16)

    row = pl.BlockSpec((tr, D), lambda i: (i, 0))
    vec = pl.BlockSpec((1, D), lambda i: (0, 0))
    return pl.pallas_call(
        body, grid=(T // tr,), in_specs=[row, row, vec, row], out_specs=[row, row, vec],
        out_shape=[S((T, D), F32), S((T, D), BF16), S((1, D), F32)], name=name,
        compiler_params=_cparams("arbitrary"))(x, dy, g, dres)


def _mm(a, b, name, res=None, out_dtype=F32, plan=None, norm_g=None, form="kn", row0=0, rows=None):
    pieces = _pieces(a)
    M = pieces[0].shape[0]
    widths = [p.shape[1] for p in pieces]
    K = sum(widths)
    N = {"kn": lambda: b.shape[1], "kn4": lambda: N_CHIPS * b.shape[2], "nk": lambda: b.shape[0], "nk4": lambda: rows}[form]()
    tn = N if norm_g is not None else (b.shape[2] if form == "kn4" else _tile(N, 1408))
    tk = b.shape[2] if form == "nk4" else _common_tile(widths, 1536)
    def vmem_bytes(rows):
        out_bytes = jnp.dtype(out_dtype).itemsize + 2 * (norm_g is not None)
        blocks = len(pieces) * rows * tk * 2 + tk * tn * 2 + rows * tn * (4 * (res is not None) + out_bytes)
        return 2 * blocks + rows * tn * 4 * (K != tk)

    tm = _tile(M, 2048 if vmem_bytes(2048) <= MM_TILE_BUDGET else 1024, 8)
    assert all(w % tk == 0 for w in widths) and row0 % (tk if form == "kn4" else tn) == 0
    nk, npc = K // tk, len(pieces)
    layout = _piece_layout(widths, tk)
    tile = pl.BlockSpec((tm, tn), lambda i, j, k: (i, j))
    vec = pl.BlockSpec((1, tn), lambda i, j, k: (0, j))
    b_spec = {"kn": lambda: pl.BlockSpec((tk, tn), lambda i, j, k: (k, j)),
              "kn4": lambda: pl.BlockSpec((None, tk, tn), lambda i, j, k: (j, row0 // tk + k, 0)),
              "nk": lambda: pl.BlockSpec((tn, tk), lambda i, j, k: (j, k)),
              "nk4": lambda: pl.BlockSpec((None, tn, tk), lambda i, j, k: (k, row0 // tn + j, 0))}[form]()
    dot = _dot if form in ("kn", "kn4") else _dot_nt
    extra_in, extra_specs = [], []
    if res is not None:
        extra_in, extra_specs = [res], [tile]
    if norm_g is not None:
        extra_in, extra_specs = extra_in + [norm_g], extra_specs + [vec]
    n_extra = len(extra_in)
    if norm_g is not None:
        out_shape, out_specs = [S((M, N), out_dtype), S((M, N), BF16)], [tile, tile]
    else:
        out_shape, out_specs = [S((M, N), out_dtype)], [tile]

    def body(*refs):
        a_refs, b_ref = refs[:npc], refs[npc]
        ext = list(refs[npc + 1:npc + 1 + n_extra])
        outs = refs[npc + 1 + n_extra:npc + 1 + n_extra + len(out_shape)]

        def finish(v):
            if res is not None:
                v = v + ext[0][...]
            outs[0][...] = v.astype(outs[0].dtype)
            if norm_g is not None:
                r = lax.rsqrt(jnp.mean(v * v, axis=-1, keepdims=True) + EPS)
                outs[1][...] = (v * r * ext[-1][...]).astype(BF16)

        if nk == 1:
            finish(dot(a_refs[0][...], b_ref[...]))
            return
        acc = refs[-1]
        k = pl.program_id(2)
        for a_ref, (s, c) in zip(a_refs, layout):
            def add(a_ref=a_ref):
                acc[...] += dot(a_ref[...], b_ref[...])

            if s == 0:
                @pl.when(k == 0)
                def _(a_ref=a_ref):
                    acc[...] = dot(a_ref[...], b_ref[...])

                if c > 1:
                    pl.when((k > 0) & (k < c))(add)
            else:
                pl.when((k >= s) & (k < s + c))(add)

        @pl.when(k == nk - 1)
        def _():
            finish(acc[...])

    in_specs = [pl.BlockSpec((tm, tk), lambda i, j, k, s=s, c=c: (i, jnp.clip(k - s, 0, c - 1))) for s, c in layout]
    outs, extra = _call(
        body, grid=(M // tm, N // tn, nk), in_specs=in_specs + [b_spec] + extra_specs, out_specs=out_specs,
        out_shape=out_shape, scratch_shapes=[pltpu.VMEM((tm, tn), F32)] if nk > 1 else [],
        name=name, sem=("parallel", "parallel", "arbitrary"), args=pieces + [b] + extra_in, plan=plan)
    outs = outs[0] if len(outs) == 1 else tuple(outs)
    return outs if plan is None else (outs, extra)


def _mm_tn(a, g, name, out_dtype=F32):
    ap, gp = _pieces(a), _pieces(g)
    T = ap[0].shape[0]
    aw, gw = [p.shape[1] for p in ap], [p.shape[1] for p in gp]
    tm, tn = _common_tile(aw, 1408), _common_tile(gw, 1408)
    narrow = out_dtype != F32
    tall = 2 * 2 * 2048 * (len(ap) * tm + len(gp) * tn) + tm * tn * (2 * jnp.dtype(out_dtype).itemsize + 4 * narrow)
    tt = _tile(T, 2048 if tall <= MM_TILE_BUDGET else 1024, 8)
    alay, glay = _piece_layout(aw, tm), _piece_layout(gw, tn)
    na = len(ap)

    def inside(idx, s, c, single):
        return None if single else (idx >= s) & (idx < s + c)

    nt = T // tt

    def body(*refs):
        a_refs, g_refs = refs[:na], refs[na:na + len(gp)]
        o_ref = refs[na + len(gp)]
        acc = refs[-1] if narrow else o_ref
        i, j, k = pl.program_id(0), pl.program_id(1), pl.program_id(2)

        @pl.when(k == 0)
        def _():
            acc[...] = jnp.zeros_like(acc)

        for a_ref, (sa, ca) in zip(a_refs, alay):
            for g_ref, (sg, cg) in zip(g_refs, glay):
                def add(a_ref=a_ref, g_ref=g_ref):
                    acc[...] += _dot_tn(a_ref[...], g_ref[...])

                conds = [c for c in (inside(i, sa, ca, na == 1), inside(j, sg, cg, len(gp) == 1)) if c is not None]
                if not conds:
                    add()
                else:
                    pl.when(functools.reduce(lambda u, v: u & v, conds))(add)

        if narrow:
            @pl.when(k == nt - 1)
            def _():
                o_ref[...] = acc[...].astype(out_dtype)

    def spec(tile, lay, single, axis):
        s, c = lay

        def index(i, j, k):
            idx = (i, j)[axis]
            if single:
                return (k, idx)
            on = (idx >= s) & (idx < s + c)
            return (jnp.where(on, k, 0), jnp.clip(idx - s, 0, c - 1))

        return pl.BlockSpec((tt, tile), index)

    in_specs = [spec(tm, lay, na == 1, 0) for lay in alay] + [spec(tn, lay, len(gp) == 1, 1) for lay in glay]
    return pl.pallas_call(
        body, grid=(sum(aw) // tm, sum(gw) // tn, nt), in_specs=in_specs,
        out_specs=pl.BlockSpec((tm, tn), lambda i, j, k: (i, j)),
        out_shape=S((sum(aw), sum(gw)), out_dtype), scratch_shapes=[pltpu.VMEM((tm, tn), F32)] if narrow else [],
        name=name, compiler_params=_cparams("parallel", "parallel", "arbitrary"))(*ap, *gp)


def _rotate(x, c2, s2):
    return x * c2 + pltpu.roll(x, RET_QK // 2, 1) * s2


def _unrotate(d, c2, s2):
    return d * c2 - pltpu.roll(d, RET_QK // 2, 1) * s2


RET_PAIR = 4
RET_STEPS = RET_HEADS // RET_PAIR


def _ret_specs(RB, blockmap):
    qk, vg = RET_PAIR * RET_QK, RET_PAIR * RET_V
    q = pl.BlockSpec((RB, qk), lambda h, n: (blockmap(n), h))
    k = pl.BlockSpec((RB, qk), lambda h, n: (blockmap(n), RET_STEPS + h))
    v = pl.BlockSpec((RB, vg), lambda h, n: (blockmap(n), RET_STEPS + h))
    g = pl.BlockSpec((RB, vg), lambda h, n: (blockmap(n), 2 * RET_STEPS + h))
    tab = pl.BlockSpec((RB, RET_QK), lambda h, n: (blockmap(n), 0))
    return q, k, v, g, tab


def _ret_decay_specs():
    return [pl.BlockSpec((RET_PAIR, CHUNK, CHUNK), lambda h, n: (h, 0, 0)),
            pl.BlockSpec((RET_PAIR, CHUNK, RET_QK), lambda h, n: (h, 0, 0)),
            pl.BlockSpec((RET_PAIR, CHUNK, RET_QK), lambda h, n: (h, 0, 0)),
            pl.BlockSpec((RET_PAIR, 1, RET_V), lambda h, n: (h, 0, 0))]


def _ret_cols(e):
    return slice(e * RET_QK, (e + 1) * RET_QK), slice(e * RET_V, (e + 1) * RET_V)


def _retention_fwd(z, c2, s2, dintra, qdec, kdec, cdec):
    T = z.shape[0]
    RB = min(512, T)
    nch, nb = RB // CHUNK, T // RB

    def body(q_ref, k_ref, v_ref, g_ref, c2_ref, s2_ref, di_ref, qd_ref, kd_ref, cd_ref, ya_ref, r_ref, st_ref, state):
        @pl.when(pl.program_id(1) == 0)
        def _():
            state[...] = jnp.zeros_like(state)

        for c in range(nch):
            rows = slice(c * CHUNK, (c + 1) * CHUNK)
            c2v, s2v = c2_ref[rows, :], s2_ref[rows, :]
            for e in range(RET_PAIR):
                qk, vg = _ret_cols(e)
                dmat, qdv, kdv, cdv = di_ref[e], qd_ref[e], kd_ref[e], cd_ref[e]
                qr = _rotate(q_ref[rows, qk], c2v, s2v)
                kr = _rotate(k_ref[rows, qk], c2v, s2v) * RET_SCALE
                vb = v_ref[rows, vg].astype(BF16)
                sm = _dot_nt(qr.astype(BF16), kr.astype(BF16)) * dmat
                sb = state[e].astype(BF16)
                st_ref[e, c] = sb
                o = _dot(sm.astype(BF16), vb) + _dot((qr * qdv).astype(BF16), sb)
                state[e] = state[e] * cdv + _dot_tn((kr * kdv).astype(BF16), vb)
                r_ref[rows, vg] = o
                mu = jnp.mean(o, axis=-1, keepdims=True)
                oc = o - mu
                rn = oc * lax.rsqrt(jnp.mean(oc * oc, axis=-1, keepdims=True) + EPS)
                gv = g_ref[rows, vg]
                ya_ref[rows, vg] = (gv * _sigmoid(gv) * rn).astype(BF16)

    q, k, v, g, tab = _ret_specs(RB, lambda n: n)
    wide = pl.BlockSpec((RB, RET_PAIR * RET_V), lambda h, n: (n, h))
    return pl.pallas_call(
        body, grid=(RET_STEPS, nb),
        in_specs=[q, k, v, g, tab, tab] + _ret_decay_specs(),
        out_specs=[wide, wide, pl.BlockSpec((RET_PAIR, nch, RET_QK, RET_V), lambda h, n: (h, n, 0, 0))],
        out_shape=[S((T, RET_HEADS * RET_V), BF16), S((T, RET_HEADS * RET_V), F32),
                   S((RET_HEADS, T // CHUNK, RET_QK, RET_V), BF16)],
        scratch_shapes=[pltpu.VMEM((RET_PAIR, RET_QK, RET_V), F32)], name="retention_fwd",
        compiler_params=_cparams("parallel", "arbitrary"))(z, z, z, z, c2, s2, dintra, qdec, kdec, cdec)


def _retention_bwd(z, c2, s2, dintra, qdec, kdec, cdec, r, dycat, st):
    T = z.shape[0]
    RB = min(512, T)
    nch, nb = RB // CHUNK, T // RB

    def body(q_ref, k_ref, v_ref, g_ref, c2_ref, s2_ref, di_ref, qd_ref, kd_ref, cd_ref, r_ref, dy_ref, st_ref,
             dq_ref, dk_ref, dv_ref, dg_ref, dstate):
        @pl.when(pl.program_id(1) == 0)
        def _():
            dstate[...] = jnp.zeros_like(dstate)

        for c in reversed(range(nch)):
            rows = slice(c * CHUNK, (c + 1) * CHUNK)
            c2v, s2v = c2_ref[rows, :], s2_ref[rows, :]
            for e in range(RET_PAIR):
                qk, vg = _ret_cols(e)
                dmat, qdv, kdv, cdv = di_ref[e], qd_ref[e], kd_ref[e], cd_ref[e]
                qr = _rotate(q_ref[rows, qk], c2v, s2v)
                kr = _rotate(k_ref[rows, qk], c2v, s2v) * RET_SCALE
                qb, kb = qr.astype(BF16), kr.astype(BF16)
                vb = v_ref[rows, vg].astype(BF16)
                o, gv, dy = r_ref[rows, vg], g_ref[rows, vg], dy_ref[rows, vg]
                mu = jnp.mean(o, axis=-1, keepdims=True)
                oc = o - mu
                rstd = lax.rsqrt(jnp.mean(oc * oc, axis=-1, keepdims=True) + EPS)
                rn = oc * rstd
                sg = _sigmoid(gv)
                dg_ref[rows, vg] = (dy * rn * (sg * (1.0 + gv * (1.0 - sg)))).astype(BF16)
                drn = dy * (gv * sg)
                do = rstd * (drn - jnp.mean(drn, axis=-1, keepdims=True) - rn * jnp.mean(drn * rn, axis=-1, keepdims=True))
                dob = do.astype(BF16)
                sm = (_dot_nt(qb, kb) * dmat).astype(BF16)
                kdb = (kr * kdv).astype(BF16)
                dsb = dstate[e].astype(BF16)
                dv_ref[rows, vg] = (_dot_tn(sm, dob) + _dot(kdb, dsb)).astype(BF16)
                ds = (_dot_nt(dob, vb) * dmat).astype(BF16)
                dqr = _dot(ds, kb) + _dot_nt(dob, st_ref[e, c]) * qdv
                dkr = (_dot_tn(ds, qb) + _dot_nt(vb, dsb) * kdv) * RET_SCALE
                dstate[e] = dstate[e] * cdv + _dot_tn((qr * qdv).astype(BF16), dob)
                dq_ref[rows, qk] = _unrotate(dqr, c2v, s2v).astype(BF16)
                dk_ref[rows, qk] = _unrotate(dkr, c2v, s2v).astype(BF16)

    rev = lambda n: nb - 1 - n
    q, k, v, g, tab = _ret_specs(RB, rev)
    wide = pl.BlockSpec((RB, RET_PAIR * RET_V), lambda h, n: (rev(n), h))
    narrow = pl.BlockSpec((RB, RET_PAIR * RET_QK), lambda h, n: (rev(n), h))
    return pl.pallas_call(
        body, grid=(RET_STEPS, nb),
        in_specs=[q, k, v, g, tab, tab] + _ret_decay_specs() + [
            wide, wide, pl.BlockSpec((RET_PAIR, nch, RET_QK, RET_V), lambda h, n: (h, rev(n), 0, 0))],
        out_specs=[narrow, narrow, wide, wide],
        out_shape=[S((T, RET_HEADS * RET_QK), BF16), S((T, RET_HEADS * RET_QK), BF16),
                   S((T, RET_HEADS * RET_V), BF16), S((T, RET_HEADS * RET_V), BF16)],
        scratch_shapes=[pltpu.VMEM((RET_PAIR, RET_QK, RET_V), F32)], name="retention_bwd",
        compiler_params=_cparams("parallel", "arbitrary"))(z, z, z, z, c2, s2, dintra, qdec, kdec, cdec, r, dycat, st)


def _rel_index(i):
    r = lax.broadcasted_iota(jnp.int32, (3 * LANES, 5 * LANES), 0)
    j = lax.broadcasted_iota(jnp.int32, (3 * LANES, 5 * LANES), 1)
    idx = jnp.clip(i + PADK - j, -MAX_REL, MAX_REL) + MAX_REL
    return (r == idx).astype(BF16)


def _split3(v):
    hi = v.astype(BF16)
    r1 = v - hi.astype(F32)
    mid = r1.astype(BF16)
    lo = (r1 - mid.astype(F32)).astype(BF16)
    return hi, mid, lo


def _bias_build(rb):
    rbp = jnp.pad(rb, ((0, 0), (0, 3 * LANES - N_REL)))

    def body(rb_ref, o_ref):
        e = _rel_index(pl.program_id(0))
        hi, mid, lo = _split3(rb_ref[...])
        o_ref[...] = _dot(hi, e) + _dot(mid, e) + _dot(lo, e)

    return pl.pallas_call(
        body, grid=(CHUNK,), in_specs=[pl.BlockSpec((ATT_HEADS, 3 * LANES), lambda i: (0, 0))],
        out_specs=pl.BlockSpec((None, ATT_HEADS, 5 * LANES), lambda i: (i, 0, 0)),
        out_shape=S((CHUNK, ATT_HEADS, 5 * LANES), F32), name="bias_build",
        compiler_params=_cparams("parallel"))(rbp)


ATT_RB = 512
ATT_QT = 256
ATT_CPT = ATT_QT // CHUNK
ATT_KT = ATT_QT + PADK
ATT_QCOL = (2 * RET_HEADS * RET_QK + 2 * RET_HEADS * RET_V) // LANES
ATT_KCOL = ATT_QCOL + ATT_HEADS * ATT_D // LANES
ATT_VCOL = ATT_KCOL + ATT_HEADS * ATT_D // LANES


def _bias_grad(dbt):
    def body(d_ref, o_ref):
        @pl.when(pl.program_id(0) == 0)
        def _():
            o_ref[...] = jnp.zeros_like(o_ref)

        e = _rel_index(pl.program_id(0))
        d = d_ref[0]
        for ci in range(1, ATT_CPT):
            d = d + d_ref[ci]
        hi, mid, lo = _split3(d)
        o_ref[...] += _dot_nt(hi, e) + _dot_nt(mid, e) + _dot_nt(lo, e)

    return pl.pallas_call(
        body, grid=(CHUNK,),
        in_specs=[pl.BlockSpec((ATT_CPT, None, ATT_HEADS, 5 * LANES), lambda i: (0, i, 0, 0))],
        out_specs=pl.BlockSpec((ATT_HEADS, 3 * LANES), lambda i: (0, 0)),
        out_shape=S((ATT_HEADS, 3 * LANES), F32), name="bias_grad",
        compiler_params=_cparams("arbitrary"))(dbt)


def _bias_tiles(bias):
    parts = [jnp.pad(bias, ((0, 0), (0, 0), (CHUNK * ci, ATT_KT - BAND - CHUNK * ci)), constant_values=NEG_INF)
             for ci in range(ATT_CPT)]
    return jnp.stack(parts, axis=1).reshape(ATT_HEADS, ATT_QT, ATT_KT)


def _bias_bands(dbias_tiles):
    d = dbias_tiles.reshape(ATT_HEADS, ATT_CPT, CHUNK, ATT_KT)
    bands = jnp.stack([d[:, ci, :, CHUNK * ci:CHUNK * ci + BAND] for ci in range(ATT_CPT)])
    return jnp.pad(jnp.transpose(bands, (0, 2, 1, 3)), ((0, 0), (0, 0), (0, 0), (0, 5 * LANES - BAND)))


def _att_fill(kw, vw, klo, khi, vlo, vhi):
    kw[0:ATT_RB, :] = klo[...].astype(BF16)
    kw[ATT_RB:, :] = khi[...].astype(BF16)
    vw[0:ATT_RB, :] = vlo[...].astype(BF16)
    vw[ATT_RB:, :] = vhi[...].astype(BF16)


def _att_probs(qm, kwin, bias, first_key):
    s = _dot_nt(qm, kwin) * ATT_SCALE + bias
    col = lax.broadcasted_iota(jnp.int32, (ATT_QT, ATT_KT), 1)
    s = jnp.where(col + first_key >= 0, s, NEG_INF)
    p = jnp.exp(s - jnp.max(s, axis=-1, keepdims=True))
    return p / jnp.sum(p, axis=-1, keepdims=True)


def _att_in_specs(nq):
    qn = lambda n: jnp.minimum(n, nq - 1)
    blk = lambda col, back: pl.BlockSpec((ATT_RB, LANES), lambda hp, n: (jnp.maximum(qn(n) - back, 0), col + hp))
    return [blk(ATT_QCOL, 0), blk(ATT_KCOL, 1), blk(ATT_KCOL, 0), blk(ATT_VCOL, 1), blk(ATT_VCOL, 0),
            pl.BlockSpec((2, ATT_QT, ATT_KT), lambda hp, n: (hp, 0, 0))]


def _attention_fwd(z, bias_t, plan=None):
    T = z.shape[0]
    nq = T // ATT_RB

    def body(q_ref, klo, khi, vlo, vhi, b_ref, o_ref, kw, vw):
        _att_fill(kw, vw, klo, khi, vlo, vhi)
        lane = lax.broadcasted_iota(jnp.int32, (ATT_QT, LANES), 1)
        n = pl.program_id(1)
        for t in range(ATT_RB // ATT_QT):
            rows = slice(t * ATT_QT, (t + 1) * ATT_QT)
            win = slice(t * ATT_QT, t * ATT_QT + ATT_KT)
            qc = q_ref[rows, :]
            outs = []
            for e in range(2):
                qm = jnp.where((lane >= ATT_D) == (e == 1), qc, 0.0).astype(BF16)
                p = _att_probs(qm, kw[win, :], b_ref[e], (n - 1) * ATT_RB + t * ATT_QT)
                outs.append(_dot(p.astype(BF16), vw[win, :]))
            o_ref[rows, :] = jnp.where(lane < ATT_D, outs[0], outs[1]).astype(BF16)

    (yb,), extra = _call(
        body, grid=(ATT_HEADS // 2, nq), in_specs=_att_in_specs(nq),
        out_specs=[pl.BlockSpec((ATT_RB, LANES), lambda hp, n: (n, hp))],
        out_shape=[S((T, ATT_HEADS * ATT_D), BF16)],
        scratch_shapes=[pltpu.VMEM((2 * ATT_RB, LANES), BF16), pltpu.VMEM((2 * ATT_RB, LANES), BF16)],
        name="attention_fwd", sem=("parallel", "parallel"), args=(z, z, z, z, z, bias_t), plan=plan)
    return yb, extra


def _attention_bwd(z, bias_t, dycat, plan=None):
    T = z.shape[0]
    nq = T // ATT_RB
    dycol = RET_HEADS * RET_V // LANES

    def body(q_ref, klo, khi, vlo, vhi, b_ref, dy_ref, dq_ref, dk_ref, dv_ref, db_ref, kw, vw, dkw, dvw):
        n = pl.program_id(1)

        @pl.when(n == 0)
        def _():
            dkw[...] = jnp.zeros_like(dkw)
            dvw[...] = jnp.zeros_like(dvw)
            db_ref[...] = jnp.zeros_like(db_ref)

        @pl.when(n > 0)
        def _():
            dkw[0:ATT_RB, :] = dkw[ATT_RB:, :]
            dvw[0:ATT_RB, :] = dvw[ATT_RB:, :]
            dkw[ATT_RB:, :] = jnp.zeros((ATT_RB, LANES), F32)
            dvw[ATT_RB:, :] = jnp.zeros((ATT_RB, LANES), F32)

        @pl.when(n < nq)
        def _():
            _att_fill(kw, vw, klo, khi, vlo, vhi)
            lane = lax.broadcasted_iota(jnp.int32, (ATT_QT, LANES), 1)
            for t in range(ATT_RB // ATT_QT):
                rows = slice(t * ATT_QT, (t + 1) * ATT_QT)
                win = slice(t * ATT_QT, t * ATT_QT + ATT_KT)
                qc, dyc = q_ref[rows, :], dy_ref[rows, :]
                kwin, vwin = kw[win, :], vw[win, :]
                dq = jnp.zeros((ATT_QT, LANES), F32)
                for e in range(2):
                    mine = (lane >= ATT_D) == (e == 1)
                    qm = jnp.where(mine, qc, 0.0).astype(BF16)
                    dom = jnp.where(mine, dyc, 0.0).astype(BF16)
                    p = _att_probs(qm, kwin, b_ref[e], (n - 1) * ATT_RB + t * ATT_QT)
                    dp = _dot_nt(dom, vwin)
                    ds = p * (dp - jnp.sum(dp * p, axis=-1, keepdims=True))
                    db_ref[e] += ds
                    dsb = (ds * ATT_SCALE).astype(BF16)
                    dq = dq + jnp.where(mine, _dot(dsb, kwin), 0.0)
                    dkw[win, :] += _dot_tn(dsb, qm)
                    dvw[win, :] += _dot_tn(p.astype(BF16), dom)
                dq_ref[rows, :] = dq.astype(BF16)

        dk_ref[...] = dkw[0:ATT_RB, :].astype(BF16)
        dv_ref[...] = dvw[0:ATT_RB, :].astype(BF16)

    qn = lambda n: jnp.minimum(n, nq - 1)
    out_kv = pl.BlockSpec((ATT_RB, LANES), lambda hp, n: (jnp.maximum(n - 1, 0), hp))
    return _call(
        body, grid=(ATT_HEADS // 2, nq + 1),
        in_specs=_att_in_specs(nq) + [pl.BlockSpec((ATT_RB, LANES), lambda hp, n: (qn(n), dycol + hp))],
        out_specs=[pl.BlockSpec((ATT_RB, LANES), lambda hp, n: (qn(n), hp)), out_kv, out_kv,
                   pl.BlockSpec((2, ATT_QT, ATT_KT), lambda hp, n: (hp, 0, 0))],
        out_shape=[S((T, ATT_HEADS * ATT_D), BF16), S((T, ATT_HEADS * ATT_D), BF16),
                   S((T, ATT_HEADS * ATT_D), BF16), S((ATT_HEADS, ATT_QT, ATT_KT), F32)],
        scratch_shapes=[pltpu.VMEM((2 * ATT_RB, LANES), BF16), pltpu.VMEM((2 * ATT_RB, LANES), BF16),
                        pltpu.VMEM((2 * ATT_RB, LANES), F32), pltpu.VMEM((2 * ATT_RB, LANES), F32)],
        name="attention_bwd", sem=("parallel", "arbitrary"), args=(z, z, z, z, z, bias_t, dycat), plan=plan)


HALO = 8


def _causal_conv(ext, w_ref, b_ref):
    back2, back1 = pltpu.roll(ext, 2, 0), pltpu.roll(ext, 1, 0)
    zc = w_ref[0:1, :] * back2 + w_ref[1:2, :] * back1 + w_ref[2:3, :] * ext + b_ref[...]
    return zc[HALO:], back2, back1


def _ffn_down(z, cw, cb, wd, res, name, norm_g=None):
    T = z.shape[0]
    D = wd.shape[1]
    tb, tc = _tile(T, 1024, 8), 256
    nct = FFN_HIDDEN // tc
    per = tb // HALO

    def body(gp_ref, g_ref, up_ref, u_ref, wg_ref, wu_ref, bg_ref, bu_ref, wd_ref, res_ref, *rest):
        ng_ref = rest[0] if norm_g is not None else None
        f_ref, gc_ref, uc_ref, o_ref = rest[norm_g is not None:][:4]
        hn_ref = rest[5] if norm_g is not None else None
        acc = rest[-1]
        first, j = pl.program_id(0) == 0, pl.program_id(1)

        def conv(p_ref, blk_ref, w_ref, b_ref):
            prev = jnp.where(first, 0.0, p_ref[...])
            return _causal_conv(jnp.concatenate([prev, blk_ref[...]], axis=0), w_ref, b_ref)[0]

        gc, uc = conv(gp_ref, g_ref, wg_ref, bg_ref), conv(up_ref, u_ref, wu_ref, bu_ref)
        gc_ref[...] = gc.astype(BF16)
        uc_ref[...] = uc.astype(BF16)
        f = (_gelu(gc) * uc).astype(BF16)
        f_ref[...] = f
        p = _dot(f, wd_ref[...])

        @pl.when(j == 0)
        def _():
            acc[...] = p

        @pl.when(j > 0)
        def _():
            acc[...] += p

        @pl.when(j == nct - 1)
        def _():
            v = acc[...] + res_ref[...]
            o_ref[...] = v
            if norm_g is not None:
                r = lax.rsqrt(jnp.mean(v * v, axis=-1, keepdims=True) + EPS)
                hn_ref[...] = (v * r * ng_ref[...]).astype(BF16)

    def zspecs(off):
        return [pl.BlockSpec((HALO, tc), lambda i, j: (jnp.maximum(i * per - 1, 0), j + off)),
                pl.BlockSpec((tb, tc), lambda i, j: (i, j + off))]

    wspec = lambda off, rows: pl.BlockSpec((rows, tc), lambda i, j: (0, j + off))
    row = pl.BlockSpec((tb, D), lambda i, j: (i, 0))
    vec = pl.BlockSpec((1, D), lambda i, j: (0, 0))
    normed = norm_g is not None
    return pl.pallas_call(
        body, grid=(T // tb, nct),
        in_specs=zspecs(0) + zspecs(nct) + [wspec(0, 3), wspec(nct, 3), wspec(0, 1), wspec(nct, 1),
                                            pl.BlockSpec((tc, D), lambda i, j: (j, 0)), row] + [vec] * normed,
        out_specs=[pl.BlockSpec((tb, tc), lambda i, j: (i, j))] * 3 + [row] + [row] * normed,
        out_shape=[S((T, FFN_HIDDEN), BF16)] * 3 + [S((T, D), F32)] + [S((T, D), BF16)] * normed,
        scratch_shapes=[pltpu.VMEM((tb, D), F32)], name=name,
        compiler_params=_cparams("parallel", "arbitrary"))(z, z, z, z, cw, cw, cb, cb, wd, res, *([norm_g] * normed))


HALO16 = 16


def _convglu_bwd(z, gc, uc, df, cw, name):
    T = z.shape[0]
    tb, tc = _tile(T, 1024, 8), 256
    nct = FFN_HIDDEN // tc
    nrb = T // tb

    def body(g_ref, u_ref, gc_ref, gcn_ref, uc_ref, ucn_ref, df_ref, dfn_ref, wg_ref, wu_ref,
             dzg_ref, dzu_ref, dwg_ref, dwu_ref, dbg_ref, dbu_ref):
        i = pl.program_id(1)
        first, last = i == 0, i == nrb - 1

        @pl.when(first)
        def _():
            for ref in (dwg_ref, dwu_ref, dbg_ref, dbu_ref):
                ref[...] = jnp.zeros_like(ref)

        ext = lambda blk_ref, n_ref: jnp.concatenate([blk_ref[...].astype(F32), n_ref[...].astype(F32)[0:HALO]], axis=0)
        gcv, ucv = ext(gc_ref, gcn_ref), ext(uc_ref, ucn_ref)
        dfe = jnp.concatenate([df_ref[...], jnp.where(last, 0.0, dfn_ref[...])], axis=0)
        ge, gd = _gelu_and_grad(gcv)
        dgc, duc = dfe * ucv * gd, dfe * ge
        n = tb + HALO

        def back(d, z_ref, w_ref, dz_ref, dw_ref, db_ref):
            ahead1, ahead2 = pltpu.roll(d, n - 1, 0), pltpu.roll(d, n - 2, 0)
            dz_ref[...] = (w_ref[2:3, :] * d + w_ref[1:2, :] * ahead1 + w_ref[0:1, :] * ahead2)[:tb].astype(BF16)
            db_ref[...] += jnp.sum(d[:tb], axis=0, keepdims=True)
            zv = z_ref[...]
            for k, dk in enumerate((ahead2, ahead1, d)):
                dw_ref[k:k + 1, :] += jnp.sum(dk[:tb] * zv, axis=0, keepdims=True)

        back(dgc, g_ref, wg_ref, dzg_ref, dwg_ref, dbg_ref)
        back(duc, u_ref, wu_ref, dzu_ref, dwu_ref, dbu_ref)

    blk = pl.BlockSpec((tb, tc), lambda j, i: (i, j))
    after = lambda rows: pl.BlockSpec((rows, tc), lambda j, i: (jnp.minimum((i + 1) * (tb // rows), T // rows - 1), j))
    zspec = lambda off: pl.BlockSpec((tb, tc), lambda j, i: (i, j + off))
    wspec = lambda off: pl.BlockSpec((3, tc), lambda j, i: (0, j + off))
    acc = lambda rows: pl.BlockSpec((rows, tc), lambda j, i: (0, j))
    return pl.pallas_call(
        body, grid=(nct, nrb),
        in_specs=[zspec(0), zspec(nct), blk, after(HALO16), blk, after(HALO16), blk, after(HALO), wspec(0), wspec(nct)],
        out_specs=[blk, blk, acc(3), acc(3), acc(1), acc(1)],
        out_shape=[S((T, FFN_HIDDEN), BF16), S((T, FFN_HIDDEN), BF16), S((3, FFN_HIDDEN), F32), S((3, FFN_HIDDEN), F32),
                   S((1, FFN_HIDDEN), F32), S((1, FFN_HIDDEN), F32)],
        name=name, compiler_params=_cparams("parallel", "arbitrary"))(z, z, gc, gc, uc, uc, df, df, cw, cw)


SGU_RB = 256


def _sgu_weights(ws_ref):
    i = lax.broadcasted_iota(jnp.int32, (SGU_BLOCK, SGU_BLOCK), 0)
    j = lax.broadcasted_iota(jnp.int32, (SGU_BLOCK, SGU_BLOCK), 1)
    mask = (j < CHUNK) | (i >= CHUNK)
    return mask, [jnp.where(mask, ws_ref[g], 0.0).astype(BF16) for g in range(SGU_GROUPS)]


def _sgu_norm(zv, lng, lnb):
    mu = jnp.mean(zv, axis=-1, keepdims=True)
    vc = zv - mu
    rstd = lax.rsqrt(jnp.mean(vc * vc, axis=-1, keepdims=True) + EPS)
    vh = vc * rstd
    return vh, rstd, vh * lng + lnb


def _sgu_fwd(zpre, lng, lnb, ws, bst):
    T = zpre.shape[0]
    nb = SGU_RB // SGU_BLOCK

    def body(z_ref, lng_ref, lnb_ref, ws_ref, bst_ref, o_ref):
        _, wm = _sgu_weights(ws_ref)
        u = _gelu(z_ref[:, :SGU_WIDTH])
        _, _, vn = _sgu_norm(_gelu(z_ref[:, SGU_WIDTH:]), lng_ref[...], lnb_ref[...])
        vnb = vn.astype(BF16)
        for b in range(nb):
            rows = slice(b * SGU_BLOCK, (b + 1) * SGU_BLOCK)
            for g in range(SGU_GROUPS):
                cols = slice(g * SGU_GW, (g + 1) * SGU_GW)
                mixed = _dot(wm[g], vnb[rows, cols]) + bst_ref[:, g:g + 1]
                o_ref[rows, cols] = (u[rows, cols] * mixed).astype(BF16)

    vec = pl.BlockSpec((1, SGU_WIDTH), lambda i: (0, 0))
    return pl.pallas_call(
        body, grid=(T // SGU_RB,),
        in_specs=[pl.BlockSpec((SGU_RB, 2 * SGU_WIDTH), lambda i: (i, 0)), vec, vec,
                  pl.BlockSpec((SGU_GROUPS, SGU_BLOCK, SGU_BLOCK), lambda i: (0, 0, 0)),
                  pl.BlockSpec((SGU_BLOCK, SGU_GROUPS), lambda i: (0, 0))],
        out_specs=pl.BlockSpec((SGU_RB, SGU_WIDTH), lambda i: (i, 0)),
        out_shape=S((T, SGU_WIDTH), BF16), name="sgu_fwd", compiler_params=_cparams("parallel"))(zpre, lng, lnb, ws, bst)


def _sgu_bwd(zpre, dy, lng, lnb, ws, bst):
    T = zpre.shape[0]
    nb = SGU_RB // SGU_BLOCK

    def body(z_ref, dy_ref, lng_ref, lnb_ref, ws_ref, bst_ref, dz_ref, dws_ref, dbst_ref, dlng_ref, dlnb_ref, dvn):
        @pl.when(pl.program_id(0) == 0)
        def _():
            for ref in (dws_ref, dbst_ref, dlng_ref, dlnb_ref):
                ref[...] = jnp.zeros_like(ref)

        mask, wm = _sgu_weights(ws_ref)
        u, ud = _gelu_and_grad(z_ref[:, :SGU_WIDTH])
        v, vd = _gelu_and_grad(z_ref[:, SGU_WIDTH:])
        vh, rstd, vn = _sgu_norm(v, lng_ref[...], lnb_ref[...])
        vnb = vn.astype(BF16)
        lane8 = lax.broadcasted_iota(jnp.int32, (SGU_BLOCK, SGU_GROUPS), 1)
        dbs = jnp.zeros((SGU_BLOCK, SGU_GROUPS), F32)
        for b in range(nb):
            rows = slice(b * SGU_BLOCK, (b + 1) * SGU_BLOCK)
            for g in range(SGU_GROUPS):
                cols = slice(g * SGU_GW, (g + 1) * SGU_GW)
                vg = vnb[rows, cols]
                mixed = _dot(wm[g], vg) + bst_ref[:, g:g + 1]
                dyv = dy_ref[rows, cols]
                dz_ref[rows, cols] = (dyv * mixed * ud[rows, cols]).astype(BF16)
                dmix = dyv * u[rows, cols]
                dmb = dmix.astype(BF16)
                dvn[rows, cols] = _dot_tn(wm[g], dmb)
                dws_ref[g] += jnp.where(mask, _dot_nt(dmb, vg), 0.0)
                dbs = dbs + jnp.where(lane8 == g, jnp.sum(dmix, axis=-1, keepdims=True), 0.0)
        dbst_ref[...] += dbs
        dvnv = dvn[...]
        dlng_ref[...] += jnp.sum(dvnv * vh, axis=0, keepdims=True)
        dlnb_ref[...] += jnp.sum(dvnv, axis=0, keepdims=True)
        dvh = dvnv * lng_ref[...]
        dv = rstd * (dvh - jnp.mean(dvh, axis=-1, keepdims=True) - vh * jnp.mean(dvh * vh, axis=-1, keepdims=True))
        dz_ref[:, SGU_WIDTH:] = (dv * vd).astype(BF16)

    vec = pl.BlockSpec((1, SGU_WIDTH), lambda i: (0, 0))
    wsp = pl.BlockSpec((SGU_GROUPS, SGU_BLOCK, SGU_BLOCK), lambda i: (0, 0, 0))
    bsp = pl.BlockSpec((SGU_BLOCK, SGU_GROUPS), lambda i: (0, 0))
    return pl.pallas_call(
        body, grid=(T // SGU_RB,),
        in_specs=[pl.BlockSpec((SGU_RB, 2 * SGU_WIDTH), lambda i: (i, 0)),
                  pl.BlockSpec((SGU_RB, SGU_WIDTH), lambda i: (i, 0)), vec, vec, wsp, bsp],
        out_specs=[pl.BlockSpec((SGU_RB, 2 * SGU_WIDTH), lambda i: (i, 0)), wsp, bsp, vec, vec],
        out_shape=[S((T, 2 * SGU_WIDTH), BF16), S((SGU_GROUPS, SGU_BLOCK, SGU_BLOCK), F32),
                   S((SGU_BLOCK, SGU_GROUPS), F32), S((1, SGU_WIDTH), F32), S((1, SGU_WIDTH), F32)],
        scratch_shapes=[pltpu.VMEM((SGU_RB, SGU_WIDTH), F32)], name="sgu_bwd",
        compiler_params=_cparams("arbitrary"))(zpre, dy, lng, lnb, ws, bst)


def _loss_head(h, tgt, g):
    T, D = h.shape
    tr = _tile(T, 512, 8)

    def body(h_ref, t_ref, g_ref, ls_ref, dh_ref, dhb_ref, dg_ref):
        @pl.when(pl.program_id(0) == 0)
        def _():
            ls_ref[...] = jnp.zeros_like(ls_ref)
            dg_ref[...] = jnp.zeros_like(dg_ref)

        hv = h_ref[...]
        r = lax.rsqrt(jnp.mean(hv * hv, axis=-1, keepdims=True) + EPS)
        xh = hv * r
        diff = xh * g_ref[...] - t_ref[...]
        per_row = jnp.mean(diff * diff, axis=-1, keepdims=True)
        ls_ref[...] += jnp.sum(per_row, axis=0, keepdims=True)
        dy = diff * (1.0 / D)
        dg_ref[...] += jnp.sum(dy * xh, axis=0, keepdims=True)
        dxh = dy * g_ref[...]
        dh = r * (dxh - xh * jnp.mean(dxh * xh, axis=-1, keepdims=True))
        dh_ref[...] = dh
        dhb_ref[...] = dh.astype(BF16)

    row = pl.BlockSpec((tr, D), lambda i: (i, 0))
    vec = pl.BlockSpec((1, D), lambda i: (0, 0))
    return pl.pallas_call(
        body, grid=(T // tr,), in_specs=[row, row, vec],
        out_specs=[pl.BlockSpec((1, LANES), lambda i: (0, 0)), row, row, vec],
        out_shape=[S((1, LANES), F32), S((T, D), F32), S((T, D), BF16), S((1, D), F32)],
        name="loss_head", compiler_params=_cparams("arbitrary"))(h, tgt, g)


ANY = pl.BlockSpec(memory_space=pl.ANY)
COPY_PARTS = 4
SWAP_PARTS = 8
DMA = pltpu.SemaphoreType.DMA


def _place():
    return lax.axis_index("x"), lax.axis_index("y"), lax.axis_index("c")


def _nparts(rows, unit, want):
    n = want
    while n > 1 and rows % (unit * n):
        n //= 2
    return n


def _row_unit(dtype):
    return 16 if jnp.dtype(dtype).itemsize == 2 else 8


def _remote(src, dst, send_sems, recv_sems, k, to):
    return pltpu.make_async_remote_copy(src_ref=src, dst_ref=dst, send_sem=send_sems.at[k], recv_sem=recv_sems.at[k],
                                        device_id=to, device_id_type=MESH)


def _sem_ranges(counts):
    first, total = [], 0
    for c in counts:
        first.append(total)
        total += c
    return first, total


class _Gather:
    def __init__(self, shards):
        self.srcs = list(shards)
        self.halves = [a.shape[0] // 2 for a in shards]
        self.units = [_row_unit(a.dtype) for a in shards]
        self.parts = [_nparts(h, u, COPY_PARTS) for h, u in zip(self.halves, self.units)]
        self.first, total = _sem_ranges([3 * n for n in self.parts])
        self.out_shapes = [S((N_CHIPS,) + a.shape, a.dtype) for a in shards]
        self.scratch = [DMA((total,))] * 4
        self.has_relay = True

    def _ops(self, srcs, outs, sems):
        ici_s, ici_r, rel_s, rel_r = sems
        x, y, c = _place()
        me, sibling = (x, y, c), (x, y, 1 - c)
        chips = [(1 - x, y), (x, 1 - y), (1 - x, 1 - y)]
        send, arrive, relay, relayed = [], [], [], []
        for p_ref, out_ref, Rh, unit, n, base in zip(srcs, outs, self.halves, self.units, self.parts, self.first):
            rp = Rh // n

            def part(px, py, pc, k, out_ref=out_ref, Rh=Rh, unit=unit, rp=rp):
                return out_ref.at[2 * px + py, pl.ds(pl.multiple_of(pc * Rh + k * rp, unit), rp), :]

            def mine(k, p_ref=p_ref, Rh=Rh, unit=unit, rp=rp):
                return p_ref.at[pl.ds(pl.multiple_of(c * Rh + k * rp, unit), rp), :]

            for j, chip in enumerate(chips):
                for k in range(n):
                    s = base + j * n + k
                    send.append(_remote(mine(k), part(x, y, c, k), ici_s, ici_r, s, (*chip, c)))
                    arrive.append(_remote(mine(k), part(*chip, c, k), ici_s, ici_r, s, me))
                    relay.append(_remote(part(*chip, c, k), part(*chip, c, k), rel_s, rel_r, s, sibling))
                    relayed.append(_remote(part(*chip, c, k), part(*chip, 1 - c, k), rel_s, rel_r, s, me))
        return send, arrive, relay, relayed

    def start(self, *refs):
        for cp in self._ops(*refs)[0]:
            cp.start()

    def relay(self, *refs):
        _, arrive, relay, _ = self._ops(*refs)
        for a, r in zip(arrive, relay):
            a.wait_recv()
            r.start()

    def finish(self, *refs):
        send, _, relay, relayed = self._ops(*refs)
        for cp in relayed:
            cp.wait_recv()
        for cp in send + relay:
            cp.wait_send()


class _Exchange:
    def __init__(self, items):
        self.srcs = [a for a, _ in items]
        self.kinds = [k for _, k in items]
        self.blocks = []
        for a, kind in items:
            R, Ccols = a.shape
            self.blocks.append({"cols": (R // 2, Ccols // N_CHIPS), "rows": (R // (2 * N_CHIPS), Ccols), "all": (R, Ccols)}[kind])
        self.units = [_row_unit(a.dtype) for a in self.srcs]
        self.parts = [_nparts(b[0], u, COPY_PARTS) for b, u in zip(self.blocks, self.units)]
        self.first, total = _sem_ranges([N_DEV - 2 + n for n in self.parts])
        self.out_shapes = [S((N_DEV,) + b, a.dtype) for a, b in zip(self.srcs, self.blocks)]
        self.scratch = [DMA((total,))] * 2
        self.has_relay = False

    def _ops(self, srcs, outs, sems):
        ss, rs = sems
        x, y, c = _place()
        send, arrive = [], []
        for src_ref, out_ref, kind, (Rb, Cb), unit, n, base in zip(srcs, outs, self.kinds, self.blocks, self.units, self.parts,
                                                                  self.first):
            rp = Rb // n

            def block_for(px, py, pc, r0, rows, src_ref=src_ref, kind=kind, Rb=Rb, Cb=Cb, unit=unit):
                if kind == "cols":
                    return src_ref.at[pl.ds(pl.multiple_of(pc * Rb + r0, unit), rows),
                                      pl.ds(pl.multiple_of((2 * px + py) * Cb, LANES), Cb)]
                if kind == "rows":
                    return src_ref.at[pl.ds(pl.multiple_of((2 * (2 * px + py) + pc) * Rb + r0, unit), rows), :]
                return src_ref.at[pl.ds(r0, rows), :]

            def slot(d, r0, rows, out_ref=out_ref):
                return out_ref.at[d, pl.ds(r0, rows), :]

            me = 4 * x + 2 * y + c
            for k in range(1, N_DEV):
                peer = (x ^ ((k >> 2) & 1), y ^ ((k >> 1) & 1), c ^ (k & 1))
                pieces = [(N_DEV - 2 + q, q * rp, rp) for q in range(n)] if k == 1 else [(k - 2, 0, Rb)]
                for sem, r0, rows in pieces:
                    send.append(_remote(block_for(*peer, r0, rows), slot(me, r0, rows), ss, rs, base + sem, peer))
                    arrive.append(_remote(block_for(*peer, r0, rows), slot(4 * peer[0] + 2 * peer[1] + peer[2], r0, rows),
                                          ss, rs, base + sem, peer))
        return send, arrive

    def start(self, *refs):
        for cp in self._ops(*refs)[0]:
            cp.start()

    def finish(self, *refs):
        send, arrive = self._ops(*refs)
        for cp in arrive:
            cp.wait_recv()
        for cp in send:
            cp.wait_send()


class _Swap:
    def __init__(self, halves):
        self.srcs = list(halves)
        self.parts = [_nparts(a.shape[0], _row_unit(a.dtype), SWAP_PARTS) for a in halves]
        self.first, total = _sem_ranges(self.parts)
        self.out_shapes = [S(a.shape, a.dtype) for a in halves]
        self.scratch = [DMA((total,))] * 2
        self.has_relay = False

    def _ops(self, srcs, outs, sems):
        ss, rs = sems
        x, y, c = _place()
        copies = []
        for h_ref, out_ref, n, base in zip(srcs, outs, self.parts, self.first):
            rp = h_ref.shape[0] // n
            for k in range(n):
                rows = pl.ds(k * rp, rp)
                copies.append(_remote(h_ref.at[rows, :], out_ref.at[rows, :], ss, rs, base + k, (x, y, 1 - c)))
        return copies, copies

    start = _Exchange.start
    finish = _Exchange.finish


def _run_plan(plan, name):
    ni, no = len(plan.srcs), len(plan.out_shapes)

    def body(*refs):
        parts = (refs[:ni], refs[ni:ni + no], refs[ni + no:])
        plan.start(*parts)
        if plan.has_relay:
            plan.relay(*parts)
        plan.finish(*parts)

    return pl.pallas_call(body, out_shape=plan.out_shapes, in_specs=[ANY] * ni, out_specs=[ANY] * no,
                          scratch_shapes=plan.scratch, name=name)(*plan.srcs)


def _call(body, *, grid, in_specs, out_specs, out_shape, name, sem, args, scratch_shapes=(), plan=None):
    if plan is None:
        return pl.pallas_call(body, grid=grid, in_specs=in_specs, out_specs=out_specs, out_shape=out_shape,
                              scratch_shapes=list(scratch_shapes), name=name, compiler_params=_cparams(*sem))(*args), None
    n_in, n_out, n_scr = len(in_specs), len(out_shape), len(scratch_shapes)
    pi, po = len(plan.srcs), len(plan.out_shapes)
    total = math.prod(grid)

    def wrapped(*refs):
        a, refs = refs[:n_in], refs[n_in:]
        pa, refs = refs[:pi], refs[pi:]
        o, refs = refs[:n_out], refs[n_out:]
        pout, refs = refs[:po], refs[po:]
        scr, psem = refs[:n_scr], refs[n_scr:]
        step = 0
        for d, gsize in enumerate(grid):
            step = step * gsize + pl.program_id(d)

        @pl.when(step == 0)
        def _():
            plan.start(pa, pout, psem)

        body(*a, *o, *scr)
        if plan.has_relay:
            @pl.when(step == (3 * total) // 4)
            def _():
                plan.relay(pa, pout, psem)

        @pl.when(step == total - 1)
        def _():
            plan.finish(pa, pout, psem)

    outs = pl.pallas_call(
        wrapped, grid=grid, in_specs=list(in_specs) + [ANY] * pi, out_specs=list(out_specs) + [ANY] * po,
        out_shape=list(out_shape) + plan.out_shapes, scratch_shapes=list(scratch_shapes) + plan.scratch, name=name,
        compiler_params=_cparams(*["arbitrary"] * len(grid)))(*args, *plan.srcs)
    return outs[:n_out], outs[n_out:]


SMEM = pl.BlockSpec(memory_space=pltpu.SMEM)


def _sum_slots(buf, own, me, name):
    n, R, W = buf.shape
    tr = _tile(R, 256, 8)

    def body(me_ref, b_ref, own_ref, o_ref):
        acc = None
        for s in range(n):
            blk = jnp.where(me_ref[0] == s, own_ref[...], b_ref[s]).astype(F32)
            acc = blk if acc is None else acc + blk
        o_ref[...] = acc

    return pl.pallas_call(
        body, grid=(R // tr,),
        in_specs=[SMEM, pl.BlockSpec((n, tr, W), lambda i: (0, i, 0)), pl.BlockSpec((tr, W), lambda i: (i, 0))],
        out_specs=pl.BlockSpec((tr, W), lambda i: (i, 0)), out_shape=S((R, W), F32), name=name,
        compiler_params=_cparams("parallel"))(me, buf, own)


def _adamw_update(wv, gv, mv, vv):
    mn = ADAM_B1 * mv + (1.0 - ADAM_B1) * gv
    vn = ADAM_B2 * vv + (1.0 - ADAM_B2) * (gv * gv)
    m_hat = mn / (1.0 - ADAM_B1 ** ADAM_STEP)
    v_hat = vn / (1.0 - ADAM_B2 ** ADAM_STEP)
    return -ADAM_LR * (m_hat / (jnp.sqrt(v_hat) + ADAM_EPS) + ADAM_WD * wv), mn, vn


def _adamw(w, g, m, v, name):
    R, W = w.shape
    tr = _tile(R, 256, 8)

    def body(w_ref, g_ref, m_ref, v_ref, d_ref, mo_ref, vo_ref):
        d_ref[...], mo_ref[...], vo_ref[...] = _adamw_update(w_ref[...], g_ref[...], m_ref[...], v_ref[...])

    blk = pl.BlockSpec((tr, W), lambda i: (i, 0))
    return pl.pallas_call(
        body, grid=(R // tr,), in_specs=[blk] * 4, out_specs=[blk] * 3, out_shape=[S((R, W), F32)] * 3, name=name,
        compiler_params=_cparams("parallel"))(w, g, m, v)


def _adamw_shard(w, halves, core, m, v, name):
    L, R, C = w.shape
    Rh = R // 2
    tr = _tile(Rh, 256, 8)
    nbh = Rh // tr

    def body(c_ref, w_ref, m_ref, v_ref, *rest):
        pairs, (g_ref, d_ref, mo_ref, vo_ref) = rest[:2 * L], rest[2 * L:]
        l, i = pl.program_id(0), pl.program_id(1)
        mine_rows = i // nbh == c_ref[0]
        gv = None
        for lp in range(L):
            cand = jnp.where(mine_rows, pairs[2 * lp][...], pairs[2 * lp + 1][...])
            gv = cand if gv is None else jnp.where(l == lp, cand, gv)
        g_ref[...] = gv
        d_ref[...], mo_ref[...], vo_ref[...] = _adamw_update(w_ref[...], gv, m_ref[...], v_ref[...])

    blk = pl.BlockSpec((None, tr, C), lambda l, i: (l, i, 0))
    half = lambda lp: pl.BlockSpec((tr, C), lambda l, i: (jnp.where(l == lp, i % nbh, 0), 0))
    return pl.pallas_call(
        body, grid=(L, R // tr), in_specs=[SMEM, blk, blk, blk] + [half(lp) for lp in range(L) for _ in range(2)],
        out_specs=[blk] * 4, out_shape=[S((L, R, C), F32)] * 4, name=name,
        compiler_params=_cparams("parallel", "parallel"))(core, w, m, v, *[h for pair in halves for h in pair])


def _tables(T):
    f32 = F32
    half = RET_QK // 2
    inv = 1.0 / (10000.0 ** jnp.linspace(0.0, 1.0, half, dtype=f32))
    ang = jnp.arange(T).astype(f32)[:, None] * inv[None, :]
    cos, sin = jnp.cos(ang), jnp.sin(ang)
    c2 = jnp.concatenate([cos, cos], axis=-1)
    s2 = jnp.concatenate([-sin, sin], axis=-1)
    log_g = jnp.log1p(-jnp.exp2(-5.0 - jnp.arange(RET_HEADS, dtype=f32)))
    idx = jnp.arange(CHUNK, dtype=f32)
    dintra = jnp.exp(log_g[:, None, None] * jnp.abs(idx[:, None] - idx[None, :]))
    kdec = jnp.exp(log_g[None, :] * (CHUNK - 1 - idx)[:, None]).T
    qdec = jnp.exp(log_g[None, :] * (idx + 1.0)[:, None]).T
    cdec = jnp.exp(log_g * CHUNK)
    bc = lambda a, w: jnp.broadcast_to(a[:, :, None], (RET_HEADS, a.shape[1], w))
    return c2, s2, dintra, bc(qdec, RET_QK), bc(kdec, RET_QK), jnp.broadcast_to(cdec[:, None, None], (RET_HEADS, 1, RET_V))


def _first_forms(g4):
    return {"ab_w_in4": g4["ab_w_in"],
            "ab_w_inT": jnp.transpose(g4["ab_w_in"], (0, 2, 1)).reshape(-1, D_MODEL),
            "ab_w_out": g4["ab_w_out"].reshape(-1, D_MODEL)}


def _late_forms(g4):
    wd = g4["ffn_w_down"]
    per = wd.shape[1] // 2
    return {"c_w_in4": g4["c_w_in"], "c_w_out": g4["c_w_out"].reshape(-1, D_MODEL), "ffn_w_up4": g4["ffn_w_up"],
            "ffn_w_down": [wd[:, l * per:(l + 1) * per].reshape(-1, D_MODEL) for l in range(2)]}


def _local_step(x, tgt, p, first=None, late=None, exchange=False):
    T = x.shape[0]
    tab = _tables(T)
    row = lambda a: a.reshape(1, -1)
    tr = lambda w: jnp.transpose(w)
    width = D_MODEL

    hn0, first_out = _rmsnorm_fwd(x, row(p["attn_norm_g"][0]), "norm_a0", plan=first[0] if first else None)
    if first:
        p = {**p, **first[1](first_out)}
    z0 = _mm(hn0, p["ab_w_in4"], "mm_ab_in", form="kn4")
    ya, r, st = _retention_fwd(z0, *tab)
    bias_t = _bias_tiles(jnp.transpose(_bias_build(p["ab_rel_bias"][0]), (1, 0, 2))[:, :, :BAND])
    yb, late_out = _attention_fwd(z0, bias_t, plan=late[0] if late else None)
    if late:
        p = {**p, **late[1](late_out)}
    h1, hf0 = _mm([ya, yb], p["ab_w_out"], "mm_ab_out", res=x, norm_g=row(p["ffn_norm_g"][0]))

    def ffn_fwd(h, hf, l, next_g):
        zf = _mm(hf, p["ffn_w_up4"], f"mm_up{l}", form="kn4", row0=l * width)
        f, gc, uc, *out = _ffn_down(zf, p["ffn_conv_w"][l], row(p["ffn_conv_b"][l]), p["ffn_w_down"][l], h, f"ffn_down{l}",
                                    norm_g=next_g)
        return (zf, f, gc, uc), (tuple(out) if next_g is not None else out[0])

    kept0, (h2, hn1) = ffn_fwd(h1, hf0, 0, row(p["attn_norm_g"][1]))
    zc = _mm(hn1, p["c_w_in4"], "mm_c_in", form="kn4")
    lng, lnb, bst, ws = row(p["c_ln_g"][0]), row(p["c_ln_b"][0]), tr(p["c_b_s"][0]), p["c_w_s"][0]
    y1 = _sgu_fwd(zc, lng, lnb, ws, bst)
    h3, hf1 = _mm(y1, p["c_w_out"], "mm_c_out", res=h2, norm_g=row(p["ffn_norm_g"][1]))
    kept1, h4 = ffn_fwd(h3, hf1, 1, None)
    lsum, dh4, dh4b, dgfin = _loss_head(h4, tgt, row(p["final_norm_g"]))

    g, big = {}, {}

    def ffn_bwd(dh, dhb, h_in, hf, kept, l):
        zf, f, gc, uc = kept
        big[f"ffn_w_down{l}"] = _mm_tn(f, dhb, f"mmt_down{l}", out_dtype=BF16)
        df = _mm(dhb, p["ffn_w_down"][l], f"mmb_down{l}", form="nk")
        dzg, dzu, dwg, dwu, dbg, dbu = _convglu_bwd(zf, gc, uc, df, p["ffn_conv_w"][l], f"convglu_bwd{l}")
        big[f"ffn_w_up{l}"] = _mm_tn(hf, [dzg, dzu], f"mmt_up{l}", out_dtype=BF16)
        dhf = _mm([dzg, dzu], p["ffn_w_up4"], f"mmb_up{l}", form="nk4", row0=l * width, rows=width, out_dtype=BF16)
        dh_in, dh_in_b, dgf = _rmsnorm_bwd(h_in, dhf, row(p["ffn_norm_g"][l]), dh, f"norm_f{l}_bwd")
        return dh_in, dh_in_b, dict(ffn_norm_g=dgf[0], ffn_conv_w=jnp.concatenate([dwg, dwu], axis=1),
                                    ffn_conv_b=jnp.concatenate([dbg, dbu], axis=1)[0])

    dh3, dh3b, gf1 = ffn_bwd(dh4, dh4b, h3, hf1, kept1, 1)
    big["c_w_out"] = _mm_tn(y1, dh3b, "mmt_c_out", out_dtype=BF16)
    dy1 = _mm(dh3b, p["c_w_out"], "mmb_c_out", form="nk")
    dzc, dws, dbst, dlng, dlnb = _sgu_bwd(zc, dy1, lng, lnb, ws, bst)
    g["c_w_s"], g["c_b_s"], g["c_ln_g"], g["c_ln_b"] = dws[None], tr(dbst)[None], dlng, dlnb
    big["c_w_in"] = _mm_tn(hn1, dzc, "mmt_c_in", out_dtype=BF16)
    dhn1 = _mm(dzc, p["c_w_in4"], "mmb_c_in", form="nk4", rows=width, out_dtype=BF16)
    dh2, dh2b, dga1 = _rmsnorm_bwd(h2, dhn1, row(p["attn_norm_g"][1]), dh3, "norm_a1_bwd")
    dh1, dh1b, gf0 = ffn_bwd(dh2, dh2b, h1, hf0, kept0, 0)
    for k in gf0:
        g[k] = jnp.stack([gf0[k], gf1[k]])
    big["ab_w_out"] = _mm_tn([ya, yb], dh1b, "mmt_ab_out", out_dtype=BF16)
    dycat = _mm(dh1b, p["ab_w_out"], "mmb_ab_out", form="nk")
    g["final_norm_g"] = dgfin[0]
    early_pack = _pack([dga1[0]] + [g[n] for n in EARLY_SMALL], 32)
    late_plan = _Exchange([(big[n], kind) for n, kind in LATE_ITEMS] + [(early_pack, "all")]) if exchange else None
    (dqb, dkb, dvb, dbias_t), late_slots = _attention_bwd(z0, bias_t, dycat, plan=late_plan)
    dqa, dka, dva, dga = _retention_bwd(z0, *tab, r, dycat, st)
    dz0 = [dqa, dka, dva, dga, dqb, dkb, dvb]
    big["ab_w_in"] = _mm_tn(hn0, dz0, "mmt_ab_in", out_dtype=BF16)
    slots = {}
    if exchange:
        dhn0, first_slots = _mm(dz0, p["ab_w_inT"], "mmb_ab_in", out_dtype=BF16,
                                plan=_Exchange([(big[n], kind) for n, kind in FIRST_ITEMS]))
        slots = dict(first=first_slots, late=late_slots[:-1], early=(early_pack, late_slots[-1]))
    else:
        dhn0 = _mm(dz0, p["ab_w_inT"], "mmb_ab_in", out_dtype=BF16)
    gx, _, dga0 = _rmsnorm_bwd(x, dhn0, row(p["attn_norm_g"][0]), dh1, "norm_a0_bwd")
    g["ab_rel_bias"] = _bias_grad(_bias_bands(dbias_t))[None, :, :N_REL]
    g["attn_norm_g"] = jnp.stack([dga0[0], dga1[0]])
    return lsum[0, 0], gx, g, big, slots


FIRST_BIG = ["ab_w_in", "ab_w_out"]
LATE_BIG = ["c_w_in", "c_w_out", "ffn_w_up", "ffn_w_down"]
BIG = FIRST_BIG + LATE_BIG
FIRST_ITEMS = [("ab_w_in", "cols"), ("ab_w_out", "rows")]
LATE_ITEMS = [("c_w_in", "cols"), ("c_w_out", "rows"), ("ffn_w_up0", "cols"), ("ffn_w_up1", "cols"),
              ("ffn_w_down0", "rows"), ("ffn_w_down1", "rows")]
LAYERS_OF = {"ab_w_in": ["ab_w_in"], "ab_w_out": ["ab_w_out"], "c_w_in": ["c_w_in"], "c_w_out": ["c_w_out"],
             "ffn_w_up": ["ffn_w_up0", "ffn_w_up1"], "ffn_w_down": ["ffn_w_down0", "ffn_w_down1"]}
SMALL_SHARDED = [("c_ln_g", 1), ("c_ln_b", 1), ("ffn_conv_w", 2)]
REPLICATED = ["attn_norm_g", "ffn_norm_g", "ab_rel_bias", "c_w_s", "c_b_s", "ffn_conv_b", "final_norm_g"]
EARLY_SMALL = ["ffn_norm_g", "c_w_s", "c_b_s", "ffn_conv_b", "final_norm_g", "c_ln_g", "c_ln_b", "ffn_conv_w"]


def _rows_of(n_elems):
    return -(-n_elems // PACK_W)


def _flat_rows(a):
    f = a.reshape(-1)
    rows = _rows_of(f.shape[0])
    return jnp.pad(f, (0, rows * PACK_W - f.shape[0])).reshape(rows, PACK_W)


def _pad_rows(a, mult):
    extra = (-a.shape[0]) % mult
    return jnp.pad(a, ((0, extra), (0, 0))) if extra else a


def _pack(arrs, mult):
    return _pad_rows(jnp.concatenate([_flat_rows(a) for a in arrs], axis=0), mult)


def _unpack(buf, shapes):
    out, r = [], 0
    for shp in shapes:
        n = math.prod(shp)
        rows = _rows_of(n)
        out.append(buf[r:r + rows].reshape(-1)[:n].reshape(shp))
        r += rows
    return out


def _from_shards(sh, axis):
    m = jnp.moveaxis(sh, 0, axis)
    shp = m.shape
    return m.reshape(shp[:axis] + (shp[axis] * shp[axis + 1],) + shp[axis + 2:])


def _as_bf16_pairs(a):
    return lax.bitcast_convert_type(a.astype(F32), BF16)


def _from_bf16_pairs(a):
    return lax.bitcast_convert_type(a, F32)


def kernel(x, attn_norm_g, ffn_norm_g, ab_w_in, ab_w_out, ab_rel_bias, c_w_in, c_ln_g, c_ln_b, c_w_s, c_b_s, c_w_out, ffn_w_up, ffn_conv_w, ffn_conv_b, ffn_w_down, final_norm_g, loss_target, m_attn_norm_g, m_ffn_norm_g, m_ab_w_in, m_ab_w_out, m_ab_rel_bias, m_c_w_in, m_c_ln_g, m_c_ln_b, m_c_w_s, m_c_b_s, m_c_w_out, m_ffn_w_up, m_ffn_conv_w, m_ffn_conv_b, m_ffn_w_down, m_final_norm_g, v_attn_norm_g, v_ffn_norm_g, v_ab_w_in, v_ab_w_out, v_ab_rel_bias, v_c_w_in, v_c_ln_g, v_c_ln_b, v_c_w_s, v_c_b_s, v_c_w_out, v_ffn_w_up, v_ffn_conv_w, v_ffn_conv_b, v_ffn_w_down, v_final_norm_g):
    w = dict(attn_norm_g=attn_norm_g, ffn_norm_g=ffn_norm_g, ab_w_in=ab_w_in, ab_w_out=ab_w_out, ab_rel_bias=ab_rel_bias,
             c_w_in=c_w_in, c_ln_g=c_ln_g, c_ln_b=c_ln_b, c_w_s=c_w_s, c_b_s=c_b_s, c_w_out=c_w_out, ffn_w_up=ffn_w_up,
             ffn_conv_w=ffn_conv_w, ffn_conv_b=ffn_conv_b, ffn_w_down=ffn_w_down, final_norm_g=final_norm_g)
    m = dict(attn_norm_g=m_attn_norm_g, ffn_norm_g=m_ffn_norm_g, ab_w_in=m_ab_w_in, ab_w_out=m_ab_w_out,
             ab_rel_bias=m_ab_rel_bias, c_w_in=m_c_w_in, c_ln_g=m_c_ln_g, c_ln_b=m_c_ln_b, c_w_s=m_c_w_s, c_b_s=m_c_b_s,
             c_w_out=m_c_w_out, ffn_w_up=m_ffn_w_up, ffn_conv_w=m_ffn_conv_w, ffn_conv_b=m_ffn_conv_b,
             ffn_w_down=m_ffn_w_down, final_norm_g=m_final_norm_g)
    v = dict(attn_norm_g=v_attn_norm_g, ffn_norm_g=v_ffn_norm_g, ab_w_in=v_ab_w_in, ab_w_out=v_ab_w_out,
             ab_rel_bias=v_ab_rel_bias, c_w_in=v_c_w_in, c_ln_g=v_c_ln_g, c_ln_b=v_c_ln_b, c_w_s=v_c_w_s, c_b_s=v_c_b_s,
             c_w_out=v_c_w_out, ffn_w_up=v_ffn_w_up, ffn_conv_w=v_ffn_conv_w, ffn_conv_b=v_ffn_conv_b,
             ffn_w_down=v_ffn_w_down, final_norm_g=v_final_norm_g)
    names = list(w)
    chip = 2 * lax.axis_index("x") + lax.axis_index("y")

    core = lax.axis_index("c")
    core_arr = core.reshape(1).astype(jnp.int32)
    me_arr = (2 * chip + core).reshape(1).astype(jnp.int32)
    two_d = lambda a: a.reshape(-1, a.shape[-1])
    with_own = lambda gathered, own: lax.dynamic_update_slice(gathered, own[None], (chip, 0, 0))

    send_first = [two_d(w[n]).astype(BF16) for n in FIRST_BIG]
    finish_first = lambda got: _first_forms({n: with_own(a, own) for n, a, own in zip(FIRST_BIG, got, send_first)})
    full = {n: w[n] for n in REPLICATED}
    small_send = [_as_bf16_pairs(w[n]) for n, _ in SMALL_SHARDED]
    send_late = [two_d(w[n]).astype(BF16) for n in LATE_BIG] + [_pack(small_send, 32)]

    def finish_late(got):
        whole = [with_own(a, own) for a, own in zip(got, send_late)]
        forms = _late_forms(dict(zip(LATE_BIG, whole)))
        parts = [_unpack(whole[-1][s], [a.shape for a in small_send]) for s in range(N_CHIPS)]
        for i, (n, axis) in enumerate(SMALL_SHARDED):
            forms[n] = _from_shards(_from_bf16_pairs(jnp.stack([parts[s][i] for s in range(N_CHIPS)])), axis)
        return forms

    lsum, grad_x, g, big, slots = _local_step(x[0], loss_target[0], full, first=(_Gather(send_first), finish_first),
                                              late=(_Gather(send_late), finish_late), exchange=True)
    loss = lax.psum(0.5 * lsum, ("x", "y", "c"))

    def own_block(a, kind):
        rows, cols = a.shape
        if kind == "cols":
            return lax.dynamic_slice(a, (core * (rows // 2), chip * (cols // N_CHIPS)), (rows // 2, cols // N_CHIPS))
        per = rows // N_DEV
        return lax.dynamic_slice(a, ((2 * chip + core) * per, 0), (per, cols))

    reduced = {}
    for key, items in (("late", LATE_ITEMS), ("first", FIRST_ITEMS)):
        halves = [_sum_slots(got, own_block(big[n], kind), me_arr, f"sum_{n}") for (n, kind), got in zip(items, slots[key])]
        others = _run_plan(_Swap(halves), f"swap_{key}")
        reduced.update({n: (h, o) for (n, _), h, o in zip(items, halves, others)})
    big_outs = [{}, {}, {}, {}]
    for n in BIG:
        res = _adamw_shard(w[n], [reduced[layer] for layer in LAYERS_OF[n]], core_arr, m[n], v[n], f"adamw_{n}")
        for k in range(4):
            big_outs[k][n] = res[k]

    small_names = REPLICATED + [n for n, _ in SMALL_SHARDED]
    early_pack, early_slots = slots["early"]
    early = _unpack(_sum_slots(early_slots, early_pack, me_arr, "sum_early"), [(D_MODEL,)] + [g[n].shape for n in EARLY_SMALL])
    last_pack = _pack([g["attn_norm_g"][0], g["ab_rel_bias"]], 8)
    last_slots = _run_plan(_Exchange([(last_pack, "all")]), "exchange_small")[0]
    norm_a0, rel_bias = _unpack(_sum_slots(last_slots, last_pack, me_arr, "sum_small"), [(D_MODEL,), g["ab_rel_bias"].shape])
    gsmall_full = dict(zip(EARLY_SMALL, early[1:]), attn_norm_g=jnp.stack([norm_a0, early[0]]), ab_rel_bias=rel_bias)
    for n, axis in SMALL_SHARDED:
        size = w[n].shape[axis]
        gsmall_full[n] = lax.dynamic_slice_in_dim(gsmall_full[n], chip * size, size, axis)
    pack_small = lambda d: _pack([d[n] for n in small_names], 8)
    small_out = _adamw(pack_small(w), pack_small(gsmall_full), pack_small(m), pack_small(v), "adamw_small")
    small_shapes = [w[n].shape for n in small_names]

    outs = [{**big_outs[0], **gsmall_full}]
    for k in range(3):
        outs.append({**big_outs[k + 1], **dict(zip(small_names, _unpack(small_out[k], small_shapes)))})
    return (loss, grad_x[None], *[o[n] for o in outs for n in names])
```

```python
import functools
import math

import jax
import jax.numpy as jnp
from jax import lax
from jax.experimental import pallas as pl
from jax.experimental.pallas import tpu as pltpu

F32 = jnp.float32
BF16 = jnp.bfloat16
S = jax.ShapeDtypeStruct
MESH = pl.DeviceIdType.MESH

D_MODEL = 1024
CHUNK = 64
EPS = 1e-6
NEG_INF = -1e30
RET_HEADS, RET_QK, RET_V = 4, 128, 256
ATT_HEADS, ATT_D, ATT_PAST, MAX_REL = 8, 64, 8, 128
BAND = (ATT_PAST + 1) * CHUNK
PADK = ATT_PAST * CHUNK
SGU_BLOCK, SGU_GROUPS, SGU_WIDTH = 128, 8, 2048
SGU_GW = SGU_WIDTH // SGU_GROUPS
FFN_HIDDEN = 2816
N_REL = 2 * MAX_REL + 1
RET_SCALE = RET_QK ** -0.5
ATT_SCALE = ATT_D ** -0.5
ADAM_LR, ADAM_B1, ADAM_B2, ADAM_EPS, ADAM_WD, ADAM_STEP = 0.001, 0.9, 0.999, 1e-08, 0.01, 10

V7X_VMEM_BYTES = 64 * 1024 * 1024
VMEM_LIMIT = V7X_VMEM_BYTES * 7 // 8
MM_TILE_BUDGET = V7X_VMEM_BYTES * 11 // 16
LANES = 128
PACK_W = 1024
N_CHIPS = 4
N_DEV = 8

GELU_C = math.sqrt(2.0 / math.pi)
GELU_A = 0.044715


def _cparams(*sem):
    return pltpu.CompilerParams(dimension_semantics=tuple(sem) if sem else None, vmem_limit_bytes=VMEM_LIMIT)


def _tile(n, target, unit=LANES):
    best = None
    for t in range(unit, min(n, target) + 1, unit):
        if n % t == 0:
            best = t
    return best if best is not None else n


def _gelu(x):
    t = jnp.tanh(GELU_C * (x + GELU_A * x * x * x))
    return 0.5 * x * (1.0 + t)


def _gelu_and_grad(x):
    x2 = x * x
    t = jnp.tanh(GELU_C * (x + GELU_A * x2 * x))
    g = 0.5 * x * (1.0 + t)
    dg = 0.5 * (1.0 + t) + 0.5 * x * (1.0 - t * t) * (GELU_C * (1.0 + 3.0 * GELU_A * x2))
    return g, dg


def _sigmoid(x):
    return 1.0 / (1.0 + jnp.exp(-x))


def _dot(a, b):
    return jnp.dot(a, b, preferred_element_type=F32)


def _dot_nt(a, b):
    return lax.dot_general(a, b, (((1,), (1,)), ((), ())), preferred_element_type=F32)


def _dot_tn(a, b):
    return lax.dot_general(a, b, (((0,), (0,)), ((), ())), preferred_element_type=F32)


def _rmsnorm_fwd(x, g, name, plan=None):
    T, D = x.shape
    tr = _tile(T, 512, 8)

    def body(x_ref, g_ref, o_ref):
        xv = x_ref[...]
        r = lax.rsqrt(jnp.mean(xv * xv, axis=-1, keepdims=True) + EPS)
        o_ref[...] = (xv * r * g_ref[...]).astype(o_ref.dtype)

    (out,), extra = _call(
        body, grid=(T // tr,),
        in_specs=[pl.BlockSpec((tr, D), lambda i: (i, 0)), pl.BlockSpec((1, D), lambda i: (0, 0))],
        out_specs=[pl.BlockSpec((tr, D), lambda i: (i, 0))],
        out_shape=[S((T, D), BF16)], name=name, sem=("parallel",), args=(x, g), plan=plan)
    return out, extra


def _pieces(a):
    return list(a) if isinstance(a, (list, tuple)) else [a]


def _piece_layout(widths, tile):
    out, s = [], 0
    for w in widths:
        out.append((s, w // tile))
        s += w // tile
    return out


def _common_tile(widths, target):
    return _tile(functools.reduce(math.gcd, widths), target)


def _rmsnorm_bwd(x, dy, g, dres, name):
    T, D = x.shape
    tr = _tile(T, 512, 8)

    def body(x_ref, dy_ref, g_ref, dres_ref, dx_ref, dxb_ref, dg_ref):
        @pl.when(pl.program_id(0) == 0)
        def _():
            dg_ref[...] = jnp.zeros_like(dg_ref)

        xv = x_ref[...]
        r = lax.rsqrt(jnp.mean(xv * xv, axis=-1, keepdims=True) + EPS)
        xh = xv * r
        dyv = dy_ref[...].astype(F32)
        dg_ref[...] += jnp.sum(dyv * xh, axis=0, keepdims=True)
        dxh = dyv * g_ref[...]
        dx = dres_ref[...] + r * (dxh - xh * jnp.mean(dxh * xh, axis=-1, keepdims=True))
        dx_ref[...] = dx
        dxb_ref[...] = dx.astype(BF16)

    row = pl.BlockSpec((tr, D), lambda i: (i, 0))
    vec = pl.BlockSpec((1, D), lambda i: (0, 0))
    return pl.pallas_call(
        body, grid=(T // tr,), in_specs=[row, row, vec, row], out_specs=[row, row, vec],
        out_shape=[S((T, D), F32), S((T, D), BF16), S((1, D), F32)], name=name,
        compiler_params=_cparams("arbitrary"))(x, dy, g, dres)


def _mm(a, b, name, res=None, out_dtype=F32, plan=None, norm_g=None, form="kn", row0=0, rows=None):
    pieces = _pieces(a)
    M = pieces[0].shape[0]
    widths = [p.shape[1] for p in pieces]
    K = sum(widths)
    N = {"kn": lambda: b.shape[1], "kn4": lambda: N_CHIPS * b.shape[2], "nk": lambda: b.shape[0], "nk4": lambda: rows}[form]()
    tn = N if norm_g is not None else (b.shape[2] if form == "kn4" else _tile(N, 1408))
    tk = b.shape[2] if form == "nk4" else _common_tile(widths, 1536)
    def vmem_bytes(rows):
        out_bytes = jnp.dtype(out_dtype).itemsize + 2 * (norm_g is not None)
        blocks = len(pieces) * rows * tk * 2 + tk * tn * 2 + rows * tn * (4 * (res is not None) + out_bytes)
        return 2 * blocks + rows * tn * 4 * (K != tk)

    tm = _tile(M, 2048 if vmem_bytes(2048) <= MM_TILE_BUDGET else 1024, 8)
    assert all(w % tk == 0 for w in widths) and row0 % (tk if form == "kn4" else tn) == 0
    nk, npc = K // tk, len(pieces)
    layout = _piece_layout(widths, tk)
    tile = pl.BlockSpec((tm, tn), lambda i, j, k: (i, j))
    vec = pl.BlockSpec((1, tn), lambda i, j, k: (0, j))
    b_spec = {"kn": lambda: pl.BlockSpec((tk, tn), lambda i, j, k: (k, j)),
              "kn4": lambda: pl.BlockSpec((None, tk, tn), lambda i, j, k: (j, row0 // tk + k, 0)),
              "nk": lambda: pl.BlockSpec((tn, tk), lambda i, j, k: (j, k)),
              "nk4": lambda: pl.BlockSpec((None, tn, tk), lambda i, j, k: (k, row0 // tn + j, 0))}[form]()
    dot = _dot if form in ("kn", "kn4") else _dot_nt
    extra_in, extra_specs = [], []
    if res is not None:
        extra_in, extra_specs = [res], [tile]
    if norm_g is not None:
        extra_in, extra_specs = extra_in + [norm_g], extra_specs + [vec]
    n_extra = len(extra_in)
    if norm_g is not None:
        out_shape, out_specs = [S((M, N), out_dtype), S((M, N), BF16)], [tile, tile]
    else:
        out_shape, out_specs = [S((M, N), out_dtype)], [tile]

    def body(*refs):
        a_refs, b_ref = refs[:npc], refs[npc]
        ext = list(refs[npc + 1:npc + 1 + n_extra])
        outs = refs[npc + 1 + n_extra:npc + 1 + n_extra + len(out_shape)]

        def finish(v):
            if res is not None:
                v = v + ext[0][...]
            outs[0][...] = v.astype(outs[0].dtype)
            if norm_g is not None:
                r = lax.rsqrt(jnp.mean(v * v, axis=-1, keepdims=True) + EPS)
                outs[1][...] = (v * r * ext[-1][...]).astype(BF16)

        if nk == 1:
            finish(dot(a_refs[0][...], b_ref[...]))
            return
        acc = refs[-1]
        k = pl.program_id(2)
        for a_ref, (s, c) in zip(a_refs, layout):
            def add(a_ref=a_ref):
                acc[...] += dot(a_ref[...], b_ref[...])

            if s == 0:
                @pl.when(k == 0)
                def _(a_ref=a_ref):
                    acc[...] = dot(a_ref[...], b_ref[...])

                if c > 1:
                    pl.when((k > 0) & (k < c))(add)
            else:
                pl.when((k >= s) & (k < s + c))(add)

        @pl.when(k == nk - 1)
        def _():
            finish(acc[...])

    in_specs = [pl.BlockSpec((tm, tk), lambda i, j, k, s=s, c=c: (i, jnp.clip(k - s, 0, c - 1))) for s, c in layout]
    outs, extra = _call(
        body, grid=(M // tm, N // tn, nk), in_specs=in_specs + [b_spec] + extra_specs, out_specs=out_specs,
        out_shape=out_shape, scratch_shapes=[pltpu.VMEM((tm, tn), F32)] if nk > 1 else [],
        name=name, sem=("parallel", "parallel", "arbitrary"), args=pieces + [b] + extra_in, plan=plan)
    outs = outs[0] if len(outs) == 1 else tuple(outs)
    return outs if plan is None else (outs, extra)


def _mm_tn(a, g, name, out_dtype=F32):
    ap, gp = _pieces(a), _pieces(g)
    T = ap[0].shape[0]
    aw, gw = [p.shape[1] for p in ap], [p.shape[1] for p in gp]
    tm, tn = _common_tile(aw, 1408), _common_tile(gw, 1408)
    narrow = out_dtype != F32
    tall = 2 * 2 * 2048 * (len(ap) * tm + len(gp) * tn) + tm * tn * (2 * jnp.dtype(out_dtype).itemsize + 4 * narrow)
    tt = _tile(T, 2048 if tall <= MM_TILE_BUDGET else 1024, 8)
    alay, glay = _piece_layout(aw, tm), _piece_layout(gw, tn)
    na = len(ap)

    def inside(idx, s, c, single):
        return None if single else (idx >= s) & (idx < s + c)

    nt = T // tt

    def body(*refs):
        a_refs, g_refs = refs[:na], refs[na:na + len(gp)]
        o_ref = refs[na + len(gp)]
        acc = refs[-1] if narrow else o_ref
        i, j, k = pl.program_id(0), pl.program_id(1), pl.program_id(2)

        @pl.when(k == 0)
        def _():
            acc[...] = jnp.zeros_like(acc)

        for a_ref, (sa, ca) in zip(a_refs, alay):
            for g_ref, (sg, cg) in zip(g_refs, glay):
                def add(a_ref=a_ref, g_ref=g_ref):
                    acc[...] += _dot_tn(a_ref[...], g_ref[...])

                conds = [c for c in (inside(i, sa, ca, na == 1), inside(j, sg, cg, len(gp) == 1)) if c is not None]
                if not conds:
                    add()
                else:
                    pl.when(functools.reduce(lambda u, v: u & v, conds))(add)

        if narrow:
            @pl.when(k == nt - 1)
            def _():
                o_ref[...] = acc[...].astype(out_dtype)

    def spec(tile, lay, single, axis):
        s, c = lay

        def index(i, j, k):
            idx = (i, j)[axis]
            if single:
                return (k, idx)
            on = (idx >= s) & (idx < s + c)
            return (jnp.where(on, k, 0), jnp.clip(idx - s, 0, c - 1))

        return pl.BlockSpec((tt, tile), index)

    in_specs = [spec(tm, lay, na == 1, 0) for lay in alay] + [spec(tn, lay, len(gp) == 1, 1) for lay in glay]
    return pl.pallas_call(
        body, grid=(sum(aw) // tm, sum(gw) // tn, nt), in_specs=in_specs,
        out_specs=pl.BlockSpec((tm, tn), lambda i, j, k: (i, j)),
        out_shape=S((sum(aw), sum(gw)), out_dtype), scratch_shapes=[pltpu.VMEM((tm, tn), F32)] if narrow else [],
        name=name, compiler_params=_cparams("parallel", "parallel", "arbitrary"))(*ap, *gp)


def _rotate(x, c2, s2):
    return x * c2 + pltpu.roll(x, RET_QK // 2, 1) * s2


def _unrotate(d, c2, s2):
    return d * c2 - pltpu.roll(d, RET_QK // 2, 1) * s2


RET_PAIR = 4
RET_STEPS = RET_HEADS // RET_PAIR


def _ret_specs(RB, blockmap):
    qk, vg = RET_PAIR * RET_QK, RET_PAIR * RET_V
    q = pl.BlockSpec((RB, qk), lambda h, n: (blockmap(n), h))
    k = pl.BlockSpec((RB, qk), lambda h, n: (blockmap(n), RET_STEPS + h))
    v = pl.BlockSpec((RB, vg), lambda h, n: (blockmap(n), RET_STEPS + h))
    g = pl.BlockSpec((RB, vg), lambda h, n: (blockmap(n), 2 * RET_STEPS + h))
    tab = pl.BlockSpec((RB, RET_QK), lambda h, n: (blockmap(n), 0))
    return q, k, v, g, tab


def _ret_decay_specs():
    return [pl.BlockSpec((RET_PAIR, CHUNK, CHUNK), lambda h, n: (h, 0, 0)),
            pl.BlockSpec((RET_PAIR, CHUNK, RET_QK), lambda h, n: (h, 0, 0)),
            pl.BlockSpec((RET_PAIR, CHUNK, RET_QK), lambda h, n: (h, 0, 0)),
            pl.BlockSpec((RET_PAIR, 1, RET_V), lambda h, n: (h, 0, 0))]


def _ret_cols(e):
    return slice(e * RET_QK, (e + 1) * RET_QK), slice(e * RET_V, (e + 1) * RET_V)


def _retention_fwd(z, c2, s2, dintra, qdec, kdec, cdec):
    T = z.shape[0]
    RB = min(512, T)
    nch, nb = RB // CHUNK, T // RB

    def body(q_ref, k_ref, v_ref, g_ref, c2_ref, s2_ref, di_ref, qd_ref, kd_ref, cd_ref, ya_ref, r_ref, st_ref, state):
        @pl.when(pl.program_id(1) == 0)
        def _():
            state[...] = jnp.zeros_like(state)

        for c in range(nch):
            rows = slice(c * CHUNK, (c + 1) * CHUNK)
            c2v, s2v = c2_ref[rows, :], s2_ref[rows, :]
            for e in range(RET_PAIR):
                qk, vg = _ret_cols(e)
                dmat, qdv, kdv, cdv = di_ref[e], qd_ref[e], kd_ref[e], cd_ref[e]
                qr = _rotate(q_ref[rows, qk].astype(F32), c2v, s2v)
                kr = _rotate(k_ref[rows, qk].astype(F32), c2v, s2v) * RET_SCALE
                vb = v_ref[rows, vg].astype(BF16)
                sm = _dot_nt(qr.astype(BF16), kr.astype(BF16)) * dmat
                sb = state[e].astype(BF16)
                st_ref[e, c] = sb
                o = _dot(sm.astype(BF16), vb) + _dot((qr * qdv).astype(BF16), sb)
                state[e] = state[e] * cdv + _dot_tn((kr * kdv).astype(BF16), vb)
                r_ref[rows, vg] = o
                mu = jnp.mean(o, axis=-1, keepdims=True)
                oc = o - mu
                rn = oc * lax.rsqrt(jnp.mean(oc * oc, axis=-1, keepdims=True) + EPS)
                gv = g_ref[rows, vg].astype(F32)
                ya_ref[rows, vg] = (gv * _sigmoid(gv) * rn).astype(BF16)

    q, k, v, g, tab = _ret_specs(RB, lambda n: n)
    wide = pl.BlockSpec((RB, RET_PAIR * RET_V), lambda h, n: (n, h))
    return pl.pallas_call(
        body, grid=(RET_STEPS, nb),
        in_specs=[q, k, v, g, tab, tab] + _ret_decay_specs(),
        out_specs=[wide, wide, pl.BlockSpec((RET_PAIR, nch, RET_QK, RET_V), lambda h, n: (h, n, 0, 0))],
        out_shape=[S((T, RET_HEADS * RET_V), BF16), S((T, RET_HEADS * RET_V), F32),
                   S((RET_HEADS, T // CHUNK, RET_QK, RET_V), BF16)],
        scratch_shapes=[pltpu.VMEM((RET_PAIR, RET_QK, RET_V), F32)], name="retention_fwd",
        compiler_params=_cparams("parallel", "arbitrary"))(z, z, z, z, c2, s2, dintra, qdec, kdec, cdec)


def _retention_bwd(z, c2, s2, dintra, qdec, kdec, cdec, r, dycat, st):
    T = z.shape[0]
    RB = min(512, T)
    nch, nb = RB // CHUNK, T // RB

    def body(q_ref, k_ref, v_ref, g_ref, c2_ref, s2_ref, di_ref, qd_ref, kd_ref, cd_ref, r_ref, dy_ref, st_ref,
             dq_ref, dk_ref, dv_ref, dg_ref, dstate):
        @pl.when(pl.program_id(1) == 0)
        def _():
            dstate[...] = jnp.zeros_like(dstate)

        for c in reversed(range(nch)):
            rows = slice(c * CHUNK, (c + 1) * CHUNK)
            c2v, s2v = c2_ref[rows, :], s2_ref[rows, :]
            for e in range(RET_PAIR):
                qk, vg = _ret_cols(e)
                dmat, qdv, kdv, cdv = di_ref[e], qd_ref[e], kd_ref[e], cd_ref[e]
                qr = _rotate(q_ref[rows, qk].astype(F32), c2v, s2v)
                kr = _rotate(k_ref[rows, qk].astype(F32), c2v, s2v) * RET_SCALE
                qb, kb = qr.astype(BF16), kr.astype(BF16)
                vb = v_ref[rows, vg].astype(BF16)
                o, gv, dy = r_ref[rows, vg], g_ref[rows, vg].astype(F32), dy_ref[rows, vg]
                mu = jnp.mean(o, axis=-1, keepdims=True)
                oc = o - mu
                rstd = lax.rsqrt(jnp.mean(oc * oc, axis=-1, keepdims=True) + EPS)
                rn = oc * rstd
                sg = _sigmoid(gv)
                dg_ref[rows, vg] = (dy * rn * (sg * (1.0 + gv * (1.0 - sg)))).astype(BF16)
                drn = dy * (gv * sg)
                do = rstd * (drn - jnp.mean(drn, axis=-1, keepdims=True) - rn * jnp.mean(drn * rn, axis=-1, keepdims=True))
                dob = do.astype(BF16)
                sm = (_dot_nt(qb, kb) * dmat).astype(BF16)
                kdb = (kr * kdv).astype(BF16)
                dsb = dstate[e].astype(BF16)
                dv_ref[rows, vg] = (_dot_tn(sm, dob) + _dot(kdb, dsb)).astype(BF16)
                ds = (_dot_nt(dob, vb) * dmat).astype(BF16)
                dqr = _dot(ds, kb) + _dot_nt(dob, st_ref[e, c]) * qdv
                dkr = (_dot_tn(ds, qb) + _dot_nt(vb, dsb) * kdv) * RET_SCALE
                dstate[e] = dstate[e] * cdv + _dot_tn((qr * qdv).astype(BF16), dob)
                dq_ref[rows, qk] = _unrotate(dqr, c2v, s2v).astype(BF16)
                dk_ref[rows, qk] = _unrotate(dkr, c2v, s2v).astype(BF16)

    rev = lambda n: nb - 1 - n
    q, k, v, g, tab = _ret_specs(RB, rev)
    wide = pl.BlockSpec((RB, RET_PAIR * RET_V), lambda h, n: (rev(n), h))
    narrow = pl.BlockSpec((RB, RET_PAIR * RET_QK), lambda h, n: (rev(n), h))
    return pl.pallas_call(
        body, grid=(RET_STEPS, nb),
        in_specs=[q, k, v, g, tab, tab] + _ret_decay_specs() + [
            wide, wide, pl.BlockSpec((RET_PAIR, nch, RET_QK, RET_V), lambda h, n: (h, rev(n), 0, 0))],
        out_specs=[narrow, narrow, wide, wide],
        out_shape=[S((T, RET_HEADS * RET_QK), BF16), S((T, RET_HEADS * RET_QK), BF16),
                   S((T, RET_HEADS * RET_V), BF16), S((T, RET_HEADS * RET_V), BF16)],
        scratch_shapes=[pltpu.VMEM((RET_PAIR, RET_QK, RET_V), F32)], name="retention_bwd",
        compiler_params=_cparams("parallel", "arbitrary"))(z, z, z, z, c2, s2, dintra, qdec, kdec, cdec, r, dycat, st)


def _rel_index(i):
    r = lax.broadcasted_iota(jnp.int32, (3 * LANES, 5 * LANES), 0)
    j = lax.broadcasted_iota(jnp.int32, (3 * LANES, 5 * LANES), 1)
    idx = jnp.clip(i + PADK - j, -MAX_REL, MAX_REL) + MAX_REL
    return (r == idx).astype(BF16)


def _split3(v):
    hi = v.astype(BF16)
    r1 = v - hi.astype(F32)
    mid = r1.astype(BF16)
    lo = (r1 - mid.astype(F32)).astype(BF16)
    return hi, mid, lo


def _bias_build(rb):
    rbp = jnp.pad(rb, ((0, 0), (0, 3 * LANES - N_REL)))

    def body(rb_ref, o_ref):
        e = _rel_index(pl.program_id(0))
        hi, mid, lo = _split3(rb_ref[...])
        o_ref[...] = _dot(hi, e) + _dot(mid, e) + _dot(lo, e)

    return pl.pallas_call(
        body, grid=(CHUNK,), in_specs=[pl.BlockSpec((ATT_HEADS, 3 * LANES), lambda i: (0, 0))],
        out_specs=pl.BlockSpec((None, ATT_HEADS, 5 * LANES), lambda i: (i, 0, 0)),
        out_shape=S((CHUNK, ATT_HEADS, 5 * LANES), F32), name="bias_build",
        compiler_params=_cparams("parallel"))(rbp)


ATT_RB = 512
ATT_QT = 256
ATT_CPT = ATT_QT // CHUNK
ATT_KT = ATT_QT + PADK
ATT_QCOL = (2 * RET_HEADS * RET_QK + 2 * RET_HEADS * RET_V) // LANES
ATT_KCOL = ATT_QCOL + ATT_HEADS * ATT_D // LANES
ATT_VCOL = ATT_KCOL + ATT_HEADS * ATT_D // LANES


def _bias_grad(dbt):
    def body(d_ref, o_ref):
        @pl.when(pl.program_id(0) == 0)
        def _():
            o_ref[...] = jnp.zeros_like(o_ref)

        e = _rel_index(pl.program_id(0))
        d = d_ref[0]
        for ci in range(1, ATT_CPT):
            d = d + d_ref[ci]
        hi, mid, lo = _split3(d)
        o_ref[...] += _dot_nt(hi, e) + _dot_nt(mid, e) + _dot_nt(lo, e)

    return pl.pallas_call(
        body, grid=(CHUNK,),
        in_specs=[pl.BlockSpec((ATT_CPT, None, ATT_HEADS, 5 * LANES), lambda i: (0, i, 0, 0))],
        out_specs=pl.BlockSpec((ATT_HEADS, 3 * LANES), lambda i: (0, 0)),
        out_shape=S((ATT_HEADS, 3 * LANES), F32), name="bias_grad",
        compiler_params=_cparams("arbitrary"))(dbt)


def _bias_tiles(bias):
    parts = [jnp.pad(bias, ((0, 0), (0, 0), (CHUNK * ci, ATT_KT - BAND - CHUNK * ci)), constant_values=NEG_INF)
             for ci in range(ATT_CPT)]
    return jnp.stack(parts, axis=1).reshape(ATT_HEADS, ATT_QT, ATT_KT)


def _bias_bands(dbias_tiles):
    d = dbias_tiles.reshape(ATT_HEADS, ATT_CPT, CHUNK, ATT_KT)
    bands = jnp.stack([d[:, ci, :, CHUNK * ci:CHUNK * ci + BAND] for ci in range(ATT_CPT)])
    return jnp.pad(jnp.transpose(bands, (0, 2, 1, 3)), ((0, 0), (0, 0), (0, 0), (0, 5 * LANES - BAND)))


def _att_fill(kw, vw, klo, khi, vlo, vhi):
    kw[0:ATT_RB, :] = klo[...].astype(BF16)
    kw[ATT_RB:, :] = khi[...].astype(BF16)
    vw[0:ATT_RB, :] = vlo[...].astype(BF16)
    vw[ATT_RB:, :] = vhi[...].astype(BF16)


def _att_probs(qm, kwin, bias, first_key):
    s = _dot_nt(qm, kwin) * ATT_SCALE + bias
    col = lax.broadcasted_iota(jnp.int32, (ATT_QT, ATT_KT), 1)
    s = jnp.where(col + first_key >= 0, s, NEG_INF)
    p = jnp.exp(s - jnp.max(s, axis=-1, keepdims=True))
    return p / jnp.sum(p, axis=-1, keepdims=True)


def _att_in_specs(nq):
    qn = lambda n: jnp.minimum(n, nq - 1)
    blk = lambda col, back: pl.BlockSpec((ATT_RB, LANES), lambda hp, n: (jnp.maximum(qn(n) - back, 0), col + hp))
    return [blk(ATT_QCOL, 0), blk(ATT_KCOL, 1), blk(ATT_KCOL, 0), blk(ATT_VCOL, 1), blk(ATT_VCOL, 0),
            pl.BlockSpec((2, ATT_QT, ATT_KT), lambda hp, n: (hp, 0, 0))]


def _attention_fwd(z, bias_t, plan=None):
    T = z.shape[0]
    nq = T // ATT_RB

    def body(q_ref, klo, khi, vlo, vhi, b_ref, o_ref, kw, vw):
        _att_fill(kw, vw, klo, khi, vlo, vhi)
        lane = lax.broadcasted_iota(jnp.int32, (ATT_QT, LANES), 1)
        n = pl.program_id(1)
        for t in range(ATT_RB // ATT_QT):
            rows = slice(t * ATT_QT, (t + 1) * ATT_QT)
            win = slice(t * ATT_QT, t * ATT_QT + ATT_KT)
            qc = q_ref[rows, :]
            outs = []
            for e in range(2):
                qm = jnp.where((lane >= ATT_D) == (e == 1), qc, 0.0).astype(BF16)
                p = _att_probs(qm, kw[win, :], b_ref[e], (n - 1) * ATT_RB + t * ATT_QT)
                outs.append(_dot(p.astype(BF16), vw[win, :]))
            o_ref[rows, :] = jnp.where(lane < ATT_D, outs[0], outs[1]).astype(BF16)

    (yb,), extra = _call(
        body, grid=(ATT_HEADS // 2, nq), in_specs=_att_in_specs(nq),
        out_specs=[pl.BlockSpec((ATT_RB, LANES), lambda hp, n: (n, hp))],
        out_shape=[S((T, ATT_HEADS * ATT_D), BF16)],
        scratch_shapes=[pltpu.VMEM((2 * ATT_RB, LANES), BF16), pltpu.VMEM((2 * ATT_RB, LANES), BF16)],
        name="attention_fwd", sem=("parallel", "parallel"), args=(z, z, z, z, z, bias_t), plan=plan)
    return yb, extra


def _attention_bwd(z, bias_t, dycat, plan=None):
    T = z.shape[0]
    nq = T // ATT_RB
    dycol = RET_HEADS * RET_V // LANES

    def body(q_ref, klo, khi, vlo, vhi, b_ref, dy_ref, dq_ref, dk_ref, dv_ref, db_ref, kw, vw, dkw, dvw):
        n = pl.program_id(1)

        @pl.when(n == 0)
        def _():
            dkw[...] = jnp.zeros_like(dkw)
            dvw[...] = jnp.zeros_like(dvw)
            db_ref[...] = jnp.zeros_like(db_ref)

        @pl.when(n > 0)
        def _():
            dkw[0:ATT_RB, :] = dkw[ATT_RB:, :]
            dvw[0:ATT_RB, :] = dvw[ATT_RB:, :]
            dkw[ATT_RB:, :] = jnp.zeros((ATT_RB, LANES), F32)
            dvw[ATT_RB:, :] = jnp.zeros((ATT_RB, LANES), F32)

        @pl.when(n < nq)
        def _():
            _att_fill(kw, vw, klo, khi, vlo, vhi)
            lane = lax.broadcasted_iota(jnp.int32, (ATT_QT, LANES), 1)
            for t in range(ATT_RB // ATT_QT):
                rows = slice(t * ATT_QT, (t + 1) * ATT_QT)
                win = slice(t * ATT_QT, t * ATT_QT + ATT_KT)
                qc, dyc = q_ref[rows, :], dy_ref[rows, :]
                kwin, vwin = kw[win, :], vw[win, :]
                dq = jnp.zeros((ATT_QT, LANES), F32)
                for e in range(2):
                    mine = (lane >= ATT_D) == (e == 1)
                    qm = jnp.where(mine, qc, 0.0).astype(BF16)
                    dom = jnp.where(mine, dyc, 0.0).astype(BF16)
                    p = _att_probs(qm, kwin, b_ref[e], (n - 1) * ATT_RB + t * ATT_QT)
                    dp = _dot_nt(dom, vwin)
                    ds = p * (dp - jnp.sum(dp * p, axis=-1, keepdims=True))
                    db_ref[e] += ds
                    dsb = (ds * ATT_SCALE).astype(BF16)
                    dq = dq + jnp.where(mine, _dot(dsb, kwin), 0.0)
                    dkw[win, :] += _dot_tn(dsb, qm)
                    dvw[win, :] += _dot_tn(p.astype(BF16), dom)
                dq_ref[rows, :] = dq.astype(BF16)

        dk_ref[...] = dkw[0:ATT_RB, :].astype(BF16)
        dv_ref[...] = dvw[0:ATT_RB, :].astype(BF16)

    qn = lambda n: jnp.minimum(n, nq - 1)
    out_kv = pl.BlockSpec((ATT_RB, LANES), lambda hp, n: (jnp.maximum(n - 1, 0), hp))
    return _call(
        body, grid=(ATT_HEADS // 2, nq + 1),
        in_specs=_att_in_specs(nq) + [pl.BlockSpec((ATT_RB, LANES), lambda hp, n: (qn(n), dycol + hp))],
        out_specs=[pl.BlockSpec((ATT_RB, LANES), lambda hp, n: (qn(n), hp)), out_kv, out_kv,
                   pl.BlockSpec((2, ATT_QT, ATT_KT), lambda hp, n: (hp, 0, 0))],
        out_shape=[S((T, ATT_HEADS * ATT_D), BF16), S((T, ATT_HEADS * ATT_D), BF16),
                   S((T, ATT_HEADS * ATT_D), BF16), S((ATT_HEADS, ATT_QT, ATT_KT), F32)],
        scratch_shapes=[pltpu.VMEM((2 * ATT_RB, LANES), BF16), pltpu.VMEM((2 * ATT_RB, LANES), BF16),
                        pltpu.VMEM((2 * ATT_RB, LANES), F32), pltpu.VMEM((2 * ATT_RB, LANES), F32)],
        name="attention_bwd", sem=("parallel", "arbitrary"), args=(z, z, z, z, z, bias_t, dycat), plan=plan)


HALO = 8


def _causal_conv(ext, w_ref, b_ref):
    back2, back1 = pltpu.roll(ext, 2, 0), pltpu.roll(ext, 1, 0)
    zc = w_ref[0:1, :] * back2 + w_ref[1:2, :] * back1 + w_ref[2:3, :] * ext + b_ref[...]
    return zc[HALO:], back2, back1


def _ffn_down(z, cw, cb, wd, res, name, norm_g=None):
    T = z.shape[0]
    D = wd.shape[1]
    tb, tc = _tile(T, 1024, 8), 256
    nct = FFN_HIDDEN // tc
    per = tb // HALO16

    def body(gp_ref, g_ref, up_ref, u_ref, wg_ref, wu_ref, bg_ref, bu_ref, wd_ref, res_ref, *rest):
        ng_ref = rest[0] if norm_g is not None else None
        f_ref, gc_ref, uc_ref, o_ref = rest[norm_g is not None:][:4]
        hn_ref = rest[5] if norm_g is not None else None
        acc = rest[-1]
        first, j = pl.program_id(0) == 0, pl.program_id(1)

        def conv(p_ref, blk_ref, w_ref, b_ref):
            prev = jnp.where(first, 0.0, p_ref[...].astype(F32)[HALO16 - HALO:])
            return _causal_conv(jnp.concatenate([prev, blk_ref[...].astype(F32)], axis=0), w_ref, b_ref)[0]

        gc, uc = conv(gp_ref, g_ref, wg_ref, bg_ref), conv(up_ref, u_ref, wu_ref, bu_ref)
        gc_ref[...] = gc.astype(BF16)
        uc_ref[...] = uc.astype(BF16)
        f = (_gelu(gc) * uc).astype(BF16)
        f_ref[...] = f
        p = _dot(f, wd_ref[...])

        @pl.when(j == 0)
        def _():
            acc[...] = p

        @pl.when(j > 0)
        def _():
            acc[...] += p

        @pl.when(j == nct - 1)
        def _():
            v = acc[...] + res_ref[...]
            o_ref[...] = v
            if norm_g is not None:
                r = lax.rsqrt(jnp.mean(v * v, axis=-1, keepdims=True) + EPS)
                hn_ref[...] = (v * r * ng_ref[...]).astype(BF16)

    def zspecs(off):
        return [pl.BlockSpec((HALO16, tc), lambda i, j: (jnp.maximum(i * per - 1, 0), j + off)),
                pl.BlockSpec((tb, tc), lambda i, j: (i, j + off))]

    wspec = lambda off, rows: pl.BlockSpec((rows, tc), lambda i, j: (0, j + off))
    row = pl.BlockSpec((tb, D), lambda i, j: (i, 0))
    vec = pl.BlockSpec((1, D), lambda i, j: (0, 0))
    normed = norm_g is not None
    return pl.pallas_call(
        body, grid=(T // tb, nct),
        in_specs=zspecs(0) + zspecs(nct) + [wspec(0, 3), wspec(nct, 3), wspec(0, 1), wspec(nct, 1),
                                            pl.BlockSpec((tc, D), lambda i, j: (j, 0)), row] + [vec] * normed,
        out_specs=[pl.BlockSpec((tb, tc), lambda i, j: (i, j))] * 3 + [row] + [row] * normed,
        out_shape=[S((T, FFN_HIDDEN), BF16)] * 3 + [S((T, D), F32)] + [S((T, D), BF16)] * normed,
        scratch_shapes=[pltpu.VMEM((tb, D), F32)], name=name,
        compiler_params=_cparams("parallel", "arbitrary"))(z, z, z, z, cw, cw, cb, cb, wd, res, *([norm_g] * normed))


HALO16 = 16


def _convglu_bwd(z, gc, uc, df, cw, name):
    T = z.shape[0]
    tb, tc = _tile(T, 1024, 8), 256
    nct = FFN_HIDDEN // tc
    nrb = T // tb

    def body(g_ref, u_ref, gc_ref, gcn_ref, uc_ref, ucn_ref, df_ref, dfn_ref, wg_ref, wu_ref,
             dzg_ref, dzu_ref, dwg_ref, dwu_ref, dbg_ref, dbu_ref):
        i = pl.program_id(1)
        first, last = i == 0, i == nrb - 1

        @pl.when(first)
        def _():
            for ref in (dwg_ref, dwu_ref, dbg_ref, dbu_ref):
                ref[...] = jnp.zeros_like(ref)

        ext = lambda blk_ref, n_ref: jnp.concatenate([blk_ref[...].astype(F32), n_ref[...].astype(F32)[0:HALO]], axis=0)
        gcv, ucv = ext(gc_ref, gcn_ref), ext(uc_ref, ucn_ref)
        dfe = jnp.concatenate([df_ref[...], jnp.where(last, 0.0, dfn_ref[...])], axis=0)
        ge, gd = _gelu_and_grad(gcv)
        dgc, duc = dfe * ucv * gd, dfe * ge
        n = tb + HALO

        def back(d, z_ref, w_ref, dz_ref, dw_ref, db_ref):
            ahead1, ahead2 = pltpu.roll(d, n - 1, 0), pltpu.roll(d, n - 2, 0)
            dz_ref[...] = (w_ref[2:3, :] * d + w_ref[1:2, :] * ahead1 + w_ref[0:1, :] * ahead2)[:tb].astype(BF16)
            db_ref[...] += jnp.sum(d[:tb], axis=0, keepdims=True)
            zv = z_ref[...].astype(F32)
            for k, dk in enumerate((ahead2, ahead1, d)):
                dw_ref[k:k + 1, :] += jnp.sum(dk[:tb] * zv, axis=0, keepdims=True)

        back(dgc, g_ref, wg_ref, dzg_ref, dwg_ref, dbg_ref)
        back(duc, u_ref, wu_ref, dzu_ref, dwu_ref, dbu_ref)

    blk = pl.BlockSpec((tb, tc), lambda j, i: (i, j))
    after = lambda rows: pl.BlockSpec((rows, tc), lambda j, i: (jnp.minimum((i + 1) * (tb // rows), T // rows - 1), j))
    zspec = lambda off: pl.BlockSpec((tb, tc), lambda j, i: (i, j + off))
    wspec = lambda off: pl.BlockSpec((3, tc), lambda j, i: (0, j + off))
    acc = lambda rows: pl.BlockSpec((rows, tc), lambda j, i: (0, j))
    return pl.pallas_call(
        body, grid=(nct, nrb),
        in_specs=[zspec(0), zspec(nct), blk, after(HALO16), blk, after(HALO16), blk, after(HALO), wspec(0), wspec(nct)],
        out_specs=[blk, blk, acc(3), acc(3), acc(1), acc(1)],
        out_shape=[S((T, FFN_HIDDEN), BF16), S((T, FFN_HIDDEN), BF16), S((3, FFN_HIDDEN), F32), S((3, FFN_HIDDEN), F32),
                   S((1, FFN_HIDDEN), F32), S((1, FFN_HIDDEN), F32)],
        name=name, compiler_params=_cparams("parallel", "arbitrary"))(z, z, gc, gc, uc, uc, df, df, cw, cw)


SGU_RB = 256


def _sgu_weights(ws_ref):
    i = lax.broadcasted_iota(jnp.int32, (SGU_BLOCK, SGU_BLOCK), 0)
    j = lax.broadcasted_iota(jnp.int32, (SGU_BLOCK, SGU_BLOCK), 1)
    mask = (j < CHUNK) | (i >= CHUNK)
    return mask, [jnp.where(mask, ws_ref[g], 0.0).astype(BF16) for g in range(SGU_GROUPS)]


def _sgu_norm(zv, lng, lnb):
    mu = jnp.mean(zv, axis=-1, keepdims=True)
    vc = zv - mu
    rstd = lax.rsqrt(jnp.mean(vc * vc, axis=-1, keepdims=True) + EPS)
    vh = vc * rstd
    return vh, rstd, vh * lng + lnb


def _sgu_fwd(zpre, lng, lnb, ws, bst):
    T = zpre.shape[0]
    nb = SGU_RB // SGU_BLOCK

    def body(z_ref, lng_ref, lnb_ref, ws_ref, bst_ref, o_ref):
        _, wm = _sgu_weights(ws_ref)
        u = _gelu(z_ref[:, :SGU_WIDTH].astype(F32))
        _, _, vn = _sgu_norm(_gelu(z_ref[:, SGU_WIDTH:].astype(F32)), lng_ref[...], lnb_ref[...])
        vnb = vn.astype(BF16)
        for b in range(nb):
            rows = slice(b * SGU_BLOCK, (b + 1) * SGU_BLOCK)
            for g in range(SGU_GROUPS):
                cols = slice(g * SGU_GW, (g + 1) * SGU_GW)
                mixed = _dot(wm[g], vnb[rows, cols]) + bst_ref[:, g:g + 1]
                o_ref[rows, cols] = (u[rows, cols] * mixed).astype(BF16)

    vec = pl.BlockSpec((1, SGU_WIDTH), lambda i: (0, 0))
    return pl.pallas_call(
        body, grid=(T // SGU_RB,),
        in_specs=[pl.BlockSpec((SGU_RB, 2 * SGU_WIDTH), lambda i: (i, 0)), vec, vec,
                  pl.BlockSpec((SGU_GROUPS, SGU_BLOCK, SGU_BLOCK), lambda i: (0, 0, 0)),
                  pl.BlockSpec((SGU_BLOCK, SGU_GROUPS), lambda i: (0, 0))],
        out_specs=pl.BlockSpec((SGU_RB, SGU_WIDTH), lambda i: (i, 0)),
        out_shape=S((T, SGU_WIDTH), BF16), name="sgu_fwd", compiler_params=_cparams("parallel"))(zpre, lng, lnb, ws, bst)


def _sgu_bwd(zpre, dy, lng, lnb, ws, bst):
    T = zpre.shape[0]
    nb = SGU_RB // SGU_BLOCK

    def body(z_ref, dy_ref, lng_ref, lnb_ref, ws_ref, bst_ref, dz_ref, dws_ref, dbst_ref, dlng_ref, dlnb_ref, dvn):
        @pl.when(pl.program_id(0) == 0)
        def _():
            for ref in (dws_ref, dbst_ref, dlng_ref, dlnb_ref):
                ref[...] = jnp.zeros_like(ref)

        mask, wm = _sgu_weights(ws_ref)
        u, ud = _gelu_and_grad(z_ref[:, :SGU_WIDTH].astype(F32))
        v, vd = _gelu_and_grad(z_ref[:, SGU_WIDTH:].astype(F32))
        vh, rstd, vn = _sgu_norm(v, lng_ref[...], lnb_ref[...])
        vnb = vn.astype(BF16)
        lane8 = lax.broadcasted_iota(jnp.int32, (SGU_BLOCK, SGU_GROUPS), 1)
        dbs = jnp.zeros((SGU_BLOCK, SGU_GROUPS), F32)
        for b in range(nb):
            rows = slice(b * SGU_BLOCK, (b + 1) * SGU_BLOCK)
            for g in range(SGU_GROUPS):
                cols = slice(g * SGU_GW, (g + 1) * SGU_GW)
                vg = vnb[rows, cols]
                mixed = _dot(wm[g], vg) + bst_ref[:, g:g + 1]
                dyv = dy_ref[rows, cols]
                dz_ref[rows, cols] = (dyv * mixed * ud[rows, cols]).astype(BF16)
                dmix = dyv * u[rows, cols]
                dmb = dmix.astype(BF16)
                dvn[rows, cols] = _dot_tn(wm[g], dmb)
                dws_ref[g] += jnp.where(mask, _dot_nt(dmb, vg), 0.0)
                dbs = dbs + jnp.where(lane8 == g, jnp.sum(dmix, axis=-1, keepdims=True), 0.0)
        dbst_ref[...] += dbs
        dvnv = dvn[...]
        dlng_ref[...] += jnp.sum(dvnv * vh, axis=0, keepdims=True)
        dlnb_ref[...] += jnp.sum(dvnv, axis=0, keepdims=True)
        dvh = dvnv * lng_ref[...]
        dv = rstd * (dvh - jnp.mean(dvh, axis=-1, keepdims=True) - vh * jnp.mean(dvh * vh, axis=-1, keepdims=True))
        dz_ref[:, SGU_WIDTH:] = (dv * vd).astype(BF16)

    vec = pl.BlockSpec((1, SGU_WIDTH), lambda i: (0, 0))
    wsp = pl.BlockSpec((SGU_GROUPS, SGU_BLOCK, SGU_BLOCK), lambda i: (0, 0, 0))
    bsp = pl.BlockSpec((SGU_BLOCK, SGU_GROUPS), lambda i: (0, 0))
    return pl.pallas_call(
        body, grid=(T // SGU_RB,),
        in_specs=[pl.BlockSpec((SGU_RB, 2 * SGU_WIDTH), lambda i: (i, 0)),
                  pl.BlockSpec((SGU_RB, SGU_WIDTH), lambda i: (i, 0)), vec, vec, wsp, bsp],
        out_specs=[pl.BlockSpec((SGU_RB, 2 * SGU_WIDTH), lambda i: (i, 0)), wsp, bsp, vec, vec],
        out_shape=[S((T, 2 * SGU_WIDTH), BF16), S((SGU_GROUPS, SGU_BLOCK, SGU_BLOCK), F32),
                   S((SGU_BLOCK, SGU_GROUPS), F32), S((1, SGU_WIDTH), F32), S((1, SGU_WIDTH), F32)],
        scratch_shapes=[pltpu.VMEM((SGU_RB, SGU_WIDTH), F32)], name="sgu_bwd",
        compiler_params=_cparams("arbitrary"))(zpre, dy, lng, lnb, ws, bst)


def _loss_head(h, tgt, g):
    T, D = h.shape
    tr = _tile(T, 512, 8)

    def body(h_ref, t_ref, g_ref, ls_ref, dh_ref, dhb_ref, dg_ref):
        @pl.when(pl.program_id(0) == 0)
        def _():
            ls_ref[...] = jnp.zeros_like(ls_ref)
            dg_ref[...] = jnp.zeros_like(dg_ref)

        hv = h_ref[...]
        r = lax.rsqrt(jnp.mean(hv * hv, axis=-1, keepdims=True) + EPS)
        xh = hv * r
        diff = xh * g_ref[...] - t_ref[...]
        per_row = jnp.mean(diff * diff, axis=-1, keepdims=True)
        ls_ref[...] += jnp.sum(per_row, axis=0, keepdims=True)
        dy = diff * (1.0 / D)
        dg_ref[...] += jnp.sum(dy * xh, axis=0, keepdims=True)
        dxh = dy * g_ref[...]
        dh = r * (dxh - xh * jnp.mean(dxh * xh, axis=-1, keepdims=True))
        dh_ref[...] = dh
        dhb_ref[...] = dh.astype(BF16)

    row = pl.BlockSpec((tr, D), lambda i: (i, 0))
    vec = pl.BlockSpec((1, D), lambda i: (0, 0))
    return pl.pallas_call(
        body, grid=(T // tr,), in_specs=[row, row, vec],
        out_specs=[pl.BlockSpec((1, LANES), lambda i: (0, 0)), row, row, vec],
        out_shape=[S((1, LANES), F32), S((T, D), F32), S((T, D), BF16), S((1, D), F32)],
        name="loss_head", compiler_params=_cparams("arbitrary"))(h, tgt, g)


ANY = pl.BlockSpec(memory_space=pl.ANY)
COPY_PARTS = 4
SWAP_PARTS = 8
DMA = pltpu.SemaphoreType.DMA


def _place():
    return lax.axis_index("x"), lax.axis_index("y"), lax.axis_index("c")


def _nparts(rows, unit, want):
    n = want
    while n > 1 and rows % (unit * n):
        n //= 2
    return n


def _row_unit(dtype):
    return 16 if jnp.dtype(dtype).itemsize == 2 else 8


def _remote(src, dst, send_sems, recv_sems, k, to):
    return pltpu.make_async_remote_copy(src_ref=src, dst_ref=dst, send_sem=send_sems.at[k], recv_sem=recv_sems.at[k],
                                        device_id=to, device_id_type=MESH)


def _sem_ranges(counts):
    first, total = [], 0
    for c in counts:
        first.append(total)
        total += c
    return first, total


class _Gather:
    def __init__(self, shards):
        self.srcs = list(shards)
        self.halves = [a.shape[0] // 2 for a in shards]
        self.units = [_row_unit(a.dtype) for a in shards]
        self.parts = [_nparts(h, u, COPY_PARTS) for h, u in zip(self.halves, self.units)]
        self.first, total = _sem_ranges([3 * n for n in self.parts])
        self.out_shapes = [S((N_CHIPS,) + a.shape, a.dtype) for a in shards]
        self.scratch = [DMA((total,))] * 4
        self.has_relay = True

    def _ops(self, srcs, outs, sems):
        ici_s, ici_r, rel_s, rel_r = sems
        x, y, c = _place()
        me, sibling = (x, y, c), (x, y, 1 - c)
        chips = [(1 - x, y), (x, 1 - y), (1 - x, 1 - y)]
        send, arrive, relay, relayed = [], [], [], []
        for p_ref, out_ref, Rh, unit, n, base in zip(srcs, outs, self.halves, self.units, self.parts, self.first):
            rp = Rh // n

            def part(px, py, pc, k, out_ref=out_ref, Rh=Rh, unit=unit, rp=rp):
                return out_ref.at[2 * px + py, pl.ds(pl.multiple_of(pc * Rh + k * rp, unit), rp), :]

            def mine(k, p_ref=p_ref, Rh=Rh, unit=unit, rp=rp):
                return p_ref.at[pl.ds(pl.multiple_of(c * Rh + k * rp, unit), rp), :]

            for j, chip in enumerate(chips):
                for k in range(n):
                    s = base + j * n + k
                    send.append(_remote(mine(k), part(x, y, c, k), ici_s, ici_r, s, (*chip, c)))
                    arrive.append(_remote(mine(k), part(*chip, c, k), ici_s, ici_r, s, me))
                    relay.append(_remote(part(*chip, c, k), part(*chip, c, k), rel_s, rel_r, s, sibling))
                    relayed.append(_remote(part(*chip, c, k), part(*chip, 1 - c, k), rel_s, rel_r, s, me))
        return send, arrive, relay, relayed

    def start(self, *refs):
        for cp in self._ops(*refs)[0]:
            cp.start()

    def relay(self, *refs):
        _, arrive, relay, _ = self._ops(*refs)
        for a, r in zip(arrive, relay):
            a.wait_recv()
            r.start()

    def finish(self, *refs):
        send, _, relay, relayed = self._ops(*refs)
        for cp in relayed:
            cp.wait_recv()
        for cp in send + relay:
            cp.wait_send()


class _Exchange:
    def __init__(self, items):
        self.srcs = [a for a, _ in items]
        self.kinds = [k for _, k in items]
        self.blocks = []
        for a, kind in items:
            R, Ccols = a.shape
            self.blocks.append({"cols": (R // 2, Ccols // N_CHIPS), "rows": (R // (2 * N_CHIPS), Ccols), "all": (R, Ccols)}[kind])
        self.units = [_row_unit(a.dtype) for a in self.srcs]
        self.parts = [_nparts(b[0], u, COPY_PARTS) for b, u in zip(self.blocks, self.units)]
        self.first, total = _sem_ranges([N_DEV - 2 + n for n in self.parts])
        self.out_shapes = [S((N_DEV,) + b, a.dtype) for a, b in zip(self.srcs, self.blocks)]
        self.scratch = [DMA((total,))] * 2
        self.has_relay = False

    def _ops(self, srcs, outs, sems):
        ss, rs = sems
        x, y, c = _place()
        send, arrive = [], []
        for src_ref, out_ref, kind, (Rb, Cb), unit, n, base in zip(srcs, outs, self.kinds, self.blocks, self.units, self.parts,
                                                                  self.first):
            rp = Rb // n

            def block_for(px, py, pc, r0, rows, src_ref=src_ref, kind=kind, Rb=Rb, Cb=Cb, unit=unit):
                if kind == "cols":
                    return src_ref.at[pl.ds(pl.multiple_of(pc * Rb + r0, unit), rows),
                                      pl.ds(pl.multiple_of((2 * px + py) * Cb, LANES), Cb)]
                if kind == "rows":
                    return src_ref.at[pl.ds(pl.multiple_of((2 * (2 * px + py) + pc) * Rb + r0, unit), rows), :]
                return src_ref.at[pl.ds(r0, rows), :]

            def slot(d, r0, rows, out_ref=out_ref):
                return out_ref.at[d, pl.ds(r0, rows), :]

            me = 4 * x + 2 * y + c
            for k in range(1, N_DEV):
                peer = (x ^ ((k >> 2) & 1), y ^ ((k >> 1) & 1), c ^ (k & 1))
                pieces = [(N_DEV - 2 + q, q * rp, rp) for q in range(n)] if k == 1 else [(k - 2, 0, Rb)]
                for sem, r0, rows in pieces:
                    send.append(_remote(block_for(*peer, r0, rows), slot(me, r0, rows), ss, rs, base + sem, peer))
                    arrive.append(_remote(block_for(*peer, r0, rows), slot(4 * peer[0] + 2 * peer[1] + peer[2], r0, rows),
                                          ss, rs, base + sem, peer))
        return send, arrive

    def start(self, *refs):
        for cp in self._ops(*refs)[0]:
            cp.start()

    def finish(self, *refs):
        send, arrive = self._ops(*refs)
        for cp in arrive:
            cp.wait_recv()
        for cp in send:
            cp.wait_send()


class _Swap:
    def __init__(self, halves):
        self.srcs = list(halves)
        self.parts = [_nparts(a.shape[0], _row_unit(a.dtype), SWAP_PARTS) for a in halves]
        self.first, total = _sem_ranges(self.parts)
        self.out_shapes = [S(a.shape, a.dtype) for a in halves]
        self.scratch = [DMA((total,))] * 2
        self.has_relay = False

    def _ops(self, srcs, outs, sems):
        ss, rs = sems
        x, y, c = _place()
        copies = []
        for h_ref, out_ref, n, base in zip(srcs, outs, self.parts, self.first):
            rp = h_ref.shape[0] // n
            for k in range(n):
                rows = pl.ds(k * rp, rp)
                copies.append(_remote(h_ref.at[rows, :], out_ref.at[rows, :], ss, rs, base + k, (x, y, 1 - c)))
        return copies, copies

    start = _Exchange.start
    finish = _Exchange.finish


def _run_plan(plan, name):
    ni, no = len(plan.srcs), len(plan.out_shapes)

    def body(*refs):
        parts = (refs[:ni], refs[ni:ni + no], refs[ni + no:])
        plan.start(*parts)
        if plan.has_relay:
            plan.relay(*parts)
        plan.finish(*parts)

    return pl.pallas_call(body, out_shape=plan.out_shapes, in_specs=[ANY] * ni, out_specs=[ANY] * no,
                          scratch_shapes=plan.scratch, name=name)(*plan.srcs)


def _call(body, *, grid, in_specs, out_specs, out_shape, name, sem, args, scratch_shapes=(), plan=None):
    if plan is None:
        return pl.pallas_call(body, grid=grid, in_specs=in_specs, out_specs=out_specs, out_shape=out_shape,
                              scratch_shapes=list(scratch_shapes), name=name, compiler_params=_cparams(*sem))(*args), None
    n_in, n_out, n_scr = len(in_specs), len(out_shape), len(scratch_shapes)
    pi, po = len(plan.srcs), len(plan.out_shapes)
    total = math.prod(grid)

    def wrapped(*refs):
        a, refs = refs[:n_in], refs[n_in:]
        pa, refs = refs[:pi], refs[pi:]
        o, refs = refs[:n_out], refs[n_out:]
        pout, refs = refs[:po], refs[po:]
        scr, psem = refs[:n_scr], refs[n_scr:]
        step = 0
        for d, gsize in enumerate(grid):
            step = step * gsize + pl.program_id(d)

        @pl.when(step == 0)
        def _():
            plan.start(pa, pout, psem)

        body(*a, *o, *scr)
        if plan.has_relay:
            @pl.when(step == (3 * total) // 4)
            def _():
                plan.relay(pa, pout, psem)

        @pl.when(step == total - 1)
        def _():
            plan.finish(pa, pout, psem)

    outs = pl.pallas_call(
        wrapped, grid=grid, in_specs=list(in_specs) + [ANY] * pi, out_specs=list(out_specs) + [ANY] * po,
        out_shape=list(out_shape) + plan.out_shapes, scratch_shapes=list(scratch_shapes) + plan.scratch, name=name,
        compiler_params=_cparams(*["arbitrary"] * len(grid)))(*args, *plan.srcs)
    return outs[:n_out], outs[n_out:]


SMEM = pl.BlockSpec(memory_space=pltpu.SMEM)


def _sum_slots(buf, own, me, name):
    n, R, W = buf.shape
    tr = _tile(R, 256, 8)

    def body(me_ref, b_ref, own_ref, o_ref):
        acc = None
        for s in range(n):
            blk = jnp.where(me_ref[0] == s, own_ref[...], b_ref[s]).astype(F32)
            acc = blk if acc is None else acc + blk
        o_ref[...] = acc

    return pl.pallas_call(
        body, grid=(R // tr,),
        in_specs=[SMEM, pl.BlockSpec((n, tr, W), lambda i: (0, i, 0)), pl.BlockSpec((tr, W), lambda i: (i, 0))],
        out_specs=pl.BlockSpec((tr, W), lambda i: (i, 0)), out_shape=S((R, W), F32), name=name,
        compiler_params=_cparams("parallel"))(me, buf, own)


def _adamw_update(wv, gv, mv, vv):
    mn = ADAM_B1 * mv + (1.0 - ADAM_B1) * gv
    vn = ADAM_B2 * vv + (1.0 - ADAM_B2) * (gv * gv)
    m_hat = mn / (1.0 - ADAM_B1 ** ADAM_STEP)
    v_hat = vn / (1.0 - ADAM_B2 ** ADAM_STEP)
    return -ADAM_LR * (m_hat / (jnp.sqrt(v_hat) + ADAM_EPS) + ADAM_WD * wv), mn, vn


def _adamw(w, g, m, v, name):
    R, W = w.shape
    tr = _tile(R, 256, 8)

    def body(w_ref, g_ref, m_ref, v_ref, d_ref, mo_ref, vo_ref):
        d_ref[...], mo_ref[...], vo_ref[...] = _adamw_update(w_ref[...], g_ref[...], m_ref[...], v_ref[...])

    blk = pl.BlockSpec((tr, W), lambda i: (i, 0))
    return pl.pallas_call(
        body, grid=(R // tr,), in_specs=[blk] * 4, out_specs=[blk] * 3, out_shape=[S((R, W), F32)] * 3, name=name,
        compiler_params=_cparams("parallel"))(w, g, m, v)


def _adamw_shard(w, halves, core, m, v, name):
    L, R, C = w.shape
    Rh = R // 2
    tr = _tile(Rh, 256, 8)
    nbh = Rh // tr

    def body(c_ref, w_ref, m_ref, v_ref, *rest):
        pairs, (g_ref, d_ref, mo_ref, vo_ref) = rest[:2 * L], rest[2 * L:]
        l, i = pl.program_id(0), pl.program_id(1)
        mine_rows = i // nbh == c_ref[0]
        gv = None
        for lp in range(L):
            cand = jnp.where(mine_rows, pairs[2 * lp][...], pairs[2 * lp + 1][...])
            gv = cand if gv is None else jnp.where(l == lp, cand, gv)
        g_ref[...] = gv
        d_ref[...], mo_ref[...], vo_ref[...] = _adamw_update(w_ref[...], gv, m_ref[...], v_ref[...])

    blk = pl.BlockSpec((None, tr, C), lambda l, i: (l, i, 0))
    half = lambda lp: pl.BlockSpec((tr, C), lambda l, i: (jnp.where(l == lp, i % nbh, 0), 0))
    return pl.pallas_call(
        body, grid=(L, R // tr), in_specs=[SMEM, blk, blk, blk] + [half(lp) for lp in range(L) for _ in range(2)],
        out_specs=[blk] * 4, out_shape=[S((L, R, C), F32)] * 4, name=name,
        compiler_params=_cparams("parallel", "parallel"))(core, w, m, v, *[h for pair in halves for h in pair])


def _tables(T):
    f32 = F32
    half = RET_QK // 2
    inv = 1.0 / (10000.0 ** jnp.linspace(0.0, 1.0, half, dtype=f32))
    ang = jnp.arange(T).astype(f32)[:, None] * inv[None, :]
    cos, sin = jnp.cos(ang), jnp.sin(ang)
    c2 = jnp.concatenate([cos, cos], axis=-1)
    s2 = jnp.concatenate([-sin, sin], axis=-1)
    log_g = jnp.log1p(-jnp.exp2(-5.0 - jnp.arange(RET_HEADS, dtype=f32)))
    idx = jnp.arange(CHUNK, dtype=f32)
    dintra = jnp.exp(log_g[:, None, None] * jnp.abs(idx[:, None] - idx[None, :]))
    kdec = jnp.exp(log_g[None, :] * (CHUNK - 1 - idx)[:, None]).T
    qdec = jnp.exp(log_g[None, :] * (idx + 1.0)[:, None]).T
    cdec = jnp.exp(log_g * CHUNK)
    bc = lambda a, w: jnp.broadcast_to(a[:, :, None], (RET_HEADS, a.shape[1], w))
    return c2, s2, dintra, bc(qdec, RET_QK), bc(kdec, RET_QK), jnp.broadcast_to(cdec[:, None, None], (RET_HEADS, 1, RET_V))


def _first_forms(g4):
    return {"ab_w_in4": g4["ab_w_in"],
            "ab_w_inT": jnp.transpose(g4["ab_w_in"], (0, 2, 1)).reshape(-1, D_MODEL),
            "ab_w_out": g4["ab_w_out"].reshape(-1, D_MODEL)}


def _late_forms(g4):
    wd = g4["ffn_w_down"]
    per = wd.shape[1] // 2
    return {"c_w_in4": g4["c_w_in"], "c_w_out": g4["c_w_out"].reshape(-1, D_MODEL), "ffn_w_up4": g4["ffn_w_up"],
            "ffn_w_down": [wd[:, l * per:(l + 1) * per].reshape(-1, D_MODEL) for l in range(2)]}


def _local_step(x, tgt, p, first=None, late=None, exchange=False):
    T = x.shape[0]
    tab = _tables(T)
    row = lambda a: a.reshape(1, -1)
    tr = lambda w: jnp.transpose(w)
    width = D_MODEL

    hn0, first_out = _rmsnorm_fwd(x, row(p["attn_norm_g"][0]), "norm_a0", plan=first[0] if first else None)
    if first:
        p = {**p, **first[1](first_out)}
    z0 = _mm(hn0, p["ab_w_in4"], "mm_ab_in", form="kn4", out_dtype=BF16)
    ya, r, st = _retention_fwd(z0, *tab)
    bias_t = _bias_tiles(jnp.transpose(_bias_build(p["ab_rel_bias"][0]), (1, 0, 2))[:, :, :BAND])
    yb, late_out = _attention_fwd(z0, bias_t, plan=late[0] if late else None)
    if late:
        p = {**p, **late[1](late_out)}
    h1, hf0 = _mm([ya, yb], p["ab_w_out"], "mm_ab_out", res=x, norm_g=row(p["ffn_norm_g"][0]))

    def ffn_fwd(h, hf, l, next_g):
        zf = _mm(hf, p["ffn_w_up4"], f"mm_up{l}", form="kn4", row0=l * width, out_dtype=BF16)
        f, gc, uc, *out = _ffn_down(zf, p["ffn_conv_w"][l], row(p["ffn_conv_b"][l]), p["ffn_w_down"][l], h, f"ffn_down{l}",
                                    norm_g=next_g)
        return (zf, f, gc, uc), (tuple(out) if next_g is not None else out[0])

    kept0, (h2, hn1) = ffn_fwd(h1, hf0, 0, row(p["attn_norm_g"][1]))
    zc = _mm(hn1, p["c_w_in4"], "mm_c_in", form="kn4", out_dtype=BF16)
    lng, lnb, bst, ws = row(p["c_ln_g"][0]), row(p["c_ln_b"][0]), tr(p["c_b_s"][0]), p["c_w_s"][0]
    y1 = _sgu_fwd(zc, lng, lnb, ws, bst)
    h3, hf1 = _mm(y1, p["c_w_out"], "mm_c_out", res=h2, norm_g=row(p["ffn_norm_g"][1]))
    kept1, h4 = ffn_fwd(h3, hf1, 1, None)
    lsum, dh4, dh4b, dgfin = _loss_head(h4, tgt, row(p["final_norm_g"]))

    g, big = {}, {}

    def ffn_bwd(dh, dhb, h_in, hf, kept, l):
        zf, f, gc, uc = kept
        big[f"ffn_w_down{l}"] = _mm_tn(f, dhb, f"mmt_down{l}", out_dtype=BF16)
        df = _mm(dhb, p["ffn_w_down"][l], f"mmb_down{l}", form="nk")
        dzg, dzu, dwg, dwu, dbg, dbu = _convglu_bwd(zf, gc, uc, df, p["ffn_conv_w"][l], f"convglu_bwd{l}")
        big[f"ffn_w_up{l}"] = _mm_tn(hf, [dzg, dzu], f"mmt_up{l}", out_dtype=BF16)
        dhf = _mm([dzg, dzu], p["ffn_w_up4"], f"mmb_up{l}", form="nk4", row0=l * width, rows=width, out_dtype=BF16)
        dh_in, dh_in_b, dgf = _rmsnorm_bwd(h_in, dhf, row(p["ffn_norm_g"][l]), dh, f"norm_f{l}_bwd")
        return dh_in, dh_in_b, dict(ffn_norm_g=dgf[0], ffn_conv_w=jnp.concatenate([dwg, dwu], axis=1),
                                    ffn_conv_b=jnp.concatenate([dbg, dbu], axis=1)[0])

    dh3, dh3b, gf1 = ffn_bwd(dh4, dh4b, h3, hf1, kept1, 1)
    big["c_w_out"] = _mm_tn(y1, dh3b, "mmt_c_out", out_dtype=BF16)
    dy1 = _mm(dh3b, p["c_w_out"], "mmb_c_out", form="nk")
    dzc, dws, dbst, dlng, dlnb = _sgu_bwd(zc, dy1, lng, lnb, ws, bst)
    g["c_w_s"], g["c_b_s"], g["c_ln_g"], g["c_ln_b"] = dws[None], tr(dbst)[None], dlng, dlnb
    big["c_w_in"] = _mm_tn(hn1, dzc, "mmt_c_in", out_dtype=BF16)
    dhn1 = _mm(dzc, p["c_w_in4"], "mmb_c_in", form="nk4", rows=width, out_dtype=BF16)
    dh2, dh2b, dga1 = _rmsnorm_bwd(h2, dhn1, row(p["attn_norm_g"][1]), dh3, "norm_a1_bwd")
    dh1, dh1b, gf0 = ffn_bwd(dh2, dh2b, h1, hf0, kept0, 0)
    for k in gf0:
        g[k] = jnp.stack([gf0[k], gf1[k]])
    big["ab_w_out"] = _mm_tn([ya, yb], dh1b, "mmt_ab_out", out_dtype=BF16)
    dycat = _mm(dh1b, p["ab_w_out"], "mmb_ab_out", form="nk")
    g["final_norm_g"] = dgfin[0]
    early_pack = _pack([dga1[0]] + [g[n] for n in EARLY_SMALL], 32)
    late_plan = _Exchange([(big[n], kind) for n, kind in LATE_ITEMS] + [(early_pack, "all")]) if exchange else None
    (dqb, dkb, dvb, dbias_t), late_slots = _attention_bwd(z0, bias_t, dycat, plan=late_plan)
    dqa, dka, dva, dga = _retention_bwd(z0, *tab, r, dycat, st)
    dz0 = [dqa, dka, dva, dga, dqb, dkb, dvb]
    big["ab_w_in"] = _mm_tn(hn0, dz0, "mmt_ab_in", out_dtype=BF16)
    slots = {}
    if exchange:
        dhn0, first_slots = _mm(dz0, p["ab_w_inT"], "mmb_ab_in", out_dtype=BF16,
                                plan=_Exchange([(big[n], kind) for n, kind in FIRST_ITEMS]))
        slots = dict(first=first_slots, late=late_slots[:-1], early=(early_pack, late_slots[-1]))
    else:
        dhn0 = _mm(dz0, p["ab_w_inT"], "mmb_ab_in", out_dtype=BF16)
    gx, _, dga0 = _rmsnorm_bwd(x, dhn0, row(p["attn_norm_g"][0]), dh1, "norm_a0_bwd")
    g["ab_rel_bias"] = _bias_grad(_bias_bands(dbias_t))[None, :, :N_REL]
    g["attn_norm_g"] = jnp.stack([dga0[0], dga1[0]])
    return lsum[0, 0], gx, g, big, slots


FIRST_BIG = ["ab_w_in", "ab_w_out"]
LATE_BIG = ["c_w_in", "c_w_out", "ffn_w_up", "ffn_w_down"]
BIG = FIRST_BIG + LATE_BIG
FIRST_ITEMS = [("ab_w_in", "cols"), ("ab_w_out", "rows")]
LATE_ITEMS = [("c_w_in", "cols"), ("c_w_out", "rows"), ("ffn_w_up0", "cols"), ("ffn_w_up1", "cols"),
              ("ffn_w_down0", "rows"), ("ffn_w_down1", "rows")]
LAYERS_OF = {"ab_w_in": ["ab_w_in"], "ab_w_out": ["ab_w_out"], "c_w_in": ["c_w_in"], "c_w_out": ["c_w_out"],
             "ffn_w_up": ["ffn_w_up0", "ffn_w_up1"], "ffn_w_down": ["ffn_w_down0", "ffn_w_down1"]}
SMALL_SHARDED = [("c_ln_g", 1), ("c_ln_b", 1), ("ffn_conv_w", 2)]
REPLICATED = ["attn_norm_g", "ffn_norm_g", "ab_rel_bias", "c_w_s", "c_b_s", "ffn_conv_b", "final_norm_g"]
EARLY_SMALL = ["ffn_norm_g", "c_w_s", "c_b_s", "ffn_conv_b", "final_norm_g", "c_ln_g", "c_ln_b", "ffn_conv_w"]


def _rows_of(n_elems):
    return -(-n_elems // PACK_W)


def _flat_rows(a):
    f = a.reshape(-1)
    rows = _rows_of(f.shape[0])
    return jnp.pad(f, (0, rows * PACK_W - f.shape[0])).reshape(rows, PACK_W)


def _pad_rows(a, mult):
    extra = (-a.shape[0]) % mult
    return jnp.pad(a, ((0, extra), (0, 0))) if extra else a


def _pack(arrs, mult):
    return _pad_rows(jnp.concatenate([_flat_rows(a) for a in arrs], axis=0), mult)


def _unpack(buf, shapes):
    out, r = [], 0
    for shp in shapes:
        n = math.prod(shp)
        rows = _rows_of(n)
        out.append(buf[r:r + rows].reshape(-1)[:n].reshape(shp))
        r += rows
    return out


def _from_shards(sh, axis):
    m = jnp.moveaxis(sh, 0, axis)
    shp = m.shape
    return m.reshape(shp[:axis] + (shp[axis] * shp[axis + 1],) + shp[axis + 2:])


def _as_bf16_pairs(a):
    return lax.bitcast_convert_type(a.astype(F32), BF16)


def _from_bf16_pairs(a):
    return lax.bitcast_convert_type(a, F32)


def kernel(x, attn_norm_g, ffn_norm_g, ab_w_in, ab_w_out, ab_rel_bias, c_w_in, c_ln_g, c_ln_b, c_w_s, c_b_s, c_w_out, ffn_w_up, ffn_conv_w, ffn_conv_b, ffn_w_down, final_norm_g, loss_target, m_attn_norm_g, m_ffn_norm_g, m_ab_w_in, m_ab_w_out, m_ab_rel_bias, m_c_w_in, m_c_ln_g, m_c_ln_b, m_c_w_s, m_c_b_s, m_c_w_out, m_ffn_w_up, m_ffn_conv_w, m_ffn_conv_b, m_ffn_w_down, m_final_norm_g, v_attn_norm_g, v_ffn_norm_g, v_ab_w_in, v_ab_w_out, v_ab_rel_bias, v_c_w_in, v_c_ln_g, v_c_ln_b, v_c_w_s, v_c_b_s, v_c_w_out, v_ffn_w_up, v_ffn_conv_w, v_ffn_conv_b, v_ffn_w_down, v_final_norm_g):
    w = dict(attn_norm_g=attn_norm_g, ffn_norm_g=ffn_norm_g, ab_w_in=ab_w_in, ab_w_out=ab_w_out, ab_rel_bias=ab_rel_bias,
             c_w_in=c_w_in, c_ln_g=c_ln_g, c_ln_b=c_ln_b, c_w_s=c_w_s, c_b_s=c_b_s, c_w_out=c_w_out, ffn_w_up=ffn_w_up,
             ffn_conv_w=ffn_conv_w, ffn_conv_b=ffn_conv_b, ffn_w_down=ffn_w_down, final_norm_g=final_norm_g)
    m = dict(attn_norm_g=m_attn_norm_g, ffn_norm_g=m_ffn_norm_g, ab_w_in=m_ab_w_in, ab_w_out=m_ab_w_out,
             ab_rel_bias=m_ab_rel_bias, c_w_in=m_c_w_in, c_ln_g=m_c_ln_g, c_ln_b=m_c_ln_b, c_w_s=m_c_w_s, c_b_s=m_c_b_s,
             c_w_out=m_c_w_out, ffn_w_up=m_ffn_w_up, ffn_conv_w=m_ffn_conv_w, ffn_conv_b=m_ffn_conv_b,
             ffn_w_down=m_ffn_w_down, final_norm_g=m_final_norm_g)
    v = dict(attn_norm_g=v_attn_norm_g, ffn_norm_g=v_ffn_norm_g, ab_w_in=v_ab_w_in, ab_w_out=v_ab_w_out,
             ab_rel_bias=v_ab_rel_bias, c_w_in=v_c_w_in, c_ln_g=v_c_ln_g, c_ln_b=v_c_ln_b, c_w_s=v_c_w_s, c_b_s=v_c_b_s,
             c_w_out=v_c_w_out, ffn_w_up=v_ffn_w_up, ffn_conv_w=v_ffn_conv_w, ffn_conv_b=v_ffn_conv_b,
             ffn_w_down=v_ffn_w_down, final_norm_g=v_final_norm_g)
    names = list(w)
    chip = 2 * lax.axis_index("x") + lax.axis_index("y")

    core = lax.axis_index("c")
    core_arr = core.reshape(1).astype(jnp.int32)
    me_arr = (2 * chip + core).reshape(1).astype(jnp.int32)
    two_d = lambda a: a.reshape(-1, a.shape[-1])
    with_own = lambda gathered, own: lax.dynamic_update_slice(gathered, own[None], (chip, 0, 0))

    send_first = [two_d(w[n]).astype(BF16) for n in FIRST_BIG]
    finish_first = lambda got: _first_forms({n: with_own(a, own) for n, a, own in zip(FIRST_BIG, got, send_first)})
    full = {n: w[n] for n in REPLICATED}
    small_send = [_as_bf16_pairs(w[n]) for n, _ in SMALL_SHARDED]
    send_late = [two_d(w[n]).astype(BF16) for n in LATE_BIG] + [_pack(small_send, 32)]

    def finish_late(got):
        whole = [with_own(a, own) for a, own in zip(got, send_late)]
        forms = _late_forms(dict(zip(LATE_BIG, whole)))
        parts = [_unpack(whole[-1][s], [a.shape for a in small_send]) for s in range(N_CHIPS)]
        for i, (n, axis) in enumerate(SMALL_SHARDED):
            forms[n] = _from_shards(_from_bf16_pairs(jnp.stack([parts[s][i] for s in range(N_CHIPS)])), axis)
        return forms

    lsum, grad_x, g, big, slots = _local_step(x[0], loss_target[0], full, first=(_Gather(send_first), finish_first),
                                              late=(_Gather(send_late), finish_late), exchange=True)
    loss = lax.psum(0.5 * lsum, ("x", "y", "c"))

    def own_block(a, kind):
        rows, cols = a.shape
        if kind == "cols":
            return lax.dynamic_slice(a, (core * (rows // 2), chip * (cols // N_CHIPS)), (rows // 2, cols // N_CHIPS))
        per = rows // N_DEV
        return lax.dynamic_slice(a, ((2 * chip + core) * per, 0), (per, cols))

    reduced = {}
    for key, items in (("late", LATE_ITEMS), ("first", FIRST_ITEMS)):
        halves = [_sum_slots(got, own_block(big[n], kind), me_arr, f"sum_{n}") for (n, kind), got in zip(items, slots[key])]
        others = _run_plan(_Swap(halves), f"swap_{key}")
        reduced.update({n: (h, o) for (n, _), h, o in zip(items, halves, others)})
    big_outs = [{}, {}, {}, {}]
    for n in BIG:
        res = _adamw_shard(w[n], [reduced[layer] for layer in LAYERS_OF[n]], core_arr, m[n], v[n], f"adamw_{n}")
        for k in range(4):
            big_outs[k][n] = res[k]

    small_names = REPLICATED + [n for n, _ in SMALL_SHARDED]
    early_pack, early_slots = slots["early"]
    early = _unpack(_sum_slots(early_slots, early_pack, me_arr, "sum_early"), [(D_MODEL,)] + [g[n].shape for n in EARLY_SMALL])
    last_pack = _pack([g["attn_norm_g"][0], g["ab_rel_bias"]], 8)
    last_slots = _run_plan(_Exchange([(last_pack, "all")]), "exchange_small")[0]
    norm_a0, rel_bias = _unpack(_sum_slots(last_slots, last_pack, me_arr, "sum_small"), [(D_MODEL,), g["ab_rel_bias"].shape])
    gsmall_full = dict(zip(EARLY_SMALL, early[1:]), attn_norm_g=jnp.stack([norm_a0, early[0]]), ab_rel_bias=rel_bias)
    for n, axis in SMALL_SHARDED:
        size = w[n].shape[axis]
        gsmall_full[n] = lax.dynamic_slice_in_dim(gsmall_full[n], chip * size, size, axis)
    pack_small = lambda d: _pack([d[n] for n in small_names], 8)
    small_out = _adamw(pack_small(w), pack_small(gsmall_full), pack_small(m), pack_small(v), "adamw_small")
    small_shapes = [w[n].shape for n in small_names]

    outs = [{**big_outs[0], **gsmall_full}]
    for k in range(3):
        outs.append({**big_outs[k + 1], **dict(zip(small_names, _unpack(small_out[k], small_shapes)))})
    return (loss, grad_x[None], *[o[n] for o in outs for n in names])
```

```python
import functools
import math

import jax
import jax.numpy as jnp
from jax import lax
from jax.experimental import pallas as pl
from jax.experimental.pallas import tpu as pltpu

F32 = jnp.float32
BF16 = jnp.bfloat16
S = jax.ShapeDtypeStruct
MESH = pl.DeviceIdType.MESH

D_MODEL = 1024
CHUNK = 64
EPS = 1e-6
NEG_INF = -1e30
RET_HEADS, RET_QK, RET_V = 4, 128, 256
ATT_HEADS, ATT_D, ATT_PAST, MAX_REL = 8, 64, 8, 128
BAND = (ATT_PAST + 1) * CHUNK
PADK = ATT_PAST * CHUNK
SGU_BLOCK, SGU_GROUPS, SGU_WIDTH = 128, 8, 2048
SGU_GW = SGU_WIDTH // SGU_GROUPS
FFN_HIDDEN = 2816
N_REL = 2 * MAX_REL + 1
RET_SCALE = RET_QK ** -0.5
ATT_SCALE = ATT_D ** -0.5
ADAM_LR, ADAM_B1, ADAM_B2, ADAM_EPS, ADAM_WD, ADAM_STEP = 0.001, 0.9, 0.999, 1e-08, 0.01, 10

V7X_VMEM_BYTES = 64 * 1024 * 1024
VMEM_LIMIT = V7X_VMEM_BYTES * 7 // 8
MM_TILE_BUDGET = V7X_VMEM_BYTES * 11 // 16
LANES = 128
PACK_W = 1024
N_CHIPS = 4
N_DEV = 8

GELU_C = math.sqrt(2.0 / math.pi)
GELU_A = 0.044715


def _cparams(*sem):
    return pltpu.CompilerParams(dimension_semantics=tuple(sem) if sem else None, vmem_limit_bytes=VMEM_LIMIT)


def _tile(n, target, unit=LANES):
    best = None
    for t in range(unit, min(n, target) + 1, unit):
        if n % t == 0:
            best = t
    return best if best is not None else n


def _gelu(x):
    t = jnp.tanh(GELU_C * (x + GELU_A * x * x * x))
    return 0.5 * x * (1.0 + t)


def _gelu_and_grad(x):
    x2 = x * x
    t = jnp.tanh(GELU_C * (x + GELU_A * x2 * x))
    g = 0.5 * x * (1.0 + t)
    dg = 0.5 * (1.0 + t) + 0.5 * x * (1.0 - t * t) * (GELU_C * (1.0 + 3.0 * GELU_A * x2))
    return g, dg


def _sigmoid(x):
    return 1.0 / (1.0 + jnp.exp(-x))


def _dot(a, b):
    return jnp.dot(a, b, preferred_element_type=F32)


def _dot_nt(a, b):
    return lax.dot_general(a, b, (((1,), (1,)), ((), ())), preferred_element_type=F32)


def _dot_tn(a, b):
    return lax.dot_general(a, b, (((0,), (0,)), ((), ())), preferred_element_type=F32)


def _rmsnorm_fwd(x, g, name, plan=None):
    T, D = x.shape
    tr = _tile(T, 512, 8)

    def body(x_ref, g_ref, o_ref):
        xv = x_ref[...]
        r = lax.rsqrt(jnp.mean(xv * xv, axis=-1, keepdims=True) + EPS)
        o_ref[...] = (xv * r * g_ref[...]).astype(o_ref.dtype)

    (out,), extra = _call(
        body, grid=(T // tr,),
        in_specs=[pl.BlockSpec((tr, D), lambda i: (i, 0)), pl.BlockSpec((1, D), lambda i: (0, 0))],
        out_specs=[pl.BlockSpec((tr, D), lambda i: (i, 0))],
        out_shape=[S((T, D), BF16)], name=name, sem=("parallel",), args=(x, g), plan=plan)
    return out, extra


def _pieces(a):
    return list(a) if isinstance(a, (list, tuple)) else [a]


def _piece_layout(widths, tile):
    out, s = [], 0
    for w in widths:
        out.append((s, w // tile))
        s += w // tile
    return out


def _common_tile(widths, target):
    return _tile(functools.reduce(math.gcd, widths), target)


def _rmsnorm_bwd(x, dy, g, dres, name):
    T, D = x.shape
    tr = _tile(T, 512, 8)

    def body(x_ref, dy_ref, g_ref, dres_ref, dx_ref, dxb_ref, dg_ref):
        @pl.when(pl.program_id(0) == 0)
        def _():
            dg_ref[...] = jnp.zeros_like(dg_ref)

        xv = x_ref[...]
        r = lax.rsqrt(jnp.mean(xv * xv, axis=-1, keepdims=True) + EPS)
        xh = xv * r
        dyv = dy_ref[...].astype(F32)
        dg_ref[...] += jnp.sum(dyv * xh, axis=0, keepdims=True)
        dxh = dyv * g_ref[...]
        dx = dres_ref[...] + r * (dxh - xh * jnp.mean(dxh * xh, axis=-1, keepdims=True))
        dx_ref[...] = dx
        dxb_ref[...] = dx.astype(BF16)

    row = pl.BlockSpec((tr, D), lambda i: (i, 0))
    vec = pl.BlockSpec((1, D), lambda i: (0, 0))
    return pl.pallas_call(
        body, grid=(T // tr,), in_specs=[row, row, vec, row], out_specs=[row, row, vec],
        out_shape=[S((T, D), F32), S((T, D), BF16), S((1, D), F32)], name=name,
        compiler_params=_cparams("arbitrary"))(x, dy, g, dres)


def _mm(a, b, name, res=None, out_dtype=F32, plan=None, norm_g=None, form="kn", row0=0, rows=None):
    pieces = _pieces(a)
    M = pieces[0].shape[0]
    widths = [p.shape[1] for p in pieces]
    K = sum(widths)
    N = {"kn": lambda: b.shape[1], "kn4": lambda: N_CHIPS * b.shape[2], "nk": lambda: b.shape[0], "nk4": lambda: rows}[form]()
    tn = N if norm_g is not None else (b.shape[2] if form == "kn4" else _tile(N, 1408))
    tk = b.shape[2] if form == "nk4" else _common_tile(widths, 1536)
    def vmem_bytes(rows):
        out_bytes = jnp.dtype(out_dtype).itemsize + 2 * (norm_g is not None)
        blocks = len(pieces) * rows * tk * 2 + tk * tn * 2 + rows * tn * (4 * (res is not None) + out_bytes)
        return 2 * blocks + rows * tn * 4 * (K != tk)

    tm = _tile(M, 2048 if vmem_bytes(2048) <= MM_TILE_BUDGET else 1024, 8)
    assert all(w % tk == 0 for w in widths) and row0 % (tk if form == "kn4" else tn) == 0
    nk, npc = K // tk, len(pieces)
    layout = _piece_layout(widths, tk)
    tile = pl.BlockSpec((tm, tn), lambda i, j, k: (i, j))
    vec = pl.BlockSpec((1, tn), lambda i, j, k: (0, j))
    b_spec = {"kn": lambda: pl.BlockSpec((tk, tn), lambda i, j, k: (k, j)),
              "kn4": lambda: pl.BlockSpec((None, tk, tn), lambda i, j, k: (j, row0 // tk + k, 0)),
              "nk": lambda: pl.BlockSpec((tn, tk), lambda i, j, k: (j, k)),
              "nk4": lambda: pl.BlockSpec((None, tn, tk), lambda i, j, k: (k, row0 // tn + j, 0))}[form]()
    dot = _dot if form in ("kn", "kn4") else _dot_nt
    extra_in, extra_specs = [], []
    if res is not None:
        extra_in, extra_specs = [res], [tile]
    if norm_g is not None:
        extra_in, extra_specs = extra_in + [norm_g], extra_specs + [vec]
    n_extra = len(extra_in)
    if norm_g is not None:
        out_shape, out_specs = [S((M, N), out_dtype), S((M, N), BF16)], [tile, tile]
    else:
        out_shape, out_specs = [S((M, N), out_dtype)], [tile]

    def body(*refs):
        a_refs, b_ref = refs[:npc], refs[npc]
        ext = list(refs[npc + 1:npc + 1 + n_extra])
        outs = refs[npc + 1 + n_extra:npc + 1 + n_extra + len(out_shape)]

        def finish(v):
            if res is not None:
                v = v + ext[0][...]
            outs[0][...] = v.astype(outs[0].dtype)
            if norm_g is not None:
                r = lax.rsqrt(jnp.mean(v * v, axis=-1, keepdims=True) + EPS)
                outs[1][...] = (v * r * ext[-1][...]).astype(BF16)

        if nk == 1:
            finish(dot(a_refs[0][...], b_ref[...]))
            return
        acc = refs[-1]
        k = pl.program_id(2)
        for a_ref, (s, c) in zip(a_refs, layout):
            def add(a_ref=a_ref):
                acc[...] += dot(a_ref[...], b_ref[...])

            if s == 0:
                @pl.when(k == 0)
                def _(a_ref=a_ref):
                    acc[...] = dot(a_ref[...], b_ref[...])

                if c > 1:
                    pl.when((k > 0) & (k < c))(add)
            else:
                pl.when((k >= s) & (k < s + c))(add)

        @pl.when(k == nk - 1)
        def _():
            finish(acc[...])

    in_specs = [pl.BlockSpec((tm, tk), lambda i, j, k, s=s, c=c: (i, jnp.clip(k - s, 0, c - 1))) for s, c in layout]
    outs, extra = _call(
        body, grid=(M // tm, N // tn, nk), in_specs=in_specs + [b_spec] + extra_specs, out_specs=out_specs,
        out_shape=out_shape, scratch_shapes=[pltpu.VMEM((tm, tn), F32)] if nk > 1 else [],
        name=name, sem=("parallel", "parallel", "arbitrary"), args=pieces + [b] + extra_in, plan=plan)
    outs = outs[0] if len(outs) == 1 else tuple(outs)
    return outs if plan is None else (outs, extra)


def _mm_tn(a, g, name, out_dtype=F32):
    ap, gp = _pieces(a), _pieces(g)
    T = ap[0].shape[0]
    aw, gw = [p.shape[1] for p in ap], [p.shape[1] for p in gp]
    tm, tn = _common_tile(aw, 1408), _common_tile(gw, 1408)
    narrow = out_dtype != F32
    tall = 2 * 2 * 2048 * (len(ap) * tm + len(gp) * tn) + tm * tn * (2 * jnp.dtype(out_dtype).itemsize + 4 * narrow)
    tt = _tile(T, 2048 if tall <= MM_TILE_BUDGET else 1024, 8)
    alay, glay = _piece_layout(aw, tm), _piece_layout(gw, tn)
    na = len(ap)

    def inside(idx, s, c, single):
        return None if single else (idx >= s) & (idx < s + c)

    nt = T // tt

    def body(*refs):
        a_refs, g_refs = refs[:na], refs[na:na + len(gp)]
        o_ref = refs[na + len(gp)]
        acc = refs[-1] if narrow else o_ref
        i, j, k = pl.program_id(0), pl.program_id(1), pl.program_id(2)

        @pl.when(k == 0)
        def _():
            acc[...] = jnp.zeros_like(acc)

        for a_ref, (sa, ca) in zip(a_refs, alay):
            for g_ref, (sg, cg) in zip(g_refs, glay):
                def add(a_ref=a_ref, g_ref=g_ref):
                    acc[...] += _dot_tn(a_ref[...], g_ref[...])

                conds = [c for c in (inside(i, sa, ca, na == 1), inside(j, sg, cg, len(gp) == 1)) if c is not None]
                if not conds:
                    add()
                else:
                    pl.when(functools.reduce(lambda u, v: u & v, conds))(add)

        if narrow:
            @pl.when(k == nt - 1)
            def _():
                o_ref[...] = acc[...].astype(out_dtype)

    def spec(tile, lay, single, axis):
        s, c = lay

        def index(i, j, k):
            idx = (i, j)[axis]
            if single:
                return (k, idx)
            on = (idx >= s) & (idx < s + c)
            return (jnp.where(on, k, 0), jnp.clip(idx - s, 0, c - 1))

        return pl.BlockSpec((tt, tile), index)

    in_specs = [spec(tm, lay, na == 1, 0) for lay in alay] + [spec(tn, lay, len(gp) == 1, 1) for lay in glay]
    return pl.pallas_call(
        body, grid=(sum(aw) // tm, sum(gw) // tn, nt), in_specs=in_specs,
        out_specs=pl.BlockSpec((tm, tn), lambda i, j, k: (i, j)),
        out_shape=S((sum(aw), sum(gw)), out_dtype), scratch_shapes=[pltpu.VMEM((tm, tn), F32)] if narrow else [],
        name=name, compiler_params=_cparams("parallel", "parallel", "arbitrary"))(*ap, *gp)


def _rotate(x, c2, s2):
    return x * c2 + pltpu.roll(x, RET_QK // 2, 1) * s2


def _unrotate(d, c2, s2):
    return d * c2 - pltpu.roll(d, RET_QK // 2, 1) * s2


RET_PAIR = 4
RET_STEPS = RET_HEADS // RET_PAIR


def _ret_specs(RB, blockmap):
    qk, vg = RET_PAIR * RET_QK, RET_PAIR * RET_V
    q = pl.BlockSpec((RB, qk), lambda h, n: (blockmap(n), h))
    k = pl.BlockSpec((RB, qk), lambda h, n: (blockmap(n), RET_STEPS + h))
    v = pl.BlockSpec((RB, vg), lambda h, n: (blockmap(n), RET_STEPS + h))
    g = pl.BlockSpec((RB, vg), lambda h, n: (blockmap(n), 2 * RET_STEPS + h))
    tab = pl.BlockSpec((RB, RET_QK), lambda h, n: (blockmap(n), 0))
    return q, k, v, g, tab


def _ret_decay_specs():
    return [pl.BlockSpec((RET_PAIR, CHUNK, CHUNK), lambda h, n: (h, 0, 0)),
            pl.BlockSpec((RET_PAIR, CHUNK, RET_QK), lambda h, n: (h, 0, 0)),
            pl.BlockSpec((RET_PAIR, CHUNK, RET_QK), lambda h, n: (h, 0, 0)),
            pl.BlockSpec((RET_PAIR, 1, RET_V), lambda h, n: (h, 0, 0))]


def _ret_cols(e):
    return slice(e * RET_QK, (e + 1) * RET_QK), slice(e * RET_V, (e + 1) * RET_V)


def _retention_fwd(z, c2, s2, dintra, qdec, kdec, cdec):
    T = z.shape[0]
    RB = min(512, T)
    nch, nb = RB // CHUNK, T // RB

    def body(q_ref, k_ref, v_ref, g_ref, c2_ref, s2_ref, di_ref, qd_ref, kd_ref, cd_ref, ya_ref, r_ref, st_ref, state):
        @pl.when(pl.program_id(1) == 0)
        def _():
            state[...] = jnp.zeros_like(state)

        for c in range(nch):
            rows = slice(c * CHUNK, (c + 1) * CHUNK)
            c2v, s2v = c2_ref[rows, :], s2_ref[rows, :]
            for e in range(RET_PAIR):
                qk, vg = _ret_cols(e)
                dmat, qdv, kdv, cdv = di_ref[e], qd_ref[e], kd_ref[e], cd_ref[e]
                qr = _rotate(q_ref[rows, qk].astype(F32), c2v, s2v)
                kr = _rotate(k_ref[rows, qk].astype(F32), c2v, s2v) * RET_SCALE
                vb = v_ref[rows, vg].astype(BF16)
                sm = _dot_nt(qr.astype(BF16), kr.astype(BF16)) * dmat
                sb = state[e].astype(BF16)
                st_ref[e, c] = sb
                o = _dot(sm.astype(BF16), vb) + _dot((qr * qdv).astype(BF16), sb)
                state[e] = state[e] * cdv + _dot_tn((kr * kdv).astype(BF16), vb)
                r_ref[rows, vg] = o
                mu = jnp.mean(o, axis=-1, keepdims=True)
                oc = o - mu
                rn = oc * lax.rsqrt(jnp.mean(oc * oc, axis=-1, keepdims=True) + EPS)
                gv = g_ref[rows, vg].astype(F32)
                ya_ref[rows, vg] = (gv * _sigmoid(gv) * rn).astype(BF16)

    q, k, v, g, tab = _ret_specs(RB, lambda n: n)
    wide = pl.BlockSpec((RB, RET_PAIR * RET_V), lambda h, n: (n, h))
    return pl.pallas_call(
        body, grid=(RET_STEPS, nb),
        in_specs=[q, k, v, g, tab, tab] + _ret_decay_specs(),
        out_specs=[wide, wide, pl.BlockSpec((RET_PAIR, nch, RET_QK, RET_V), lambda h, n: (h, n, 0, 0))],
        out_shape=[S((T, RET_HEADS * RET_V), BF16), S((T, RET_HEADS * RET_V), F32),
                   S((RET_HEADS, T // CHUNK, RET_QK, RET_V), BF16)],
        scratch_shapes=[pltpu.VMEM((RET_PAIR, RET_QK, RET_V), F32)], name="retention_fwd",
        compiler_params=_cparams("parallel", "arbitrary"))(z, z, z, z, c2, s2, dintra, qdec, kdec, cdec)


def _retention_bwd(z, c2, s2, dintra, qdec, kdec, cdec, r, dycat, st):
    T = z.shape[0]
    RB = min(512, T)
    nch, nb = RB // CHUNK, T // RB

    def body(q_ref, k_ref, v_ref, g_ref, c2_ref, s2_ref, di_ref, qd_ref, kd_ref, cd_ref, r_ref, dy_ref, st_ref,
             dq_ref, dk_ref, dv_ref, dg_ref, dstate):
        @pl.when(pl.program_id(1) == 0)
        def _():
            dstate[...] = jnp.zeros_like(dstate)

        for c in reversed(range(nch)):
            rows = slice(c * CHUNK, (c + 1) * CHUNK)
            c2v, s2v = c2_ref[rows, :], s2_ref[rows, :]
            for e in range(RET_PAIR):
                qk, vg = _ret_cols(e)
                dmat, qdv, kdv, cdv = di_ref[e], qd_ref[e], kd_ref[e], cd_ref[e]
                qr = _rotate(q_ref[rows, qk].astype(F32), c2v, s2v)
                kr = _rotate(k_ref[rows, qk].astype(F32), c2v, s2v) * RET_SCALE
                qb, kb = qr.astype(BF16), kr.astype(BF16)
                vb = v_ref[rows, vg].astype(BF16)
                o, gv, dy = r_ref[rows, vg], g_ref[rows, vg].astype(F32), dy_ref[rows, vg].astype(F32)
                mu = jnp.mean(o, axis=-1, keepdims=True)
                oc = o - mu
                rstd = lax.rsqrt(jnp.mean(oc * oc, axis=-1, keepdims=True) + EPS)
                rn = oc * rstd
                sg = _sigmoid(gv)
                dg_ref[rows, vg] = (dy * rn * (sg * (1.0 + gv * (1.0 - sg)))).astype(BF16)
                drn = dy * (gv * sg)
                do = rstd * (drn - jnp.mean(drn, axis=-1, keepdims=True) - rn * jnp.mean(drn * rn, axis=-1, keepdims=True))
                dob = do.astype(BF16)
                sm = (_dot_nt(qb, kb) * dmat).astype(BF16)
                kdb = (kr * kdv).astype(BF16)
                dsb = dstate[e].astype(BF16)
                dv_ref[rows, vg] = (_dot_tn(sm, dob) + _dot(kdb, dsb)).astype(BF16)
                ds = (_dot_nt(dob, vb) * dmat).astype(BF16)
                dqr = _dot(ds, kb) + _dot_nt(dob, st_ref[e, c]) * qdv
                dkr = (_dot_tn(ds, qb) + _dot_nt(vb, dsb) * kdv) * RET_SCALE
                dstate[e] = dstate[e] * cdv + _dot_tn((qr * qdv).astype(BF16), dob)
                dq_ref[rows, qk] = _unrotate(dqr, c2v, s2v).astype(BF16)
                dk_ref[rows, qk] = _unrotate(dkr, c2v, s2v).astype(BF16)

    rev = lambda n: nb - 1 - n
    q, k, v, g, tab = _ret_specs(RB, rev)
    wide = pl.BlockSpec((RB, RET_PAIR * RET_V), lambda h, n: (rev(n), h))
    narrow = pl.BlockSpec((RB, RET_PAIR * RET_QK), lambda h, n: (rev(n), h))
    return pl.pallas_call(
        body, grid=(RET_STEPS, nb),
        in_specs=[q, k, v, g, tab, tab] + _ret_decay_specs() + [
            wide, wide, pl.BlockSpec((RET_PAIR, nch, RET_QK, RET_V), lambda h, n: (h, rev(n), 0, 0))],
        out_specs=[narrow, narrow, wide, wide],
        out_shape=[S((T, RET_HEADS * RET_QK), BF16), S((T, RET_HEADS * RET_QK), BF16),
                   S((T, RET_HEADS * RET_V), BF16), S((T, RET_HEADS * RET_V), BF16)],
        scratch_shapes=[pltpu.VMEM((RET_PAIR, RET_QK, RET_V), F32)], name="retention_bwd",
        compiler_params=_cparams("parallel", "arbitrary"))(z, z, z, z, c2, s2, dintra, qdec, kdec, cdec, r, dycat, st)


def _rel_index(i):
    r = lax.broadcasted_iota(jnp.int32, (3 * LANES, 5 * LANES), 0)
    j = lax.broadcasted_iota(jnp.int32, (3 * LANES, 5 * LANES), 1)
    idx = jnp.clip(i + PADK - j, -MAX_REL, MAX_REL) + MAX_REL
    return (r == idx).astype(BF16)


def _split3(v):
    hi = v.astype(BF16)
    r1 = v - hi.astype(F32)
    mid = r1.astype(BF16)
    lo = (r1 - mid.astype(F32)).astype(BF16)
    return hi, mid, lo


def _bias_build(rb):
    rbp = jnp.pad(rb, ((0, 0), (0, 3 * LANES - N_REL)))

    def body(rb_ref, o_ref):
        e = _rel_index(pl.program_id(0))
        hi, mid, lo = _split3(rb_ref[...])
        o_ref[...] = _dot(hi, e) + _dot(mid, e) + _dot(lo, e)

    return pl.pallas_call(
        body, grid=(CHUNK,), in_specs=[pl.BlockSpec((ATT_HEADS, 3 * LANES), lambda i: (0, 0))],
        out_specs=pl.BlockSpec((None, ATT_HEADS, 5 * LANES), lambda i: (i, 0, 0)),
        out_shape=S((CHUNK, ATT_HEADS, 5 * LANES), F32), name="bias_build",
        compiler_params=_cparams("parallel"))(rbp)


ATT_RB = 512
ATT_QT = 256
ATT_CPT = ATT_QT // CHUNK
ATT_KT = ATT_QT + PADK
ATT_QCOL = (2 * RET_HEADS * RET_QK + 2 * RET_HEADS * RET_V) // LANES
ATT_KCOL = ATT_QCOL + ATT_HEADS * ATT_D // LANES
ATT_VCOL = ATT_KCOL + ATT_HEADS * ATT_D // LANES


def _bias_grad(dbt):
    def body(d_ref, o_ref):
        @pl.when(pl.program_id(0) == 0)
        def _():
            o_ref[...] = jnp.zeros_like(o_ref)

        e = _rel_index(pl.program_id(0))
        d = d_ref[0]
        for ci in range(1, ATT_CPT):
            d = d + d_ref[ci]
        hi, mid, lo = _split3(d)
        o_ref[...] += _dot_nt(hi, e) + _dot_nt(mid, e) + _dot_nt(lo, e)

    return pl.pallas_call(
        body, grid=(CHUNK,),
        in_specs=[pl.BlockSpec((ATT_CPT, None, ATT_HEADS, 5 * LANES), lambda i: (0, i, 0, 0))],
        out_specs=pl.BlockSpec((ATT_HEADS, 3 * LANES), lambda i: (0, 0)),
        out_shape=S((ATT_HEADS, 3 * LANES), F32), name="bias_grad",
        compiler_params=_cparams("arbitrary"))(dbt)


def _bias_tiles(bias):
    parts = [jnp.pad(bias, ((0, 0), (0, 0), (CHUNK * ci, ATT_KT - BAND - CHUNK * ci)), constant_values=NEG_INF)
             for ci in range(ATT_CPT)]
    return jnp.stack(parts, axis=1).reshape(ATT_HEADS, ATT_QT, ATT_KT)


def _bias_bands(dbias_tiles):
    d = dbias_tiles.reshape(ATT_HEADS, ATT_CPT, CHUNK, ATT_KT)
    bands = jnp.stack([d[:, ci, :, CHUNK * ci:CHUNK * ci + BAND] for ci in range(ATT_CPT)])
    return jnp.pad(jnp.transpose(bands, (0, 2, 1, 3)), ((0, 0), (0, 0), (0, 0), (0, 5 * LANES - BAND)))


def _att_fill(kw, vw, klo, khi, vlo, vhi):
    kw[0:ATT_RB, :] = klo[...].astype(BF16)
    kw[ATT_RB:, :] = khi[...].astype(BF16)
    vw[0:ATT_RB, :] = vlo[...].astype(BF16)
    vw[ATT_RB:, :] = vhi[...].astype(BF16)


def _att_probs(qm, kwin, bias, first_key):
    s = _dot_nt(qm, kwin) * ATT_SCALE + bias
    col = lax.broadcasted_iota(jnp.int32, (ATT_QT, ATT_KT), 1)
    s = jnp.where(col + first_key >= 0, s, NEG_INF)
    p = jnp.exp(s - jnp.max(s, axis=-1, keepdims=True))
    return p / jnp.sum(p, axis=-1, keepdims=True)


def _att_in_specs(nq):
    qn = lambda n: jnp.minimum(n, nq - 1)
    blk = lambda col, back: pl.BlockSpec((ATT_RB, LANES), lambda hp, n: (jnp.maximum(qn(n) - back, 0), col + hp))
    return [blk(ATT_QCOL, 0), blk(ATT_KCOL, 1), blk(ATT_KCOL, 0), blk(ATT_VCOL, 1), blk(ATT_VCOL, 0),
            pl.BlockSpec((2, ATT_QT, ATT_KT), lambda hp, n: (hp, 0, 0))]


def _attention_fwd(z, bias_t, plan=None):
    T = z.shape[0]
    nq = T // ATT_RB

    def body(q_ref, klo, khi, vlo, vhi, b_ref, o_ref, kw, vw):
        _att_fill(kw, vw, klo, khi, vlo, vhi)
        lane = lax.broadcasted_iota(jnp.int32, (ATT_QT, LANES), 1)
        n = pl.program_id(1)
        for t in range(ATT_RB // ATT_QT):
            rows = slice(t * ATT_QT, (t + 1) * ATT_QT)
            win = slice(t * ATT_QT, t * ATT_QT + ATT_KT)
            qc = q_ref[rows, :]
            outs = []
            for e in range(2):
                qm = jnp.where((lane >= ATT_D) == (e == 1), qc, 0.0).astype(BF16)
                p = _att_probs(qm, kw[win, :], b_ref[e], (n - 1) * ATT_RB + t * ATT_QT)
                outs.append(_dot(p.astype(BF16), vw[win, :]))
            o_ref[rows, :] = jnp.where(lane < ATT_D, outs[0], outs[1]).astype(BF16)

    (yb,), extra = _call(
        body, grid=(ATT_HEADS // 2, nq), in_specs=_att_in_specs(nq),
        out_specs=[pl.BlockSpec((ATT_RB, LANES), lambda hp, n: (n, hp))],
        out_shape=[S((T, ATT_HEADS * ATT_D), BF16)],
        scratch_shapes=[pltpu.VMEM((2 * ATT_RB, LANES), BF16), pltpu.VMEM((2 * ATT_RB, LANES), BF16)],
        name="attention_fwd", sem=("parallel", "parallel"), args=(z, z, z, z, z, bias_t), plan=plan)
    return yb, extra


def _attention_bwd(z, bias_t, dycat, plan=None):
    T = z.shape[0]
    nq = T // ATT_RB
    dycol = RET_HEADS * RET_V // LANES

    def body(q_ref, klo, khi, vlo, vhi, b_ref, dy_ref, dq_ref, dk_ref, dv_ref, db_ref, kw, vw, dkw, dvw):
        n = pl.program_id(1)

        @pl.when(n == 0)
        def _():
            dkw[...] = jnp.zeros_like(dkw)
            dvw[...] = jnp.zeros_like(dvw)
            db_ref[...] = jnp.zeros_like(db_ref)

        @pl.when(n > 0)
        def _():
            dkw[0:ATT_RB, :] = dkw[ATT_RB:, :]
            dvw[0:ATT_RB, :] = dvw[ATT_RB:, :]
            dkw[ATT_RB:, :] = jnp.zeros((ATT_RB, LANES), F32)
            dvw[ATT_RB:, :] = jnp.zeros((ATT_RB, LANES), F32)

        @pl.when(n < nq)
        def _():
            _att_fill(kw, vw, klo, khi, vlo, vhi)
            lane = lax.broadcasted_iota(jnp.int32, (ATT_QT, LANES), 1)
            for t in range(ATT_RB // ATT_QT):
                rows = slice(t * ATT_QT, (t + 1) * ATT_QT)
                win = slice(t * ATT_QT, t * ATT_QT + ATT_KT)
                qc, dyc = q_ref[rows, :], dy_ref[rows, :]
                kwin, vwin = kw[win, :], vw[win, :]
                dq = jnp.zeros((ATT_QT, LANES), F32)
                for e in range(2):
                    mine = (lane >= ATT_D) == (e == 1)
                    qm = jnp.where(mine, qc, 0.0).astype(BF16)
                    dom = jnp.where(mine, dyc, 0.0).astype(BF16)
                    p = _att_probs(qm, kwin, b_ref[e], (n - 1) * ATT_RB + t * ATT_QT)
                    dp = _dot_nt(dom, vwin)
                    ds = p * (dp - jnp.sum(dp * p, axis=-1, keepdims=True))
                    db_ref[e] += ds
                    dsb = (ds * ATT_SCALE).astype(BF16)
                    dq = dq + jnp.where(mine, _dot(dsb, kwin), 0.0)
                    dkw[win, :] += _dot_tn(dsb, qm)
                    dvw[win, :] += _dot_tn(p.astype(BF16), dom)
                dq_ref[rows, :] = dq.astype(BF16)

        dk_ref[...] = dkw[0:ATT_RB, :].astype(BF16)
        dv_ref[...] = dvw[0:ATT_RB, :].astype(BF16)

    qn = lambda n: jnp.minimum(n, nq - 1)
    out_kv = pl.BlockSpec((ATT_RB, LANES), lambda hp, n: (jnp.maximum(n - 1, 0), hp))
    return _call(
        body, grid=(ATT_HEADS // 2, nq + 1),
        in_specs=_att_in_specs(nq) + [pl.BlockSpec((ATT_RB, LANES), lambda hp, n: (qn(n), dycol + hp))],
        out_specs=[pl.BlockSpec((ATT_RB, LANES), lambda hp, n: (qn(n), hp)), out_kv, out_kv,
                   pl.BlockSpec((2, ATT_QT, ATT_KT), lambda hp, n: (hp, 0, 0))],
        out_shape=[S((T, ATT_HEADS * ATT_D), BF16), S((T, ATT_HEADS * ATT_D), BF16),
                   S((T, ATT_HEADS * ATT_D), BF16), S((ATT_HEADS, ATT_QT, ATT_KT), F32)],
        scratch_shapes=[pltpu.VMEM((2 * ATT_RB, LANES), BF16), pltpu.VMEM((2 * ATT_RB, LANES), BF16),
                        pltpu.VMEM((2 * ATT_RB, LANES), F32), pltpu.VMEM((2 * ATT_RB, LANES), F32)],
        name="attention_bwd", sem=("parallel", "arbitrary"), args=(z, z, z, z, z, bias_t, dycat), plan=plan)


HALO = 8


def _causal_conv(ext, w_ref, b_ref):
    back2, back1 = pltpu.roll(ext, 2, 0), pltpu.roll(ext, 1, 0)
    zc = w_ref[0:1, :] * back2 + w_ref[1:2, :] * back1 + w_ref[2:3, :] * ext + b_ref[...]
    return zc[HALO:], back2, back1


def _ffn_down(z, cw, cb, wd, res, name, norm_g=None):
    T = z.shape[0]
    D = wd.shape[1]
    tb, tc = _tile(T, 1024, 8), 256
    nct = FFN_HIDDEN // tc
    per = tb // HALO16

    def body(gp_ref, g_ref, up_ref, u_ref, wg_ref, wu_ref, bg_ref, bu_ref, wd_ref, res_ref, *rest):
        ng_ref = rest[0] if norm_g is not None else None
        f_ref, gc_ref, uc_ref, o_ref = rest[norm_g is not None:][:4]
        hn_ref = rest[5] if norm_g is not None else None
        acc = rest[-1]
        first, j = pl.program_id(0) == 0, pl.program_id(1)

        def conv(p_ref, blk_ref, w_ref, b_ref):
            prev = jnp.where(first, 0.0, p_ref[...].astype(F32)[HALO16 - HALO:])
            return _causal_conv(jnp.concatenate([prev, blk_ref[...].astype(F32)], axis=0), w_ref, b_ref)[0]

        gc, uc = conv(gp_ref, g_ref, wg_ref, bg_ref), conv(up_ref, u_ref, wu_ref, bu_ref)
        gc_ref[...] = gc.astype(BF16)
        uc_ref[...] = uc.astype(BF16)
        f = (_gelu(gc) * uc).astype(BF16)
        f_ref[...] = f
        p = _dot(f, wd_ref[...])

        @pl.when(j == 0)
        def _():
            acc[...] = p

        @pl.when(j > 0)
        def _():
            acc[...] += p

        @pl.when(j == nct - 1)
        def _():
            v = acc[...] + res_ref[...]
            o_ref[...] = v
            if norm_g is not None:
                r = lax.rsqrt(jnp.mean(v * v, axis=-1, keepdims=True) + EPS)
                hn_ref[...] = (v * r * ng_ref[...]).astype(BF16)

    def zspecs(off):
        return [pl.BlockSpec((HALO16, tc), lambda i, j: (jnp.maximum(i * per - 1, 0), j + off)),
                pl.BlockSpec((tb, tc), lambda i, j: (i, j + off))]

    wspec = lambda off, rows: pl.BlockSpec((rows, tc), lambda i, j: (0, j + off))
    row = pl.BlockSpec((tb, D), lambda i, j: (i, 0))
    vec = pl.BlockSpec((1, D), lambda i, j: (0, 0))
    normed = norm_g is not None
    return pl.pallas_call(
        body, grid=(T // tb, nct),
        in_specs=zspecs(0) + zspecs(nct) + [wspec(0, 3), wspec(nct, 3), wspec(0, 1), wspec(nct, 1),
                                            pl.BlockSpec((tc, D), lambda i, j: (j, 0)), row] + [vec] * normed,
        out_specs=[pl.BlockSpec((tb, tc), lambda i, j: (i, j))] * 3 + [row] + [row] * normed,
        out_shape=[S((T, FFN_HIDDEN), BF16)] * 3 + [S((T, D), F32)] + [S((T, D), BF16)] * normed,
        scratch_shapes=[pltpu.VMEM((tb, D), F32)], name=name,
        compiler_params=_cparams("parallel", "arbitrary"))(z, z, z, z, cw, cw, cb, cb, wd, res, *([norm_g] * normed))


HALO16 = 16


def _convglu_bwd(z, gc, uc, df, cw, name):
    T = z.shape[0]
    tb, tc = _tile(T, 1024, 8), 256
    nct = FFN_HIDDEN // tc
    nrb = T // tb

    def body(g_ref, u_ref, gc_ref, gcn_ref, uc_ref, ucn_ref, df_ref, dfn_ref, wg_ref, wu_ref,
             dzg_ref, dzu_ref, dwg_ref, dwu_ref, dbg_ref, dbu_ref):
        i = pl.program_id(1)
        first, last = i == 0, i == nrb - 1

        @pl.when(first)
        def _():
            for ref in (dwg_ref, dwu_ref, dbg_ref, dbu_ref):
                ref[...] = jnp.zeros_like(ref)

        ext = lambda blk_ref, n_ref: jnp.concatenate([blk_ref[...].astype(F32), n_ref[...].astype(F32)[0:HALO]], axis=0)
        gcv, ucv = ext(gc_ref, gcn_ref), ext(uc_ref, ucn_ref)
        dfe = jnp.concatenate([df_ref[...].astype(F32), jnp.where(last, 0.0, dfn_ref[...].astype(F32)[0:HALO])], axis=0)
        ge, gd = _gelu_and_grad(gcv)
        dgc, duc = dfe * ucv * gd, dfe * ge
        n = tb + HALO

        def back(d, z_ref, w_ref, dz_ref, dw_ref, db_ref):
            ahead1, ahead2 = pltpu.roll(d, n - 1, 0), pltpu.roll(d, n - 2, 0)
            dz_ref[...] = (w_ref[2:3, :] * d + w_ref[1:2, :] * ahead1 + w_ref[0:1, :] * ahead2)[:tb].astype(BF16)
            db_ref[...] += jnp.sum(d[:tb], axis=0, keepdims=True)
            zv = z_ref[...].astype(F32)
            for k, dk in enumerate((ahead2, ahead1, d)):
                dw_ref[k:k + 1, :] += jnp.sum(dk[:tb] * zv, axis=0, keepdims=True)

        back(dgc, g_ref, wg_ref, dzg_ref, dwg_ref, dbg_ref)
        back(duc, u_ref, wu_ref, dzu_ref, dwu_ref, dbu_ref)

    blk = pl.BlockSpec((tb, tc), lambda j, i: (i, j))
    after = lambda rows: pl.BlockSpec((rows, tc), lambda j, i: (jnp.minimum((i + 1) * (tb // rows), T // rows - 1), j))
    zspec = lambda off: pl.BlockSpec((tb, tc), lambda j, i: (i, j + off))
    wspec = lambda off: pl.BlockSpec((3, tc), lambda j, i: (0, j + off))
    acc = lambda rows: pl.BlockSpec((rows, tc), lambda j, i: (0, j))
    return pl.pallas_call(
        body, grid=(nct, nrb),
        in_specs=[zspec(0), zspec(nct), blk, after(HALO16), blk, after(HALO16), blk, after(HALO16), wspec(0), wspec(nct)],
        out_specs=[blk, blk, acc(3), acc(3), acc(1), acc(1)],
        out_shape=[S((T, FFN_HIDDEN), BF16), S((T, FFN_HIDDEN), BF16), S((3, FFN_HIDDEN), F32), S((3, FFN_HIDDEN), F32),
                   S((1, FFN_HIDDEN), F32), S((1, FFN_HIDDEN), F32)],
        name=name, compiler_params=_cparams("parallel", "arbitrary"))(z, z, gc, gc, uc, uc, df, df, cw, cw)


SGU_RB = 256


def _sgu_weights(ws_ref):
    i = lax.broadcasted_iota(jnp.int32, (SGU_BLOCK, SGU_BLOCK), 0)
    j = lax.broadcasted_iota(jnp.int32, (SGU_BLOCK, SGU_BLOCK), 1)
    mask = (j < CHUNK) | (i >= CHUNK)
    return mask, [jnp.where(mask, ws_ref[g], 0.0).astype(BF16) for g in range(SGU_GROUPS)]


def _sgu_norm(zv, lng, lnb):
    mu = jnp.mean(zv, axis=-1, keepdims=True)
    vc = zv - mu
    rstd = lax.rsqrt(jnp.mean(vc * vc, axis=-1, keepdims=True) + EPS)
    vh = vc * rstd
    return vh, rstd, vh * lng + lnb


def _sgu_fwd(zpre, lng, lnb, ws, bst):
    T = zpre.shape[0]
    nb = SGU_RB // SGU_BLOCK

    def body(z_ref, lng_ref, lnb_ref, ws_ref, bst_ref, o_ref):
        _, wm = _sgu_weights(ws_ref)
        u = _gelu(z_ref[:, :SGU_WIDTH].astype(F32))
        _, _, vn = _sgu_norm(_gelu(z_ref[:, SGU_WIDTH:].astype(F32)), lng_ref[...], lnb_ref[...])
        vnb = vn.astype(BF16)
        for b in range(nb):
            rows = slice(b * SGU_BLOCK, (b + 1) * SGU_BLOCK)
            for g in range(SGU_GROUPS):
                cols = slice(g * SGU_GW, (g + 1) * SGU_GW)
                mixed = _dot(wm[g], vnb[rows, cols]) + bst_ref[:, g:g + 1]
                o_ref[rows, cols] = (u[rows, cols] * mixed).astype(BF16)

    vec = pl.BlockSpec((1, SGU_WIDTH), lambda i: (0, 0))
    return pl.pallas_call(
        body, grid=(T // SGU_RB,),
        in_specs=[pl.BlockSpec((SGU_RB, 2 * SGU_WIDTH), lambda i: (i, 0)), vec, vec,
                  pl.BlockSpec((SGU_GROUPS, SGU_BLOCK, SGU_BLOCK), lambda i: (0, 0, 0)),
                  pl.BlockSpec((SGU_BLOCK, SGU_GROUPS), lambda i: (0, 0))],
        out_specs=pl.BlockSpec((SGU_RB, SGU_WIDTH), lambda i: (i, 0)),
        out_shape=S((T, SGU_WIDTH), BF16), name="sgu_fwd", compiler_params=_cparams("parallel"))(zpre, lng, lnb, ws, bst)


def _sgu_bwd(zpre, dy, lng, lnb, ws, bst):
    T = zpre.shape[0]
    nb = SGU_RB // SGU_BLOCK

    def body(z_ref, dy_ref, lng_ref, lnb_ref, ws_ref, bst_ref, dz_ref, dws_ref, dbst_ref, dlng_ref, dlnb_ref, dvn):
        @pl.when(pl.program_id(0) == 0)
        def _():
            for ref in (dws_ref, dbst_ref, dlng_ref, dlnb_ref):
                ref[...] = jnp.zeros_like(ref)

        mask, wm = _sgu_weights(ws_ref)
        u, ud = _gelu_and_grad(z_ref[:, :SGU_WIDTH].astype(F32))
        v, vd = _gelu_and_grad(z_ref[:, SGU_WIDTH:].astype(F32))
        vh, rstd, vn = _sgu_norm(v, lng_ref[...], lnb_ref[...])
        vnb = vn.astype(BF16)
        lane8 = lax.broadcasted_iota(jnp.int32, (SGU_BLOCK, SGU_GROUPS), 1)
        dbs = jnp.zeros((SGU_BLOCK, SGU_GROUPS), F32)
        for b in range(nb):
            rows = slice(b * SGU_BLOCK, (b + 1) * SGU_BLOCK)
            for g in range(SGU_GROUPS):
                cols = slice(g * SGU_GW, (g + 1) * SGU_GW)
                vg = vnb[rows, cols]
                mixed = _dot(wm[g], vg) + bst_ref[:, g:g + 1]
                dyv = dy_ref[rows, cols].astype(F32)
                dz_ref[rows, cols] = (dyv * mixed * ud[rows, cols]).astype(BF16)
                dmix = dyv * u[rows, cols]
                dmb = dmix.astype(BF16)
                dvn[rows, cols] = _dot_tn(wm[g], dmb)
                dws_ref[g] += jnp.where(mask, _dot_nt(dmb, vg), 0.0)
                dbs = dbs + jnp.where(lane8 == g, jnp.sum(dmix, axis=-1, keepdims=True), 0.0)
        dbst_ref[...] += dbs
        dvnv = dvn[...]
        dlng_ref[...] += jnp.sum(dvnv * vh, axis=0, keepdims=True)
        dlnb_ref[...] += jnp.sum(dvnv, axis=0, keepdims=True)
        dvh = dvnv * lng_ref[...]
        dv = rstd * (dvh - jnp.mean(dvh, axis=-1, keepdims=True) - vh * jnp.mean(dvh * vh, axis=-1, keepdims=True))
        dz_ref[:, SGU_WIDTH:] = (dv * vd).astype(BF16)

    vec = pl.BlockSpec((1, SGU_WIDTH), lambda i: (0, 0))
    wsp = pl.BlockSpec((SGU_GROUPS, SGU_BLOCK, SGU_BLOCK), lambda i: (0, 0, 0))
    bsp = pl.BlockSpec((SGU_BLOCK, SGU_GROUPS), lambda i: (0, 0))
    return pl.pallas_call(
        body, grid=(T // SGU_RB,),
        in_specs=[pl.BlockSpec((SGU_RB, 2 * SGU_WIDTH), lambda i: (i, 0)),
                  pl.BlockSpec((SGU_RB, SGU_WIDTH), lambda i: (i, 0)), vec, vec, wsp, bsp],
        out_specs=[pl.BlockSpec((SGU_RB, 2 * SGU_WIDTH), lambda i: (i, 0)), wsp, bsp, vec, vec],
        out_shape=[S((T, 2 * SGU_WIDTH), BF16), S((SGU_GROUPS, SGU_BLOCK, SGU_BLOCK), F32),
                   S((SGU_BLOCK, SGU_GROUPS), F32), S((1, SGU_WIDTH), F32), S((1, SGU_WIDTH), F32)],
        scratch_shapes=[pltpu.VMEM((SGU_RB, SGU_WIDTH), F32)], name="sgu_bwd",
        compiler_params=_cparams("arbitrary"))(zpre, dy, lng, lnb, ws, bst)


def _loss_head(h, tgt, g):
    T, D = h.shape
    tr = _tile(T, 512, 8)

    def body(h_ref, t_ref, g_ref, ls_ref, dh_ref, dhb_ref, dg_ref):
        @pl.when(pl.program_id(0) == 0)
        def _():
            ls_ref[...] = jnp.zeros_like(ls_ref)
            dg_ref[...] = jnp.zeros_like(dg_ref)

        hv = h_ref[...]
        r = lax.rsqrt(jnp.mean(hv * hv, axis=-1, keepdims=True) + EPS)
        xh = hv * r
        diff = xh * g_ref[...] - t_ref[...]
        per_row = jnp.mean(diff * diff, axis=-1, keepdims=True)
        ls_ref[...] += jnp.sum(per_row, axis=0, keepdims=True)
        dy = diff * (1.0 / D)
        dg_ref[...] += jnp.sum(dy * xh, axis=0, keepdims=True)
        dxh = dy * g_ref[...]
        dh = r * (dxh - xh * jnp.mean(dxh * xh, axis=-1, keepdims=True))
        dh_ref[...] = dh
        dhb_ref[...] = dh.astype(BF16)

    row = pl.BlockSpec((tr, D), lambda i: (i, 0))
    vec = pl.BlockSpec((1, D), lambda i: (0, 0))
    return pl.pallas_call(
        body, grid=(T // tr,), in_specs=[row, row, vec],
        out_specs=[pl.BlockSpec((1, LANES), lambda i: (0, 0)), row, row, vec],
        out_shape=[S((1, LANES), F32), S((T, D), F32), S((T, D), BF16), S((1, D), F32)],
        name="loss_head", compiler_params=_cparams("arbitrary"))(h, tgt, g)


ANY = pl.BlockSpec(memory_space=pl.ANY)
COPY_PARTS = 4
SWAP_PARTS = 8
DMA = pltpu.SemaphoreType.DMA


def _place():
    return lax.axis_index("x"), lax.axis_index("y"), lax.axis_index("c")


def _nparts(rows, unit, want):
    n = want
    while n > 1 and rows % (unit * n):
        n //= 2
    return n


def _row_unit(dtype):
    return 16 if jnp.dtype(dtype).itemsize == 2 else 8


def _remote(src, dst, send_sems, recv_sems, k, to):
    return pltpu.make_async_remote_copy(src_ref=src, dst_ref=dst, send_sem=send_sems.at[k], recv_sem=recv_sems.at[k],
                                        device_id=to, device_id_type=MESH)


def _sem_ranges(counts):
    first, total = [], 0
    for c in counts:
        first.append(total)
        total += c
    return first, total


class _Gather:
    def __init__(self, shards):
        self.srcs = list(shards)
        self.halves = [a.shape[0] // 2 for a in shards]
        self.units = [_row_unit(a.dtype) for a in shards]
        self.parts = [_nparts(h, u, COPY_PARTS) for h, u in zip(self.halves, self.units)]
        self.first, total = _sem_ranges([3 * n for n in self.parts])
        self.out_shapes = [S((N_CHIPS,) + a.shape, a.dtype) for a in shards]
        self.scratch = [DMA((total,))] * 4
        self.has_relay = True

    def _ops(self, srcs, outs, sems):
        ici_s, ici_r, rel_s, rel_r = sems
        x, y, c = _place()
        me, sibling = (x, y, c), (x, y, 1 - c)
        chips = [(1 - x, y), (x, 1 - y), (1 - x, 1 - y)]
        send, arrive, relay, relayed = [], [], [], []
        for p_ref, out_ref, Rh, unit, n, base in zip(srcs, outs, self.halves, self.units, self.parts, self.first):
            rp = Rh // n

            def part(px, py, pc, k, out_ref=out_ref, Rh=Rh, unit=unit, rp=rp):
                return out_ref.at[2 * px + py, pl.ds(pl.multiple_of(pc * Rh + k * rp, unit), rp), :]

            def mine(k, p_ref=p_ref, Rh=Rh, unit=unit, rp=rp):
                return p_ref.at[pl.ds(pl.multiple_of(c * Rh + k * rp, unit), rp), :]

            for j, chip in enumerate(chips):
                for k in range(n):
                    s = base + j * n + k
                    send.append(_remote(mine(k), part(x, y, c, k), ici_s, ici_r, s, (*chip, c)))
                    arrive.append(_remote(mine(k), part(*chip, c, k), ici_s, ici_r, s, me))
                    relay.append(_remote(part(*chip, c, k), part(*chip, c, k), rel_s, rel_r, s, sibling))
                    relayed.append(_remote(part(*chip, c, k), part(*chip, 1 - c, k), rel_s, rel_r, s, me))
        return send, arrive, relay, relayed

    def start(self, *refs):
        for cp in self._ops(*refs)[0]:
            cp.start()

    def relay(self, *refs):
        _, arrive, relay, _ = self._ops(*refs)
        for a, r in zip(arrive, relay):
            a.wait_recv()
            r.start()

    def finish(self, *refs):
        send, _, relay, relayed = self._ops(*refs)
        for cp in relayed:
            cp.wait_recv()
        for cp in send + relay:
            cp.wait_send()


class _Exchange:
    def __init__(self, items):
        self.srcs = [a for a, _ in items]
        self.kinds = [k for _, k in items]
        self.blocks = []
        for a, kind in items:
            R, Ccols = a.shape
            self.blocks.append({"cols": (R // 2, Ccols // N_CHIPS), "rows": (R // (2 * N_CHIPS), Ccols), "all": (R, Ccols)}[kind])
        self.units = [_row_unit(a.dtype) for a in self.srcs]
        self.parts = [_nparts(b[0], u, COPY_PARTS) for b, u in zip(self.blocks, self.units)]
        self.first, total = _sem_ranges([N_DEV - 2 + n for n in self.parts])
        self.out_shapes = [S((N_DEV,) + b, a.dtype) for a, b in zip(self.srcs, self.blocks)]
        self.scratch = [DMA((total,))] * 2
        self.has_relay = False

    def _ops(self, srcs, outs, sems):
        ss, rs = sems
        x, y, c = _place()
        send, arrive = [], []
        for src_ref, out_ref, kind, (Rb, Cb), unit, n, base in zip(srcs, outs, self.kinds, self.blocks, self.units, self.parts,
                                                                  self.first):
            rp = Rb // n

            def block_for(px, py, pc, r0, rows, src_ref=src_ref, kind=kind, Rb=Rb, Cb=Cb, unit=unit):
                if kind == "cols":
                    return src_ref.at[pl.ds(pl.multiple_of(pc * Rb + r0, unit), rows),
                                      pl.ds(pl.multiple_of((2 * px + py) * Cb, LANES), Cb)]
                if kind == "rows":
                    return src_ref.at[pl.ds(pl.multiple_of((2 * (2 * px + py) + pc) * Rb + r0, unit), rows), :]
                return src_ref.at[pl.ds(r0, rows), :]

            def slot(d, r0, rows, out_ref=out_ref):
                return out_ref.at[d, pl.ds(r0, rows), :]

            me = 4 * x + 2 * y + c
            for k in range(1, N_DEV):
                peer = (x ^ ((k >> 2) & 1), y ^ ((k >> 1) & 1), c ^ (k & 1))
                pieces = [(N_DEV - 2 + q, q * rp, rp) for q in range(n)] if k == 1 else [(k - 2, 0, Rb)]
                for sem, r0, rows in pieces:
                    send.append(_remote(block_for(*peer, r0, rows), slot(me, r0, rows), ss, rs, base + sem, peer))
                    arrive.append(_remote(block_for(*peer, r0, rows), slot(4 * peer[0] + 2 * peer[1] + peer[2], r0, rows),
                                          ss, rs, base + sem, peer))
        return send, arrive

    def start(self, *refs):
        for cp in self._ops(*refs)[0]:
            cp.start()

    def finish(self, *refs):
        send, arrive = self._ops(*refs)
        for cp in arrive:
            cp.wait_recv()
        for cp in send:
            cp.wait_send()


class _Swap:
    def __init__(self, halves):
        self.srcs = list(halves)
        self.parts = [_nparts(a.shape[0], _row_unit(a.dtype), SWAP_PARTS) for a in halves]
        self.first, total = _sem_ranges(self.parts)
        self.out_shapes = [S(a.shape, a.dtype) for a in halves]
        self.scratch = [DMA((total,))] * 2
        self.has_relay = False

    def _ops(self, srcs, outs, sems):
        ss, rs = sems
        x, y, c = _place()
        copies = []
        for h_ref, out_ref, n, base in zip(srcs, outs, self.parts, self.first):
            rp = h_ref.shape[0] // n
            for k in range(n):
                rows = pl.ds(k * rp, rp)
                copies.append(_remote(h_ref.at[rows, :], out_ref.at[rows, :], ss, rs, base + k, (x, y, 1 - c)))
        return copies, copies

    start = _Exchange.start
    finish = _Exchange.finish


def _run_plan(plan, name):
    ni, no = len(plan.srcs), len(plan.out_shapes)

    def body(*refs):
        parts = (refs[:ni], refs[ni:ni + no], refs[ni + no:])
        plan.start(*parts)
        if plan.has_relay:
            plan.relay(*parts)
        plan.finish(*parts)

    return pl.pallas_call(body, out_shape=plan.out_shapes, in_specs=[ANY] * ni, out_specs=[ANY] * no,
                          scratch_shapes=plan.scratch, name=name)(*plan.srcs)


def _call(body, *, grid, in_specs, out_specs, out_shape, name, sem, args, scratch_shapes=(), plan=None):
    if plan is None:
        return pl.pallas_call(body, grid=grid, in_specs=in_specs, out_specs=out_specs, out_shape=out_shape,
                              scratch_shapes=list(scratch_shapes), name=name, compiler_params=_cparams(*sem))(*args), None
    n_in, n_out, n_scr = len(in_specs), len(out_shape), len(scratch_shapes)
    pi, po = len(plan.srcs), len(plan.out_shapes)
    total = math.prod(grid)

    def wrapped(*refs):
        a, refs = refs[:n_in], refs[n_in:]
        pa, refs = refs[:pi], refs[pi:]
        o, refs = refs[:n_out], refs[n_out:]
        pout, refs = refs[:po], refs[po:]
        scr, psem = refs[:n_scr], refs[n_scr:]
        step = 0
        for d, gsize in enumerate(grid):
            step = step * gsize + pl.program_id(d)

        @pl.when(step == 0)
        def _():
            plan.start(pa, pout, psem)

        body(*a, *o, *scr)
        if plan.has_relay:
            @pl.when(step == (3 * total) // 4)
            def _():
                plan.relay(pa, pout, psem)

        @pl.when(step == total - 1)
        def _():
            plan.finish(pa, pout, psem)

    outs = pl.pallas_call(
        wrapped, grid=grid, in_specs=list(in_specs) + [ANY] * pi, out_specs=list(out_specs) + [ANY] * po,
        out_shape=list(out_shape) + plan.out_shapes, scratch_shapes=list(scratch_shapes) + plan.scratch, name=name,
        compiler_params=_cparams(*["arbitrary"] * len(grid)))(*args, *plan.srcs)
    return outs[:n_out], outs[n_out:]


SMEM = pl.BlockSpec(memory_space=pltpu.SMEM)


def _sum_slots(buf, own, me, name):
    n, R, W = buf.shape
    tr = _tile(R, 256, 8)

    def body(me_ref, b_ref, own_ref, o_ref):
        acc = None
        for s in range(n):
            blk = jnp.where(me_ref[0] == s, own_ref[...], b_ref[s]).astype(F32)
            acc = blk if acc is None else acc + blk
        o_ref[...] = acc

    return pl.pallas_call(
        body, grid=(R // tr,),
        in_specs=[SMEM, pl.BlockSpec((n, tr, W), lambda i: (0, i, 0)), pl.BlockSpec((tr, W), lambda i: (i, 0))],
        out_specs=pl.BlockSpec((tr, W), lambda i: (i, 0)), out_shape=S((R, W), F32), name=name,
        compiler_params=_cparams("parallel"))(me, buf, own)


def _adamw_update(wv, gv, mv, vv):
    mn = ADAM_B1 * mv + (1.0 - ADAM_B1) * gv
    vn = ADAM_B2 * vv + (1.0 - ADAM_B2) * (gv * gv)
    m_hat = mn / (1.0 - ADAM_B1 ** ADAM_STEP)
    v_hat = vn / (1.0 - ADAM_B2 ** ADAM_STEP)
    return -ADAM_LR * (m_hat / (jnp.sqrt(v_hat) + ADAM_EPS) + ADAM_WD * wv), mn, vn


def _adamw(w, g, m, v, name):
    R, W = w.shape
    tr = _tile(R, 256, 8)

    def body(w_ref, g_ref, m_ref, v_ref, d_ref, mo_ref, vo_ref):
        d_ref[...], mo_ref[...], vo_ref[...] = _adamw_update(w_ref[...], g_ref[...], m_ref[...], v_ref[...])

    blk = pl.BlockSpec((tr, W), lambda i: (i, 0))
    return pl.pallas_call(
        body, grid=(R // tr,), in_specs=[blk] * 4, out_specs=[blk] * 3, out_shape=[S((R, W), F32)] * 3, name=name,
        compiler_params=_cparams("parallel"))(w, g, m, v)


def _adamw_shard(w, halves, core, m, v, name):
    L, R, C = w.shape
    Rh = R // 2
    tr = _tile(Rh, 256, 8)
    nbh = Rh // tr

    def body(c_ref, w_ref, m_ref, v_ref, *rest):
        pairs, (g_ref, d_ref, mo_ref, vo_ref) = rest[:2 * L], rest[2 * L:]
        l, i = pl.program_id(0), pl.program_id(1)
        mine_rows = i // nbh == c_ref[0]
        gv = None
        for lp in range(L):
            cand = jnp.where(mine_rows, pairs[2 * lp][...], pairs[2 * lp + 1][...])
            gv = cand if gv is None else jnp.where(l == lp, cand, gv)
        g_ref[...] = gv
        d_ref[...], mo_ref[...], vo_ref[...] = _adamw_update(w_ref[...], gv, m_ref[...], v_ref[...])

    blk = pl.BlockSpec((None, tr, C), lambda l, i: (l, i, 0))
    half = lambda lp: pl.BlockSpec((tr, C), lambda l, i: (jnp.where(l == lp, i % nbh, 0), 0))
    return pl.pallas_call(
        body, grid=(L, R // tr), in_specs=[SMEM, blk, blk, blk] + [half(lp) for lp in range(L) for _ in range(2)],
        out_specs=[blk] * 4, out_shape=[S((L, R, C), F32)] * 4, name=name,
        compiler_params=_cparams("parallel", "parallel"))(core, w, m, v, *[h for pair in halves for h in pair])


def _tables(T):
    f32 = F32
    half = RET_QK // 2
    inv = 1.0 / (10000.0 ** jnp.linspace(0.0, 1.0, half, dtype=f32))
    ang = jnp.arange(T).astype(f32)[:, None] * inv[None, :]
    cos, sin = jnp.cos(ang), jnp.sin(ang)
    c2 = jnp.concatenate([cos, cos], axis=-1)
    s2 = jnp.concatenate([-sin, sin], axis=-1)
    log_g = jnp.log1p(-jnp.exp2(-5.0 - jnp.arange(RET_HEADS, dtype=f32)))
    idx = jnp.arange(CHUNK, dtype=f32)
    dintra = jnp.exp(log_g[:, None, None] * jnp.abs(idx[:, None] - idx[None, :]))
    kdec = jnp.exp(log_g[None, :] * (CHUNK - 1 - idx)[:, None]).T
    qdec = jnp.exp(log_g[None, :] * (idx + 1.0)[:, None]).T
    cdec = jnp.exp(log_g * CHUNK)
    bc = lambda a, w: jnp.broadcast_to(a[:, :, None], (RET_HEADS, a.shape[1], w))
    return c2, s2, dintra, bc(qdec, RET_QK), bc(kdec, RET_QK), jnp.broadcast_to(cdec[:, None, None], (RET_HEADS, 1, RET_V))


def _first_forms(g4):
    return {"ab_w_in4": g4["ab_w_in"],
            "ab_w_inT": jnp.transpose(g4["ab_w_in"], (0, 2, 1)).reshape(-1, D_MODEL),
            "ab_w_out": g4["ab_w_out"].reshape(-1, D_MODEL)}


def _late_forms(g4):
    wd = g4["ffn_w_down"]
    per = wd.shape[1] // 2
    return {"c_w_in4": g4["c_w_in"], "c_w_out": g4["c_w_out"].reshape(-1, D_MODEL), "ffn_w_up4": g4["ffn_w_up"],
            "ffn_w_down": [wd[:, l * per:(l + 1) * per].reshape(-1, D_MODEL) for l in range(2)]}


def _local_step(x, tgt, p, first=None, late=None, exchange=False):
    T = x.shape[0]
    tab = _tables(T)
    row = lambda a: a.reshape(1, -1)
    tr = lambda w: jnp.transpose(w)
    width = D_MODEL

    hn0, first_out = _rmsnorm_fwd(x, row(p["attn_norm_g"][0]), "norm_a0", plan=first[0] if first else None)
    if first:
        p = {**p, **first[1](first_out)}
    z0 = _mm(hn0, p["ab_w_in4"], "mm_ab_in", form="kn4", out_dtype=BF16)
    ya, r, st = _retention_fwd(z0, *tab)
    bias_t = _bias_tiles(jnp.transpose(_bias_build(p["ab_rel_bias"][0]), (1, 0, 2))[:, :, :BAND])
    yb, late_out = _attention_fwd(z0, bias_t, plan=late[0] if late else None)
    if late:
        p = {**p, **late[1](late_out)}
    h1, hf0 = _mm([ya, yb], p["ab_w_out"], "mm_ab_out", res=x, norm_g=row(p["ffn_norm_g"][0]))

    def ffn_fwd(h, hf, l, next_g):
        zf = _mm(hf, p["ffn_w_up4"], f"mm_up{l}", form="kn4", row0=l * width, out_dtype=BF16)
        f, gc, uc, *out = _ffn_down(zf, p["ffn_conv_w"][l], row(p["ffn_conv_b"][l]), p["ffn_w_down"][l], h, f"ffn_down{l}",
                                    norm_g=next_g)
        return (zf, f, gc, uc), (tuple(out) if next_g is not None else out[0])

    kept0, (h2, hn1) = ffn_fwd(h1, hf0, 0, row(p["attn_norm_g"][1]))
    zc = _mm(hn1, p["c_w_in4"], "mm_c_in", form="kn4", out_dtype=BF16)
    lng, lnb, bst, ws = row(p["c_ln_g"][0]), row(p["c_ln_b"][0]), tr(p["c_b_s"][0]), p["c_w_s"][0]
    y1 = _sgu_fwd(zc, lng, lnb, ws, bst)
    h3, hf1 = _mm(y1, p["c_w_out"], "mm_c_out", res=h2, norm_g=row(p["ffn_norm_g"][1]))
    kept1, h4 = ffn_fwd(h3, hf1, 1, None)
    lsum, dh4, dh4b, dgfin = _loss_head(h4, tgt, row(p["final_norm_g"]))

    g, big = {}, {}

    def ffn_bwd(dh, dhb, h_in, hf, kept, l):
        zf, f, gc, uc = kept
        big[f"ffn_w_down{l}"] = _mm_tn(f, dhb, f"mmt_down{l}", out_dtype=BF16)
        df = _mm(dhb, p["ffn_w_down"][l], f"mmb_down{l}", form="nk", out_dtype=BF16)
        dzg, dzu, dwg, dwu, dbg, dbu = _convglu_bwd(zf, gc, uc, df, p["ffn_conv_w"][l], f"convglu_bwd{l}")
        big[f"ffn_w_up{l}"] = _mm_tn(hf, [dzg, dzu], f"mmt_up{l}", out_dtype=BF16)
        dhf = _mm([dzg, dzu], p["ffn_w_up4"], f"mmb_up{l}", form="nk4", row0=l * width, rows=width, out_dtype=BF16)
        dh_in, dh_in_b, dgf = _rmsnorm_bwd(h_in, dhf, row(p["ffn_norm_g"][l]), dh, f"norm_f{l}_bwd")
        return dh_in, dh_in_b, dict(ffn_norm_g=dgf[0], ffn_conv_w=jnp.concatenate([dwg, dwu], axis=1),
                                    ffn_conv_b=jnp.concatenate([dbg, dbu], axis=1)[0])

    dh3, dh3b, gf1 = ffn_bwd(dh4, dh4b, h3, hf1, kept1, 1)
    big["c_w_out"] = _mm_tn(y1, dh3b, "mmt_c_out", out_dtype=BF16)
    dy1 = _mm(dh3b, p["c_w_out"], "mmb_c_out", form="nk", out_dtype=BF16)
    dzc, dws, dbst, dlng, dlnb = _sgu_bwd(zc, dy1, lng, lnb, ws, bst)
    g["c_w_s"], g["c_b_s"], g["c_ln_g"], g["c_ln_b"] = dws[None], tr(dbst)[None], dlng, dlnb
    big["c_w_in"] = _mm_tn(hn1, dzc, "mmt_c_in", out_dtype=BF16)
    dhn1 = _mm(dzc, p["c_w_in4"], "mmb_c_in", form="nk4", rows=width, out_dtype=BF16)
    dh2, dh2b, dga1 = _rmsnorm_bwd(h2, dhn1, row(p["attn_norm_g"][1]), dh3, "norm_a1_bwd")
    dh1, dh1b, gf0 = ffn_bwd(dh2, dh2b, h1, hf0, kept0, 0)
    for k in gf0:
        g[k] = jnp.stack([gf0[k], gf1[k]])
    big["ab_w_out"] = _mm_tn([ya, yb], dh1b, "mmt_ab_out", out_dtype=BF16)
    dycat = _mm(dh1b, p["ab_w_out"], "mmb_ab_out", form="nk", out_dtype=BF16)
    g["final_norm_g"] = dgfin[0]
    early_pack = _pack([dga1[0]] + [g[n] for n in EARLY_SMALL], 32)
    late_plan = _Exchange([(big[n], kind) for n, kind in LATE_ITEMS] + [(early_pack, "all")]) if exchange else None
    (dqb, dkb, dvb, dbias_t), late_slots = _attention_bwd(z0, bias_t, dycat, plan=late_plan)
    dqa, dka, dva, dga = _retention_bwd(z0, *tab, r, dycat, st)
    dz0 = [dqa, dka, dva, dga, dqb, dkb, dvb]
    big["ab_w_in"] = _mm_tn(hn0, dz0, "mmt_ab_in", out_dtype=BF16)
    slots = {}
    if exchange:
        dhn0, first_slots = _mm(dz0, p["ab_w_inT"], "mmb_ab_in", out_dtype=BF16,
                                plan=_Exchange([(big[n], kind) for n, kind in FIRST_ITEMS]))
        slots = dict(first=first_slots, late=late_slots[:-1], early=(early_pack, late_slots[-1]))
    else:
        dhn0 = _mm(dz0, p["ab_w_inT"], "mmb_ab_in", out_dtype=BF16)
    gx, _, dga0 = _rmsnorm_bwd(x, dhn0, row(p["attn_norm_g"][0]), dh1, "norm_a0_bwd")
    g["ab_rel_bias"] = _bias_grad(_bias_bands(dbias_t))[None, :, :N_REL]
    g["attn_norm_g"] = jnp.stack([dga0[0], dga1[0]])
    return lsum[0, 0], gx, g, big, slots


FIRST_BIG = ["ab_w_in", "ab_w_out"]
LATE_BIG = ["c_w_in", "c_w_out", "ffn_w_up", "ffn_w_down"]
BIG = FIRST_BIG + LATE_BIG
FIRST_ITEMS = [("ab_w_in", "cols"), ("ab_w_out", "rows")]
LATE_ITEMS = [("c_w_in", "cols"), ("c_w_out", "rows"), ("ffn_w_up0", "cols"), ("ffn_w_up1", "cols"),
              ("ffn_w_down0", "rows"), ("ffn_w_down1", "rows")]
LAYERS_OF = {"ab_w_in": ["ab_w_in"], "ab_w_out": ["ab_w_out"], "c_w_in": ["c_w_in"], "c_w_out": ["c_w_out"],
             "ffn_w_up": ["ffn_w_up0", "ffn_w_up1"], "ffn_w_down": ["ffn_w_down0", "ffn_w_down1"]}
SMALL_SHARDED = [("c_ln_g", 1), ("c_ln_b", 1), ("ffn_conv_w", 2)]
REPLICATED = ["attn_norm_g", "ffn_norm_g", "ab_rel_bias", "c_w_s", "c_b_s", "ffn_conv_b", "final_norm_g"]
EARLY_SMALL = ["ffn_norm_g", "c_w_s", "c_b_s", "ffn_conv_b", "final_norm_g", "c_ln_g", "c_ln_b", "ffn_conv_w"]


def _rows_of(n_elems):
    return -(-n_elems // PACK_W)


def _flat_rows(a):
    f = a.reshape(-1)
    rows = _rows_of(f.shape[0])
    return jnp.pad(f, (0, rows * PACK_W - f.shape[0])).reshape(rows, PACK_W)


def _pad_rows(a, mult):
    extra = (-a.shape[0]) % mult
    return jnp.pad(a, ((0, extra), (0, 0))) if extra else a


def _pack(arrs, mult):
    return _pad_rows(jnp.concatenate([_flat_rows(a) for a in arrs], axis=0), mult)


def _unpack(buf, shapes):
    out, r = [], 0
    for shp in shapes:
        n = math.prod(shp)
        rows = _rows_of(n)
        out.append(buf[r:r + rows].reshape(-1)[:n].reshape(shp))
        r += rows
    return out


def _from_shards(sh, axis):
    m = jnp.moveaxis(sh, 0, axis)
    shp = m.shape
    return m.reshape(shp[:axis] + (shp[axis] * shp[axis + 1],) + shp[axis + 2:])


def _as_bf16_pairs(a):
    return lax.bitcast_convert_type(a.astype(F32), BF16)


def _from_bf16_pairs(a):
    return lax.bitcast_convert_type(a, F32)


def kernel(x, attn_norm_g, ffn_norm_g, ab_w_in, ab_w_out, ab_rel_bias, c_w_in, c_ln_g, c_ln_b, c_w_s, c_b_s, c_w_out, ffn_w_up, ffn_conv_w, ffn_conv_b, ffn_w_down, final_norm_g, loss_target, m_attn_norm_g, m_ffn_norm_g, m_ab_w_in, m_ab_w_out, m_ab_rel_bias, m_c_w_in, m_c_ln_g, m_c_ln_b, m_c_w_s, m_c_b_s, m_c_w_out, m_ffn_w_up, m_ffn_conv_w, m_ffn_conv_b, m_ffn_w_down, m_final_norm_g, v_attn_norm_g, v_ffn_norm_g, v_ab_w_in, v_ab_w_out, v_ab_rel_bias, v_c_w_in, v_c_ln_g, v_c_ln_b, v_c_w_s, v_c_b_s, v_c_w_out, v_ffn_w_up, v_ffn_conv_w, v_ffn_conv_b, v_ffn_w_down, v_final_norm_g):
    w = dict(attn_norm_g=attn_norm_g, ffn_norm_g=ffn_norm_g, ab_w_in=ab_w_in, ab_w_out=ab_w_out, ab_rel_bias=ab_rel_bias,
             c_w_in=c_w_in, c_ln_g=c_ln_g, c_ln_b=c_ln_b, c_w_s=c_w_s, c_b_s=c_b_s, c_w_out=c_w_out, ffn_w_up=ffn_w_up,
             ffn_conv_w=ffn_conv_w, ffn_conv_b=ffn_conv_b, ffn_w_down=ffn_w_down, final_norm_g=final_norm_g)
    m = dict(attn_norm_g=m_attn_norm_g, ffn_norm_g=m_ffn_norm_g, ab_w_in=m_ab_w_in, ab_w_out=m_ab_w_out,
             ab_rel_bias=m_ab_rel_bias, c_w_in=m_c_w_in, c_ln_g=m_c_ln_g, c_ln_b=m_c_ln_b, c_w_s=m_c_w_s, c_b_s=m_c_b_s,
             c_w_out=m_c_w_out, ffn_w_up=m_ffn_w_up, ffn_conv_w=m_ffn_conv_w, ffn_conv_b=m_ffn_conv_b,
             ffn_w_down=m_ffn_w_down, final_norm_g=m_final_norm_g)
    v = dict(attn_norm_g=v_attn_norm_g, ffn_norm_g=v_ffn_norm_g, ab_w_in=v_ab_w_in, ab_w_out=v_ab_w_out,
             ab_rel_bias=v_ab_rel_bias, c_w_in=v_c_w_in, c_ln_g=v_c_ln_g, c_ln_b=v_c_ln_b, c_w_s=v_c_w_s, c_b_s=v_c_b_s,
             c_w_out=v_c_w_out, ffn_w_up=v_ffn_w_up, ffn_conv_w=v_ffn_conv_w, ffn_conv_b=v_ffn_conv_b,
             ffn_w_down=v_ffn_w_down, final_norm_g=v_final_norm_g)
    names = list(w)
    chip = 2 * lax.axis_index("x") + lax.axis_index("y")

    core = lax.axis_index("c")
    core_arr = core.reshape(1).astype(jnp.int32)
    me_arr = (2 * chip + core).reshape(1).astype(jnp.int32)
    two_d = lambda a: a.reshape(-1, a.shape[-1])
    with_own = lambda gathered, own: lax.dynamic_update_slice(gathered, own[None], (chip, 0, 0))

    send_first = [two_d(w[n]).astype(BF16) for n in FIRST_BIG]
    finish_first = lambda got: _first_forms({n: with_own(a, own) for n, a, own in zip(FIRST_BIG, got, send_first)})
    full = {n: w[n] for n in REPLICATED}
    small_send = [_as_bf16_pairs(w[n]) for n, _ in SMALL_SHARDED]
    send_late = [two_d(w[n]).astype(BF16) for n in LATE_BIG] + [_pack(small_send, 32)]

    def finish_late(got):
        whole = [with_own(a, own) for a, own in zip(got, send_late)]
        forms = _late_forms(dict(zip(LATE_BIG, whole)))
        parts = [_unpack(whole[-1][s], [a.shape for a in small_send]) for s in range(N_CHIPS)]
        for i, (n, axis) in enumerate(SMALL_SHARDED):
            forms[n] = _from_shards(_from_bf16_pairs(jnp.stack([parts[s][i] for s in range(N_CHIPS)])), axis)
        return forms

    lsum, grad_x, g, big, slots = _local_step(x[0], loss_target[0], full, first=(_Gather(send_first), finish_first),
                                              late=(_Gather(send_late), finish_late), exchange=True)
    loss = lax.psum(0.5 * lsum, ("x", "y", "c"))

    def own_block(a, kind):
        rows, cols = a.shape
        if kind == "cols":
            return lax.dynamic_slice(a, (core * (rows // 2), chip * (cols // N_CHIPS)), (rows // 2, cols // N_CHIPS))
        per = rows // N_DEV
        return lax.dynamic_slice(a, ((2 * chip + core) * per, 0), (per, cols))

    reduced = {}
    for key, items in (("late", LATE_ITEMS), ("first", FIRST_ITEMS)):
        halves = [_sum_slots(got, own_block(big[n], kind), me_arr, f"sum_{n}") for (n, kind), got in zip(items, slots[key])]
        others = _run_plan(_Swap(halves), f"swap_{key}")
        reduced.update({n: (h, o) for (n, _), h, o in zip(items, halves, others)})
    big_outs = [{}, {}, {}, {}]
    for n in BIG:
        res = _adamw_shard(w[n], [reduced[layer] for layer in LAYERS_OF[n]], core_arr, m[n], v[n], f"adamw_{n}")
        for k in range(4):
            big_outs[k][n] = res[k]

    small_names = REPLICATED + [n for n, _ in SMALL_SHARDED]
    early_pack, early_slots = slots["early"]
    early = _unpack(_sum_slots(early_slots, early_pack, me_arr, "sum_early"), [(D_MODEL,)] + [g[n].shape for n in EARLY_SMALL])
    last_pack = _pack([g["attn_norm_g"][0], g["ab_rel_bias"]], 8)
    last_slots = _run_plan(_Exchange([(last_pack, "all")]), "exchange_small")[0]
    norm_a0, rel_bias = _unpack(_sum_slots(last_slots, last_pack, me_arr, "sum_small"), [(D_MODEL,), g["ab_rel_bias"].shape])
    gsmall_full = dict(zip(EARLY_SMALL, early[1:]), attn_norm_g=jnp.stack([norm_a0, early[0]]), ab_rel_bias=rel_bias)
    for n, axis in SMALL_SHARDED:
        size = w[n].shape[axis]
        gsmall_full[n] = lax.dynamic_slice_in_dim(gsmall_full[n], chip * size, size, axis)
    pack_small = lambda d: _pack([d[n] for n in small_names], 8)
    small_out = _adamw(pack_small(w), pack_small(gsmall_full), pack_small(m), pack_small(v), "adamw_small")
    small_shapes = [w[n].shape for n in small_names]

    outs = [{**big_outs[0], **gsmall_full}]
    for k in range(3):
        outs.append({**big_outs[k + 1], **dict(zip(small_names, _unpack(small_out[k], small_shapes)))})
    return (loss, grad_x[None], *[o[n] for o in outs for n in names])
```

```python
import functools
import math

import jax
import jax.numpy as jnp
from jax import lax
from jax.experimental import pallas as pl
from jax.experimental.pallas import tpu as pltpu

F32 = jnp.float32
BF16 = jnp.bfloat16
S = jax.ShapeDtypeStruct
MESH = pl.DeviceIdType.MESH

D_MODEL = 1024
CHUNK = 64
EPS = 1e-6
NEG_INF = -1e30
RET_HEADS, RET_QK, RET_V = 4, 128, 256
ATT_HEADS, ATT_D, ATT_PAST, MAX_REL = 8, 64, 8, 128
BAND = (ATT_PAST + 1) * CHUNK
PADK = ATT_PAST * CHUNK
SGU_BLOCK, SGU_GROUPS, SGU_WIDTH = 128, 8, 2048
SGU_GW = SGU_WIDTH // SGU_GROUPS
FFN_HIDDEN = 2816
N_REL = 2 * MAX_REL + 1
RET_SCALE = RET_QK ** -0.5
ATT_SCALE = ATT_D ** -0.5
ADAM_LR, ADAM_B1, ADAM_B2, ADAM_EPS, ADAM_WD, ADAM_STEP = 0.001, 0.9, 0.999, 1e-08, 0.01, 10

V7X_VMEM_BYTES = 64 * 1024 * 1024
VMEM_LIMIT = V7X_VMEM_BYTES * 7 // 8
MM_TILE_BUDGET = V7X_VMEM_BYTES * 3 // 4
LANES = 128
PACK_W = 1024
N_CHIPS = 4
N_DEV = 8

GELU_C = math.sqrt(2.0 / math.pi)
GELU_A = 0.044715


def _cparams(*sem):
    return pltpu.CompilerParams(dimension_semantics=tuple(sem) if sem else None, vmem_limit_bytes=VMEM_LIMIT)


def _tile(n, target, unit=LANES):
    best = None
    for t in range(unit, min(n, target) + 1, unit):
        if n % t == 0:
            best = t
    return best if best is not None else n


def _gelu(x):
    t = jnp.tanh(GELU_C * (x + GELU_A * x * x * x))
    return 0.5 * x * (1.0 + t)


def _gelu_and_grad(x):
    x2 = x * x
    t = jnp.tanh(GELU_C * (x + GELU_A * x2 * x))
    g = 0.5 * x * (1.0 + t)
    dg = 0.5 * (1.0 + t) + 0.5 * x * (1.0 - t * t) * (GELU_C * (1.0 + 3.0 * GELU_A * x2))
    return g, dg


def _sigmoid(x):
    return 1.0 / (1.0 + jnp.exp(-x))


def _dot(a, b):
    return jnp.dot(a, b, preferred_element_type=F32)


def _dot_nt(a, b):
    return lax.dot_general(a, b, (((1,), (1,)), ((), ())), preferred_element_type=F32)


def _dot_tn(a, b):
    return lax.dot_general(a, b, (((0,), (0,)), ((), ())), preferred_element_type=F32)


def _rmsnorm_fwd(x, g, name, plan=None):
    T, D = x.shape
    tr = _tile(T, 512, 8)

    def body(x_ref, g_ref, o_ref):
        xv = x_ref[...]
        r = lax.rsqrt(jnp.mean(xv * xv, axis=-1, keepdims=True) + EPS)
        o_ref[...] = (xv * r * g_ref[...]).astype(o_ref.dtype)

    (out,), extra = _call(
        body, grid=(T // tr,),
        in_specs=[pl.BlockSpec((tr, D), lambda i: (i, 0)), pl.BlockSpec((1, D), lambda i: (0, 0))],
        out_specs=[pl.BlockSpec((tr, D), lambda i: (i, 0))],
        out_shape=[S((T, D), BF16)], name=name, sem=("parallel",), args=(x, g), plan=plan)
    return out, extra


def _pieces(a):
    return list(a) if isinstance(a, (list, tuple)) else [a]


def _piece_layout(widths, tile):
    out, s = [], 0
    for w in widths:
        out.append((s, w // tile))
        s += w // tile
    return out


def _common_tile(widths, target):
    return _tile(functools.reduce(math.gcd, widths), target)


def _rmsnorm_bwd(x, dy, g, dres, name):
    T, D = x.shape
    tr = _tile(T, 512, 8)

    def body(x_ref, dy_ref, g_ref, dres_ref, dx_ref, dxb_ref, dg_ref):
        @pl.when(pl.program_id(0) == 0)
        def _():
            dg_ref[...] = jnp.zeros_like(dg_ref)

        xv = x_ref[...]
        r = lax.rsqrt(jnp.mean(xv * xv, axis=-1, keepdims=True) + EPS)
        xh = xv * r
        dyv = dy_ref[...].astype(F32)
        dg_ref[...] += jnp.sum(dyv * xh, axis=0, keepdims=True)
        dxh = dyv * g_ref[...]
        dx = dres_ref[...] + r * (dxh - xh * jnp.mean(dxh * xh, axis=-1, keepdims=True))
        dx_ref[...] = dx
        dxb_ref[...] = dx.astype(BF16)

    row = pl.BlockSpec((tr, D), lambda i: (i, 0))
    vec = pl.BlockSpec((1, D), lambda i: (0, 0))
    return pl.pallas_call(
        body, grid=(T // tr,), in_specs=[row, row, vec, row], out_specs=[row, row, vec],
        out_shape=[S((T, D), F32), S((T, D), BF16), S((1, D), F32)], name=name,
        compiler_params=_cparams("arbitrary"))(x, dy, g, dres)


def _mm(a, b, name, res=None, out_dtype=F32, plan=None, norm_g=None, form="kn", row0=0, rows=None):
    pieces = _pieces(a)
    M = pieces[0].shape[0]
    widths = [p.shape[1] for p in pieces]
    K = sum(widths)
    N = {"kn": lambda: b.shape[1], "kn4": lambda: N_CHIPS * b.shape[2], "nk": lambda: b.shape[0], "nk4": lambda: rows}[form]()
    tn = N if norm_g is not None else (b.shape[2] if form == "kn4" else _tile(N, 1408))
    tk = b.shape[2] if form == "nk4" else _common_tile(widths, 1536)
    def vmem_bytes(rows):
        out_bytes = jnp.dtype(out_dtype).itemsize + 2 * (norm_g is not None)
        blocks = len(pieces) * rows * tk * 2 + tk * tn * 2 + rows * tn * (4 * (res is not None) + out_bytes)
        return 2 * blocks + rows * tn * 4 * (K != tk)

    tm = _tile(M, 2048 if vmem_bytes(2048) <= MM_TILE_BUDGET else 1024, 8)
    assert all(w % tk == 0 for w in widths) and row0 % (tk if form == "kn4" else tn) == 0
    nk, npc = K // tk, len(pieces)
    layout = _piece_layout(widths, tk)
    tile = pl.BlockSpec((tm, tn), lambda i, j, k: (i, j))
    vec = pl.BlockSpec((1, tn), lambda i, j, k: (0, j))
    b_spec = {"kn": lambda: pl.BlockSpec((tk, tn), lambda i, j, k: (k, j)),
              "kn4": lambda: pl.BlockSpec((None, tk, tn), lambda i, j, k: (j, row0 // tk + k, 0)),
              "nk": lambda: pl.BlockSpec((tn, tk), lambda i, j, k: (j, k)),
              "nk4": lambda: pl.BlockSpec((None, tn, tk), lambda i, j, k: (k, row0 // tn + j, 0))}[form]()
    dot = _dot if form in ("kn", "kn4") else _dot_nt
    extra_in, extra_specs = [], []
    if res is not None:
        extra_in, extra_specs = [res], [tile]
    if norm_g is not None:
        extra_in, extra_specs = extra_in + [norm_g], extra_specs + [vec]
    n_extra = len(extra_in)
    if norm_g is not None:
        out_shape, out_specs = [S((M, N), out_dtype), S((M, N), BF16)], [tile, tile]
    else:
        out_shape, out_specs = [S((M, N), out_dtype)], [tile]

    def body(*refs):
        a_refs, b_ref = refs[:npc], refs[npc]
        ext = list(refs[npc + 1:npc + 1 + n_extra])
        outs = refs[npc + 1 + n_extra:npc + 1 + n_extra + len(out_shape)]

        def finish(v):
            if res is not None:
                v = v + ext[0][...]
            outs[0][...] = v.astype(outs[0].dtype)
            if norm_g is not None:
                r = lax.rsqrt(jnp.mean(v * v, axis=-1, keepdims=True) + EPS)
                outs[1][...] = (v * r * ext[-1][...]).astype(BF16)

        if nk == 1:
            finish(dot(a_refs[0][...], b_ref[...]))
            return
        acc = refs[-1]
        k = pl.program_id(2)
        for a_ref, (s, c) in zip(a_refs, layout):
            def add(a_ref=a_ref):
                acc[...] += dot(a_ref[...], b_ref[...])

            if s == 0:
                @pl.when(k == 0)
                def _(a_ref=a_ref):
                    acc[...] = dot(a_ref[...], b_ref[...])

                if c > 1:
                    pl.when((k > 0) & (k < c))(add)
            else:
                pl.when((k >= s) & (k < s + c))(add)

        @pl.when(k == nk - 1)
        def _():
            finish(acc[...])

    in_specs = [pl.BlockSpec((tm, tk), lambda i, j, k, s=s, c=c: (i, jnp.clip(k - s, 0, c - 1))) for s, c in layout]
    outs, extra = _call(
        body, grid=(M // tm, N // tn, nk), in_specs=in_specs + [b_spec] + extra_specs, out_specs=out_specs,
        out_shape=out_shape, scratch_shapes=[pltpu.VMEM((tm, tn), F32)] if nk > 1 else [],
        name=name, sem=("parallel", "parallel", "arbitrary"), args=pieces + [b] + extra_in, plan=plan)
    outs = outs[0] if len(outs) == 1 else tuple(outs)
    return outs if plan is None else (outs, extra)


def _mm_tn(a, g, name, out_dtype=F32):
    ap, gp = _pieces(a), _pieces(g)
    T = ap[0].shape[0]
    aw, gw = [p.shape[1] for p in ap], [p.shape[1] for p in gp]
    tm, tn = _common_tile(aw, 1408), _common_tile(gw, 1408)
    narrow = out_dtype != F32
    tall = 2 * 2 * 2048 * (len(ap) * tm + len(gp) * tn) + tm * tn * (2 * jnp.dtype(out_dtype).itemsize + 4 * narrow)
    tt = _tile(T, 2048 if tall <= MM_TILE_BUDGET else 1024, 8)
    alay, glay = _piece_layout(aw, tm), _piece_layout(gw, tn)
    na = len(ap)

    def inside(idx, s, c, single):
        return None if single else (idx >= s) & (idx < s + c)

    nt = T // tt

    def body(*refs):
        a_refs, g_refs = refs[:na], refs[na:na + len(gp)]
        o_ref = refs[na + len(gp)]
        acc = refs[-1] if narrow else o_ref
        i, j, k = pl.program_id(0), pl.program_id(1), pl.program_id(2)

        @pl.when(k == 0)
        def _():
            acc[...] = jnp.zeros_like(acc)

        for a_ref, (sa, ca) in zip(a_refs, alay):
            for g_ref, (sg, cg) in zip(g_refs, glay):
                def add(a_ref=a_ref, g_ref=g_ref):
                    acc[...] += _dot_tn(a_ref[...], g_ref[...])

                conds = [c for c in (inside(i, sa, ca, na == 1), inside(j, sg, cg, len(gp) == 1)) if c is not None]
                if not conds:
                    add()
                else:
                    pl.when(functools.reduce(lambda u, v: u & v, conds))(add)

        if narrow:
            @pl.when(k == nt - 1)
            def _():
                o_ref[...] = acc[...].astype(out_dtype)

    def spec(tile, lay, single, axis):
        s, c = lay

        def index(i, j, k):
            idx = (i, j)[axis]
            if single:
                return (k, idx)
            on = (idx >= s) & (idx < s + c)
            return (jnp.where(on, k, 0), jnp.clip(idx - s, 0, c - 1))

        return pl.BlockSpec((tt, tile), index)

    in_specs = [spec(tm, lay, na == 1, 0) for lay in alay] + [spec(tn, lay, len(gp) == 1, 1) for lay in glay]
    return pl.pallas_call(
        body, grid=(sum(aw) // tm, sum(gw) // tn, nt), in_specs=in_specs,
        out_specs=pl.BlockSpec((tm, tn), lambda i, j, k: (i, j)),
        out_shape=S((sum(aw), sum(gw)), out_dtype), scratch_shapes=[pltpu.VMEM((tm, tn), F32)] if narrow else [],
        name=name, compiler_params=_cparams("parallel", "parallel", "arbitrary"))(*ap, *gp)


def _rotate(x, c2, s2):
    return x * c2 + pltpu.roll(x, RET_QK // 2, 1) * s2


def _unrotate(d, c2, s2):
    return d * c2 - pltpu.roll(d, RET_QK // 2, 1) * s2


RET_PAIR = 4
RET_STEPS = RET_HEADS // RET_PAIR


def _ret_specs(RB, blockmap):
    qk, vg = RET_PAIR * RET_QK, RET_PAIR * RET_V
    q = pl.BlockSpec((RB, qk), lambda h, n: (blockmap(n), h))
    k = pl.BlockSpec((RB, qk), lambda h, n: (blockmap(n), RET_STEPS + h))
    v = pl.BlockSpec((RB, vg), lambda h, n: (blockmap(n), RET_STEPS + h))
    g = pl.BlockSpec((RB, vg), lambda h, n: (blockmap(n), 2 * RET_STEPS + h))
    tab = pl.BlockSpec((RB, RET_QK), lambda h, n: (blockmap(n), 0))
    return q, k, v, g, tab


def _ret_decay_specs():
    return [pl.BlockSpec((RET_PAIR, CHUNK, CHUNK), lambda h, n: (h, 0, 0)),
            pl.BlockSpec((RET_PAIR, CHUNK, RET_QK), lambda h, n: (h, 0, 0)),
            pl.BlockSpec((RET_PAIR, CHUNK, RET_QK), lambda h, n: (h, 0, 0)),
            pl.BlockSpec((RET_PAIR, 1, RET_V), lambda h, n: (h, 0, 0))]


def _ret_cols(e):
    return slice(e * RET_QK, (e + 1) * RET_QK), slice(e * RET_V, (e + 1) * RET_V)


def _retention_fwd(z, c2, s2, dintra, qdec, kdec, cdec):
    T = z.shape[0]
    RB = min(512, T)
    nch, nb = RB // CHUNK, T // RB

    def body(q_ref, k_ref, v_ref, g_ref, c2_ref, s2_ref, di_ref, qd_ref, kd_ref, cd_ref, ya_ref, r_ref, st_ref, state):
        @pl.when(pl.program_id(1) == 0)
        def _():
            state[...] = jnp.zeros_like(state)

        for c in range(nch):
            rows = slice(c * CHUNK, (c + 1) * CHUNK)
            c2v, s2v = c2_ref[rows, :], s2_ref[rows, :]
            for e in range(RET_PAIR):
                qk, vg = _ret_cols(e)
                dmat, qdv, kdv, cdv = di_ref[e], qd_ref[e], kd_ref[e], cd_ref[e]
                qr = _rotate(q_ref[rows, qk].astype(F32), c2v, s2v)
                kr = _rotate(k_ref[rows, qk].astype(F32), c2v, s2v) * RET_SCALE
                vb = v_ref[rows, vg].astype(BF16)
                sm = _dot_nt(qr.astype(BF16), kr.astype(BF16)) * dmat
                sb = state[e].astype(BF16)
                st_ref[e, c] = sb
                o = _dot(sm.astype(BF16), vb) + _dot((qr * qdv).astype(BF16), sb)
                state[e] = state[e] * cdv + _dot_tn((kr * kdv).astype(BF16), vb)
                r_ref[rows, vg] = o
                mu = jnp.mean(o, axis=-1, keepdims=True)
                oc = o - mu
                rn = oc * lax.rsqrt(jnp.mean(oc * oc, axis=-1, keepdims=True) + EPS)
                gv = g_ref[rows, vg].astype(F32)
                ya_ref[rows, vg] = (gv * _sigmoid(gv) * rn).astype(BF16)

    q, k, v, g, tab = _ret_specs(RB, lambda n: n)
    wide = pl.BlockSpec((RB, RET_PAIR * RET_V), lambda h, n: (n, h))
    return pl.pallas_call(
        body, grid=(RET_STEPS, nb),
        in_specs=[q, k, v, g, tab, tab] + _ret_decay_specs(),
        out_specs=[wide, wide, pl.BlockSpec((RET_PAIR, nch, RET_QK, RET_V), lambda h, n: (h, n, 0, 0))],
        out_shape=[S((T, RET_HEADS * RET_V), BF16), S((T, RET_HEADS * RET_V), F32),
                   S((RET_HEADS, T // CHUNK, RET_QK, RET_V), BF16)],
        scratch_shapes=[pltpu.VMEM((RET_PAIR, RET_QK, RET_V), F32)], name="retention_fwd",
        compiler_params=_cparams("parallel", "arbitrary"))(z, z, z, z, c2, s2, dintra, qdec, kdec, cdec)


def _retention_bwd(z, c2, s2, dintra, qdec, kdec, cdec, r, dycat, st):
    T = z.shape[0]
    RB = min(512, T)
    nch, nb = RB // CHUNK, T // RB

    def body(q_ref, k_ref, v_ref, g_ref, c2_ref, s2_ref, di_ref, qd_ref, kd_ref, cd_ref, r_ref, dy_ref, st_ref,
             dq_ref, dk_ref, dv_ref, dg_ref, dstate):
        @pl.when(pl.program_id(1) == 0)
        def _():
            dstate[...] = jnp.zeros_like(dstate)

        for c in reversed(range(nch)):
            rows = slice(c * CHUNK, (c + 1) * CHUNK)
            c2v, s2v = c2_ref[rows, :], s2_ref[rows, :]
            for e in range(RET_PAIR):
                qk, vg = _ret_cols(e)
                dmat, qdv, kdv, cdv = di_ref[e], qd_ref[e], kd_ref[e], cd_ref[e]
                qr = _rotate(q_ref[rows, qk].astype(F32), c2v, s2v)
                kr = _rotate(k_ref[rows, qk].astype(F32), c2v, s2v) * RET_SCALE
                qb, kb = qr.astype(BF16), kr.astype(BF16)
                vb = v_ref[rows, vg].astype(BF16)
                o, gv, dy = r_ref[rows, vg], g_ref[rows, vg].astype(F32), dy_ref[rows, vg]
                mu = jnp.mean(o, axis=-1, keepdims=True)
                oc = o - mu
                rstd = lax.rsqrt(jnp.mean(oc * oc, axis=-1, keepdims=True) + EPS)
                rn = oc * rstd
                sg = _sigmoid(gv)
                dg_ref[rows, vg] = (dy * rn * (sg * (1.0 + gv * (1.0 - sg)))).astype(BF16)
                drn = dy * (gv * sg)
                do = rstd * (drn - jnp.mean(drn, axis=-1, keepdims=True) - rn * jnp.mean(drn * rn, axis=-1, keepdims=True))
                dob = do.astype(BF16)
                sm = (_dot_nt(qb, kb) * dmat).astype(BF16)
                kdb = (kr * kdv).astype(BF16)
                dsb = dstate[e].astype(BF16)
                dv_ref[rows, vg] = (_dot_tn(sm, dob) + _dot(kdb, dsb)).astype(BF16)
                ds = (_dot_nt(dob, vb) * dmat).astype(BF16)
                dqr = _dot(ds, kb) + _dot_nt(dob, st_ref[e, c]) * qdv
                dkr = (_dot_tn(ds, qb) + _dot_nt(vb, dsb) * kdv) * RET_SCALE
                dstate[e] = dstate[e] * cdv + _dot_tn((qr * qdv).astype(BF16), dob)
                dq_ref[rows, qk] = _unrotate(dqr, c2v, s2v).astype(BF16)
                dk_ref[rows, qk] = _unrotate(dkr, c2v, s2v).astype(BF16)

    rev = lambda n: nb - 1 - n
    q, k, v, g, tab = _ret_specs(RB, rev)
    wide = pl.BlockSpec((RB, RET_PAIR * RET_V), lambda h, n: (rev(n), h))
    narrow = pl.BlockSpec((RB, RET_PAIR * RET_QK), lambda h, n: (rev(n), h))
    return pl.pallas_call(
        body, grid=(RET_STEPS, nb),
        in_specs=[q, k, v, g, tab, tab] + _ret_decay_specs() + [
            wide, wide, pl.BlockSpec((RET_PAIR, nch, RET_QK, RET_V), lambda h, n: (h, rev(n), 0, 0))],
        out_specs=[narrow, narrow, wide, wide],
        out_shape=[S((T, RET_HEADS * RET_QK), BF16), S((T, RET_HEADS * RET_QK), BF16),
                   S((T, RET_HEADS * RET_V), BF16), S((T, RET_HEADS * RET_V), BF16)],
        scratch_shapes=[pltpu.VMEM((RET_PAIR, RET_QK, RET_V), F32)], name="retention_bwd",
        compiler_params=_cparams("parallel", "arbitrary"))(z, z, z, z, c2, s2, dintra, qdec, kdec, cdec, r, dycat, st)


def _rel_index(i):
    r = lax.broadcasted_iota(jnp.int32, (3 * LANES, 5 * LANES), 0)
    j = lax.broadcasted_iota(jnp.int32, (3 * LANES, 5 * LANES), 1)
    idx = jnp.clip(i + PADK - j, -MAX_REL, MAX_REL) + MAX_REL
    return (r == idx).astype(BF16)


def _split3(v):
    hi = v.astype(BF16)
    r1 = v - hi.astype(F32)
    mid = r1.astype(BF16)
    lo = (r1 - mid.astype(F32)).astype(BF16)
    return hi, mid, lo


def _bias_build(rb):
    rbp = jnp.pad(rb, ((0, 0), (0, 3 * LANES - N_REL)))

    def body(rb_ref, o_ref):
        e = _rel_index(pl.program_id(0))
        hi, mid, lo = _split3(rb_ref[...])
        o_ref[...] = _dot(hi, e) + _dot(mid, e) + _dot(lo, e)

    return pl.pallas_call(
        body, grid=(CHUNK,), in_specs=[pl.BlockSpec((ATT_HEADS, 3 * LANES), lambda i: (0, 0))],
        out_specs=pl.BlockSpec((None, ATT_HEADS, 5 * LANES), lambda i: (i, 0, 0)),
        out_shape=S((CHUNK, ATT_HEADS, 5 * LANES), F32), name="bias_build",
        compiler_params=_cparams("parallel"))(rbp)


ATT_RB = 512
ATT_QT = 256
ATT_CPT = ATT_QT // CHUNK
ATT_KT = ATT_QT + PADK
ATT_QCOL = (2 * RET_HEADS * RET_QK + 2 * RET_HEADS * RET_V) // LANES
ATT_KCOL = ATT_QCOL + ATT_HEADS * ATT_D // LANES
ATT_VCOL = ATT_KCOL + ATT_HEADS * ATT_D // LANES


def _bias_grad(dbt):
    def body(d_ref, o_ref):
        @pl.when(pl.program_id(0) == 0)
        def _():
            o_ref[...] = jnp.zeros_like(o_ref)

        e = _rel_index(pl.program_id(0))
        d = d_ref[0]
        for ci in range(1, ATT_CPT):
            d = d + d_ref[ci]
        hi, mid, lo = _split3(d)
        o_ref[...] += _dot_nt(hi, e) + _dot_nt(mid, e) + _dot_nt(lo, e)

    return pl.pallas_call(
        body, grid=(CHUNK,),
        in_specs=[pl.BlockSpec((ATT_CPT, None, ATT_HEADS, 5 * LANES), lambda i: (0, i, 0, 0))],
        out_specs=pl.BlockSpec((ATT_HEADS, 3 * LANES), lambda i: (0, 0)),
        out_shape=S((ATT_HEADS, 3 * LANES), F32), name="bias_grad",
        compiler_params=_cparams("arbitrary"))(dbt)


def _bias_tiles(bias):
    parts = [jnp.pad(bias, ((0, 0), (0, 0), (CHUNK * ci, ATT_KT - BAND - CHUNK * ci)), constant_values=NEG_INF)
             for ci in range(ATT_CPT)]
    return jnp.stack(parts, axis=1).reshape(ATT_HEADS, ATT_QT, ATT_KT)


def _bias_bands(dbias_tiles):
    d = dbias_tiles.reshape(ATT_HEADS, ATT_CPT, CHUNK, ATT_KT)
    bands = jnp.stack([d[:, ci, :, CHUNK * ci:CHUNK * ci + BAND] for ci in range(ATT_CPT)])
    return jnp.pad(jnp.transpose(bands, (0, 2, 1, 3)), ((0, 0), (0, 0), (0, 0), (0, 5 * LANES - BAND)))


def _att_fill(kw, vw, klo, khi, vlo, vhi):
    kw[0:ATT_RB, :] = klo[...].astype(BF16)
    kw[ATT_RB:, :] = khi[...].astype(BF16)
    vw[0:ATT_RB, :] = vlo[...].astype(BF16)
    vw[ATT_RB:, :] = vhi[...].astype(BF16)


def _att_probs(qm, kwin, bias, first_key):
    s = _dot_nt(qm, kwin) * ATT_SCALE + bias
    col = lax.broadcasted_iota(jnp.int32, (ATT_QT, ATT_KT), 1)
    s = jnp.where(col + first_key >= 0, s, NEG_INF)
    p = jnp.exp(s - jnp.max(s, axis=-1, keepdims=True))
    return p / jnp.sum(p, axis=-1, keepdims=True)


def _att_in_specs(nq):
    qn = lambda n: jnp.minimum(n, nq - 1)
    blk = lambda col, back: pl.BlockSpec((ATT_RB, LANES), lambda hp, n: (jnp.maximum(qn(n) - back, 0), col + hp))
    return [blk(ATT_QCOL, 0), blk(ATT_KCOL, 1), blk(ATT_KCOL, 0), blk(ATT_VCOL, 1), blk(ATT_VCOL, 0),
            pl.BlockSpec((2, ATT_QT, ATT_KT), lambda hp, n: (hp, 0, 0))]


def _attention_fwd(z, bias_t, plan=None):
    T = z.shape[0]
    nq = T // ATT_RB

    def body(q_ref, klo, khi, vlo, vhi, b_ref, o_ref, kw, vw):
        _att_fill(kw, vw, klo, khi, vlo, vhi)
        lane = lax.broadcasted_iota(jnp.int32, (ATT_QT, LANES), 1)
        n = pl.program_id(1)
        for t in range(ATT_RB // ATT_QT):
            rows = slice(t * ATT_QT, (t + 1) * ATT_QT)
            win = slice(t * ATT_QT, t * ATT_QT + ATT_KT)
            qc = q_ref[rows, :]
            outs = []
            for e in range(2):
                qm = jnp.where((lane >= ATT_D) == (e == 1), qc, 0.0).astype(BF16)
                p = _att_probs(qm, kw[win, :], b_ref[e], (n - 1) * ATT_RB + t * ATT_QT)
                outs.append(_dot(p.astype(BF16), vw[win, :]))
            o_ref[rows, :] = jnp.where(lane < ATT_D, outs[0], outs[1]).astype(BF16)

    (yb,), extra = _call(
        body, grid=(ATT_HEADS // 2, nq), in_specs=_att_in_specs(nq),
        out_specs=[pl.BlockSpec((ATT_RB, LANES), lambda hp, n: (n, hp))],
        out_shape=[S((T, ATT_HEADS * ATT_D), BF16)],
        scratch_shapes=[pltpu.VMEM((2 * ATT_RB, LANES), BF16), pltpu.VMEM((2 * ATT_RB, LANES), BF16)],
        name="attention_fwd", sem=("parallel", "parallel"), args=(z, z, z, z, z, bias_t), plan=plan)
    return yb, extra


def _attention_bwd(z, bias_t, dycat, plan=None):
    T = z.shape[0]
    nq = T // ATT_RB
    dycol = RET_HEADS * RET_V // LANES

    def body(q_ref, klo, khi, vlo, vhi, b_ref, dy_ref, dq_ref, dk_ref, dv_ref, db_ref, kw, vw, dkw, dvw):
        n = pl.program_id(1)

        @pl.when(n == 0)
        def _():
            dkw[...] = jnp.zeros_like(dkw)
            dvw[...] = jnp.zeros_like(dvw)
            db_ref[...] = jnp.zeros_like(db_ref)

        @pl.when(n > 0)
        def _():
            dkw[0:ATT_RB, :] = dkw[ATT_RB:, :]
            dvw[0:ATT_RB, :] = dvw[ATT_RB:, :]
            dkw[ATT_RB:, :] = jnp.zeros((ATT_RB, LANES), F32)
            dvw[ATT_RB:, :] = jnp.zeros((ATT_RB, LANES), F32)

        @pl.when(n < nq)
        def _():
            _att_fill(kw, vw, klo, khi, vlo, vhi)
            lane = lax.broadcasted_iota(jnp.int32, (ATT_QT, LANES), 1)
            for t in range(ATT_RB // ATT_QT):
                rows = slice(t * ATT_QT, (t + 1) * ATT_QT)
                win = slice(t * ATT_QT, t * ATT_QT + ATT_KT)
                qc, dyc = q_ref[rows, :], dy_ref[rows, :]
                kwin, vwin = kw[win, :], vw[win, :]
                dq = jnp.zeros((ATT_QT, LANES), F32)
                for e in range(2):
                    mine = (lane >= ATT_D) == (e == 1)
                    qm = jnp.where(mine, qc, 0.0).astype(BF16)
                    dom = jnp.where(mine, dyc, 0.0).astype(BF16)
                    p = _att_probs(qm, kwin, b_ref[e], (n - 1) * ATT_RB + t * ATT_QT)
                    dp = _dot_nt(dom, vwin)
                    ds = p * (dp - jnp.sum(dp * p, axis=-1, keepdims=True))
                    db_ref[e] += ds
                    dsb = (ds * ATT_SCALE).astype(BF16)
                    dq = dq + jnp.where(mine, _dot(dsb, kwin), 0.0)
                    dkw[win, :] += _dot_tn(dsb, qm)
                    dvw[win, :] += _dot_tn(p.astype(BF16), dom)
                dq_ref[rows, :] = dq.astype(BF16)

        dk_ref[...] = dkw[0:ATT_RB, :].astype(BF16)
        dv_ref[...] = dvw[0:ATT_RB, :].astype(BF16)

    qn = lambda n: jnp.minimum(n, nq - 1)
    out_kv = pl.BlockSpec((ATT_RB, LANES), lambda hp, n: (jnp.maximum(n - 1, 0), hp))
    return _call(
        body, grid=(ATT_HEADS // 2, nq + 1),
        in_specs=_att_in_specs(nq) + [pl.BlockSpec((ATT_RB, LANES), lambda hp, n: (qn(n), dycol + hp))],
        out_specs=[pl.BlockSpec((ATT_RB, LANES), lambda hp, n: (qn(n), hp)), out_kv, out_kv,
                   pl.BlockSpec((2, ATT_QT, ATT_KT), lambda hp, n: (hp, 0, 0))],
        out_shape=[S((T, ATT_HEADS * ATT_D), BF16), S((T, ATT_HEADS * ATT_D), BF16),
                   S((T, ATT_HEADS * ATT_D), BF16), S((ATT_HEADS, ATT_QT, ATT_KT), F32)],
        scratch_shapes=[pltpu.VMEM((2 * ATT_RB, LANES), BF16), pltpu.VMEM((2 * ATT_RB, LANES), BF16),
                        pltpu.VMEM((2 * ATT_RB, LANES), F32), pltpu.VMEM((2 * ATT_RB, LANES), F32)],
        name="attention_bwd", sem=("parallel", "arbitrary"), args=(z, z, z, z, z, bias_t, dycat), plan=plan)


HALO = 8


def _causal_conv(ext, w_ref, b_ref):
    back2, back1 = pltpu.roll(ext, 2, 0), pltpu.roll(ext, 1, 0)
    zc = w_ref[0:1, :] * back2 + w_ref[1:2, :] * back1 + w_ref[2:3, :] * ext + b_ref[...]
    return zc[HALO:], back2, back1


def _ffn_down(z, cw, cb, wd, res, name, norm_g=None):
    T = z.shape[0]
    D = wd.shape[1]
    tb, tc = _tile(T, 1024, 8), 256
    nct = FFN_HIDDEN // tc
    per = tb // HALO16

    def body(gp_ref, g_ref, up_ref, u_ref, wg_ref, wu_ref, bg_ref, bu_ref, wd_ref, res_ref, *rest):
        ng_ref = rest[0] if norm_g is not None else None
        f_ref, gc_ref, uc_ref, o_ref = rest[norm_g is not None:][:4]
        hn_ref = rest[5] if norm_g is not None else None
        acc = rest[-1]
        first, j = pl.program_id(0) == 0, pl.program_id(1)

        def conv(p_ref, blk_ref, w_ref, b_ref):
            prev = jnp.where(first, 0.0, p_ref[...].astype(F32)[HALO16 - HALO:])
            return _causal_conv(jnp.concatenate([prev, blk_ref[...].astype(F32)], axis=0), w_ref, b_ref)[0]

        gc, uc = conv(gp_ref, g_ref, wg_ref, bg_ref), conv(up_ref, u_ref, wu_ref, bu_ref)
        gc_ref[...] = gc.astype(BF16)
        uc_ref[...] = uc.astype(BF16)
        f = (_gelu(gc) * uc).astype(BF16)
        f_ref[...] = f
        p = _dot(f, wd_ref[...])

        @pl.when(j == 0)
        def _():
            acc[...] = p

        @pl.when(j > 0)
        def _():
            acc[...] += p

        @pl.when(j == nct - 1)
        def _():
            v = acc[...] + res_ref[...]
            o_ref[...] = v
            if norm_g is not None:
                r = lax.rsqrt(jnp.mean(v * v, axis=-1, keepdims=True) + EPS)
                hn_ref[...] = (v * r * ng_ref[...]).astype(BF16)

    def zspecs(off):
        return [pl.BlockSpec((HALO16, tc), lambda i, j: (jnp.maximum(i * per - 1, 0), j + off)),
                pl.BlockSpec((tb, tc), lambda i, j: (i, j + off))]

    wspec = lambda off, rows: pl.BlockSpec((rows, tc), lambda i, j: (0, j + off))
    row = pl.BlockSpec((tb, D), lambda i, j: (i, 0))
    vec = pl.BlockSpec((1, D), lambda i, j: (0, 0))
    normed = norm_g is not None
    return pl.pallas_call(
        body, grid=(T // tb, nct),
        in_specs=zspecs(0) + zspecs(nct) + [wspec(0, 3), wspec(nct, 3), wspec(0, 1), wspec(nct, 1),
                                            pl.BlockSpec((tc, D), lambda i, j: (j, 0)), row] + [vec] * normed,
        out_specs=[pl.BlockSpec((tb, tc), lambda i, j: (i, j))] * 3 + [row] + [row] * normed,
        out_shape=[S((T, FFN_HIDDEN), BF16)] * 3 + [S((T, D), F32)] + [S((T, D), BF16)] * normed,
        scratch_shapes=[pltpu.VMEM((tb, D), F32)], name=name,
        compiler_params=_cparams("parallel", "arbitrary"))(z, z, z, z, cw, cw, cb, cb, wd, res, *([norm_g] * normed))


HALO16 = 16


def _convglu_bwd(z, gc, uc, df, cw, name):
    T = z.shape[0]
    tb, tc = _tile(T, 1024, 8), 256
    nct = FFN_HIDDEN // tc
    nrb = T // tb

    def body(g_ref, u_ref, gc_ref, gcn_ref, uc_ref, ucn_ref, df_ref, dfn_ref, wg_ref, wu_ref,
             dzg_ref, dzu_ref, dwg_ref, dwu_ref, dbg_ref, dbu_ref):
        i = pl.program_id(1)
        first, last = i == 0, i == nrb - 1

        @pl.when(first)
        def _():
            for ref in (dwg_ref, dwu_ref, dbg_ref, dbu_ref):
                ref[...] = jnp.zeros_like(ref)

        ext = lambda blk_ref, n_ref: jnp.concatenate([blk_ref[...].astype(F32), n_ref[...].astype(F32)[0:HALO]], axis=0)
        gcv, ucv = ext(gc_ref, gcn_ref), ext(uc_ref, ucn_ref)
        dfe = jnp.concatenate([df_ref[...], jnp.where(last, 0.0, dfn_ref[...])], axis=0)
        ge, gd = _gelu_and_grad(gcv)
        dgc, duc = dfe * ucv * gd, dfe * ge
        n = tb + HALO

        def back(d, z_ref, w_ref, dz_ref, dw_ref, db_ref):
            ahead1, ahead2 = pltpu.roll(d, n - 1, 0), pltpu.roll(d, n - 2, 0)
            dz_ref[...] = (w_ref[2:3, :] * d + w_ref[1:2, :] * ahead1 + w_ref[0:1, :] * ahead2)[:tb].astype(BF16)
            db_ref[...] += jnp.sum(d[:tb], axis=0, keepdims=True)
            zv = z_ref[...].astype(F32)
            for k, dk in enumerate((ahead2, ahead1, d)):
                dw_ref[k:k + 1, :] += jnp.sum(dk[:tb] * zv, axis=0, keepdims=True)

        back(dgc, g_ref, wg_ref, dzg_ref, dwg_ref, dbg_ref)
        back(duc, u_ref, wu_ref, dzu_ref, dwu_ref, dbu_ref)

    blk = pl.BlockSpec((tb, tc), lambda j, i: (i, j))
    after = lambda rows: pl.BlockSpec((rows, tc), lambda j, i: (jnp.minimum((i + 1) * (tb // rows), T // rows - 1), j))
    zspec = lambda off: pl.BlockSpec((tb, tc), lambda j, i: (i, j + off))
    wspec = lambda off: pl.BlockSpec((3, tc), lambda j, i: (0, j + off))
    acc = lambda rows: pl.BlockSpec((rows, tc), lambda j, i: (0, j))
    return pl.pallas_call(
        body, grid=(nct, nrb),
        in_specs=[zspec(0), zspec(nct), blk, after(HALO16), blk, after(HALO16), blk, after(HALO), wspec(0), wspec(nct)],
        out_specs=[blk, blk, acc(3), acc(3), acc(1), acc(1)],
        out_shape=[S((T, FFN_HIDDEN), BF16), S((T, FFN_HIDDEN), BF16), S((3, FFN_HIDDEN), F32), S((3, FFN_HIDDEN), F32),
                   S((1, FFN_HIDDEN), F32), S((1, FFN_HIDDEN), F32)],
        name=name, compiler_params=_cparams("parallel", "arbitrary"))(z, z, gc, gc, uc, uc, df, df, cw, cw)


SGU_RB = 256


def _sgu_weights(ws_ref):
    i = lax.broadcasted_iota(jnp.int32, (SGU_BLOCK, SGU_BLOCK), 0)
    j = lax.broadcasted_iota(jnp.int32, (SGU_BLOCK, SGU_BLOCK), 1)
    mask = (j < CHUNK) | (i >= CHUNK)
    return mask, [jnp.where(mask, ws_ref[g], 0.0).astype(BF16) for g in range(SGU_GROUPS)]


def _sgu_norm(zv, lng, lnb):
    mu = jnp.mean(zv, axis=-1, keepdims=True)
    vc = zv - mu
    rstd = lax.rsqrt(jnp.mean(vc * vc, axis=-1, keepdims=True) + EPS)
    vh = vc * rstd
    return vh, rstd, vh * lng + lnb


def _sgu_fwd(zpre, lng, lnb, ws, bst):
    T = zpre.shape[0]
    nb = SGU_RB // SGU_BLOCK

    def body(z_ref, lng_ref, lnb_ref, ws_ref, bst_ref, o_ref):
        _, wm = _sgu_weights(ws_ref)
        u = _gelu(z_ref[:, :SGU_WIDTH].astype(F32))
        _, _, vn = _sgu_norm(_gelu(z_ref[:, SGU_WIDTH:].astype(F32)), lng_ref[...], lnb_ref[...])
        vnb = vn.astype(BF16)
        for b in range(nb):
            rows = slice(b * SGU_BLOCK, (b + 1) * SGU_BLOCK)
            for g in range(SGU_GROUPS):
                cols = slice(g * SGU_GW, (g + 1) * SGU_GW)
                mixed = _dot(wm[g], vnb[rows, cols]) + bst_ref[:, g:g + 1]
                o_ref[rows, cols] = (u[rows, cols] * mixed).astype(BF16)

    vec = pl.BlockSpec((1, SGU_WIDTH), lambda i: (0, 0))
    return pl.pallas_call(
        body, grid=(T // SGU_RB,),
        in_specs=[pl.BlockSpec((SGU_RB, 2 * SGU_WIDTH), lambda i: (i, 0)), vec, vec,
                  pl.BlockSpec((SGU_GROUPS, SGU_BLOCK, SGU_BLOCK), lambda i: (0, 0, 0)),
                  pl.BlockSpec((SGU_BLOCK, SGU_GROUPS), lambda i: (0, 0))],
        out_specs=pl.BlockSpec((SGU_RB, SGU_WIDTH), lambda i: (i, 0)),
        out_shape=S((T, SGU_WIDTH), BF16), name="sgu_fwd", compiler_params=_cparams("parallel"))(zpre, lng, lnb, ws, bst)


def _sgu_bwd(zpre, dy, lng, lnb, ws, bst):
    T = zpre.shape[0]
    nb = SGU_RB // SGU_BLOCK

    def body(z_ref, dy_ref, lng_ref, lnb_ref, ws_ref, bst_ref, dz_ref, dws_ref, dbst_ref, dlng_ref, dlnb_ref, dvn):
        @pl.when(pl.program_id(0) == 0)
        def _():
            for ref in (dws_ref, dbst_ref, dlng_ref, dlnb_ref):
                ref[...] = jnp.zeros_like(ref)

        mask, wm = _sgu_weights(ws_ref)
        u, ud = _gelu_and_grad(z_ref[:, :SGU_WIDTH].astype(F32))
        v, vd = _gelu_and_grad(z_ref[:, SGU_WIDTH:].astype(F32))
        vh, rstd, vn = _sgu_norm(v, lng_ref[...], lnb_ref[...])
        vnb = vn.astype(BF16)
        lane8 = lax.broadcasted_iota(jnp.int32, (SGU_BLOCK, SGU_GROUPS), 1)
        dbs = jnp.zeros((SGU_BLOCK, SGU_GROUPS), F32)
        for b in range(nb):
            rows = slice(b * SGU_BLOCK, (b + 1) * SGU_BLOCK)
            for g in range(SGU_GROUPS):
                cols = slice(g * SGU_GW, (g + 1) * SGU_GW)
                vg = vnb[rows, cols]
                mixed = _dot(wm[g], vg) + bst_ref[:, g:g + 1]
                dyv = dy_ref[rows, cols]
                dz_ref[rows, cols] = (dyv * mixed * ud[rows, cols]).astype(BF16)
                dmix = dyv * u[rows, cols]
                dmb = dmix.astype(BF16)
                dvn[rows, cols] = _dot_tn(wm[g], dmb)
                dws_ref[g] += jnp.where(mask, _dot_nt(dmb, vg), 0.0)
                dbs = dbs + jnp.where(lane8 == g, jnp.sum(dmix, axis=-1, keepdims=True), 0.0)
        dbst_ref[...] += dbs
        dvnv = dvn[...]
        dlng_ref[...] += jnp.sum(dvnv * vh, axis=0, keepdims=True)
        dlnb_ref[...] += jnp.sum(dvnv, axis=0, keepdims=True)
        dvh = dvnv * lng_ref[...]
        dv = rstd * (dvh - jnp.mean(dvh, axis=-1, keepdims=True) - vh * jnp.mean(dvh * vh, axis=-1, keepdims=True))
        dz_ref[:, SGU_WIDTH:] = (dv * vd).astype(BF16)

    vec = pl.BlockSpec((1, SGU_WIDTH), lambda i: (0, 0))
    wsp = pl.BlockSpec((SGU_GROUPS, SGU_BLOCK, SGU_BLOCK), lambda i: (0, 0, 0))
    bsp = pl.BlockSpec((SGU_BLOCK, SGU_GROUPS), lambda i: (0, 0))
    return pl.pallas_call(
        body, grid=(T // SGU_RB,),
        in_specs=[pl.BlockSpec((SGU_RB, 2 * SGU_WIDTH), lambda i: (i, 0)),
                  pl.BlockSpec((SGU_RB, SGU_WIDTH), lambda i: (i, 0)), vec, vec, wsp, bsp],
        out_specs=[pl.BlockSpec((SGU_RB, 2 * SGU_WIDTH), lambda i: (i, 0)), wsp, bsp, vec, vec],
        out_shape=[S((T, 2 * SGU_WIDTH), BF16), S((SGU_GROUPS, SGU_BLOCK, SGU_BLOCK), F32),
                   S((SGU_BLOCK, SGU_GROUPS), F32), S((1, SGU_WIDTH), F32), S((1, SGU_WIDTH), F32)],
        scratch_shapes=[pltpu.VMEM((SGU_RB, SGU_WIDTH), F32)], name="sgu_bwd",
        compiler_params=_cparams("arbitrary"))(zpre, dy, lng, lnb, ws, bst)


def _loss_head(h, tgt, g):
    T, D = h.shape
    tr = _tile(T, 512, 8)

    def body(h_ref, t_ref, g_ref, ls_ref, dh_ref, dhb_ref, dg_ref):
        @pl.when(pl.program_id(0) == 0)
        def _():
            ls_ref[...] = jnp.zeros_like(ls_ref)
            dg_ref[...] = jnp.zeros_like(dg_ref)

        hv = h_ref[...]
        r = lax.rsqrt(jnp.mean(hv * hv, axis=-1, keepdims=True) + EPS)
        xh = hv * r
        diff = xh * g_ref[...] - t_ref[...]
        per_row = jnp.mean(diff * diff, axis=-1, keepdims=True)
        ls_ref[...] += jnp.sum(per_row, axis=0, keepdims=True)
        dy = diff * (1.0 / D)
        dg_ref[...] += jnp.sum(dy * xh, axis=0, keepdims=True)
        dxh = dy * g_ref[...]
        dh = r * (dxh - xh * jnp.mean(dxh * xh, axis=-1, keepdims=True))
        dh_ref[...] = dh
        dhb_ref[...] = dh.astype(BF16)

    row = pl.BlockSpec((tr, D), lambda i: (i, 0))
    vec = pl.BlockSpec((1, D), lambda i: (0, 0))
    return pl.pallas_call(
        body, grid=(T // tr,), in_specs=[row, row, vec],
        out_specs=[pl.BlockSpec((1, LANES), lambda i: (0, 0)), row, row, vec],
        out_shape=[S((1, LANES), F32), S((T, D), F32), S((T, D), BF16), S((1, D), F32)],
        name="loss_head", compiler_params=_cparams("arbitrary"))(h, tgt, g)


ANY = pl.BlockSpec(memory_space=pl.ANY)
COPY_PARTS = 4
SWAP_PARTS = 8
DMA = pltpu.SemaphoreType.DMA


def _place():
    return lax.axis_index("x"), lax.axis_index("y"), lax.axis_index("c")


def _nparts(rows, unit, want):
    n = want
    while n > 1 and rows % (unit * n):
        n //= 2
    return n


def _row_unit(dtype):
    return 16 if jnp.dtype(dtype).itemsize == 2 else 8


def _remote(src, dst, send_sems, recv_sems, k, to):
    return pltpu.make_async_remote_copy(src_ref=src, dst_ref=dst, send_sem=send_sems.at[k], recv_sem=recv_sems.at[k],
                                        device_id=to, device_id_type=MESH)


def _sem_ranges(counts):
    first, total = [], 0
    for c in counts:
        first.append(total)
        total += c
    return first, total


class _Gather:
    def __init__(self, shards):
        self.srcs = list(shards)
        self.halves = [a.shape[0] // 2 for a in shards]
        self.units = [_row_unit(a.dtype) for a in shards]
        self.parts = [_nparts(h, u, COPY_PARTS) for h, u in zip(self.halves, self.units)]
        self.first, total = _sem_ranges([3 * n for n in self.parts])
        self.out_shapes = [S((N_CHIPS,) + a.shape, a.dtype) for a in shards]
        self.scratch = [DMA((total,))] * 4
        self.has_relay = True

    def _ops(self, srcs, outs, sems):
        ici_s, ici_r, rel_s, rel_r = sems
        x, y, c = _place()
        me, sibling = (x, y, c), (x, y, 1 - c)
        chips = [(1 - x, y), (x, 1 - y), (1 - x, 1 - y)]
        send, arrive, relay, relayed = [], [], [], []
        for p_ref, out_ref, Rh, unit, n, base in zip(srcs, outs, self.halves, self.units, self.parts, self.first):
            rp = Rh // n

            def part(px, py, pc, k, out_ref=out_ref, Rh=Rh, unit=unit, rp=rp):
                return out_ref.at[2 * px + py, pl.ds(pl.multiple_of(pc * Rh + k * rp, unit), rp), :]

            def mine(k, p_ref=p_ref, Rh=Rh, unit=unit, rp=rp):
                return p_ref.at[pl.ds(pl.multiple_of(c * Rh + k * rp, unit), rp), :]

            for j, chip in enumerate(chips):
                for k in range(n):
                    s = base + j * n + k
                    send.append(_remote(mine(k), part(x, y, c, k), ici_s, ici_r, s, (*chip, c)))
                    arrive.append(_remote(mine(k), part(*chip, c, k), ici_s, ici_r, s, me))
                    relay.append(_remote(part(*chip, c, k), part(*chip, c, k), rel_s, rel_r, s, sibling))
                    relayed.append(_remote(part(*chip, c, k), part(*chip, 1 - c, k), rel_s, rel_r, s, me))
        return send, arrive, relay, relayed

    def start(self, *refs):
        for cp in self._ops(*refs)[0]:
            cp.start()

    def relay(self, *refs):
        _, arrive, relay, _ = self._ops(*refs)
        for a, r in zip(arrive, relay):
            a.wait_recv()
            r.start()

    def finish(self, *refs):
        send, _, relay, relayed = self._ops(*refs)
        for cp in relayed:
            cp.wait_recv()
        for cp in send + relay:
            cp.wait_send()


class _Exchange:
    def __init__(self, items):
        self.srcs = [a for a, _ in items]
        self.kinds = [k for _, k in items]
        self.blocks = []
        for a, kind in items:
            R, Ccols = a.shape
            self.blocks.append({"cols": (R // 2, Ccols // N_CHIPS), "rows": (R // (2 * N_CHIPS), Ccols), "all": (R, Ccols)}[kind])
        self.units = [_row_unit(a.dtype) for a in self.srcs]
        self.parts = [_nparts(b[0], u, COPY_PARTS) for b, u in zip(self.blocks, self.units)]
        self.first, total = _sem_ranges([N_DEV - 2 + n for n in self.parts])
        self.out_shapes = [S((N_DEV,) + b, a.dtype) for a, b in zip(self.srcs, self.blocks)]
        self.scratch = [DMA((total,))] * 2
        self.has_relay = False

    def _ops(self, srcs, outs, sems):
        ss, rs = sems
        x, y, c = _place()
        send, arrive = [], []
        for src_ref, out_ref, kind, (Rb, Cb), unit, n, base in zip(srcs, outs, self.kinds, self.blocks, self.units, self.parts,
                                                                  self.first):
            rp = Rb // n

            def block_for(px, py, pc, r0, rows, src_ref=src_ref, kind=kind, Rb=Rb, Cb=Cb, unit=unit):
                if kind == "cols":
                    return src_ref.at[pl.ds(pl.multiple_of(pc * Rb + r0, unit), rows),
                                      pl.ds(pl.multiple_of((2 * px + py) * Cb, LANES), Cb)]
                if kind == "rows":
                    return src_ref.at[pl.ds(pl.multiple_of((2 * (2 * px + py) + pc) * Rb + r0, unit), rows), :]
                return src_ref.at[pl.ds(r0, rows), :]

            def slot(d, r0, rows, out_ref=out_ref):
                return out_ref.at[d, pl.ds(r0, rows), :]

            me = 4 * x + 2 * y + c
            for k in range(1, N_DEV):
                peer = (x ^ ((k >> 2) & 1), y ^ ((k >> 1) & 1), c ^ (k & 1))
                pieces = [(N_DEV - 2 + q, q * rp, rp) for q in range(n)] if k == 1 else [(k - 2, 0, Rb)]
                for sem, r0, rows in pieces:
                    send.append(_remote(block_for(*peer, r0, rows), slot(me, r0, rows), ss, rs, base + sem, peer))
                    arrive.append(_remote(block_for(*peer, r0, rows), slot(4 * peer[0] + 2 * peer[1] + peer[2], r0, rows),
                                          ss, rs, base + sem, peer))
        return send, arrive

    def start(self, *refs):
        for cp in self._ops(*refs)[0]:
            cp.start()

    def finish(self, *refs):
        send, arrive = self._ops(*refs)
        for cp in arrive:
            cp.wait_recv()
        for cp in send:
            cp.wait_send()


class _Swap:
    def __init__(self, halves):
        self.srcs = list(halves)
        self.parts = [_nparts(a.shape[0], _row_unit(a.dtype), SWAP_PARTS) for a in halves]
        self.first, total = _sem_ranges(self.parts)
        self.out_shapes = [S(a.shape, a.dtype) for a in halves]
        self.scratch = [DMA((total,))] * 2
        self.has_relay = False

    def _ops(self, srcs, outs, sems):
        ss, rs = sems
        x, y, c = _place()
        copies = []
        for h_ref, out_ref, n, base in zip(srcs, outs, self.parts, self.first):
            rp = h_ref.shape[0] // n
            for k in range(n):
                rows = pl.ds(k * rp, rp)
                copies.append(_remote(h_ref.at[rows, :], out_ref.at[rows, :], ss, rs, base + k, (x, y, 1 - c)))
        return copies, copies

    start = _Exchange.start
    finish = _Exchange.finish


def _run_plan(plan, name):
    ni, no = len(plan.srcs), len(plan.out_shapes)

    def body(*refs):
        parts = (refs[:ni], refs[ni:ni + no], refs[ni + no:])
        plan.start(*parts)
        if plan.has_relay:
            plan.relay(*parts)
        plan.finish(*parts)

    return pl.pallas_call(body, out_shape=plan.out_shapes, in_specs=[ANY] * ni, out_specs=[ANY] * no,
                          scratch_shapes=plan.scratch, name=name)(*plan.srcs)


def _call(body, *, grid, in_specs, out_specs, out_shape, name, sem, args, scratch_shapes=(), plan=None):
    if plan is None:
        return pl.pallas_call(body, grid=grid, in_specs=in_specs, out_specs=out_specs, out_shape=out_shape,
                              scratch_shapes=list(scratch_shapes), name=name, compiler_params=_cparams(*sem))(*args), None
    n_in, n_out, n_scr = len(in_specs), len(out_shape), len(scratch_shapes)
    pi, po = len(plan.srcs), len(plan.out_shapes)
    total = math.prod(grid)

    def wrapped(*refs):
        a, refs = refs[:n_in], refs[n_in:]
        pa, refs = refs[:pi], refs[pi:]
        o, refs = refs[:n_out], refs[n_out:]
        pout, refs = refs[:po], refs[po:]
        scr, psem = refs[:n_scr], refs[n_scr:]
        step = 0
        for d, gsize in enumerate(grid):
            step = step * gsize + pl.program_id(d)

        @pl.when(step == 0)
        def _():
            plan.start(pa, pout, psem)

        body(*a, *o, *scr)
        if plan.has_relay:
            @pl.when(step == (3 * total) // 4)
            def _():
                plan.relay(pa, pout, psem)

        @pl.when(step == total - 1)
        def _():
            plan.finish(pa, pout, psem)

    outs = pl.pallas_call(
        wrapped, grid=grid, in_specs=list(in_specs) + [ANY] * pi, out_specs=list(out_specs) + [ANY] * po,
        out_shape=list(out_shape) + plan.out_shapes, scratch_shapes=list(scratch_shapes) + plan.scratch, name=name,
        compiler_params=_cparams(*["arbitrary"] * len(grid)))(*args, *plan.srcs)
    return outs[:n_out], outs[n_out:]


SMEM = pl.BlockSpec(memory_space=pltpu.SMEM)


def _sum_slots(buf, own, me, name):
    n, R, W = buf.shape
    tr = _tile(R, 256, 8)

    def body(me_ref, b_ref, own_ref, o_ref):
        acc = None
        for s in range(n):
            blk = jnp.where(me_ref[0] == s, own_ref[...], b_ref[s]).astype(F32)
            acc = blk if acc is None else acc + blk
        o_ref[...] = acc

    return pl.pallas_call(
        body, grid=(R // tr,),
        in_specs=[SMEM, pl.BlockSpec((n, tr, W), lambda i: (0, i, 0)), pl.BlockSpec((tr, W), lambda i: (i, 0))],
        out_specs=pl.BlockSpec((tr, W), lambda i: (i, 0)), out_shape=S((R, W), F32), name=name,
        compiler_params=_cparams("parallel"))(me, buf, own)


def _adamw_update(wv, gv, mv, vv):
    mn = ADAM_B1 * mv + (1.0 - ADAM_B1) * gv
    vn = ADAM_B2 * vv + (1.0 - ADAM_B2) * (gv * gv)
    m_hat = mn / (1.0 - ADAM_B1 ** ADAM_STEP)
    v_hat = vn / (1.0 - ADAM_B2 ** ADAM_STEP)
    return -ADAM_LR * (m_hat / (jnp.sqrt(v_hat) + ADAM_EPS) + ADAM_WD * wv), mn, vn


def _adamw(w, g, m, v, name):
    R, W = w.shape
    tr = _tile(R, 256, 8)

    def body(w_ref, g_ref, m_ref, v_ref, d_ref, mo_ref, vo_ref):
        d_ref[...], mo_ref[...], vo_ref[...] = _adamw_update(w_ref[...], g_ref[...], m_ref[...], v_ref[...])

    blk = pl.BlockSpec((tr, W), lambda i: (i, 0))
    return pl.pallas_call(
        body, grid=(R // tr,), in_specs=[blk] * 4, out_specs=[blk] * 3, out_shape=[S((R, W), F32)] * 3, name=name,
        compiler_params=_cparams("parallel"))(w, g, m, v)


def _adamw_shard(w, halves, core, m, v, name):
    L, R, C = w.shape
    Rh = R // 2
    tr = _tile(Rh, 256, 8)
    nbh = Rh // tr

    def body(c_ref, w_ref, m_ref, v_ref, *rest):
        pairs, (g_ref, d_ref, mo_ref, vo_ref) = rest[:2 * L], rest[2 * L:]
        l, i = pl.program_id(0), pl.program_id(1)
        mine_rows = i // nbh == c_ref[0]
        gv = None
        for lp in range(L):
            cand = jnp.where(mine_rows, pairs[2 * lp][...], pairs[2 * lp + 1][...])
            gv = cand if gv is None else jnp.where(l == lp, cand, gv)
        g_ref[...] = gv
        d_ref[...], mo_ref[...], vo_ref[...] = _adamw_update(w_ref[...], gv, m_ref[...], v_ref[...])

    blk = pl.BlockSpec((None, tr, C), lambda l, i: (l, i, 0))
    half = lambda lp: pl.BlockSpec((tr, C), lambda l, i: (jnp.where(l == lp, i % nbh, 0), 0))
    return pl.pallas_call(
        body, grid=(L, R // tr), in_specs=[SMEM, blk, blk, blk] + [half(lp) for lp in range(L) for _ in range(2)],
        out_specs=[blk] * 4, out_shape=[S((L, R, C), F32)] * 4, name=name,
        compiler_params=_cparams("parallel", "parallel"))(core, w, m, v, *[h for pair in halves for h in pair])


def _tables(T):
    f32 = F32
    half = RET_QK // 2
    inv = 1.0 / (10000.0 ** jnp.linspace(0.0, 1.0, half, dtype=f32))
    ang = jnp.arange(T).astype(f32)[:, None] * inv[None, :]
    cos, sin = jnp.cos(ang), jnp.sin(ang)
    c2 = jnp.concatenate([cos, cos], axis=-1)
    s2 = jnp.concatenate([-sin, sin], axis=-1)
    log_g = jnp.log1p(-jnp.exp2(-5.0 - jnp.arange(RET_HEADS, dtype=f32)))
    idx = jnp.arange(CHUNK, dtype=f32)
    dintra = jnp.exp(log_g[:, None, None] * jnp.abs(idx[:, None] - idx[None, :]))
    kdec = jnp.exp(log_g[None, :] * (CHUNK - 1 - idx)[:, None]).T
    qdec = jnp.exp(log_g[None, :] * (idx + 1.0)[:, None]).T
    cdec = jnp.exp(log_g * CHUNK)
    bc = lambda a, w: jnp.broadcast_to(a[:, :, None], (RET_HEADS, a.shape[1], w))
    return c2, s2, dintra, bc(qdec, RET_QK), bc(kdec, RET_QK), jnp.broadcast_to(cdec[:, None, None], (RET_HEADS, 1, RET_V))


def _first_forms(g4):
    return {"ab_w_in4": g4["ab_w_in"],
            "ab_w_inT": jnp.transpose(g4["ab_w_in"], (0, 2, 1)).reshape(-1, D_MODEL),
            "ab_w_out": g4["ab_w_out"].reshape(-1, D_MODEL)}


def _late_forms(g4):
    wd = g4["ffn_w_down"]
    per = wd.shape[1] // 2
    return {"c_w_in4": g4["c_w_in"], "c_w_out": g4["c_w_out"].reshape(-1, D_MODEL), "ffn_w_up4": g4["ffn_w_up"],
            "ffn_w_down": [wd[:, l * per:(l + 1) * per].reshape(-1, D_MODEL) for l in range(2)]}


def _local_step(x, tgt, p, first=None, late=None, exchange=False):
    T = x.shape[0]
    tab = _tables(T)
    row = lambda a: a.reshape(1, -1)
    tr = lambda w: jnp.transpose(w)
    width = D_MODEL

    hn0, first_out = _rmsnorm_fwd(x, row(p["attn_norm_g"][0]), "norm_a0", plan=first[0] if first else None)
    if first:
        p = {**p, **first[1](first_out)}
    z0 = _mm(hn0, p["ab_w_in4"], "mm_ab_in", form="kn4", out_dtype=BF16)
    ya, r, st = _retention_fwd(z0, *tab)
    bias_t = _bias_tiles(jnp.transpose(_bias_build(p["ab_rel_bias"][0]), (1, 0, 2))[:, :, :BAND])
    yb, late_out = _attention_fwd(z0, bias_t, plan=late[0] if late else None)
    if late:
        p = {**p, **late[1](late_out)}
    h1, hf0 = _mm([ya, yb], p["ab_w_out"], "mm_ab_out", res=x, norm_g=row(p["ffn_norm_g"][0]))

    def ffn_fwd(h, hf, l, next_g):
        zf = _mm(hf, p["ffn_w_up4"], f"mm_up{l}", form="kn4", row0=l * width, out_dtype=BF16)
        f, gc, uc, *out = _ffn_down(zf, p["ffn_conv_w"][l], row(p["ffn_conv_b"][l]), p["ffn_w_down"][l], h, f"ffn_down{l}",
                                    norm_g=next_g)
        return (zf, f, gc, uc), (tuple(out) if next_g is not None else out[0])

    kept0, (h2, hn1) = ffn_fwd(h1, hf0, 0, row(p["attn_norm_g"][1]))
    zc = _mm(hn1, p["c_w_in4"], "mm_c_in", form="kn4", out_dtype=BF16)
    lng, lnb, bst, ws = row(p["c_ln_g"][0]), row(p["c_ln_b"][0]), tr(p["c_b_s"][0]), p["c_w_s"][0]
    y1 = _sgu_fwd(zc, lng, lnb, ws, bst)
    h3, hf1 = _mm(y1, p["c_w_out"], "mm_c_out", res=h2, norm_g=row(p["ffn_norm_g"][1]))
    kept1, h4 = ffn_fwd(h3, hf1, 1, None)
    lsum, dh4, dh4b, dgfin = _loss_head(h4, tgt, row(p["final_norm_g"]))

    g, big = {}, {}

    def ffn_bwd(dh, dhb, h_in, hf, kept, l):
        zf, f, gc, uc = kept
        big[f"ffn_w_down{l}"] = _mm_tn(f, dhb, f"mmt_down{l}", out_dtype=BF16)
        df = _mm(dhb, p["ffn_w_down"][l], f"mmb_down{l}", form="nk")
        dzg, dzu, dwg, dwu, dbg, dbu = _convglu_bwd(zf, gc, uc, df, p["ffn_conv_w"][l], f"convglu_bwd{l}")
        big[f"ffn_w_up{l}"] = _mm_tn(hf, [dzg, dzu], f"mmt_up{l}", out_dtype=BF16)
        dhf = _mm([dzg, dzu], p["ffn_w_up4"], f"mmb_up{l}", form="nk4", row0=l * width, rows=width, out_dtype=BF16)
        dh_in, dh_in_b, dgf = _rmsnorm_bwd(h_in, dhf, row(p["ffn_norm_g"][l]), dh, f"norm_f{l}_bwd")
        return dh_in, dh_in_b, dict(ffn_norm_g=dgf[0], ffn_conv_w=jnp.concatenate([dwg, dwu], axis=1),
                                    ffn_conv_b=jnp.concatenate([dbg, dbu], axis=1)[0])

    dh3, dh3b, gf1 = ffn_bwd(dh4, dh4b, h3, hf1, kept1, 1)
    big["c_w_out"] = _mm_tn(y1, dh3b, "mmt_c_out", out_dtype=BF16)
    dy1 = _mm(dh3b, p["c_w_out"], "mmb_c_out", form="nk")
    dzc, dws, dbst, dlng, dlnb = _sgu_bwd(zc, dy1, lng, lnb, ws, bst)
    g["c_w_s"], g["c_b_s"], g["c_ln_g"], g["c_ln_b"] = dws[None], tr(dbst)[None], dlng, dlnb
    big["c_w_in"] = _mm_tn(hn1, dzc, "mmt_c_in", out_dtype=BF16)
    dhn1 = _mm(dzc, p["c_w_in4"], "mmb_c_in", form="nk4", rows=width, out_dtype=BF16)
    dh2, dh2b, dga1 = _rmsnorm_bwd(h2, dhn1, row(p["attn_norm_g"][1]), dh3, "norm_a1_bwd")
    dh1, dh1b, gf0 = ffn_bwd(dh2, dh2b, h1, hf0, kept0, 0)
    for k in gf0:
        g[k] = jnp.stack([gf0[k], gf1[k]])
    big["ab_w_out"] = _mm_tn([ya, yb], dh1b, "mmt_ab_out", out_dtype=BF16)
    dycat = _mm(dh1b, p["ab_w_out"], "mmb_ab_out", form="nk")
    g["final_norm_g"] = dgfin[0]
    early_pack = _pack([dga1[0]] + [g[n] for n in EARLY_SMALL], 32)
    late_plan = _Exchange([(big[n], kind) for n, kind in LATE_ITEMS] + [(early_pack, "all")]) if exchange else None
    (dqb, dkb, dvb, dbias_t), late_slots = _attention_bwd(z0, bias_t, dycat, plan=late_plan)
    dqa, dka, dva, dga = _retention_bwd(z0, *tab, r, dycat, st)
    dz0 = [dqa, dka, dva, dga, dqb, dkb, dvb]
    big["ab_w_in"] = _mm_tn(hn0, dz0, "mmt_ab_in", out_dtype=BF16)
    slots = {}
    if exchange:
        dhn0, first_slots = _mm(dz0, p["ab_w_inT"], "mmb_ab_in", out_dtype=BF16,
                                plan=_Exchange([(big[n], kind) for n, kind in FIRST_ITEMS]))
        slots = dict(first=first_slots, late=late_slots[:-1], early=(early_pack, late_slots[-1]))
    else:
        dhn0 = _mm(dz0, p["ab_w_inT"], "mmb_ab_in", out_dtype=BF16)
    gx, _, dga0 = _rmsnorm_bwd(x, dhn0, row(p["attn_norm_g"][0]), dh1, "norm_a0_bwd")
    g["ab_rel_bias"] = _bias_grad(_bias_bands(dbias_t))[None, :, :N_REL]
    g["attn_norm_g"] = jnp.stack([dga0[0], dga1[0]])
    return lsum[0, 0], gx, g, big, slots


FIRST_BIG = ["ab_w_in", "ab_w_out"]
LATE_BIG = ["c_w_in", "c_w_out", "ffn_w_up", "ffn_w_down"]
BIG = FIRST_BIG + LATE_BIG
FIRST_ITEMS = [("ab_w_in", "cols"), ("ab_w_out", "rows")]
LATE_ITEMS = [("c_w_in", "cols"), ("c_w_out", "rows"), ("ffn_w_up0", "cols"), ("ffn_w_up1", "cols"),
              ("ffn_w_down0", "rows"), ("ffn_w_down1", "rows")]
LAYERS_OF = {"ab_w_in": ["ab_w_in"], "ab_w_out": ["ab_w_out"], "c_w_in": ["c_w_in"], "c_w_out": ["c_w_out"],
             "ffn_w_up": ["ffn_w_up0", "ffn_w_up1"], "ffn_w_down": ["ffn_w_down0", "ffn_w_down1"]}
SMALL_SHARDED = [("c_ln_g", 1), ("c_ln_b", 1), ("ffn_conv_w", 2)]
REPLICATED = ["attn_norm_g", "ffn_norm_g", "ab_rel_bias", "c_w_s", "c_b_s", "ffn_conv_b", "final_norm_g"]
EARLY_SMALL = ["ffn_norm_g", "c_w_s", "c_b_s", "ffn_conv_b", "final_norm_g", "c_ln_g", "c_ln_b", "ffn_conv_w"]


def _rows_of(n_elems):
    return -(-n_elems // PACK_W)


def _flat_rows(a):
    f = a.reshape(-1)
    rows = _rows_of(f.shape[0])
    return jnp.pad(f, (0, rows * PACK_W - f.shape[0])).reshape(rows, PACK_W)


def _pad_rows(a, mult):
    extra = (-a.shape[0]) % mult
    return jnp.pad(a, ((0, extra), (0, 0))) if extra else a


def _pack(arrs, mult):
    return _pad_rows(jnp.concatenate([_flat_rows(a) for a in arrs], axis=0), mult)


def _unpack(buf, shapes):
    out, r = [], 0
    for shp in shapes:
        n = math.prod(shp)
        rows = _rows_of(n)
        out.append(buf[r:r + rows].reshape(-1)[:n].reshape(shp))
        r += rows
    return out


def _from_shards(sh, axis):
    m = jnp.moveaxis(sh, 0, axis)
    shp = m.shape
    return m.reshape(shp[:axis] + (shp[axis] * shp[axis + 1],) + shp[axis + 2:])


def _as_bf16_pairs(a):
    return lax.bitcast_convert_type(a.astype(F32), BF16)


def _from_bf16_pairs(a):
    return lax.bitcast_convert_type(a, F32)


def kernel(x, attn_norm_g, ffn_norm_g, ab_w_in, ab_w_out, ab_rel_bias, c_w_in, c_ln_g, c_ln_b, c_w_s, c_b_s, c_w_out, ffn_w_up, ffn_conv_w, ffn_conv_b, ffn_w_down, final_norm_g, loss_target, m_attn_norm_g, m_ffn_norm_g, m_ab_w_in, m_ab_w_out, m_ab_rel_bias, m_c_w_in, m_c_ln_g, m_c_ln_b, m_c_w_s, m_c_b_s, m_c_w_out, m_ffn_w_up, m_ffn_conv_w, m_ffn_conv_b, m_ffn_w_down, m_final_norm_g, v_attn_norm_g, v_ffn_norm_g, v_ab_w_in, v_ab_w_out, v_ab_rel_bias, v_c_w_in, v_c_ln_g, v_c_ln_b, v_c_w_s, v_c_b_s, v_c_w_out, v_ffn_w_up, v_ffn_conv_w, v_ffn_conv_b, v_ffn_w_down, v_final_norm_g):
    w = dict(attn_norm_g=attn_norm_g, ffn_norm_g=ffn_norm_g, ab_w_in=ab_w_in, ab_w_out=ab_w_out, ab_rel_bias=ab_rel_bias,
             c_w_in=c_w_in, c_ln_g=c_ln_g, c_ln_b=c_ln_b, c_w_s=c_w_s, c_b_s=c_b_s, c_w_out=c_w_out, ffn_w_up=ffn_w_up,
             ffn_conv_w=ffn_conv_w, ffn_conv_b=ffn_conv_b, ffn_w_down=ffn_w_down, final_norm_g=final_norm_g)
    m = dict(attn_norm_g=m_attn_norm_g, ffn_norm_g=m_ffn_norm_g, ab_w_in=m_ab_w_in, ab_w_out=m_ab_w_out,
             ab_rel_bias=m_ab_rel_bias, c_w_in=m_c_w_in, c_ln_g=m_c_ln_g, c_ln_b=m_c_ln_b, c_w_s=m_c_w_s, c_b_s=m_c_b_s,
             c_w_out=m_c_w_out, ffn_w_up=m_ffn_w_up, ffn_conv_w=m_ffn_conv_w, ffn_conv_b=m_ffn_conv_b,
             ffn_w_down=m_ffn_w_down, final_norm_g=m_final_norm_g)
    v = dict(attn_norm_g=v_attn_norm_g, ffn_norm_g=v_ffn_norm_g, ab_w_in=v_ab_w_in, ab_w_out=v_ab_w_out,
             ab_rel_bias=v_ab_rel_bias, c_w_in=v_c_w_in, c_ln_g=v_c_ln_g, c_ln_b=v_c_ln_b, c_w_s=v_c_w_s, c_b_s=v_c_b_s,
             c_w_out=v_c_w_out, ffn_w_up=v_ffn_w_up, ffn_conv_w=v_ffn_conv_w, ffn_conv_b=v_ffn_conv_b,
             ffn_w_down=v_ffn_w_down, final_norm_g=v_final_norm_g)
    names = list(w)
    chip = 2 * lax.axis_index("x") + lax.axis_index("y")

    core = lax.axis_index("c")
    core_arr = core.reshape(1).astype(jnp.int32)
    me_arr = (2 * chip + core).reshape(1).astype(jnp.int32)
    two_d = lambda a: a.reshape(-1, a.shape[-1])
    with_own = lambda gathered, own: lax.dynamic_update_slice(gathered, own[None], (chip, 0, 0))

    send_first = [two_d(w[n]).astype(BF16) for n in FIRST_BIG]
    finish_first = lambda got: _first_forms({n: with_own(a, own) for n, a, own in zip(FIRST_BIG, got, send_first)})
    full = {n: w[n] for n in REPLICATED}
    small_send = [_as_bf16_pairs(w[n]) for n, _ in SMALL_SHARDED]
    send_late = [two_d(w[n]).astype(BF16) for n in LATE_BIG] + [_pack(small_send, 32)]

    def finish_late(got):
        whole = [with_own(a, own) for a, own in zip(got, send_late)]
        forms = _late_forms(dict(zip(LATE_BIG, whole)))
        parts = [_unpack(whole[-1][s], [a.shape for a in small_send]) for s in range(N_CHIPS)]
        for i, (n, axis) in enumerate(SMALL_SHARDED):
            forms[n] = _from_shards(_from_bf16_pairs(jnp.stack([parts[s][i] for s in range(N_CHIPS)])), axis)
        return forms

    lsum, grad_x, g, big, slots = _local_step(x[0], loss_target[0], full, first=(_Gather(send_first), finish_first),
                                              late=(_Gather(send_late), finish_late), exchange=True)
    loss = lax.psum(0.5 * lsum, ("x", "y", "c"))

    def own_block(a, kind):
        rows, cols = a.shape
        if kind == "cols":
            return lax.dynamic_slice(a, (core * (rows // 2), chip * (cols // N_CHIPS)), (rows // 2, cols // N_CHIPS))
        per = rows // N_DEV
        return lax.dynamic_slice(a, ((2 * chip + core) * per, 0), (per, cols))

    reduced = {}
    for key, items in (("late", LATE_ITEMS), ("first", FIRST_ITEMS)):
        halves = [_sum_slots(got, own_block(big[n], kind), me_arr, f"sum_{n}") for (n, kind), got in zip(items, slots[key])]
        others = _run_plan(_Swap(halves), f"swap_{key}")
        reduced.update({n: (h, o) for (n, _), h, o in zip(items, halves, others)})
    big_outs = [{}, {}, {}, {}]
    for n in BIG:
        res = _adamw_shard(w[n], [reduced[layer] for layer in LAYERS_OF[n]], core_arr, m[n], v[n], f"adamw_{n}")
        for k in range(4):
            big_outs[k][n] = res[k]

    small_names = REPLICATED + [n for n, _ in SMALL_SHARDED]
    early_pack, early_slots = slots["early"]
    early = _unpack(_sum_slots(early_slots, early_pack, me_arr, "sum_early"), [(D_MODEL,)] + [g[n].shape for n in EARLY_SMALL])
    last_pack = _pack([g["attn_norm_g"][0], g["ab_rel_bias"]], 8)
    last_slots = _run_plan(_Exchange([(last_pack, "all")]), "exchange_small")[0]
    norm_a0, rel_bias = _unpack(_sum_slots(last_slots, last_pack, me_arr, "sum_small"), [(D_MODEL,), g["ab_rel_bias"].shape])
    gsmall_full = dict(zip(EARLY_SMALL, early[1:]), attn_norm_g=jnp.stack([norm_a0, early[0]]), ab_rel_bias=rel_bias)
    for n, axis in SMALL_SHARDED:
        size = w[n].shape[axis]
        gsmall_full[n] = lax.dynamic_slice_in_dim(gsmall_full[n], chip * size, size, axis)
    pack_small = lambda d: _pack([d[n] for n in small_names], 8)
    small_out = _adamw(pack_small(w), pack_small(gsmall_full), pack_small(m), pack_small(v), "adamw_small")
    small_shapes = [w[n].shape for n in small_names]

    outs = [{**big_outs[0], **gsmall_full}]
    for k in range(3):
        outs.append({**big_outs[k + 1], **dict(zip(small_names, _unpack(small_out[k], small_shapes)))})
    return (loss, grad_x[None], *[o[n] for o in outs for n in names])
```
